```python
import math
import jax
import jax.numpy as jnp
from jax import lax
import numpy as np

D_MODEL = 1024
BATCH = 16
SEQ = 2048
DEPTH = 4

LN_EPS = 1e-5
DN_ALPHA = (2.0 * DEPTH) ** 0.25
DN_BETA = (8.0 * DEPTH) ** -0.25
FFN_RES = 0.5
D_FF = ((8 * D_MODEL // 3 + 127) // 128) * 128
POOL_WIDTH = 3 * D_MODEL // 4
POOL_WINDOWS = (2, 4, 8, 16)
POOL_GROUPS = len(POOL_WINDOWS)
POOL_GDIM = POOL_WIDTH // POOL_GROUPS
SSD_INNER = D_MODEL
SSD_HEADDIM = 64
SSD_HEADS = SSD_INNER // SSD_HEADDIM
SSD_GROUPS = 4
SSD_HPG = SSD_HEADS // SSD_GROUPS
SSD_STATE = 128
SSD_CONV = 4
SSD_CHUNK = 128
SSD_CONV_CH = SSD_INNER + 2 * SSD_GROUPS * SSD_STATE
SSD_EPS = 1e-5
ATTN_CONFIGS = ((128, 1), (512, 4), (2048, 16))
ATTN_HEAD_DIM = 64
ATTN_GROUP_HEADS = 4
ATTN_HEADS = ATTN_GROUP_HEADS * len(ATTN_CONFIGS)
ATTN_WIDTH = ATTN_HEADS * ATTN_HEAD_DIM
ATTN_OUT = ATTN_GROUP_HEADS * ATTN_HEAD_DIM
REL_BUCKETS = 32
REL_MAX_DIST = 2048
N_BRANCH = 3
IN_SIZES = (POOL_WIDTH, SSD_INNER, SSD_CONV_CH, SSD_HEADS, ATTN_WIDTH, ATTN_WIDTH, ATTN_WIDTH, N_BRANCH * D_MODEL)
IN_SPLITS = tuple(sum(IN_SIZES[:i + 1]) for i in range(len(IN_SIZES) - 1))
W_IN_COLS = sum(IN_SIZES)
DT_OFFSET = POOL_WIDTH + SSD_INNER + SSD_CONV_CH

kernel_name = 'hybrid_pool_ssd_dilated_attn_block'


def layer_norm(x, g, b):
    xf = x.astype(jnp.float32)
    mu = xf.mean(-1, keepdims=True)
    var = jnp.square(xf - mu).mean(-1, keepdims=True)
    return ((xf - mu) * lax.rsqrt(var + LN_EPS) * g + b).astype(x.dtype)


def swiglu(x, w13, w2):
    a, g = jnp.split(x @ w13, 2, axis=-1)
    return (jax.nn.silu(a) * g) @ w2


def pool_mixer(u, pool_w, pool_b, pool_scale):
    b, s, _ = u.shape
    uf = u.astype(jnp.float32).reshape(b, s, POOL_GROUPS, POOL_GDIM)
    csum = jnp.pad(jnp.cumsum(uf, axis=1), ((0, 0), (1, 0), (0, 0), (0, 0)))
    hi = jnp.arange(1, s + 1)[:, None]
    lo = jnp.maximum(hi - jnp.array(POOL_WINDOWS)[None, :], 0)
    lower = csum[:, lo, jnp.arange(POOL_GROUPS)[None, :], :]
    mean = (csum[:, 1:] - lower) / (hi - lo).astype(jnp.float32)[None, :, :, None]
    pooled = (mean - uf).astype(u.dtype)
    y = jnp.einsum('bsgc,gcd->bsgd', pooled, pool_w) + pool_b
    return y.reshape(b, s, POOL_WIDTH) * pool_scale


def causal_dwconv(x, w, bias):
    y = lax.conv_general_dilated(x, w[:, None, :], window_strides=(1,), padding=[(SSD_CONV - 1, 0)],
                                 dimension_numbers=('NWC', 'WIO', 'NWC'), feature_group_count=x.shape[-1])
    return y + bias


def segsum(a):
    cs = jnp.cumsum(a, axis=-1)
    diff = cs[..., :, None] - cs[..., None, :]
    n = a.shape[-1]
    mask = jnp.tril(jnp.ones((n, n), dtype=bool))
    return jnp.where(mask, diff, -jnp.inf)


def ssd_scan(xh, dt, a, bm, cm):
    b, s, g, e, p = xh.shape
    n = bm.shape[-1]
    nc, l = s // SSD_CHUNK, SSD_CHUNK
    xdt = (xh * dt[..., None]).reshape(b, nc, l, g, e, p)
    adt = (dt * a).reshape(b, nc, l, g, e).transpose(0, 1, 3, 4, 2)
    bm = bm.reshape(b, nc, l, g, n)
    cm = cm.reshape(b, nc, l, g, n)
    a_cum = jnp.cumsum(adt, axis=-1)
    decay = jnp.exp(segsum(adt))
    cb = jnp.einsum('bclgn,bcsgn->bcgls', cm, bm)
    y_diag = jnp.einsum('bcgls,bcgels,bcsgep->bclgep', cb, decay, xdt)
    decay_states = jnp.exp(a_cum[..., -1:] - a_cum)
    states = jnp.einsum('bclgn,bcgel,bclgep->bcgepn', bm, decay_states, xdt)
    chunk_decay = jnp.exp(a_cum[..., -1])

    def step(h, inp):
        st, dec = inp
        return h * dec[..., None, None] + st, h

    h0 = jnp.zeros((b, g, e, p, n), jnp.float32)
    _, prev = lax.scan(step, h0, (states.transpose(1, 0, 2, 3, 4, 5), chunk_decay.transpose(1, 0, 2, 3)))
    prev = prev.transpose(1, 0, 2, 3, 4, 5)
    y_off = jnp.einsum('bclgn,bcgepn,bcgel->bclgep', cm, prev, jnp.exp(a_cum))
    return (y_diag + y_off).reshape(b, s, g, e, p)


def ssd_mixer(z, xbc, dt_raw, conv_w, conv_b, dt_bias, a_log, d_skip, norm_w):
    b, s, _ = z.shape
    xbc = jax.nn.silu(causal_dwconv(xbc, conv_w, conv_b)).astype(jnp.float32)
    xs, bm, cm = jnp.split(xbc, [SSD_INNER, SSD_INNER + SSD_GROUPS * SSD_STATE], axis=-1)
    dt = jax.nn.softplus(dt_raw.astype(jnp.float32) + dt_bias.astype(jnp.float32))
    a = -jnp.exp(a_log.astype(jnp.float32))
    xh = xs.reshape(b, s, SSD_GROUPS, SSD_HPG, SSD_HEADDIM)
    y = ssd_scan(xh, dt.reshape(b, s, SSD_GROUPS, SSD_HPG), a.reshape(SSD_GROUPS, SSD_HPG),
                 bm.reshape(b, s, SSD_GROUPS, SSD_STATE), cm.reshape(b, s, SSD_GROUPS, SSD_STATE))
    y = y + d_skip.astype(jnp.float32).reshape(SSD_GROUPS, SSD_HPG)[..., None] * xh
    y = y.reshape(b, s, SSD_INNER) * jax.nn.silu(z.astype(jnp.float32))
    yg = y.reshape(b, s, SSD_GROUPS, SSD_INNER // SSD_GROUPS)
    yg = yg * lax.rsqrt(jnp.mean(jnp.square(yg), axis=-1, keepdims=True) + SSD_EPS)
    return (yg.reshape(b, s, SSD_INNER) * norm_w).astype(z.dtype)


def t5_bucket_np(dist):
    dist = np.maximum(dist, 0)
    max_exact = REL_BUCKETS // 2
    large = max_exact + (np.log(np.maximum(dist, 1) / max_exact) / np.log(REL_MAX_DIST / max_exact)
                         * (REL_BUCKETS - max_exact)).astype(np.int32)
    large = np.minimum(large, REL_BUCKETS - 1)
    return np.where(dist < max_exact, dist, large).astype(np.int32)


def dilated_group_attention(q, k, v, bias_tab, window, dilation):
    b, s, h, dh = q.shape
    span = window // dilation
    blk = span
    sub_len = s // dilation
    n_blk = -(-sub_len // blk)
    lp = n_blk * blk

    def to_sub(t):
        t = t.reshape(b, sub_len, dilation, h, dh).transpose(0, 2, 1, 3, 4)
        return jnp.pad(t, ((0, 0), (0, 0), (0, lp - sub_len), (0, 0), (0, 0)))

    def band(t):
        tb = t.reshape(b, dilation, n_blk, blk, h, dh)
        prev = jnp.pad(tb, ((0, 0), (0, 0), (1, 0), (0, 0), (0, 0), (0, 0)))[:, :, :-1]
        return jnp.concatenate([prev, tb], axis=3)

    qb = to_sub(q).reshape(b, dilation, n_blk, blk, h, dh)
    kb = band(to_sub(k))
    vb = band(to_sub(v))
    qi = np.arange(blk)[:, None]
    kj = np.arange(2 * blk)[None, :]
    delta = qi - kj + blk
    in_band = (delta >= 0) & (delta <= span)
    has_prev = (np.arange(n_blk)[:, None, None] > 0) | (kj >= blk)[None]
    valid = jnp.asarray(in_band[None] & has_prev)
    bias = jnp.take(bias_tab, jnp.asarray(t5_bucket_np(delta * dilation)), axis=0)
    bias = bias.astype(jnp.float32).transpose(2, 0, 1)
    logits = jnp.einsum('brnqhd,brnkhd->brnhqk', qb, kb).astype(jnp.float32) * (dh ** -0.5) + bias
    logits = jnp.where(valid[None, None, :, None], logits, -jnp.inf)
    m = jnp.max(logits, axis=-1, keepdims=True)
    pexp = jnp.exp(logits - m)
    den = jnp.sum(pexp, axis=-1, keepdims=True)
    out = jnp.einsum('brnhqk,brnkhd->brnqhd', (pexp / den).astype(v.dtype), vb)
    lse = (m + jnp.log(den))[..., 0]
    out = out.reshape(b, dilation, lp, h, dh)[:, :, :sub_len].transpose(0, 2, 1, 3, 4).reshape(b, s, h, dh)
    lse = lse.transpose(0, 1, 2, 4, 3).reshape(b, dilation, lp, h)[:, :, :sub_len]
    lse = lse.transpose(0, 2, 1, 3).reshape(b, s, h)
    return out, lse


def dilated_attention(q, k, v, rel_bias):
    b, s = q.shape[:2]
    outs, lses = [], []
    for gi, (window, dilation) in enumerate(ATTN_CONFIGS):
        sl = slice(gi * ATTN_GROUP_HEADS, (gi + 1) * ATTN_GROUP_HEADS)
        o, l = dilated_group_attention(q[:, :, sl], k[:, :, sl], v[:, :, sl], rel_bias[:, sl], window, dilation)
        outs.append(o)
        lses.append(l)
    wgt = jax.nn.softmax(jnp.stack(lses, axis=0), axis=0)
    y = jnp.einsum('gbsh,gbshd->bshd', wgt, jnp.stack(outs, axis=0).astype(jnp.float32))
    return y.reshape(b, s, ATTN_OUT).astype(q.dtype)


def hybrid_mixer(x, w_in, gate_b, pool_w, pool_b, pool_scale, conv_w, conv_b, dt_bias, a_log, d_skip,
                 ssd_norm, rel_bias, p_pool, p_ssd, p_attn, w_out):
    b, s, _ = x.shape
    hcat = x @ w_in
    u_pool, z, xbc, dt_raw, q, k, v, gates = jnp.split(hcat, IN_SPLITS, axis=-1)
    y_a = pool_mixer(u_pool, pool_w, pool_b, pool_scale) @ p_pool
    y_b = ssd_mixer(z, xbc, dt_raw, conv_w, conv_b, dt_bias, a_log, d_skip, ssd_norm) @ p_ssd
    hs = (b, s, ATTN_HEADS, ATTN_HEAD_DIM)
    y_c = dilated_attention(q.reshape(hs), k.reshape(hs), v.reshape(hs), rel_bias) @ p_attn
    g = jax.nn.sigmoid(gates.reshape(b, s, N_BRANCH, D_MODEL) + gate_b)
    merged = g[:, :, 0] * y_a + g[:, :, 1] * y_b + g[:, :, 2] * y_c
    return merged @ w_out


def _fwd_setup_inputs(seed: int = 0) -> dict:
    key = jax.random.key(seed)
    ks = jax.random.split(key, 32)
    f32 = jnp.float32

    def nrm(k, shape, scale):
        return jax.random.normal(k, shape, f32) * scale

    x = jax.random.normal(ks[0], (BATCH, SEQ, D_MODEL), f32)
    col_scale = jnp.ones((W_IN_COLS,), f32).at[DT_OFFSET:DT_OFFSET + SSD_HEADS].set(0.1)
    w_in = nrm(ks[1], (DEPTH, D_MODEL, W_IN_COLS), D_MODEL ** -0.5) * col_scale
    dt0 = jnp.exp(jax.random.uniform(ks[2], (DEPTH, SSD_HEADS), f32, math.log(1e-3), math.log(1e-1)))
    return {
        'x': x,
        'ffn1_w13': nrm(ks[3], (DEPTH, D_MODEL, 2 * D_FF), D_MODEL ** -0.5),
        'ffn1_w2': nrm(ks[4], (DEPTH, D_FF, D_MODEL), D_FF ** -0.5 * DN_BETA),
        'ln1_g': 1.0 + nrm(ks[5], (DEPTH, D_MODEL), 0.02),
        'ln1_b': nrm(ks[6], (DEPTH, D_MODEL), 0.02),
        'w_in': w_in,
        'gate_b': nrm(ks[7], (DEPTH, N_BRANCH, D_MODEL), 0.1),
        'pool_w': nrm(ks[8], (DEPTH, POOL_GROUPS, POOL_GDIM, POOL_GDIM), POOL_GDIM ** -0.5),
        'pool_b': nrm(ks[9], (DEPTH, POOL_GROUPS, POOL_GDIM), 0.02),
        'pool_scale': 1.0 + nrm(ks[10], (DEPTH, POOL_WIDTH), 0.02),
        'conv_w': nrm(ks[11], (DEPTH, SSD_CONV, SSD_CONV_CH), SSD_CONV ** -0.5),
        'conv_b': nrm(ks[12], (DEPTH, SSD_CONV_CH), 0.02),
        'dt_bias': dt0 + jnp.log(-jnp.expm1(-dt0)),
        'a_log': jnp.log(jax.random.uniform(ks[13], (DEPTH, SSD_HEADS), f32, 1.0, 16.0)),
        'd_skip': 1.0 + nrm(ks[14], (DEPTH, SSD_HEADS), 0.02),
        'ssd_norm': 1.0 + nrm(ks[15], (DEPTH, SSD_INNER), 0.02),
        'rel_bias': nrm(ks[16], (REL_BUCKETS, ATTN_HEADS), 0.2),
        'p_pool': nrm(ks[17], (DEPTH, POOL_WIDTH, D_MODEL), POOL_WIDTH ** -0.5),
        'p_ssd': nrm(ks[18], (DEPTH, SSD_INNER, D_MODEL), SSD_INNER ** -0.5),
        'p_attn': nrm(ks[19], (DEPTH, ATTN_OUT, D_MODEL), ATTN_OUT ** -0.5),
        'w_out': nrm(ks[20], (DEPTH, D_MODEL, D_MODEL), D_MODEL ** -0.5 * DN_BETA),
        'ln2_g': 1.0 + nrm(ks[21], (DEPTH, D_MODEL), 0.02),
        'ln2_b': nrm(ks[22], (DEPTH, D_MODEL), 0.02),
        'ffn2_w13': nrm(ks[23], (DEPTH, D_MODEL, 2 * D_FF), D_MODEL ** -0.5),
        'ffn2_w2': nrm(ks[24], (DEPTH, D_FF, D_MODEL), D_FF ** -0.5 * DN_BETA),
        'ln3_g': 1.0 + nrm(ks[25], (DEPTH, D_MODEL), 0.02),
        'ln3_b': nrm(ks[26], (DEPTH, D_MODEL), 0.02),
    }


def _fwd_reference(x, ffn1_w13, ffn1_w2, ln1_g, ln1_b, w_in, gate_b, pool_w, pool_b, pool_scale, conv_w, conv_b,
              dt_bias, a_log, d_skip, ssd_norm, rel_bias, p_pool, p_ssd, p_attn, w_out, ln2_g, ln2_b,
              ffn2_w13, ffn2_w2, ln3_g, ln3_b):
    for i in range(DEPTH):
        x = layer_norm(DN_ALPHA * x + FFN_RES * swiglu(x, ffn1_w13[i], ffn1_w2[i]), ln1_g[i], ln1_b[i])
        mix = hybrid_mixer(x, w_in[i], gate_b[i], pool_w[i], pool_b[i], pool_scale[i], conv_w[i], conv_b[i],
                           dt_bias[i], a_log[i], d_skip[i], ssd_norm[i], rel_bias, p_pool[i], p_ssd[i],
                           p_attn[i], w_out[i])
        x = layer_norm(DN_ALPHA * x + mix, ln2_g[i], ln2_b[i])
        x = layer_norm(DN_ALPHA * x + FFN_RES * swiglu(x, ffn2_w13[i], ffn2_w2[i]), ln3_g[i], ln3_b[i])
    return x


import jax as _jax
import jax.numpy as _jnp

TWIN_FORMAT = 'train_step'
FWD_PARAMS = ['x', 'ffn1_w13', 'ffn1_w2', 'ln1_g', 'ln1_b', 'w_in', 'gate_b', 'pool_w', 'pool_b', 'pool_scale', 'conv_w', 'conv_b', 'dt_bias', 'a_log', 'd_skip', 'ssd_norm', 'rel_bias', 'p_pool', 'p_ssd', 'p_attn', 'w_out', 'ln2_g', 'ln2_b', 'ffn2_w13', 'ffn2_w2', 'ln3_g', 'ln3_b']
TWIN_WEIGHTS = ['ffn1_w13', 'ffn1_w2', 'ln1_g', 'ln1_b', 'w_in', 'gate_b', 'pool_w', 'pool_b', 'pool_scale', 'conv_w', 'conv_b', 'dt_bias', 'a_log', 'd_skip', 'ssd_norm', 'rel_bias', 'p_pool', 'p_ssd', 'p_attn', 'w_out', 'ln2_g', 'ln2_b', 'ffn2_w13', 'ffn2_w2', 'ln3_g', 'ln3_b']
TWIN_DIFF_INPUT = 'x'
TWIN_INPUTS = ['x', 'ffn1_w13', 'ffn1_w2', 'ln1_g', 'ln1_b', 'w_in', 'gate_b', 'pool_w', 'pool_b', 'pool_scale', 'conv_w', 'conv_b', 'dt_bias', 'a_log', 'd_skip', 'ssd_norm', 'rel_bias', 'p_pool', 'p_ssd', 'p_attn', 'w_out', 'ln2_g', 'ln2_b', 'ffn2_w13', 'ffn2_w2', 'ln3_g', 'ln3_b', 'loss_target', 'm_ffn1_w13', 'm_ffn1_w2', 'm_ln1_g', 'm_ln1_b', 'm_w_in', 'm_gate_b', 'm_pool_w', 'm_pool_b', 'm_pool_scale', 'm_conv_w', 'm_conv_b', 'm_dt_bias', 'm_a_log', 'm_d_skip', 'm_ssd_norm', 'm_rel_bias', 'm_p_pool', 'm_p_ssd', 'm_p_attn', 'm_w_out', 'm_ln2_g', 'm_ln2_b', 'm_ffn2_w13', 'm_ffn2_w2', 'm_ln3_g', 'm_ln3_b', 'v_ffn1_w13', 'v_ffn1_w2', 'v_ln1_g', 'v_ln1_b', 'v_w_in', 'v_gate_b', 'v_pool_w', 'v_pool_b', 'v_pool_scale', 'v_conv_w', 'v_conv_b', 'v_dt_bias', 'v_a_log', 'v_d_skip', 'v_ssd_norm', 'v_rel_bias', 'v_p_pool', 'v_p_ssd', 'v_p_attn', 'v_w_out', 'v_ln2_g', 'v_ln2_b', 'v_ffn2_w13', 'v_ffn2_w2', 'v_ln3_g', 'v_ln3_b']
TWIN_OUTPUTS = ['loss', 'grad_x', 'grad_ffn1_w13', 'grad_ffn1_w2', 'grad_ln1_g', 'grad_ln1_b', 'grad_w_in', 'grad_gate_b', 'grad_pool_w', 'grad_pool_b', 'grad_pool_scale', 'grad_conv_w', 'grad_conv_b', 'grad_dt_bias', 'grad_a_log', 'grad_d_skip', 'grad_ssd_norm', 'grad_rel_bias', 'grad_p_pool', 'grad_p_ssd', 'grad_p_attn', 'grad_w_out', 'grad_ln2_g', 'grad_ln2_b', 'grad_ffn2_w13', 'grad_ffn2_w2', 'grad_ln3_g', 'grad_ln3_b', 'delta_ffn1_w13', 'delta_ffn1_w2', 'delta_ln1_g', 'delta_ln1_b', 'delta_w_in', 'delta_gate_b', 'delta_pool_w', 'delta_pool_b', 'delta_pool_scale', 'delta_conv_w', 'delta_conv_b', 'delta_dt_bias', 'delta_a_log', 'delta_d_skip', 'delta_ssd_norm', 'delta_rel_bias', 'delta_p_pool', 'delta_p_ssd', 'delta_p_attn', 'delta_w_out', 'delta_ln2_g', 'delta_ln2_b', 'delta_ffn2_w13', 'delta_ffn2_w2', 'delta_ln3_g', 'delta_ln3_b', 'new_m_ffn1_w13', 'new_m_ffn1_w2', 'new_m_ln1_g', 'new_m_ln1_b', 'new_m_w_in', 'new_m_gate_b', 'new_m_pool_w', 'new_m_pool_b', 'new_m_pool_scale', 'new_m_conv_w', 'new_m_conv_b', 'new_m_dt_bias', 'new_m_a_log', 'new_m_d_skip', 'new_m_ssd_norm', 'new_m_rel_bias', 'new_m_p_pool', 'new_m_p_ssd', 'new_m_p_attn', 'new_m_w_out', 'new_m_ln2_g', 'new_m_ln2_b', 'new_m_ffn2_w13', 'new_m_ffn2_w2', 'new_m_ln3_g', 'new_m_ln3_b', 'new_v_ffn1_w13', 'new_v_ffn1_w2', 'new_v_ln1_g', 'new_v_ln1_b', 'new_v_w_in', 'new_v_gate_b', 'new_v_pool_w', 'new_v_pool_b', 'new_v_pool_scale', 'new_v_conv_w', 'new_v_conv_b', 'new_v_dt_bias', 'new_v_a_log', 'new_v_d_skip', 'new_v_ssd_norm', 'new_v_rel_bias', 'new_v_p_pool', 'new_v_p_ssd', 'new_v_p_attn', 'new_v_w_out', 'new_v_ln2_g', 'new_v_ln2_b', 'new_v_ffn2_w13', 'new_v_ffn2_w2', 'new_v_ln3_g', 'new_v_ln3_b']
TWIN_LEAF_KINDS = {'loss': 'loss', 'grad_x': 'grad_x', 'grad_ffn1_w13': 'grad_w', 'grad_ffn1_w2': 'grad_w', 'grad_ln1_g': 'grad_w', 'grad_ln1_b': 'grad_w', 'grad_w_in': 'grad_w', 'grad_gate_b': 'grad_w', 'grad_pool_w': 'grad_w', 'grad_pool_b': 'grad_w', 'grad_pool_scale': 'grad_w', 'grad_conv_w': 'grad_w', 'grad_conv_b': 'grad_w', 'grad_dt_bias': 'grad_w', 'grad_a_log': 'grad_w', 'grad_d_skip': 'grad_w', 'grad_ssd_norm': 'grad_w', 'grad_rel_bias': 'grad_w', 'grad_p_pool': 'grad_w', 'grad_p_ssd': 'grad_w', 'grad_p_attn': 'grad_w', 'grad_w_out': 'grad_w', 'grad_ln2_g': 'grad_w', 'grad_ln2_b': 'grad_w', 'grad_ffn2_w13': 'grad_w', 'grad_ffn2_w2': 'grad_w', 'grad_ln3_g': 'grad_w', 'grad_ln3_b': 'grad_w', 'delta_ffn1_w13': 'delta_w', 'delta_ffn1_w2': 'delta_w', 'delta_ln1_g': 'delta_w', 'delta_ln1_b': 'delta_w', 'delta_w_in': 'delta_w', 'delta_gate_b': 'delta_w', 'delta_pool_w': 'delta_w', 'delta_pool_b': 'delta_w', 'delta_pool_scale': 'delta_w', 'delta_conv_w': 'delta_w', 'delta_conv_b': 'delta_w', 'delta_dt_bias': 'delta_w', 'delta_a_log': 'delta_w', 'delta_d_skip': 'delta_w', 'delta_ssd_norm': 'delta_w', 'delta_rel_bias': 'delta_w', 'delta_p_pool': 'delta_w', 'delta_p_ssd': 'delta_w', 'delta_p_attn': 'delta_w', 'delta_w_out': 'delta_w', 'delta_ln2_g': 'delta_w', 'delta_ln2_b': 'delta_w', 'delta_ffn2_w13': 'delta_w', 'delta_ffn2_w2': 'delta_w', 'delta_ln3_g': 'delta_w', 'delta_ln3_b': 'delta_w', 'new_m_ffn1_w13': 'new_m', 'new_m_ffn1_w2': 'new_m', 'new_m_ln1_g': 'new_m', 'new_m_ln1_b': 'new_m', 'new_m_w_in': 'new_m', 'new_m_gate_b': 'new_m', 'new_m_pool_w': 'new_m', 'new_m_pool_b': 'new_m', 'new_m_pool_scale': 'new_m', 'new_m_conv_w': 'new_m', 'new_m_conv_b': 'new_m', 'new_m_dt_bias': 'new_m', 'new_m_a_log': 'new_m', 'new_m_d_skip': 'new_m', 'new_m_ssd_norm': 'new_m', 'new_m_rel_bias': 'new_m', 'new_m_p_pool': 'new_m', 'new_m_p_ssd': 'new_m', 'new_m_p_attn': 'new_m', 'new_m_w_out': 'new_m', 'new_m_ln2_g': 'new_m', 'new_m_ln2_b': 'new_m', 'new_m_ffn2_w13': 'new_m', 'new_m_ffn2_w2': 'new_m', 'new_m_ln3_g': 'new_m', 'new_m_ln3_b': 'new_m', 'new_v_ffn1_w13': 'new_v', 'new_v_ffn1_w2': 'new_v', 'new_v_ln1_g': 'new_v', 'new_v_ln1_b': 'new_v', 'new_v_w_in': 'new_v', 'new_v_gate_b': 'new_v', 'new_v_pool_w': 'new_v', 'new_v_pool_b': 'new_v', 'new_v_pool_scale': 'new_v', 'new_v_conv_w': 'new_v', 'new_v_conv_b': 'new_v', 'new_v_dt_bias': 'new_v', 'new_v_a_log': 'new_v', 'new_v_d_skip': 'new_v', 'new_v_ssd_norm': 'new_v', 'new_v_rel_bias': 'new_v', 'new_v_p_pool': 'new_v', 'new_v_p_ssd': 'new_v', 'new_v_p_attn': 'new_v', 'new_v_w_out': 'new_v', 'new_v_ln2_g': 'new_v', 'new_v_ln2_b': 'new_v', 'new_v_ffn2_w13': 'new_v', 'new_v_ffn2_w2': 'new_v', 'new_v_ln3_g': 'new_v', 'new_v_ln3_b': 'new_v'}


def _forward(args):
    return _fwd_reference(*[args[k] for k in FWD_PARAMS])


def _output_shape():
    out = _jax.eval_shape(lambda: _forward(_fwd_setup_inputs(0)))
    return out.shape, out.dtype

N_MICROBATCH = 1
ADAM_LR = 0.001
ADAM_B1 = 0.9
ADAM_B2 = 0.999
ADAM_EPS = 1e-08
ADAM_WD = 0.01
ADAM_STEP = 10
PER_EXAMPLE_BATCH_AXIS = {'x': 0, 'loss_target': 0}
SHARED_INPUTS = []
_WEIGHT_DTYPES = {'ffn1_w13': _jnp.float32, 'ffn1_w2': _jnp.float32, 'ln1_g': _jnp.float32, 'ln1_b': _jnp.float32, 'w_in': _jnp.float32, 'gate_b': _jnp.float32, 'pool_w': _jnp.float32, 'pool_b': _jnp.float32, 'pool_scale': _jnp.float32, 'conv_w': _jnp.float32, 'conv_b': _jnp.float32, 'dt_bias': _jnp.float32, 'a_log': _jnp.float32, 'd_skip': _jnp.float32, 'ssd_norm': _jnp.float32, 'rel_bias': _jnp.float32, 'p_pool': _jnp.float32, 'p_ssd': _jnp.float32, 'p_attn': _jnp.float32, 'w_out': _jnp.float32, 'ln2_g': _jnp.float32, 'ln2_b': _jnp.float32, 'ffn2_w13': _jnp.float32, 'ffn2_w2': _jnp.float32, 'ln3_g': _jnp.float32, 'ln3_b': _jnp.float32}
MOMENT_SCALE = {'ffn1_w13': 8.405787e-03, 'ffn1_w2': 3.263708e-02, 'ln1_g': 9.420522e-01, 'ln1_b': 4.346470e-01, 'w_in': 1.464375e-02, 'gate_b': 7.300460e-03, 'pool_w': 2.462570e-02, 'pool_b': 6.679474e-02, 'pool_scale': 2.426917e-02, 'conv_w': 1.819264e-02, 'conv_b': 3.017046e-02, 'dt_bias': 3.128584e-02, 'a_log': 6.950629e-02, 'd_skip': 1.507418e-01, 'ssd_norm': 2.508257e-02, 'rel_bias': 1.305263e-02, 'p_pool': 2.128188e-02, 'p_ssd': 2.546453e-02, 'p_attn': 4.838244e-03, 'w_out': 7.890810e-02, 'ln2_g': 9.861120e-01, 'ln2_b': 4.417359e-01, 'ffn2_w13': 8.282770e-03, 'ffn2_w2': 3.217082e-02, 'ln3_g': 1.612320e+01, 'ln3_b': 1.221810e+00}


def _to_microbatches(a, axis):
    t = _jnp.moveaxis(a, axis, 0)
    t = t.reshape((N_MICROBATCH, t.shape[0] // N_MICROBATCH) + t.shape[1:])
    return _jnp.moveaxis(t, 1, axis + 1)


def setup_inputs(seed: int = 0) -> dict:
    inp = _fwd_setup_inputs(seed)
    key = _jax.random.fold_in(_jax.random.key(seed), 7919)
    shape, _ = _output_shape()
    out = dict(inp)
    out["loss_target"] = _jax.random.normal(_jax.random.fold_in(key, 0), shape, _jnp.float32)
    for i, name in enumerate(TWIN_WEIGHTS):
        w = inp[name].astype(_jnp.float32)
        if MOMENT_SCALE is None:
            s = _jnp.sqrt(_jnp.mean(_jnp.square(w)) + 1e-30)
        else:
            s = MOMENT_SCALE[name]
        km, kv = _jax.random.split(_jax.random.fold_in(key, i + 1))
        out[name] = w
        out["m_" + name] = s * _jax.random.normal(km, w.shape, _jnp.float32)
        out["v_" + name] = (s * s) * _jax.random.uniform(kv, w.shape, _jnp.float32, 0.5, 1.5)
    if N_MICROBATCH > 1:
        for name, axis in PER_EXAMPLE_BATCH_AXIS.items():
            out[name] = _to_microbatches(out[name], axis)
    return {'x': out['x'], 'ffn1_w13': out['ffn1_w13'], 'ffn1_w2': out['ffn1_w2'], 'ln1_g': out['ln1_g'], 'ln1_b': out['ln1_b'], 'w_in': out['w_in'], 'gate_b': out['gate_b'], 'pool_w': out['pool_w'], 'pool_b': out['pool_b'], 'pool_scale': out['pool_scale'], 'conv_w': out['conv_w'], 'conv_b': out['conv_b'], 'dt_bias': out['dt_bias'], 'a_log': out['a_log'], 'd_skip': out['d_skip'], 'ssd_norm': out['ssd_norm'], 'rel_bias': out['rel_bias'], 'p_pool': out['p_pool'], 'p_ssd': out['p_ssd'], 'p_attn': out['p_attn'], 'w_out': out['w_out'], 'ln2_g': out['ln2_g'], 'ln2_b': out['ln2_b'], 'ffn2_w13': out['ffn2_w13'], 'ffn2_w2': out['ffn2_w2'], 'ln3_g': out['ln3_g'], 'ln3_b': out['ln3_b'], 'loss_target': out['loss_target'], 'm_ffn1_w13': out['m_ffn1_w13'], 'm_ffn1_w2': out['m_ffn1_w2'], 'm_ln1_g': out['m_ln1_g'], 'm_ln1_b': out['m_ln1_b'], 'm_w_in': out['m_w_in'], 'm_gate_b': out['m_gate_b'], 'm_pool_w': out['m_pool_w'], 'm_pool_b': out['m_pool_b'], 'm_pool_scale': out['m_pool_scale'], 'm_conv_w': out['m_conv_w'], 'm_conv_b': out['m_conv_b'], 'm_dt_bias': out['m_dt_bias'], 'm_a_log': out['m_a_log'], 'm_d_skip': out['m_d_skip'], 'm_ssd_norm': out['m_ssd_norm'], 'm_rel_bias': out['m_rel_bias'], 'm_p_pool': out['m_p_pool'], 'm_p_ssd': out['m_p_ssd'], 'm_p_attn': out['m_p_attn'], 'm_w_out': out['m_w_out'], 'm_ln2_g': out['m_ln2_g'], 'm_ln2_b': out['m_ln2_b'], 'm_ffn2_w13': out['m_ffn2_w13'], 'm_ffn2_w2': out['m_ffn2_w2'], 'm_ln3_g': out['m_ln3_g'], 'm_ln3_b': out['m_ln3_b'], 'v_ffn1_w13': out['v_ffn1_w13'], 'v_ffn1_w2': out['v_ffn1_w2'], 'v_ln1_g': out['v_ln1_g'], 'v_ln1_b': out['v_ln1_b'], 'v_w_in': out['v_w_in'], 'v_gate_b': out['v_gate_b'], 'v_pool_w': out['v_pool_w'], 'v_pool_b': out['v_pool_b'], 'v_pool_scale': out['v_pool_scale'], 'v_conv_w': out['v_conv_w'], 'v_conv_b': out['v_conv_b'], 'v_dt_bias': out['v_dt_bias'], 'v_a_log': out['v_a_log'], 'v_d_skip': out['v_d_skip'], 'v_ssd_norm': out['v_ssd_norm'], 'v_rel_bias': out['v_rel_bias'], 'v_p_pool': out['v_p_pool'], 'v_p_ssd': out['v_p_ssd'], 'v_p_attn': out['v_p_attn'], 'v_w_out': out['v_w_out'], 'v_ln2_g': out['v_ln2_g'], 'v_ln2_b': out['v_ln2_b'], 'v_ffn2_w13': out['v_ffn2_w13'], 'v_ffn2_w2': out['v_ffn2_w2'], 'v_ln3_g': out['v_ln3_g'], 'v_ln3_b': out['v_ln3_b']}


def _loss(weights, diff, rest, loss_target):
    with _jax.named_scope("forward"):
        args = {**rest, TWIN_DIFF_INPUT: diff, **{k: w.astype(_WEIGHT_DTYPES[k]) for k, w in weights.items()}}
        y = _forward(args)
    with _jax.named_scope("loss_head"):
        err = _jnp.square(y.astype(_jnp.float32) - loss_target)
        return 0.5 * _jnp.sum(_jnp.mean(err, axis=-1)) if err.ndim else 0.5 * err


def _adamw(w, g, m, v):
    m = ADAM_B1 * m + (1.0 - ADAM_B1) * g
    v = ADAM_B2 * v + (1.0 - ADAM_B2) * _jnp.square(g)
    m_hat = m / (1.0 - ADAM_B1 ** ADAM_STEP)
    v_hat = v / (1.0 - ADAM_B2 ** ADAM_STEP)
    delta = -ADAM_LR * (m_hat / (_jnp.sqrt(v_hat) + ADAM_EPS) + ADAM_WD * w)
    return delta, m, v


def reference(x, ffn1_w13, ffn1_w2, ln1_g, ln1_b, w_in, gate_b, pool_w, pool_b, pool_scale, conv_w, conv_b, dt_bias, a_log, d_skip, ssd_norm, rel_bias, p_pool, p_ssd, p_attn, w_out, ln2_g, ln2_b, ffn2_w13, ffn2_w2, ln3_g, ln3_b, loss_target, m_ffn1_w13, m_ffn1_w2, m_ln1_g, m_ln1_b, m_w_in, m_gate_b, m_pool_w, m_pool_b, m_pool_scale, m_conv_w, m_conv_b, m_dt_bias, m_a_log, m_d_skip, m_ssd_norm, m_rel_bias, m_p_pool, m_p_ssd, m_p_attn, m_w_out, m_ln2_g, m_ln2_b, m_ffn2_w13, m_ffn2_w2, m_ln3_g, m_ln3_b, v_ffn1_w13, v_ffn1_w2, v_ln1_g, v_ln1_b, v_w_in, v_gate_b, v_pool_w, v_pool_b, v_pool_scale, v_conv_w, v_conv_b, v_dt_bias, v_a_log, v_d_skip, v_ssd_norm, v_rel_bias, v_p_pool, v_p_ssd, v_p_attn, v_w_out, v_ln2_g, v_ln2_b, v_ffn2_w13, v_ffn2_w2, v_ln3_g, v_ln3_b):
    given = dict(x=x, ffn1_w13=ffn1_w13, ffn1_w2=ffn1_w2, ln1_g=ln1_g, ln1_b=ln1_b, w_in=w_in, gate_b=gate_b, pool_w=pool_w, pool_b=pool_b, pool_scale=pool_scale, conv_w=conv_w, conv_b=conv_b, dt_bias=dt_bias, a_log=a_log, d_skip=d_skip, ssd_norm=ssd_norm, rel_bias=rel_bias, p_pool=p_pool, p_ssd=p_ssd, p_attn=p_attn, w_out=w_out, ln2_g=ln2_g, ln2_b=ln2_b, ffn2_w13=ffn2_w13, ffn2_w2=ffn2_w2, ln3_g=ln3_g, ln3_b=ln3_b, loss_target=loss_target, m_ffn1_w13=m_ffn1_w13, m_ffn1_w2=m_ffn1_w2, m_ln1_g=m_ln1_g, m_ln1_b=m_ln1_b, m_w_in=m_w_in, m_gate_b=m_gate_b, m_pool_w=m_pool_w, m_pool_b=m_pool_b, m_pool_scale=m_pool_scale, m_conv_w=m_conv_w, m_conv_b=m_conv_b, m_dt_bias=m_dt_bias, m_a_log=m_a_log, m_d_skip=m_d_skip, m_ssd_norm=m_ssd_norm, m_rel_bias=m_rel_bias, m_p_pool=m_p_pool, m_p_ssd=m_p_ssd, m_p_attn=m_p_attn, m_w_out=m_w_out, m_ln2_g=m_ln2_g, m_ln2_b=m_ln2_b, m_ffn2_w13=m_ffn2_w13, m_ffn2_w2=m_ffn2_w2, m_ln3_g=m_ln3_g, m_ln3_b=m_ln3_b, v_ffn1_w13=v_ffn1_w13, v_ffn1_w2=v_ffn1_w2, v_ln1_g=v_ln1_g, v_ln1_b=v_ln1_b, v_w_in=v_w_in, v_gate_b=v_gate_b, v_pool_w=v_pool_w, v_pool_b=v_pool_b, v_pool_scale=v_pool_scale, v_conv_w=v_conv_w, v_conv_b=v_conv_b, v_dt_bias=v_dt_bias, v_a_log=v_a_log, v_d_skip=v_d_skip, v_ssd_norm=v_ssd_norm, v_rel_bias=v_rel_bias, v_p_pool=v_p_pool, v_p_ssd=v_p_ssd, v_p_attn=v_p_attn, v_w_out=v_w_out, v_ln2_g=v_ln2_g, v_ln2_b=v_ln2_b, v_ffn2_w13=v_ffn2_w13, v_ffn2_w2=v_ffn2_w2, v_ln3_g=v_ln3_g, v_ln3_b=v_ln3_b)
    weights = {n: given[n] for n in TWIN_WEIGHTS}
    shared = {n: given[n] for n in SHARED_INPUTS}
    per_example = {n: given[n] for n in ['x']}
    grad_fn = _jax.value_and_grad(_loss, argnums=(0, 1))

    def one_microbatch(ex, loss_target):
        ex = dict(ex)
        diff = ex.pop(TWIN_DIFF_INPUT)
        return grad_fn(weights, diff, {**shared, **ex}, loss_target)

    if N_MICROBATCH == 1:
        loss, (grad_w, grad_x) = one_microbatch(per_example, given["loss_target"])
    else:
        def body(carry, xs):
            loss_sum, grad_sum = carry
            l_k, (gw_k, gx_k) = one_microbatch(xs[0], xs[1])
            with _jax.named_scope("update"):
                return (loss_sum + l_k, _jax.tree.map(_jnp.add, grad_sum, gw_k)), gx_k

        init = (_jnp.zeros((), _jnp.float32), _jax.tree.map(_jnp.zeros_like, weights))
        (loss, grad_w), grad_x = _jax.lax.scan(body, init, (per_example, given["loss_target"]))
    with _jax.named_scope("update"):
        delta_w, new_m, new_v = {}, {}, {}
        for n in TWIN_WEIGHTS:
            delta_w[n], new_m[n], new_v[n] = _adamw(weights[n], grad_w[n], given["m_" + n], given["v_" + n])
    return (loss, grad_x, *[grad_w[n] for n in TWIN_WEIGHTS], *[delta_w[n] for n in TWIN_WEIGHTS],
            *[new_m[n] for n in TWIN_WEIGHTS], *[new_v[n] for n in TWIN_WEIGHTS])
```

```python
import functools
import math

import numpy as np
import jax
import jax.numpy as jnp
from jax import lax
from jax.experimental import pallas as pl
from jax.experimental.pallas import tpu as pltpu

F32, BF16 = jnp.float32, jnp.bfloat16
HIGHEST = lax.Precision.HIGHEST

NDEV = 8
DEPTH = 4
D = 1024
SEQ = 2048
NB = 2
T = NB * SEQ
DFF = 2816
POOL_W = 768
POOL_WINDOWS = (2, 4, 8, 16)
POOL_GDIM = 192
SSD_HEADS = 16
CHUNK = 128
NCHUNK = SEQ // CHUNK
ATTN_CONFIGS = ((128, 1), (512, 4), (2048, 16))
ATTN_BLK = 128
REL_BUCKETS = 32
REL_MAX_DIST = 2048
LN_EPS = 1e-5
SSD_EPS = 1e-5
ALPHA = (2.0 * DEPTH) ** 0.25
FFN_RES = 0.5
NEG = -1e30

ADAM_LR, ADAM_B1, ADAM_B2, ADAM_EPS, ADAM_WD, ADAM_STEP = 0.001, 0.9, 0.999, 1e-08, 0.01, 10

Z0, XBC0, GATE0, Q0, K0, V0, U0, DT0, HC = 0, 1024, 3072, 6144, 6912, 7680, 8448, 9216, 9728
O_U, O_Z, O_XBC, O_DT, O_Q, O_K, O_V, O_G, O_END = 0, 768, 1792, 3840, 3856, 4624, 5392, 6160, 9232

VMEM_LIMIT = 56 * 1024 * 1024


def _cp(sem):
    return pltpu.CompilerParams(dimension_semantics=sem, vmem_limit_bytes=VMEM_LIMIT)


def _pick(dim, cands):
    for c in cands:
        if dim % c == 0:
            return c
    return dim


def _mm(a, b, *, ta=False, tb=False, out_dtype=F32, add=None, add_scale=1.0, name):
    (K, M) = a.shape if ta else a.shape[::-1]
    (N, K2) = b.shape if tb else b.shape[::-1]
    assert K == K2, (a.shape, b.shape, ta, tb)
    bm = _pick(M, (1024, 768, 512, 256))
    bn = _pick(N, (1024, 768, 512, 256))
    bk = K if K <= 1024 else _pick(K, (512, 256))
    nk = K // bk
    dn = (((0 if ta else 1,), (1 if tb else 0,)), ((), ()))

    def kern(*refs):
        a_ref, b_ref = refs[0], refs[1]
        add_ref = refs[2] if add is not None else None
        o_ref = refs[3] if add is not None else refs[2]
        p = lax.dot_general(a_ref[...].astype(BF16), b_ref[...].astype(BF16), dn, preferred_element_type=F32)

        def fin(acc):
            if add_ref is not None:
                acc = acc + add_scale * add_ref[...]
            o_ref[...] = acc.astype(out_dtype)

        if nk == 1:
            fin(p)
        else:
            acc_ref = refs[-1]
            k = pl.program_id(2)

            @pl.when(k == 0)
            def _():
                acc_ref[...] = p

            @pl.when(k > 0)
            def _():
                acc_ref[...] += p

            @pl.when(k == nk - 1)
            def _():
                fin(acc_ref[...])

    a_spec = pl.BlockSpec((bk, bm), lambda i, j, k: (k, i)) if ta else pl.BlockSpec((bm, bk), lambda i, j, k: (i, k))
    b_spec = pl.BlockSpec((bn, bk), lambda i, j, k: (j, k)) if tb else pl.BlockSpec((bk, bn), lambda i, j, k: (k, j))
    in_specs = [a_spec, b_spec]
    args = [a, b]
    if add is not None:
        in_specs.append(pl.BlockSpec((bm, bn), lambda i, j, k: (i, j)))
        args.append(add)
    return pl.pallas_call(
        kern, name=name, grid=(M // bm, N // bn, nk),
        in_specs=in_specs, out_specs=pl.BlockSpec((bm, bn), lambda i, j, k: (i, j)),
        out_shape=jax.ShapeDtypeStruct((M, N), out_dtype),
        scratch_shapes=[pltpu.VMEM((bm, bn), F32)] if nk > 1 else [],
        compiler_params=_cp(("parallel", "parallel", "arbitrary")),
    )(*args)


def _rowwise(fn, rows, pars, outs, accs, *, name, tile, groups=1):
    n_rows = rows[0][0].shape[0]
    nt = n_rows // tile
    n_in = len(rows) + len(pars)
    n_out = len(outs)

    def kern(*refs):
        res = fn(*[r[...] for r in refs[:n_in]])
        for o_ref, val in zip(refs[n_in:n_in + n_out], res[:n_out]):
            o_ref[...] = val.astype(o_ref.dtype)
        i = pl.program_id(1)
        for a_ref, val in zip(refs[n_in + n_out:], res[n_out:]):
            @pl.when(i == 0)
            def _(a_ref=a_ref, val=val):
                a_ref[...] = val

            @pl.when(i > 0)
            def _(a_ref=a_ref, val=val):
                a_ref[...] += val

    in_specs = [pl.BlockSpec((tile, w), lambda g, i, c0=c0: (i, c0 + g)) for (_, w, c0) in rows]
    for p, w in pars:
        if w is None:
            in_specs.append(pl.BlockSpec(p.shape, lambda g, i: (0, 0)))
        else:
            in_specs.append(pl.BlockSpec((p.shape[0], w), lambda g, i: (0, g)))
    out_specs = [pl.BlockSpec((tile, w), lambda g, i: (i, g)) for (_, w, _) in outs]
    out_specs += [pl.BlockSpec((1, w), lambda g, i: (0, g)) for (_, w) in accs]
    out_shape = [jax.ShapeDtypeStruct((n_rows, c), dt) for (c, _, dt) in outs]
    out_shape += [jax.ShapeDtypeStruct((1, c), F32) for (c, _) in accs]
    return pl.pallas_call(
        kern, name=name, grid=(groups, nt), in_specs=in_specs, out_specs=out_specs, out_shape=out_shape,
        compiler_params=_cp(("arbitrary", "arbitrary")),
    )(*[r[0] for r in rows], *[p[0] for p in pars])


def _colsum(v):
    return jnp.sum(v, axis=0, keepdims=True)


def _silu(v):
    return v * jax.nn.sigmoid(v)


def _ln_fn(x, f, g, b, *, scale):
    pre = ALPHA * x + scale * f
    mu = jnp.mean(pre, axis=-1, keepdims=True)
    var = jnp.mean(jnp.square(pre - mu), axis=-1, keepdims=True)
    return (pre - mu) * lax.rsqrt(var + LN_EPS) * g + b


def _ln_fwd(x, f, g, b, *, scale, name):
    fn = lambda x_, f_, g_, b_: [_ln_fn(x_, f_, g_, b_, scale=scale)]
    return _rowwise(fn, [(x, D, 0), (f, D, 0)], [(g, None), (b, None)], [(D, D, F32)], [], name=name, tile=512)[0]


def _ln_bwd(x, f, dy, g, b, *, scale, name):
    def fn(x_, f_, dy_, g_, b_):
        _, vjp = jax.vjp(functools.partial(_ln_fn, scale=scale), x_, f_, g_, b_)
        return list(vjp(dy_))

    return _rowwise(fn, [(x, D, 0), (f, D, 0), (dy, D, 0)], [(g, None), (b, None)],
                    [(D, D, F32), (D, D, BF16)], [(D, D), (D, D)], name=name, tile=512)


def _swiglu_fwd(h, *, name):
    fn = lambda a, g: [_silu(a) * g, jnp.concatenate([a, g], axis=1)]
    return _rowwise(fn, [(h, DFF, 0), (h, DFF, 1)], [], [(DFF, DFF, BF16), (2 * DFF, 2 * DFF, BF16)], [], name=name, tile=256)


def _swiglu_bwd(h, dact, *, name):
    def fn(a, g, da_):
        a, g = a.astype(F32), g.astype(F32)
        s = jax.nn.sigmoid(a)
        return [jnp.concatenate([da_ * g * (s * (1.0 + a * (1.0 - s))), da_ * (a * s)], axis=1)]

    return _rowwise(fn, [(h, DFF, 0), (h, DFF, 1), (dact, DFF, 0)], [], [(2 * DFF, 2 * DFF, BF16)], [], name=name, tile=256)[0]


def _affine_fn(y, b, s):
    return (y + b) * s


def _affine_fwd(y, b, s, *, name):
    return _rowwise(lambda y_, b_, s_: [_affine_fn(y_, b_, s_)], [(y, POOL_W, 0)], [(b, None), (s, None)],
                    [(POOL_W, POOL_W, BF16)], [], name=name, tile=512)[0]


def _affine_bwd(y, dyo, b, s, *, name):
    def fn(y_, d_, b_, s_):
        _, vjp = jax.vjp(_affine_fn, y_, b_, s_)
        return list(vjp(d_))

    return _rowwise(fn, [(y, POOL_W, 0), (dyo, POOL_W, 0)], [(b, None), (s, None)],
                    [(POOL_W, POOL_W, BF16)], [(POOL_W, POOL_W), (POOL_W, POOL_W)], name=name, tile=512)


def _gnorm_fn(y, z, w):
    yz = y * _silu(z)
    return yz * lax.rsqrt(jnp.mean(jnp.square(yz), axis=-1, keepdims=True) + SSD_EPS) * w


def _gnorm_fwd(y, hcat, w, *, name):
    return _rowwise(lambda y_, z_, w_: [_gnorm_fn(y_, z_, w_)], [(y, 256, 0), (hcat, 256, Z0 // 256)], [(w, 256)],
                    [(D, 256, BF16)], [], name=name, tile=512, groups=4)[0]


def _gnorm_bwd(y, hcat, dyo, w, *, name):
    def fn(y_, z_, d_, w_):
        _, vjp = jax.vjp(_gnorm_fn, y_, z_, w_)
        return list(vjp(d_))

    return _rowwise(fn, [(y, 256, 0), (hcat, 256, Z0 // 256), (dyo, 256, 0)], [(w, 256)],
                    [(D, 256, F32), (D, 256, BF16)], [(D, 256)], name=name, tile=512, groups=4)


def _merge_fn(g0, g1, g2, ya, yb, yc, b0, b1, b2):
    return jax.nn.sigmoid(g0 + b0) * ya + jax.nn.sigmoid(g1 + b1) * yb + jax.nn.sigmoid(g2 + b2) * yc


def _merge_rows(hcat, ya, yb, yc):
    c = GATE0 // 256
    return [(hcat, 256, c), (hcat, 256, c + 4), (hcat, 256, c + 8), (ya, 256, 0), (yb, 256, 0), (yc, 256, 0)]


def _merge_fwd(hcat, ya, yb, yc, gb, *, name):
    pars = [(gb[0:1], 256), (gb[1:2], 256), (gb[2:3], 256)]
    return _rowwise(lambda *v: [_merge_fn(*v)], _merge_rows(hcat, ya, yb, yc), pars, [(D, 256, BF16)], [],
                    name=name, tile=512, groups=4)[0]


def _merge_bwd(hcat, ya, yb, yc, dm, gb, *, name):
    def fn(g0, g1, g2, ya_, yb_, yc_, dm_, b0, b1, b2):
        _, vjp = jax.vjp(_merge_fn, g0, g1, g2, ya_, yb_, yc_, b0, b1, b2)
        return list(vjp(dm_))

    pars = [(gb[0:1], 256), (gb[1:2], 256), (gb[2:3], 256)]
    return _rowwise(fn, _merge_rows(hcat, ya, yb, yc) + [(dm, 256, 0)], pars,
                    [(D, 256, BF16)] * 6, [(D, 256)] * 3, name=name, tile=512, groups=4)


def _amerge_fn(o0, o1, o2, l0, l1, l2):
    m = jnp.maximum(jnp.maximum(l0, l1), l2)
    e0, e1, e2 = jnp.exp(l0 - m), jnp.exp(l1 - m), jnp.exp(l2 - m)
    return (e0 * o0 + e1 * o1 + e2 * o2) / (e0 + e1 + e2)


def _amerge_fwd(os_, ls_, *, name):
    rows = [(v, 256, 0) for v in (*os_, *ls_)]
    return _rowwise(lambda *v: [_amerge_fn(*v)], rows, [], [(256, 256, BF16)], [], name=name, tile=1024)[0]


def _amerge_bwd(os_, ls_, dy, *, name):
    def fn(*v):
        _, vjp = jax.vjp(_amerge_fn, *v[:6])
        return list(vjp(v[6]))

    rows = [(v, 256, 0) for v in (*os_, *ls_, dy)]
    return _rowwise(fn, rows, [], [(256, 256, F32)] * 6, [], name=name, tile=1024)


def _loss_kernel(y, tgt, *, name):
    def fn(y_, t_):
        e = y_ - t_
        return [e * (1.0 / D), _colsum(e * e)]

    return _rowwise(fn, [(y, D, 0), (tgt, D, 0)], [], [(D, D, F32)], [(D, D)], name=name, tile=512)


def _adamw(w, g, m, v, *, name):
    r, c = w.shape
    tile = _pick(r, (512, 256, 128, 64, 32, 16, 8))

    def fn(w_, g_, m_, v_):
        m2 = ADAM_B1 * m_ + (1.0 - ADAM_B1) * g_
        v2 = ADAM_B2 * v_ + (1.0 - ADAM_B2) * jnp.square(g_)
        m_hat = m2 / (1.0 - ADAM_B1 ** ADAM_STEP)
        v_hat = v2 / (1.0 - ADAM_B2 ** ADAM_STEP)
        return [-ADAM_LR * (m_hat / (jnp.sqrt(v_hat) + ADAM_EPS) + ADAM_WD * w_), m2, v2]

    return _rowwise(fn, [(w, c, 0), (g, c, 0), (m, c, 0), (v, c, 0)], [], [(c, c, F32)] * 3, [], name=name, tile=tile)


def _pool_lane_window(j, width):
    lane = lax.broadcasted_iota(jnp.int32, (1, width), 1) + j * width
    grp = lane // POOL_GDIM
    return grp


def _pool_select(grp, vals):
    out = vals[3]
    for gi in (2, 1, 0):
        out = jnp.where(grp == gi, vals[gi], out)
    return out


def _pool_fwd(hcat, *, name):
    width = 256

    def kern(u_ref, o_ref):
        u = u_ref[...]
        t = lax.broadcasted_iota(jnp.int32, (SEQ, 1), 0)
        grp = _pool_lane_window(pl.program_id(1), width)

        def shift(v, k):
            return jnp.where(t >= k, pltpu.roll(v, k, 0), 0.0)

        s2 = u + shift(u, 1)
        s4 = s2 + shift(s2, 2)
        s8 = s4 + shift(s4, 4)
        s16 = s8 + shift(s8, 8)
        win = _pool_select(grp, [jnp.full((1, width), float(w), F32) for w in POOL_WINDOWS])
        cnt = jnp.minimum((t + 1).astype(F32), win)
        o_ref[...] = _pool_select(grp, [s2, s4, s8, s16]) / cnt - u

    return pl.pallas_call(
        kern, name=name, grid=(NB, POOL_W // width),
        in_specs=[pl.BlockSpec((SEQ, width), lambda b, j: (b, U0 // width + j))],
        out_specs=pl.BlockSpec((SEQ, width), lambda b, j: (b, j)),
        out_shape=jax.ShapeDtypeStruct((T, POOL_W), F32), compiler_params=_cp(("parallel", "parallel")),
    )(hcat)


def _pool_bwd(dp, *, name):
    width = 256

    def kern(d_ref, o_ref):
        d = d_ref[...]
        t = lax.broadcasted_iota(jnp.int32, (SEQ, 1), 0)
        grp = _pool_lane_window(pl.program_id(1), width)
        win = _pool_select(grp, [jnp.full((1, width), float(w), F32) for w in POOL_WINDOWS])
        dm = d / jnp.minimum((t + 1).astype(F32), win)

        def shift(v, k):
            return jnp.where(t < SEQ - k, pltpu.roll(v, SEQ - k, 0), 0.0)

        r2 = dm + shift(dm, 1)
        r4 = r2 + shift(r2, 2)
        r8 = r4 + shift(r4, 4)
        r16 = r8 + shift(r8, 8)
        o_ref[...] = (_pool_select(grp, [r2, r4, r8, r16]) - d).astype(o_ref.dtype)

    return pl.pallas_call(
        kern, name=name, grid=(NB, POOL_W // width),
        in_specs=[pl.BlockSpec((SEQ, width), lambda b, j: (b, j))],
        out_specs=pl.BlockSpec((SEQ, width), lambda b, j: (b, j)),
        out_shape=jax.ShapeDtypeStruct((T, POOL_W), BF16), compiler_params=_cp(("parallel", "parallel")),
    )(dp)


CONV_W = 512


def _conv_pre(x, w, b, t):
    pre = w[3:4] * x + b
    for k in (1, 2, 3):
        pre = pre + w[3 - k:4 - k] * jnp.where(t >= k, pltpu.roll(x, k, 0), 0.0)
    return pre


def _conv_fwd(hcat, w, b, *, name):
    def kern(x_ref, w_ref, b_ref, o_ref):
        t = lax.broadcasted_iota(jnp.int32, (SEQ, 1), 0)
        o_ref[...] = _silu(_conv_pre(x_ref[...], w_ref[...], b_ref[...], t))

    return pl.pallas_call(
        kern, name=name, grid=(NB, 2048 // CONV_W),
        in_specs=[pl.BlockSpec((SEQ, CONV_W), lambda s, j: (s, XBC0 // CONV_W + j)),
                  pl.BlockSpec((4, CONV_W), lambda s, j: (0, j)), pl.BlockSpec((1, CONV_W), lambda s, j: (0, j))],
        out_specs=pl.BlockSpec((SEQ, CONV_W), lambda s, j: (s, j)),
        out_shape=jax.ShapeDtypeStruct((T, 2048), F32), compiler_params=_cp(("parallel", "parallel")),
    )(hcat, w, b)


def _conv_bwd(hcat, w, b, dy, *, name):
    def kern(x_ref, w_ref, b_ref, dy_ref, dx_ref, dw_ref, db_ref):
        t = lax.broadcasted_iota(jnp.int32, (SEQ, 1), 0)
        x, w_ = x_ref[...], w_ref[...]
        pre = _conv_pre(x, w_, b_ref[...], t)
        s = jax.nn.sigmoid(pre)
        dpre = dy_ref[...] * (s * (1.0 + pre * (1.0 - s)))
        dx = w_[3:4] * dpre
        dws = [None] * 4
        dws[3] = _colsum(dpre * x)
        for k in (1, 2, 3):
            dx = dx + w_[3 - k:4 - k] * jnp.where(t < SEQ - k, pltpu.roll(dpre, SEQ - k, 0), 0.0)
            dws[3 - k] = _colsum(dpre * jnp.where(t >= k, pltpu.roll(x, k, 0), 0.0))
        dx_ref[...] = dx.astype(dx_ref.dtype)
        db = _colsum(dpre)
        first = pl.program_id(1) == 0

        @pl.when(first)
        def _():
            for k in range(4):
                dw_ref[k:k + 1, :] = dws[k]
            db_ref[...] = db

        @pl.when(jnp.logical_not(first))
        def _():
            for k in range(4):
                dw_ref[k:k + 1, :] += dws[k]
            db_ref[...] += db

    return pl.pallas_call(
        kern, name=name, grid=(2048 // CONV_W, NB),
        in_specs=[pl.BlockSpec((SEQ, CONV_W), lambda j, s: (s, XBC0 // CONV_W + j)),
                  pl.BlockSpec((4, CONV_W), lambda j, s: (0, j)), pl.BlockSpec((1, CONV_W), lambda j, s: (0, j)),
                  pl.BlockSpec((SEQ, CONV_W), lambda j, s: (s, j))],
        out_specs=[pl.BlockSpec((SEQ, CONV_W), lambda j, s: (s, j)), pl.BlockSpec((4, CONV_W), lambda j, s: (0, j)),
                   pl.BlockSpec((1, CONV_W), lambda j, s: (0, j))],
        out_shape=[jax.ShapeDtypeStruct((T, 2048), BF16), jax.ShapeDtypeStruct((4, 2048), F32),
                   jax.ShapeDtypeStruct((1, 2048), F32)],
        compiler_params=_cp(("arbitrary", "arbitrary")),
    )(hcat, w, b, dy)


def _softplus(v):
    return jnp.maximum(v, 0.0) + jnp.log1p(jnp.exp(-jnp.abs(v)))


def _dot(a, b, dims):
    return lax.dot_general(a, b, (dims, ((), ())), preferred_element_type=F32)


_NN, _NT, _TN = ((1,), (0,)), ((1,), (1,)), ((0,), (0,))


def _ssd_specs(order):
    def spec(width, col0):
        return pl.BlockSpec((SEQ, width), lambda i, j, c0=col0: (order(i, j)[0], c0 + order(i, j)[1]))

    def par():
        return pl.BlockSpec((1, 256), lambda i, j: (0, order(i, j)[1]))

    return spec, par


def _ssd_chunk_common(c, xs_ref, b_ref, c_ref, dtx_ref, dtb, a, trif):
    r0 = pl.multiple_of(c * CHUNK, CHUNK)
    rows = pl.ds(r0, CHUNK)
    x = xs_ref[rows, :]
    bb = b_ref[rows, :].astype(BF16)
    cb = c_ref[rows, :].astype(BF16)
    raw = dtx_ref[rows, :] + dtb
    dt = _softplus(raw)
    cs = jnp.dot(trif, dt * a, precision=HIGHEST, preferred_element_type=F32)
    return rows, x, bb, cb, raw, dt, cs


def _head_decay(cs, me, tri):
    cse = jnp.max(jnp.where(me, cs, -jnp.inf), axis=1, keepdims=True)
    csb = jnp.broadcast_to(cse, (CHUNK, CHUNK))
    return jnp.where(tri, jnp.exp(csb - csb.T), 0.0)


def _ssd_fwd(xa, dtx, dtb, alog, dsk, *, name):
    spec, par = _ssd_specs(lambda g, b: (b, g))

    def kern(xs_ref, b_ref, c_ref, dtx_ref, dtb_ref, alog_ref, dsk_ref, y_ref, hs_ref, h_scr):
        row = lax.broadcasted_iota(jnp.int32, (CHUNK, CHUNK), 0)
        col = lax.broadcasted_iota(jnp.int32, (CHUNK, CHUNK), 1)
        tri = row >= col
        trif = tri.astype(F32)
        head = lax.broadcasted_iota(jnp.int32, (1, 256), 1) // 64
        dtb_, a, dsk_ = dtb_ref[...], -jnp.exp(alog_ref[...]), dsk_ref[...]
        h_scr[...] = jnp.zeros_like(h_scr)

        def chunk(c, carry):
            rows, x, bb, cb, _, dt, cs = _ssd_chunk_common(c, xs_ref, b_ref, c_ref, dtx_ref, dtb_, a, trif)
            cs_last = cs[CHUNK - 1:CHUNK, :]
            xdt = x * dt
            xdtb = xdt.astype(BF16)
            g = _dot(cb, bb, _NT)
            hin = h_scr[...]
            hs_ref[rows, :] = hin
            y = jnp.exp(cs) * _dot(cb, hin.astype(BF16), _NN) + dsk_ * x
            for e in range(4):
                me = head == e
                m = (g * _head_decay(cs, me, tri)).astype(BF16)
                y = y + jnp.where(me, _dot(m, xdtb, _NN), 0.0)
            y_ref[rows, :] = y
            st = _dot(bb, (jnp.exp(cs_last - cs) * xdt).astype(BF16), _TN)
            h_scr[...] = hin * jnp.exp(cs_last) + st
            return carry

        lax.fori_loop(0, NCHUNK, chunk, 0)

    return pl.pallas_call(
        kern, name=name, grid=(4, NB),
        in_specs=[spec(256, 0), spec(128, 8), spec(128, 12), spec(256, 0), par(), par(), par()],
        out_specs=[spec(256, 0), spec(256, 0)],
        out_shape=[jax.ShapeDtypeStruct((T, D), F32), jax.ShapeDtypeStruct((T, D), F32)],
        scratch_shapes=[pltpu.VMEM((CHUNK, 256), F32)],
        compiler_params=_cp(("parallel", "parallel")),
    )(xa, xa, xa, dtx, dtb, alog, dsk)


def _ssd_bwd(xa, dtx, dtb, alog, dsk, hs, dy, *, name):
    spec, par = _ssd_specs(lambda g, b: (b, g))

    def kern(xs_ref, b_ref, c_ref, dtx_ref, dtb_ref, alog_ref, dsk_ref, hs_ref, dy_ref,
             dx_ref, db_ref, dc_ref, ddt_ref, ddtb_ref, dalog_ref, ddsk_ref, dh_scr):
        row = lax.broadcasted_iota(jnp.int32, (CHUNK, CHUNK), 0)
        col = lax.broadcasted_iota(jnp.int32, (CHUNK, CHUNK), 1)
        tri = row >= col
        trif = tri.astype(F32)
        trit = (row <= col).astype(F32)
        is_last = lax.broadcasted_iota(jnp.int32, (CHUNK, 1), 0) == CHUNK - 1
        head = lax.broadcasted_iota(jnp.int32, (1, 256), 1) // 64
        dtb_, a, dsk_ = dtb_ref[...], -jnp.exp(alog_ref[...]), dsk_ref[...]
        dh_scr[...] = jnp.zeros_like(dh_scr)

        @pl.when(pl.program_id(1) == 0)
        def _():
            ddtb_ref[...] = jnp.zeros_like(ddtb_ref)
            dalog_ref[...] = jnp.zeros_like(dalog_ref)
            ddsk_ref[...] = jnp.zeros_like(ddsk_ref)

        def hsum(v, me):
            return jnp.sum(jnp.where(me, v, 0.0), axis=1, keepdims=True)

        def chunk(ci, carry):
            c = NCHUNK - 1 - ci
            rows, x, bb, cb, raw, dt, cs = _ssd_chunk_common(c, xs_ref, b_ref, c_ref, dtx_ref, dtb_, a, trif)
            cs_last = cs[CHUNK - 1:CHUNK, :]
            ecs = jnp.exp(cs)
            dsx = jnp.exp(cs_last - cs)
            xdt = x * dt
            xdtb = xdt.astype(BF16)
            g = _dot(cb, bb, _NT)
            hin = hs_ref[rows, :]
            hinb = hin.astype(BF16)
            dy_ = dy_ref[rows, :]
            dyb = dy_.astype(BF16)
            dh = dh_scr[...]
            dhb = dh.astype(BF16)
            bdh = _dot(bb, dhb, _NN)
            yoff = ecs * _dot(cb, hinb, _NN)
            dxdt = dsx * bdh
            t1, t2, t3 = dy_ * yoff, xdt * bdh, dh * hin
            dg = jnp.zeros((CHUNK, CHUNK), F32)
            dcs = jnp.zeros((CHUNK, 256), F32)
            for e in range(4):
                me = head == e
                l_ = _head_decay(cs, me, tri)
                m = g * l_
                dxdt = dxdt + jnp.where(me, _dot(m.astype(BF16), dyb, _TN), 0.0)
                dm = _dot(jnp.where(me, dy_, 0.0).astype(BF16), xdtb, _NT)
                dg = dg + dm * l_
                w = dm * m
                dds = hsum(t2, me)
                dse = jnp.max(jnp.where(me, dsx, -jnp.inf), axis=1, keepdims=True)
                ecl = jnp.max(jnp.where(me, jnp.exp(cs_last), -jnp.inf), axis=1, keepdims=True)
                dcs_e = hsum(t1, me) + jnp.sum(w, axis=1, keepdims=True) - jnp.sum(w.T, axis=1, keepdims=True) - dds * dse
                last = jnp.sum(dds * dse, axis=0, keepdims=True) + ecl * jnp.sum(hsum(t3, me), axis=0, keepdims=True)
                dcs_e = dcs_e + jnp.where(is_last, last, 0.0)
                dcs = dcs + jnp.where(me, dcs_e, 0.0)
            dadt = jnp.dot(trit, dcs, precision=HIGHEST, preferred_element_type=F32)
            ddt = a * dadt
            dxx = dxdt * x
            dyx = dy_ * x
            dsk_acc = jnp.zeros((1, 256), F32)
            for e in range(4):
                me = head == e
                ddt = ddt + jnp.where(me, hsum(dxx, me), 0.0)
                dsk_acc = dsk_acc + jnp.where(me, jnp.sum(hsum(dyx, me), axis=0, keepdims=True), 0.0)
            draw = ddt * jax.nn.sigmoid(raw)
            ddt_ref[rows, :] = draw
            ddtb_ref[...] += _colsum(draw)
            dalog_ref[...] += _colsum(dadt * dt) * a
            ddsk_ref[...] += dsk_acc
            dx_ref[rows, :] = dxdt * dt + dsk_ * dy_
            edy = (ecs * dy_).astype(BF16)
            dgb = dg.astype(BF16)
            dc_ref[rows, :] = _dot(dgb, bb, _NN) + _dot(edy, hinb, _NT)
            db_ref[rows, :] = _dot(dgb, cb, _TN) + _dot((dsx * xdt).astype(BF16), dhb, _NT)
            dh_scr[...] = jnp.exp(cs_last) * dh + _dot(cb, edy, _TN)
            return carry

        lax.fori_loop(0, NCHUNK, chunk, 0)

    return pl.pallas_call(
        kern, name=name, grid=(4, NB),
        in_specs=[spec(256, 0), spec(128, 8), spec(128, 12), spec(256, 0), par(), par(), par(), spec(256, 0), spec(256, 0)],
        out_specs=[spec(256, 0), spec(128, 0), spec(128, 0), spec(256, 0), par(), par(), par()],
        out_shape=[jax.ShapeDtypeStruct((T, D), F32), jax.ShapeDtypeStruct((T, 512), F32), jax.ShapeDtypeStruct((T, 512), F32),
                   jax.ShapeDtypeStruct((T, D), F32), jax.ShapeDtypeStruct((1, D), F32), jax.ShapeDtypeStruct((1, D), F32),
                   jax.ShapeDtypeStruct((1, D), F32)],
        scratch_shapes=[pltpu.VMEM((CHUNK, 256), F32)],
        compiler_params=_cp(("arbitrary", "arbitrary")),
    )(xa, xa, xa, dtx, dtb, alog, dsk, hs, dy)


def _t5_bucket_np(dist):
    dist = np.maximum(dist, 0)
    max_exact = REL_BUCKETS // 2
    large = max_exact + (np.log(np.maximum(dist, 1) / max_exact) / np.log(REL_MAX_DIST / max_exact)
                         * (REL_BUCKETS - max_exact)).astype(np.int32)
    large = np.minimum(large, REL_BUCKETS - 1)
    return np.where(dist < max_exact, dist, large).astype(np.int32)


def _attn_bias(rel_bias):
    qi = np.arange(ATTN_BLK)[:, None]
    kj = np.arange(2 * ATTN_BLK)[None, :]
    delta = qi - kj + ATTN_BLK
    out = []
    for gi, (window, dil) in enumerate(ATTN_CONFIGS):
        in_band = (delta >= 0) & (delta <= window // dil)
        bucket = jnp.asarray(_t5_bucket_np(delta * dil).reshape(-1, 1))
        one_hot = (bucket == jnp.arange(REL_BUCKETS)[None, :]).astype(F32)
        tab = jnp.dot(one_hot, rel_bias[:, 4 * gi:4 * gi + 4], precision=HIGHEST).reshape(ATTN_BLK, 2 * ATTN_BLK, 4)
        out.append(jnp.where(jnp.asarray(in_band)[None], tab.transpose(2, 0, 1), NEG))
    return jnp.stack(out)


def _attn_logits(q_ref, k_ref, bias_ref, n, cur, prev):
    qh = q_ref[cur, :].astype(BF16)
    kband = jnp.concatenate([k_ref[prev, :], k_ref[cur, :]], axis=0).astype(BF16)
    kj = lax.broadcasted_iota(jnp.int32, (1, 2 * ATTN_BLK), 1)
    ok = jnp.logical_or(n > 0, kj >= ATTN_BLK)
    return jnp.where(ok, _dot(qh, kband, _NT) * 0.125 + bias_ref[...], NEG), kband


def _attn_specs(L):
    seq = pl.BlockSpec((None, None, L, 64), lambda h, s: (s, h, 0, 0))
    tab = pl.BlockSpec((None, ATTN_BLK, 2 * ATTN_BLK), lambda h, s: (h, 0, 0))
    return seq, tab


def _attn_rows(n):
    cur = pl.ds(pl.multiple_of(n * ATTN_BLK, ATTN_BLK), ATTN_BLK)
    prev = pl.ds(pl.multiple_of(jnp.maximum(n - 1, 0) * ATTN_BLK, ATTN_BLK), ATTN_BLK)
    return cur, prev


def _attn_fwd(q, k, v, bias, *, name):
    nsub, _, L, _ = q.shape
    nblk = L // ATTN_BLK

    def kern(q_ref, k_ref, v_ref, bias_ref, o_ref, l_ref):
        def blk(n, carry):
            cur, prev = _attn_rows(n)
            logits, _ = _attn_logits(q_ref, k_ref, bias_ref, n, cur, prev)
            vband = jnp.concatenate([v_ref[prev, :], v_ref[cur, :]], axis=0).astype(BF16)
            m = jnp.max(logits, axis=-1, keepdims=True)
            p = jnp.exp(logits - m)
            den = jnp.sum(p, axis=-1, keepdims=True)
            o_ref[cur, :] = _dot((p / den).astype(BF16), vband, _NN)
            l_ref[cur, :] = jnp.broadcast_to(m + jnp.log(den), (ATTN_BLK, 64))
            return carry

        lax.fori_loop(0, nblk, blk, 0)

    seq, tab = _attn_specs(L)
    return pl.pallas_call(
        kern, name=name, grid=(4, nsub), in_specs=[seq, seq, seq, tab],
        out_specs=[seq, seq], out_shape=[jax.ShapeDtypeStruct(q.shape, F32)] * 2,
        compiler_params=_cp(("parallel", "parallel")),
    )(q, k, v, bias)


def _attn_bwd(q, k, v, o, lse, do, dl, bias, *, name):
    nsub, _, L, _ = q.shape
    nblk = L // ATTN_BLK

    def kern(q_ref, k_ref, v_ref, o_ref, l_ref, do_ref, dl_ref, bias_ref, dq_ref, dk_ref, dv_ref, dbias_ref):
        @pl.when(pl.program_id(1) == 0)
        def _():
            dbias_ref[...] = jnp.zeros_like(dbias_ref)

        dk_ref[...] = jnp.zeros_like(dk_ref)
        dv_ref[...] = jnp.zeros_like(dv_ref)

        def blk(n, carry):
            cur, prev = _attn_rows(n)
            logits, kband = _attn_logits(q_ref, k_ref, bias_ref, n, cur, prev)
            vband = jnp.concatenate([v_ref[prev, :], v_ref[cur, :]], axis=0).astype(BF16)
            p = jnp.exp(logits - l_ref[cur, 0:1])
            doh = do_ref[cur, :]
            dob = doh.astype(BF16)
            dp = _dot(dob, vband, _NT)
            dd = jnp.sum(doh * o_ref[cur, :], axis=-1, keepdims=True)
            dls = jnp.sum(dl_ref[cur, :], axis=-1, keepdims=True)
            ds = p * (dp - dd + dls)
            dbias_ref[...] += ds
            dsb = (ds * 0.125).astype(BF16)
            dq_ref[cur, :] = _dot(dsb, kband, _NN)
            dkb = _dot(dsb, q_ref[cur, :].astype(BF16), _TN)
            dvb = _dot(p.astype(BF16), dob, _TN)
            dk_ref[prev, :] += dkb[:ATTN_BLK]
            dk_ref[cur, :] += dkb[ATTN_BLK:]
            dv_ref[prev, :] += dvb[:ATTN_BLK]
            dv_ref[cur, :] += dvb[ATTN_BLK:]
            return carry

        lax.fori_loop(0, nblk, blk, 0)

    seq, tab = _attn_specs(L)
    return pl.pallas_call(
        kern, name=name, grid=(4, nsub),
        in_specs=[seq] * 7 + [tab], out_specs=[seq, seq, seq, tab],
        out_shape=[jax.ShapeDtypeStruct(q.shape, F32)] * 3 + [jax.ShapeDtypeStruct((4, ATTN_BLK, 2 * ATTN_BLK), F32)],
        compiler_params=_cp(("arbitrary", "arbitrary")),
    )(q, k, v, o, lse, do, dl, bias)


def _to_sub(t, dil):
    t = t.reshape(NB, SEQ // dil, dil, 4, 64).transpose(0, 2, 3, 1, 4)
    return t.reshape(NB * dil, 4, SEQ // dil, 64)


def _from_sub(t, dil):
    t = t.reshape(NB, dil, 4, SEQ // dil, 64).transpose(0, 3, 1, 2, 4)
    return t.reshape(T, 256)


_ANY = pl.BlockSpec(memory_space=pl.ANY)
_FLIPS = [(0, 0, 1), (1, 0, 0), (0, 1, 0), (1, 1, 0), (1, 0, 1), (0, 1, 1), (1, 1, 1)]


def _me_and_peers():
    x, y, c = lax.axis_index("x"), lax.axis_index("y"), lax.axis_index("c")
    peers = [(1 - x if fx else x, 1 - y if fy else y, 1 - c if fc else c) for fx, fy, fc in _FLIPS]
    return (x, y, c), peers


def _slot(dev):
    return 4 * dev[0] + 2 * dev[1] + dev[2]


def _all_gather(shard, *, name):
    def kern(x_ref, out_ref, send_sems, recv_sems, local_sem):
        x, y, c = lax.axis_index("x"), lax.axis_index("y"), lax.axis_index("c")
        me, sibling = (x, y, c), (x, y, 1 - c)
        chips = [(1 - x, y), (x, 1 - y), (1 - x, 1 - y)]

        def copy(k, block, to, src=None):
            return pltpu.make_async_remote_copy(
                src_ref=out_ref.at[_slot(block)] if src is None else src, dst_ref=out_ref.at[_slot(block)],
                send_sem=send_sems.at[k], recv_sem=recv_sems.at[k], device_id=to, device_id_type=pl.DeviceIdType.MESH)

        mine = pltpu.make_async_copy(x_ref, out_ref.at[_slot(me)], local_sem)
        mine.start()
        first = [copy(0, me, sibling, src=x_ref)]
        first += [copy(1 + j, me, (*chip, c), src=x_ref) for j, chip in enumerate(chips)]
        for cp in first:
            cp.start()
        passed = [copy(4 + j, (*chip, c), sibling) for j, chip in enumerate(chips)]
        for j, chip in enumerate(chips):
            copy(1 + j, (*chip, c), me).wait_recv()
            passed[j].start()
        copy(0, sibling, me).wait_recv()
        for j, chip in enumerate(chips):
            copy(4 + j, (*chip, 1 - c), me).wait_recv()
        for cp in first + passed:
            cp.wait_send()
        mine.wait()

    return pl.pallas_call(
        kern, name=name, in_specs=[_ANY], out_specs=_ANY,
        out_shape=jax.ShapeDtypeStruct((NDEV, *shard.shape), shard.dtype),
        scratch_shapes=[pltpu.SemaphoreType.DMA((7,)), pltpu.SemaphoreType.DMA((7,)), pltpu.SemaphoreType.DMA],
    )(shard)


def _all_to_all(contrib, *, name):
    def kern(x_ref, out_ref, send_sems, recv_sems, local_sem):
        me, peers = _me_and_peers()
        mine = pltpu.make_async_copy(x_ref.at[_slot(me)], out_ref.at[_slot(me)], local_sem)
        mine.start()
        sends = [pltpu.make_async_remote_copy(
            src_ref=x_ref.at[_slot(p)], dst_ref=out_ref.at[_slot(me)], send_sem=send_sems.at[k], recv_sem=recv_sems.at[k],
            device_id=p, device_id_type=pl.DeviceIdType.MESH) for k, p in enumerate(peers)]
        for cp in sends:
            cp.start()
        for k, p in enumerate(peers):
            pltpu.make_async_remote_copy(
                src_ref=x_ref.at[_slot(me)], dst_ref=out_ref.at[_slot(p)], send_sem=send_sems.at[k],
                recv_sem=recv_sems.at[k], device_id=p, device_id_type=pl.DeviceIdType.MESH).wait_recv()
        for cp in sends:
            cp.wait_send()
        mine.wait()

    return pl.pallas_call(
        kern, name=name, in_specs=[_ANY], out_specs=_ANY, out_shape=jax.ShapeDtypeStruct(contrib.shape, contrib.dtype),
        scratch_shapes=[pltpu.SemaphoreType.DMA((7,)), pltpu.SemaphoreType.DMA((7,)), pltpu.SemaphoreType.DMA],
    )(contrib)


def _sum_slots(parts, *, name):
    _, r, c = parts.shape
    tile = _pick(r, (256, 128, 64, 32, 16, 8))

    def kern(p_ref, o_ref):
        acc = p_ref[0].astype(F32)
        for d in range(1, NDEV):
            acc = acc + p_ref[d].astype(F32)
        o_ref[...] = acc

    return pl.pallas_call(
        kern, name=name, grid=(r // tile,), in_specs=[pl.BlockSpec((NDEV, tile, c), lambda i: (0, i, 0))],
        out_specs=pl.BlockSpec((tile, c), lambda i: (i, 0)), out_shape=jax.ShapeDtypeStruct((r, c), F32),
        compiler_params=_cp(("parallel",)),
    )(parts)


BIG = [
    ("ffn1_w13", (1024, 704), 1), ("ffn1_w2", (352, 1024), 0), ("w_in", (1024, 1154), 1), ("p_pool", (768, 128), 1),
    ("p_ssd", (128, 1024), 0), ("p_attn", (256, 128), 1), ("w_out", (128, 1024), 0), ("ffn2_w13", (1024, 704), 1),
    ("ffn2_w2", (352, 1024), 0),
]
BIG_ROWS = [DEPTH * r * c // 1024 for _, (r, c), _ in BIG]
PACK_ROWS = -(-sum(BIG_ROWS) // 16) * 16


def _pack_rows(blocks, lead=()):
    pad = PACK_ROWS - sum(BIG_ROWS)
    return jnp.concatenate(list(blocks) + [jnp.zeros((*lead, pad, 1024), blocks[0].dtype)], axis=len(lead))


def _pack_shards(shards, dtype):
    return _pack_rows([shards[n].astype(dtype).reshape(-1, 1024) for n, _, _ in BIG])


def _unpack_shards(packed):
    out, off = {}, 0
    for (n, (r, c), _), rows in zip(BIG, BIG_ROWS):
        out[n] = packed[off:off + rows].reshape(DEPTH, r, c)
        off += rows
    return out


def _unpack_gathered(g):
    out, off = {}, 0
    for (n, (r, c), ax), rows in zip(BIG, BIG_ROWS):
        blk = g[:, off:off + rows].reshape(NDEV, DEPTH, r, c)
        if ax == 1:
            out[n] = blk.transpose(1, 2, 0, 3).reshape(DEPTH, r, NDEV * c)
        else:
            out[n] = blk.transpose(1, 0, 2, 3).reshape(DEPTH, NDEV * r, c)
        off += rows
    return out


def _pack_full_grads(grads, dtype):
    blocks = []
    for (n, (r, c), ax), rows in zip(BIG, BIG_ROWS):
        gfull = grads[n].astype(dtype)
        if ax == 1:
            blk = gfull.reshape(DEPTH, r, NDEV, c).transpose(2, 0, 1, 3)
        else:
            blk = gfull.reshape(DEPTH, NDEV, r, c).transpose(1, 0, 2, 3)
        blocks.append(blk.reshape(NDEV, rows, 1024))
    return _pack_rows(blocks, lead=(NDEV,))


class _SmallPack:
    def __init__(self, shapes):
        self.shapes = shapes
        self.rows = [-(-int(np.prod(s)) // 128) for s in shapes]
        self.total = -(-sum(self.rows) // 8) * 8

    def pack(self, arrs):
        parts = []
        for a, s, r in zip(arrs, self.shapes, self.rows):
            assert tuple(a.shape) == tuple(s), (a.shape, s)
            flat = a.astype(F32).reshape(-1)
            parts.append(jnp.pad(flat, (0, r * 128 - flat.shape[0])).reshape(r, 128))
        parts.append(jnp.zeros((self.total - sum(self.rows), 128), F32))
        return jnp.concatenate(parts, axis=0)

    def unpack(self, packed):
        out, off = [], 0
        for s, r in zip(self.shapes, self.rows):
            out.append(packed[off:off + r].reshape(-1)[:int(np.prod(s))].reshape(s))
            off += r
        return out


def _w_in_to_padded(w):
    z = jnp.zeros((*w.shape[:-1], HC - DT0 - 16), w.dtype)
    return jnp.concatenate([w[..., O_Z:O_XBC], w[..., O_XBC:O_DT], w[..., O_G:O_END], w[..., O_Q:O_K], w[..., O_K:O_V],
                            w[..., O_V:O_G], w[..., O_U:O_Z], w[..., O_DT:O_Q], z], axis=-1)


def _w_in_from_padded(g):
    return jnp.concatenate([g[..., U0:DT0], g[..., Z0:XBC0], g[..., XBC0:GATE0], g[..., DT0:DT0 + 16], g[..., Q0:K0],
                            g[..., K0:V0], g[..., V0:U0], g[..., GATE0:Q0]], axis=-1)


def _block_diag(pw):
    out = jnp.zeros((POOL_W, POOL_W), pw.dtype)
    for gi in range(4):
        out = lax.dynamic_update_slice(out, pw[gi], (gi * POOL_GDIM, gi * POOL_GDIM))
    return out


def _expand_heads(v):
    return jnp.repeat(v, 64).reshape(1, D)


def _ffn_fwd(x, w13, w2, g, b, tag):
    h = _mm(x, w13, name=f"{tag}_h")
    act, h16 = _swiglu_fwd(h, name=f"{tag}_act")
    f = _mm(act, w2, name=f"{tag}_f")
    y = _ln_fwd(x, f, g, b, scale=FFN_RES, name=f"{tag}_ln")
    return y, (h16, act, f)


def _ffn_bwd(x, w13, w2, g, b, saved, dy, tag):
    h, act, f = saved
    dres, df, dg, db = _ln_bwd(x, f, dy, g, b, scale=FFN_RES, name=f"{tag}_ln_bwd")
    dact = _mm(df, w2, tb=True, name=f"{tag}_dact")
    dw2 = _mm(act, df, ta=True, name=f"{tag}_dw2")
    dh = _swiglu_bwd(h, dact, name=f"{tag}_act_bwd")
    dx = _mm(dh, w13, tb=True, add=dres, name=f"{tag}_dx")
    dw13 = _mm(x, dh, ta=True, name=f"{tag}_dw13")
    return dx, dw13, dw2, dg[0], db[0]


def _layer_fwd(x0, p, bias):
    x1, s1 = _ffn_fwd(x0, p["ffn1_w13"], p["ffn1_w2"], p["ln1_g"], p["ln1_b"], "ffn1")
    hcat = _mm(x1, p["w_in"], name="mix_in")
    pooled = _pool_fwd(hcat, name="pool")
    ya_lin = _mm(pooled, p["pool_bd"], name="pool_lin")
    ya_pre = _affine_fwd(ya_lin, p["pool_b"], p["pool_scale"], name="pool_affine")
    ya = _mm(ya_pre, p["p_pool"], name="pool_out")
    xa = _conv_fwd(hcat, p["conv_w"], p["conv_b"], name="conv")
    dtx = jnp.repeat(hcat[:, DT0:DT0 + SSD_HEADS], 64, axis=1)
    ysc, hs = _ssd_fwd(xa, dtx, p["dt_bias"], p["a_log"], p["d_skip"], name="ssd")
    yb_pre = _gnorm_fwd(ysc, hcat, p["ssd_norm"], name="ssd_norm")
    yb = _mm(yb_pre, p["p_ssd"], name="ssd_out")
    outs, lses = [], []
    for gi, (_, dil) in enumerate(ATTN_CONFIGS):
        q, k, v = (_to_sub(hcat[:, c0 + 256 * gi:c0 + 256 * gi + 256], dil) for c0 in (Q0, K0, V0))
        o, l = _attn_fwd(q, k, v, bias[gi], name=f"attn{gi}")
        outs.append(_from_sub(o, dil))
        lses.append(_from_sub(l, dil))
    yc_pre = _amerge_fwd(outs, lses, name="attn_merge")
    yc = _mm(yc_pre, p["p_attn"], name="attn_out")
    merged = _merge_fwd(hcat, ya, yb, yc, p["gate_b"], name="merge")
    mix = _mm(merged, p["w_out"], name="mix_out")
    x2 = _ln_fwd(x1, mix, p["ln2_g"], p["ln2_b"], scale=1.0, name="ln2")
    x3, s3 = _ffn_fwd(x2, p["ffn2_w13"], p["ffn2_w2"], p["ln3_g"], p["ln3_b"], "ffn2")
    saved = dict(x0=x0, s1=s1, x1=x1, hcat=hcat, pooled=pooled, ya_lin=ya_lin, ya_pre=ya_pre, ya=ya, xa=xa, hs=hs, ysc=ysc,
                 yb_pre=yb_pre, yb=yb, outs=outs, lses=lses, yc_pre=yc_pre, yc=yc, merged=merged, mix=mix, x2=x2, s3=s3)
    return x3, saved


def _layer_bwd(p, bias, sv, dx3):
    g = {}
    dx2, g["ffn2_w13"], g["ffn2_w2"], g["ln3_g"], g["ln3_b"] = _ffn_bwd(
        sv["x2"], p["ffn2_w13"], p["ffn2_w2"], p["ln3_g"], p["ln3_b"], sv["s3"], dx3, "ffn2")
    hcat = sv["hcat"]
    dres, dmix, dg2, db2 = _ln_bwd(sv["x1"], sv["mix"], dx2, p["ln2_g"], p["ln2_b"], scale=1.0, name="ln2_bwd")
    g["ln2_g"], g["ln2_b"] = dg2[0], db2[0]
    dmerged = _mm(dmix, p["w_out"], tb=True, name="mix_out_dx")
    g["w_out"] = _mm(sv["merged"], dmix, ta=True, name="mix_out_dw")
    dg0, dg1, dg2_, dya, dyb, dyc, gb0, gb1, gb2 = _merge_bwd(hcat, sv["ya"], sv["yb"], sv["yc"], dmerged, p["gate_b"],
                                                               name="merge_bwd")
    g["gate_b"] = jnp.concatenate([gb0, gb1, gb2], axis=0)
    dya_pre = _mm(dya, p["p_pool"], tb=True, name="pool_out_dx")
    g["p_pool"] = _mm(sv["ya_pre"], dya, ta=True, name="pool_out_dw")
    dya_lin, dpb, dps = _affine_bwd(sv["ya_lin"], dya_pre, p["pool_b"], p["pool_scale"], name="pool_affine_bwd")
    g["pool_b"], g["pool_scale"] = dpb[0].reshape(4, POOL_GDIM), dps[0]
    dpooled = _mm(dya_lin, p["pool_bd"], tb=True, name="pool_lin_dx")
    dbd = _mm(sv["pooled"], dya_lin, ta=True, name="pool_lin_dw")
    g["pool_w"] = jnp.stack([dbd[i * POOL_GDIM:(i + 1) * POOL_GDIM, i * POOL_GDIM:(i + 1) * POOL_GDIM] for i in range(4)])
    du = _pool_bwd(dpooled, name="pool_bwd")
    dyb_pre = _mm(dyb, p["p_ssd"], tb=True, name="ssd_out_dx")
    g["p_ssd"] = _mm(sv["yb_pre"], dyb, ta=True, name="ssd_out_dw")
    dysc, dz, dnw = _gnorm_bwd(sv["ysc"], hcat, dyb_pre, p["ssd_norm"], name="ssd_norm_bwd")
    g["ssd_norm"] = dnw[0]
    dtx = jnp.repeat(hcat[:, DT0:DT0 + SSD_HEADS], 64, axis=1)
    dxs, dbm, dcm, ddtx, ddtb, dalog, ddsk = _ssd_bwd(sv["xa"], dtx, p["dt_bias"], p["a_log"], p["d_skip"], sv["hs"], dysc,
                                                     name="ssd_bwd")
    g["dt_bias"], g["a_log"], g["d_skip"] = ddtb[0, ::64], dalog[0, ::64], ddsk[0, ::64]
    dxa = jnp.concatenate([dxs, dbm, dcm], axis=1)
    dxbc, dcw, dcb = _conv_bwd(hcat, p["conv_w"], p["conv_b"], dxa, name="conv_bwd")
    g["conv_w"], g["conv_b"] = dcw, dcb[0]
    ddt = jnp.pad(ddtx[:, ::64], ((0, 0), (0, HC - DT0 - SSD_HEADS))).astype(BF16)
    dyc_pre = _mm(dyc, p["p_attn"], tb=True, name="attn_out_dx")
    g["p_attn"] = _mm(sv["yc_pre"], dyc, ta=True, name="attn_out_dw")
    am = _amerge_bwd(sv["outs"], sv["lses"], dyc_pre, name="attn_merge_bwd")
    dqs, dks, dvs, dbias = [], [], [], []
    for gi, (_, dil) in enumerate(ATTN_CONFIGS):
        q, k, v = (_to_sub(hcat[:, c0 + 256 * gi:c0 + 256 * gi + 256], dil) for c0 in (Q0, K0, V0))
        o, l, do, dl = (_to_sub(t, dil) for t in (sv["outs"][gi], sv["lses"][gi], am[gi], am[3 + gi]))
        dq, dk, dv, dbi = _attn_bwd(q, k, v, o, l, do, dl, bias[gi], name=f"attn{gi}_bwd")
        dqs.append(_from_sub(dq, dil).astype(BF16))
        dks.append(_from_sub(dk, dil).astype(BF16))
        dvs.append(_from_sub(dv, dil).astype(BF16))
        dbias.append(dbi)
    g["attn_bias"] = jnp.stack(dbias)
    dhcat = jnp.concatenate([dz, dxbc, dg0, dg1, dg2_, *dqs, *dks, *dvs, du, ddt], axis=1)
    dx1 = _mm(dhcat, p["w_in"], tb=True, add=dres, name="mix_in_dx")
    g["w_in"] = _mm(sv["x1"], dhcat, ta=True, name="mix_in_dw")
    dx0, g["ffn1_w13"], g["ffn1_w2"], g["ln1_g"], g["ln1_b"] = _ffn_bwd(
        sv["x0"], p["ffn1_w13"], p["ffn1_w2"], p["ln1_g"], p["ln1_b"], sv["s1"], dx1, "ffn1")
    return dx0, g


def _local_step(x, tgt, lw, rel_bias):
    bias, bias_vjp = jax.vjp(_attn_bias, rel_bias)

    def fwd(xc, p):
        return _layer_fwd(xc, p, bias)

    y, saved = lax.scan(fwd, x, lw)
    dy, sq = _loss_kernel(y, tgt, name="loss")

    def bwd(dxc, ps):
        p, sv = ps
        return _layer_bwd(p, bias, sv, dxc)

    dx, grads = lax.scan(bwd, dy, (lw, saved), reverse=True)
    (d_rel,) = bias_vjp(jnp.sum(grads.pop("attn_bias"), axis=0))
    return sq, dx, grads, d_rel


SMALL_REPL = ["ln1_g", "ln1_b", "pool_w", "pool_b", "pool_scale", "conv_b", "dt_bias", "a_log", "d_skip", "ssd_norm",
              "ln2_g", "ln2_b", "ln3_g", "ln3_b", "rel_bias"]
SMALL_SHARD = ["gate_b", "conv_w"]
WEIGHTS = ['ffn1_w13', 'ffn1_w2', 'ln1_g', 'ln1_b', 'w_in', 'gate_b', 'pool_w', 'pool_b', 'pool_scale', 'conv_w', 'conv_b',
           'dt_bias', 'a_log', 'd_skip', 'ssd_norm', 'rel_bias', 'p_pool', 'p_ssd', 'p_attn', 'w_out', 'ln2_g', 'ln2_b',
           'ffn2_w13', 'ffn2_w2', 'ln3_g', 'ln3_b']


def _step(w, m, v, x, tgt):
    dev = 4 * lax.axis_index("x") + 2 * lax.axis_index("y") + lax.axis_index("c")
    gathered = _unpack_gathered(_all_gather(_pack_shards(w, BF16), name="gather_weights"))
    sp_in = _SmallPack([w[n].shape for n in SMALL_SHARD])
    small_g = _all_gather(sp_in.pack([w[n] for n in SMALL_SHARD]), name="gather_small")
    gate_b = jnp.stack([sp_in.unpack(small_g[d])[0] for d in range(NDEV)], axis=2).reshape(DEPTH, 3, D)
    conv_w = jnp.stack([sp_in.unpack(small_g[d])[1] for d in range(NDEV)], axis=2).reshape(DEPTH, 4, 2048)
    lw = dict(gathered)
    lw["w_in"] = _w_in_to_padded(gathered["w_in"])
    lw["pool_bd"] = jax.vmap(_block_diag)(w["pool_w"]).astype(BF16)
    lw["gate_b"], lw["conv_w"] = gate_b, conv_w
    for n in ("ln1_g", "ln1_b", "ln2_g", "ln2_b", "ln3_g", "ln3_b", "pool_scale", "conv_b", "ssd_norm"):
        lw[n] = w[n][:, None, :]
    lw["pool_b"] = w["pool_b"].reshape(DEPTH, 1, POOL_W)
    for n in ("dt_bias", "a_log", "d_skip"):
        lw[n] = jax.vmap(_expand_heads)(w[n])
    sq, dx, grads, d_rel = _local_step(x.reshape(T, D), tgt.reshape(T, D), lw, w["rel_bias"])
    loss = lax.psum(jnp.sum(sq) * (0.5 / D), ("x", "y", "c"))
    grads["w_in"] = _w_in_from_padded(grads["w_in"])
    grads["rel_bias"] = d_rel
    recv = _all_to_all(_pack_full_grads(grads, BF16), name="exchange_grads")
    gbig = _unpack_shards(_sum_slots(recv, name="sum_grads"))
    small_names = SMALL_REPL + SMALL_SHARD
    sp = _SmallPack([grads[n].shape for n in small_names])
    gsmall = sp.unpack(_sum_slots(_all_gather(sp.pack([grads[n] for n in small_names]), name="gather_small_grads"),
                                  name="sum_small_grads"))
    gout = dict(gbig)
    for n, gv in zip(small_names, gsmall):
        if n in SMALL_SHARD:
            width = w[n].shape[-1]
            gv = lax.dynamic_slice_in_dim(gv, dev * width, width, axis=2)
        gout[n] = gv
    delta, new_m, new_v = {}, {}, {}
    for n, _, _ in BIG:
        shp = w[n].shape
        two_d = lambda a: a.reshape(-1, shp[-1])
        dl, mm_, vv = _adamw(two_d(w[n]), two_d(gout[n]), two_d(m[n]), two_d(v[n]), name=f"adamw_{n}")
        delta[n], new_m[n], new_v[n] = dl.reshape(shp), mm_.reshape(shp), vv.reshape(shp)
    spa = _SmallPack([w[n].shape for n in small_names])
    res = _adamw(spa.pack([w[n] for n in small_names]), spa.pack([gout[n] for n in small_names]),
                 spa.pack([m[n] for n in small_names]), spa.pack([v[n] for n in small_names]), name="adamw_small")
    for out, packed in zip((delta, new_m, new_v), res):
        for n, val in zip(small_names, spa.unpack(packed)):
            out[n] = val
    return (loss, dx.reshape(NB, SEQ, D), *[gout[n] for n in WEIGHTS], *[delta[n] for n in WEIGHTS],
            *[new_m[n] for n in WEIGHTS], *[new_v[n] for n in WEIGHTS])


def kernel(x, ffn1_w13, ffn1_w2, ln1_g, ln1_b, w_in, gate_b, pool_w, pool_b, pool_scale, conv_w, conv_b, dt_bias, a_log, d_skip, ssd_norm, rel_bias, p_pool, p_ssd, p_attn, w_out, ln2_g, ln2_b, ffn2_w13, ffn2_w2, ln3_g, ln3_b, loss_target, m_ffn1_w13, m_ffn1_w2, m_ln1_g, m_ln1_b, m_w_in, m_gate_b, m_pool_w, m_pool_b, m_pool_scale, m_conv_w, m_conv_b, m_dt_bias, m_a_log, m_d_skip, m_ssd_norm, m_rel_bias, m_p_pool, m_p_ssd, m_p_attn, m_w_out, m_ln2_g, m_ln2_b, m_ffn2_w13, m_ffn2_w2, m_ln3_g, m_ln3_b, v_ffn1_w13, v_ffn1_w2, v_ln1_g, v_ln1_b, v_w_in, v_gate_b, v_pool_w, v_pool_b, v_pool_scale, v_conv_w, v_conv_b, v_dt_bias, v_a_log, v_d_skip, v_ssd_norm, v_rel_bias, v_p_pool, v_p_ssd, v_p_attn, v_w_out, v_ln2_g, v_ln2_b, v_ffn2_w13, v_ffn2_w2, v_ln3_g, v_ln3_b):
    given = dict(locals())
    w = {n: given[n] for n in WEIGHTS}
    m = {n: given["m_" + n] for n in WEIGHTS}
    v = {n: given["v_" + n] for n in WEIGHTS}
    return _step(w, m, v, x, loss_target)
```

```python
import functools
import math

import numpy as np
import jax
import jax.numpy as jnp
from jax import lax
from jax.experimental import pallas as pl
from jax.experimental.pallas import tpu as pltpu

F32, BF16 = jnp.float32, jnp.bfloat16
HIGHEST = lax.Precision.HIGHEST

NDEV = 8
DEPTH = 4
D = 1024
SEQ = 2048
NB = 2
T = NB * SEQ
DFF = 2816
POOL_W = 768
POOL_WINDOWS = (2, 4, 8, 16)
POOL_GDIM = 192
SSD_HEADS = 16
CHUNK = 128
NCHUNK = SEQ // CHUNK
ATTN_CONFIGS = ((128, 1), (512, 4), (2048, 16))
ATTN_BLK = 128
REL_BUCKETS = 32
REL_MAX_DIST = 2048
LN_EPS = 1e-5
SSD_EPS = 1e-5
ALPHA = (2.0 * DEPTH) ** 0.25
FFN_RES = 0.5
NEG = -1e30

ADAM_LR, ADAM_B1, ADAM_B2, ADAM_EPS, ADAM_WD, ADAM_STEP = 0.001, 0.9, 0.999, 1e-08, 0.01, 10

Z0, XBC0, GATE0, Q0, K0, V0, U0, DT0, HC = 0, 1024, 3072, 6144, 6912, 7680, 8448, 9216, 9728
O_U, O_Z, O_XBC, O_DT, O_Q, O_K, O_V, O_G, O_END = 0, 768, 1792, 3840, 3856, 4624, 5392, 6160, 9232

VMEM_LIMIT = 56 * 1024 * 1024


def _cp(sem):
    return pltpu.CompilerParams(dimension_semantics=sem, vmem_limit_bytes=VMEM_LIMIT)


def _pick(dim, cands):
    for c in cands:
        if dim % c == 0:
            return c
    return dim


def _mm(a, b, *, tb=False, out_dtype=F32, add=None, add_scale=1.0, name):
    M, K = a.shape
    (N, K2) = b.shape if tb else b.shape[::-1]
    assert K == K2, (a.shape, b.shape, tb)
    bm = _pick(M, (1024, 768, 512, 256))
    bn = _pick(N, (1024, 768, 512, 256))
    bk = K if K <= 1024 else _pick(K, (1024, 1408, 2432, 512, 256))
    nk = K // bk
    dn = (((1,), (1 if tb else 0,)), ((), ()))

    def kern(*refs):
        a_ref, b_ref = refs[0], refs[1]
        add_ref = refs[2] if add is not None else None
        o_ref = refs[3] if add is not None else refs[2]
        p = lax.dot_general(a_ref[...].astype(BF16), b_ref[...].astype(BF16), dn, preferred_element_type=F32)

        def fin(acc):
            if add_ref is not None:
                acc = acc + add_scale * add_ref[...]
            o_ref[...] = acc.astype(out_dtype)

        if nk == 1:
            fin(p)
        else:
            acc_ref = refs[-1]
            k = pl.program_id(2)

            @pl.when(k == 0)
            def _():
                acc_ref[...] = p

            @pl.when(k > 0)
            def _():
                acc_ref[...] += p

            @pl.when(k == nk - 1)
            def _():
                fin(acc_ref[...])

    a_spec = pl.BlockSpec((bm, bk), lambda i, j, k: (i, k))
    b_spec = pl.BlockSpec((bn, bk), lambda i, j, k: (j, k)) if tb else pl.BlockSpec((bk, bn), lambda i, j, k: (k, j))
    in_specs = [a_spec, b_spec]
    args = [a, b]
    if add is not None:
        in_specs.append(pl.BlockSpec((bm, bn), lambda i, j, k: (i, j)))
        args.append(add)
    return pl.pallas_call(
        kern, name=name, grid=(M // bm, N // bn, nk),
        in_specs=in_specs, out_specs=pl.BlockSpec((bm, bn), lambda i, j, k: (i, j)),
        out_shape=jax.ShapeDtypeStruct((M, N), out_dtype),
        scratch_shapes=[pltpu.VMEM((bm, bn), F32)] if nk > 1 else [],
        compiler_params=_cp(("parallel", "parallel", "arbitrary")),
    )(*args)


def _rowwise(fn, rows, pars, outs, accs, *, name, tile, groups=1):
    n_rows = rows[0][0].shape[0]
    nt = n_rows // tile
    n_in = len(rows) + len(pars)
    n_out = len(outs)

    def kern(*refs):
        res = fn(*[r[...] for r in refs[:n_in]])
        for o_ref, val in zip(refs[n_in:n_in + n_out], res[:n_out]):
            o_ref[...] = val.astype(o_ref.dtype)
        i = pl.program_id(1)
        for a_ref, val in zip(refs[n_in + n_out:], res[n_out:]):
            @pl.when(i == 0)
            def _(a_ref=a_ref, val=val):
                a_ref[...] = val

            @pl.when(i > 0)
            def _(a_ref=a_ref, val=val):
                a_ref[...] += val

    in_specs = [pl.BlockSpec((tile, w), lambda g, i, c0=c0: (i, c0 + g)) for (_, w, c0) in rows]
    for p, w in pars:
        if w is None:
            in_specs.append(pl.BlockSpec(p.shape, lambda g, i: (0, 0)))
        else:
            in_specs.append(pl.BlockSpec((p.shape[0], w), lambda g, i: (0, g)))
    out_specs = [pl.BlockSpec((tile, w), lambda g, i: (i, g)) for (_, w, _) in outs]
    out_specs += [pl.BlockSpec((1, w), lambda g, i: (0, g)) for (_, w) in accs]
    out_shape = [jax.ShapeDtypeStruct((n_rows, c), dt) for (c, _, dt) in outs]
    out_shape += [jax.ShapeDtypeStruct((1, c), F32) for (c, _) in accs]
    return pl.pallas_call(
        kern, name=name, grid=(groups, nt), in_specs=in_specs, out_specs=out_specs, out_shape=out_shape,
        compiler_params=_cp(("arbitrary", "arbitrary")),
    )(*[r[0] for r in rows], *[p[0] for p in pars])


def _colsum(v):
    return jnp.sum(v, axis=0, keepdims=True)


def _silu(v):
    return v * jax.nn.sigmoid(v)


def _ln_fn(x, f, g, b, *, scale):
    pre = ALPHA * x + scale * f
    mu = jnp.mean(pre, axis=-1, keepdims=True)
    var = jnp.mean(jnp.square(pre - mu), axis=-1, keepdims=True)
    return (pre - mu) * lax.rsqrt(var + LN_EPS) * g + b


def _ln_fwd(x, f, g, b, *, scale, name):
    fn = lambda x_, f_, g_, b_: [_ln_fn(x_, f_, g_, b_, scale=scale)] * 2
    return _rowwise(fn, [(x, D, 0), (f, D, 0)], [(g, None), (b, None)], [(D, D, F32), (D, D, BF16)], [], name=name, tile=512)


def _ln_bwd(x, f, dy, g, b, *, scale, name):
    def fn(x_, f_, dy_, g_, b_):
        _, vjp = jax.vjp(functools.partial(_ln_fn, scale=scale), x_, f_, g_, b_)
        return list(vjp(dy_))

    return _rowwise(fn, [(x, D, 0), (f, D, 0), (dy, D, 0)], [(g, None), (b, None)],
                    [(D, D, F32), (D, D, BF16)], [(D, D), (D, D)], name=name, tile=512)


def _swiglu_fwd(h, *, name):
    fn = lambda a, g: [_silu(a) * g, jnp.concatenate([a, g], axis=1)]
    return _rowwise(fn, [(h, DFF, 0), (h, DFF, 1)], [], [(DFF, DFF, BF16), (2 * DFF, 2 * DFF, BF16)], [], name=name, tile=256)


def _swiglu_bwd(h, dact, *, name):
    def fn(a, g, da_):
        a, g = a.astype(F32), g.astype(F32)
        s = jax.nn.sigmoid(a)
        return [jnp.concatenate([da_ * g * (s * (1.0 + a * (1.0 - s))), da_ * (a * s)], axis=1)]

    return _rowwise(fn, [(h, DFF, 0), (h, DFF, 1), (dact, DFF, 0)], [], [(2 * DFF, 2 * DFF, BF16)], [], name=name, tile=256)[0]


def _affine_fn(y, b, s):
    return (y + b) * s


def _affine_fwd(y, b, s, *, name):
    return _rowwise(lambda y_, b_, s_: [_affine_fn(y_, b_, s_)], [(y, POOL_W, 0)], [(b, None), (s, None)],
                    [(POOL_W, POOL_W, BF16)], [], name=name, tile=512)[0]


def _affine_bwd(y, dyo, b, s, *, name):
    def fn(y_, d_, b_, s_):
        _, vjp = jax.vjp(_affine_fn, y_, b_, s_)
        return list(vjp(d_))

    return _rowwise(fn, [(y, POOL_W, 0), (dyo, POOL_W, 0)], [(b, None), (s, None)],
                    [(POOL_W, POOL_W, BF16)], [(POOL_W, POOL_W), (POOL_W, POOL_W)], name=name, tile=512)


def _gnorm_fn(y, z, w):
    yz = y * _silu(z)
    return yz * lax.rsqrt(jnp.mean(jnp.square(yz), axis=-1, keepdims=True) + SSD_EPS) * w


def _gnorm_fwd(y, hcat, w, *, name):
    return _rowwise(lambda y_, z_, w_: [_gnorm_fn(y_, z_, w_)], [(y, 256, 0), (hcat, 256, Z0 // 256)], [(w, 256)],
                    [(D, 256, BF16)], [], name=name, tile=512, groups=4)[0]


def _gnorm_bwd(y, hcat, dyo, w, *, name):
    def fn(y_, z_, d_, w_):
        _, vjp = jax.vjp(_gnorm_fn, y_, z_, w_)
        return list(vjp(d_))

    return _rowwise(fn, [(y, 256, 0), (hcat, 256, Z0 // 256), (dyo, 256, 0)], [(w, 256)],
                    [(D, 256, F32), (D, 256, BF16)], [(D, 256)], name=name, tile=512, groups=4)


def _merge_fn(g0, g1, g2, ya, yb, yc, b0, b1, b2):
    return jax.nn.sigmoid(g0 + b0) * ya + jax.nn.sigmoid(g1 + b1) * yb + jax.nn.sigmoid(g2 + b2) * yc


def _merge_rows(hcat, ya, yb, yc):
    c = GATE0 // 256
    return [(hcat, 256, c), (hcat, 256, c + 4), (hcat, 256, c + 8), (ya, 256, 0), (yb, 256, 0), (yc, 256, 0)]


def _merge_fwd(hcat, ya, yb, yc, gb, *, name):
    pars = [(gb[0:1], 256), (gb[1:2], 256), (gb[2:3], 256)]
    return _rowwise(lambda *v: [_merge_fn(*v)], _merge_rows(hcat, ya, yb, yc), pars, [(D, 256, BF16)], [],
                    name=name, tile=512, groups=4)[0]


def _merge_bwd(hcat, ya, yb, yc, dm, gb, *, name):
    def fn(g0, g1, g2, ya_, yb_, yc_, dm_, b0, b1, b2):
        _, vjp = jax.vjp(_merge_fn, g0, g1, g2, ya_, yb_, yc_, b0, b1, b2)
        return list(vjp(dm_))

    pars = [(gb[0:1], 256), (gb[1:2], 256), (gb[2:3], 256)]
    return _rowwise(fn, _merge_rows(hcat, ya, yb, yc) + [(dm, 256, 0)], pars,
                    [(D, 256, BF16)] * 6, [(D, 256)] * 3, name=name, tile=512, groups=4)


def _amerge_fn(o0, o1, o2, l0, l1, l2):
    m = jnp.maximum(jnp.maximum(l0, l1), l2)
    e0, e1, e2 = jnp.exp(l0 - m), jnp.exp(l1 - m), jnp.exp(l2 - m)
    return (e0 * o0 + e1 * o1 + e2 * o2) / (e0 + e1 + e2)


def _amerge_fwd(os_, ls_, *, name):
    rows = [(v, 256, 0) for v in (*os_, *ls_)]
    return _rowwise(lambda *v: [_amerge_fn(*v)], rows, [], [(256, 256, BF16)], [], name=name, tile=1024)[0]


def _amerge_bwd(os_, ls_, dy, *, name):
    def fn(*v):
        _, vjp = jax.vjp(_amerge_fn, *v[:6])
        return list(vjp(v[6]))

    rows = [(v, 256, 0) for v in (*os_, *ls_, dy)]
    return _rowwise(fn, rows, [], [(256, 256, F32)] * 6, [], name=name, tile=1024)


def _loss_kernel(y, tgt, *, name):
    def fn(y_, t_):
        e = y_ - t_
        return [e * (1.0 / D), _colsum(e * e)]

    return _rowwise(fn, [(y, D, 0), (tgt, D, 0)], [], [(D, D, F32)], [(D, D)], name=name, tile=512)


def _adamw(w, g, m, v, *, name):
    r, c = w.shape
    tile = _pick(r, (512, 256, 128, 64, 32, 16, 8))

    def fn(w_, g_, m_, v_):
        m2 = ADAM_B1 * m_ + (1.0 - ADAM_B1) * g_
        v2 = ADAM_B2 * v_ + (1.0 - ADAM_B2) * jnp.square(g_)
        m_hat = m2 / (1.0 - ADAM_B1 ** ADAM_STEP)
        v_hat = v2 / (1.0 - ADAM_B2 ** ADAM_STEP)
        return [-ADAM_LR * (m_hat / (jnp.sqrt(v_hat) + ADAM_EPS) + ADAM_WD * w_), m2, v2]

    return _rowwise(fn, [(w, c, 0), (g, c, 0), (m, c, 0), (v, c, 0)], [], [(c, c, F32)] * 3, [], name=name, tile=tile)


def _pool_lane_window(j, width):
    lane = lax.broadcasted_iota(jnp.int32, (1, width), 1) + j * width
    grp = lane // POOL_GDIM
    return grp


def _pool_select(grp, vals):
    out = vals[3]
    for gi in (2, 1, 0):
        out = jnp.where(grp == gi, vals[gi], out)
    return out


def _pool_fwd(hcat, *, name):
    width = 256

    def kern(u_ref, o_ref):
        u = u_ref[...]
        t = lax.broadcasted_iota(jnp.int32, (SEQ, 1), 0)
        grp = _pool_lane_window(pl.program_id(1), width)

        def shift(v, k):
            return jnp.where(t >= k, pltpu.roll(v, k, 0), 0.0)

        s2 = u + shift(u, 1)
        s4 = s2 + shift(s2, 2)
        s8 = s4 + shift(s4, 4)
        s16 = s8 + shift(s8, 8)
        win = _pool_select(grp, [jnp.full((1, width), float(w), F32) for w in POOL_WINDOWS])
        cnt = jnp.minimum((t + 1).astype(F32), win)
        o_ref[...] = (_pool_select(grp, [s2, s4, s8, s16]) / cnt - u).astype(o_ref.dtype)

    return pl.pallas_call(
        kern, name=name, grid=(NB, POOL_W // width),
        in_specs=[pl.BlockSpec((SEQ, width), lambda b, j: (b, U0 // width + j))],
        out_specs=pl.BlockSpec((SEQ, width), lambda b, j: (b, j)),
        out_shape=jax.ShapeDtypeStruct((T, POOL_W), BF16), compiler_params=_cp(("parallel", "parallel")),
    )(hcat)


def _pool_bwd(dp, *, name):
    width = 256

    def kern(d_ref, o_ref):
        d = d_ref[...]
        t = lax.broadcasted_iota(jnp.int32, (SEQ, 1), 0)
        grp = _pool_lane_window(pl.program_id(1), width)
        win = _pool_select(grp, [jnp.full((1, width), float(w), F32) for w in POOL_WINDOWS])
        dm = d / jnp.minimum((t + 1).astype(F32), win)

        def shift(v, k):
            return jnp.where(t < SEQ - k, pltpu.roll(v, SEQ - k, 0), 0.0)

        r2 = dm + shift(dm, 1)
        r4 = r2 + shift(r2, 2)
        r8 = r4 + shift(r4, 4)
        r16 = r8 + shift(r8, 8)
        o_ref[...] = (_pool_select(grp, [r2, r4, r8, r16]) - d).astype(o_ref.dtype)

    return pl.pallas_call(
        kern, name=name, grid=(NB, POOL_W // width),
        in_specs=[pl.BlockSpec((SEQ, width), lambda b, j: (b, j))],
        out_specs=pl.BlockSpec((SEQ, width), lambda b, j: (b, j)),
        out_shape=jax.ShapeDtypeStruct((T, POOL_W), BF16), compiler_params=_cp(("parallel", "parallel")),
    )(dp)


CONV_W = 512


def _conv_pre(x, w, b, t):
    pre = w[3:4] * x + b
    for k in (1, 2, 3):
        pre = pre + w[3 - k:4 - k] * jnp.where(t >= k, pltpu.roll(x, k, 0), 0.0)
    return pre


def _conv_fwd(hcat, w, b, *, name):
    def kern(x_ref, w_ref, b_ref, o_ref):
        t = lax.broadcasted_iota(jnp.int32, (SEQ, 1), 0)
        o_ref[...] = _silu(_conv_pre(x_ref[...], w_ref[...], b_ref[...], t))

    return pl.pallas_call(
        kern, name=name, grid=(NB, 2048 // CONV_W),
        in_specs=[pl.BlockSpec((SEQ, CONV_W), lambda s, j: (s, XBC0 // CONV_W + j)),
                  pl.BlockSpec((4, CONV_W), lambda s, j: (0, j)), pl.BlockSpec((1, CONV_W), lambda s, j: (0, j))],
        out_specs=pl.BlockSpec((SEQ, CONV_W), lambda s, j: (s, j)),
        out_shape=jax.ShapeDtypeStruct((T, 2048), F32), compiler_params=_cp(("parallel", "parallel")),
    )(hcat, w, b)


def _conv_bwd(hcat, w, b, dy, *, name):
    def kern(x_ref, w_ref, b_ref, dy_ref, dx_ref, dw_ref, db_ref):
        t = lax.broadcasted_iota(jnp.int32, (SEQ, 1), 0)
        x, w_ = x_ref[...], w_ref[...]
        pre = _conv_pre(x, w_, b_ref[...], t)
        s = jax.nn.sigmoid(pre)
        dpre = dy_ref[...] * (s * (1.0 + pre * (1.0 - s)))
        dx = w_[3:4] * dpre
        dws = [None] * 4
        dws[3] = _colsum(dpre * x)
        for k in (1, 2, 3):
            dx = dx + w_[3 - k:4 - k] * jnp.where(t < SEQ - k, pltpu.roll(dpre, SEQ - k, 0), 0.0)
            dws[3 - k] = _colsum(dpre * jnp.where(t >= k, pltpu.roll(x, k, 0), 0.0))
        dx_ref[...] = dx.astype(dx_ref.dtype)
        db = _colsum(dpre)
        first = pl.program_id(1) == 0

        @pl.when(first)
        def _():
            for k in range(4):
                dw_ref[k:k + 1, :] = dws[k]
            db_ref[...] = db

        @pl.when(jnp.logical_not(first))
        def _():
            for k in range(4):
                dw_ref[k:k + 1, :] += dws[k]
            db_ref[...] += db

    return pl.pallas_call(
        kern, name=name, grid=(2048 // CONV_W, NB),
        in_specs=[pl.BlockSpec((SEQ, CONV_W), lambda j, s: (s, XBC0 // CONV_W + j)),
                  pl.BlockSpec((4, CONV_W), lambda j, s: (0, j)), pl.BlockSpec((1, CONV_W), lambda j, s: (0, j)),
                  pl.BlockSpec((SEQ, CONV_W), lambda j, s: (s, j))],
        out_specs=[pl.BlockSpec((SEQ, CONV_W), lambda j, s: (s, j)), pl.BlockSpec((4, CONV_W), lambda j, s: (0, j)),
                   pl.BlockSpec((1, CONV_W), lambda j, s: (0, j))],
        out_shape=[jax.ShapeDtypeStruct((T, 2048), BF16), jax.ShapeDtypeStruct((4, 2048), F32),
                   jax.ShapeDtypeStruct((1, 2048), F32)],
        compiler_params=_cp(("arbitrary", "arbitrary")),
    )(hcat, w, b, dy)


def _softplus(v):
    return jnp.maximum(v, 0.0) + jnp.log1p(jnp.exp(-jnp.abs(v)))


def _dot(a, b, dims):
    return lax.dot_general(a, b, (dims, ((), ())), preferred_element_type=F32)


_NN, _NT, _TN = ((1,), (0,)), ((1,), (1,)), ((0,), (0,))


def _ssd_specs(order):
    def spec(width, col0):
        return pl.BlockSpec((SEQ, width), lambda i, j, c0=col0: (order(i, j)[0], c0 + order(i, j)[1]))

    def par():
        return pl.BlockSpec((1, 256), lambda i, j: (0, order(i, j)[1]))

    return spec, par


def _ssd_chunk_common(c, xs_ref, b_ref, c_ref, dtx_ref, dtb, a, trif):
    r0 = pl.multiple_of(c * CHUNK, CHUNK)
    rows = pl.ds(r0, CHUNK)
    x = xs_ref[rows, :]
    bb = b_ref[rows, :].astype(BF16)
    cb = c_ref[rows, :].astype(BF16)
    raw = dtx_ref[rows, :] + dtb
    dt = _softplus(raw)
    cs = jnp.dot(trif, dt * a, precision=HIGHEST, preferred_element_type=F32)
    return rows, x, bb, cb, raw, dt, cs


def _head_decay(cs, me, tri):
    cse = jnp.max(jnp.where(me, cs, -jnp.inf), axis=1, keepdims=True)
    csb = jnp.broadcast_to(cse, (CHUNK, CHUNK))
    return jnp.where(tri, jnp.exp(csb - csb.T), 0.0)


def _ssd_fwd(xa, dtx, dtb, alog, dsk, *, name):
    spec, par = _ssd_specs(lambda g, b: (b, g))

    def kern(xs_ref, b_ref, c_ref, dtx_ref, dtb_ref, alog_ref, dsk_ref, y_ref, hs_ref, h_scr):
        row = lax.broadcasted_iota(jnp.int32, (CHUNK, CHUNK), 0)
        col = lax.broadcasted_iota(jnp.int32, (CHUNK, CHUNK), 1)
        tri = row >= col
        trif = tri.astype(F32)
        head = lax.broadcasted_iota(jnp.int32, (1, 256), 1) // 64
        dtb_, a, dsk_ = dtb_ref[...], -jnp.exp(alog_ref[...]), dsk_ref[...]
        h_scr[...] = jnp.zeros_like(h_scr)

        def chunk(c, carry):
            rows, x, bb, cb, _, dt, cs = _ssd_chunk_common(c, xs_ref, b_ref, c_ref, dtx_ref, dtb_, a, trif)
            cs_last = cs[CHUNK - 1:CHUNK, :]
            xdt = x * dt
            xdtb = xdt.astype(BF16)
            g = _dot(cb, bb, _NT)
            hin = h_scr[...]
            hs_ref[rows, :] = hin
            y = jnp.exp(cs) * _dot(cb, hin.astype(BF16), _NN) + dsk_ * x
            for e in range(4):
                me = head == e
                m = (g * _head_decay(cs, me, tri)).astype(BF16)
                y = y + jnp.where(me, _dot(m, xdtb, _NN), 0.0)
            y_ref[rows, :] = y
            st = _dot(bb, (jnp.exp(cs_last - cs) * xdt).astype(BF16), _TN)
            h_scr[...] = hin * jnp.exp(cs_last) + st
            return carry

        lax.fori_loop(0, NCHUNK, chunk, 0)

    return pl.pallas_call(
        kern, name=name, grid=(4, NB),
        in_specs=[spec(256, 0), spec(128, 8), spec(128, 12), spec(256, 0), par(), par(), par()],
        out_specs=[spec(256, 0), spec(256, 0)],
        out_shape=[jax.ShapeDtypeStruct((T, D), F32), jax.ShapeDtypeStruct((T, D), F32)],
        scratch_shapes=[pltpu.VMEM((CHUNK, 256), F32)],
        compiler_params=_cp(("parallel", "parallel")),
    )(xa, xa, xa, dtx, dtb, alog, dsk)


def _ssd_bwd(xa, dtx, dtb, alog, dsk, hs, dy, *, name):
    spec, par = _ssd_specs(lambda g, b: (b, g))

    def kern(xs_ref, b_ref, c_ref, dtx_ref, dtb_ref, alog_ref, dsk_ref, hs_ref, dy_ref,
             dx_ref, db_ref, dc_ref, ddt_ref, ddtb_ref, dalog_ref, ddsk_ref, dh_scr):
        row = lax.broadcasted_iota(jnp.int32, (CHUNK, CHUNK), 0)
        col = lax.broadcasted_iota(jnp.int32, (CHUNK, CHUNK), 1)
        tri = row >= col
        trif = tri.astype(F32)
        trit = (row <= col).astype(F32)
        is_last = lax.broadcasted_iota(jnp.int32, (CHUNK, 1), 0) == CHUNK - 1
        head = lax.broadcasted_iota(jnp.int32, (1, 256), 1) // 64
        dtb_, a, dsk_ = dtb_ref[...], -jnp.exp(alog_ref[...]), dsk_ref[...]
        dh_scr[...] = jnp.zeros_like(dh_scr)

        @pl.when(pl.program_id(1) == 0)
        def _():
            ddtb_ref[...] = jnp.zeros_like(ddtb_ref)
            dalog_ref[...] = jnp.zeros_like(dalog_ref)
            ddsk_ref[...] = jnp.zeros_like(ddsk_ref)

        def hsum(v, me):
            return jnp.sum(jnp.where(me, v, 0.0), axis=1, keepdims=True)

        def chunk(ci, carry):
            c = NCHUNK - 1 - ci
            rows, x, bb, cb, raw, dt, cs = _ssd_chunk_common(c, xs_ref, b_ref, c_ref, dtx_ref, dtb_, a, trif)
            cs_last = cs[CHUNK - 1:CHUNK, :]
            ecs = jnp.exp(cs)
            dsx = jnp.exp(cs_last - cs)
            xdt = x * dt
            xdtb = xdt.astype(BF16)
            g = _dot(cb, bb, _NT)
            hin = hs_ref[rows, :]
            hinb = hin.astype(BF16)
            dy_ = dy_ref[rows, :]
            dyb = dy_.astype(BF16)
            dh = dh_scr[...]
            dhb = dh.astype(BF16)
            bdh = _dot(bb, dhb, _NN)
            yoff = ecs * _dot(cb, hinb, _NN)
            dxdt = dsx * bdh
            t1, t2, t3 = dy_ * yoff, xdt * bdh, dh * hin
            dg = jnp.zeros((CHUNK, CHUNK), F32)
            dcs = jnp.zeros((CHUNK, 256), F32)
            for e in range(4):
                me = head == e
                l_ = _head_decay(cs, me, tri)
                m = g * l_
                dxdt = dxdt + jnp.where(me, _dot(m.astype(BF16), dyb, _TN), 0.0)
                dm = _dot(jnp.where(me, dy_, 0.0).astype(BF16), xdtb, _NT)
                dg = dg + dm * l_
                w = dm * m
                dds = hsum(t2, me)
                dse = jnp.max(jnp.where(me, dsx, -jnp.inf), axis=1, keepdims=True)
                ecl = jnp.max(jnp.where(me, jnp.exp(cs_last), -jnp.inf), axis=1, keepdims=True)
                dcs_e = hsum(t1, me) + jnp.sum(w, axis=1, keepdims=True) - jnp.sum(w.T, axis=1, keepdims=True) - dds * dse
                last = jnp.sum(dds * dse, axis=0, keepdims=True) + ecl * jnp.sum(hsum(t3, me), axis=0, keepdims=True)
                dcs_e = dcs_e + jnp.where(is_last, last, 0.0)
                dcs = dcs + jnp.where(me, dcs_e, 0.0)
            dadt = jnp.dot(trit, dcs, precision=HIGHEST, preferred_element_type=F32)
            ddt = a * dadt
            dxx = dxdt * x
            dyx = dy_ * x
            dsk_acc = jnp.zeros((1, 256), F32)
            for e in range(4):
                me = head == e
                ddt = ddt + jnp.where(me, hsum(dxx, me), 0.0)
                dsk_acc = dsk_acc + jnp.where(me, jnp.sum(hsum(dyx, me), axis=0, keepdims=True), 0.0)
            draw = ddt * jax.nn.sigmoid(raw)
            ddt_ref[rows, :] = draw
            ddtb_ref[...] += _colsum(draw)
            dalog_ref[...] += _colsum(dadt * dt) * a
            ddsk_ref[...] += dsk_acc
            dx_ref[rows, :] = dxdt * dt + dsk_ * dy_
            edy = (ecs * dy_).astype(BF16)
            dgb = dg.astype(BF16)
            dc_ref[rows, :] = _dot(dgb, bb, _NN) + _dot(edy, hinb, _NT)
            db_ref[rows, :] = _dot(dgb, cb, _TN) + _dot((dsx * xdt).astype(BF16), dhb, _NT)
            dh_scr[...] = jnp.exp(cs_last) * dh + _dot(cb, edy, _TN)
            return carry

        lax.fori_loop(0, NCHUNK, chunk, 0)

    return pl.pallas_call(
        kern, name=name, grid=(4, NB),
        in_specs=[spec(256, 0), spec(128, 8), spec(128, 12), spec(256, 0), par(), par(), par(), spec(256, 0), spec(256, 0)],
        out_specs=[spec(256, 0), spec(128, 0), spec(128, 0), spec(256, 0), par(), par(), par()],
        out_shape=[jax.ShapeDtypeStruct((T, D), F32), jax.ShapeDtypeStruct((T, 512), F32), jax.ShapeDtypeStruct((T, 512), F32),
                   jax.ShapeDtypeStruct((T, D), F32), jax.ShapeDtypeStruct((1, D), F32), jax.ShapeDtypeStruct((1, D), F32),
                   jax.ShapeDtypeStruct((1, D), F32)],
        scratch_shapes=[pltpu.VMEM((CHUNK, 256), F32)],
        compiler_params=_cp(("arbitrary", "arbitrary")),
    )(xa, xa, xa, dtx, dtb, alog, dsk, hs, dy)


def _t5_bucket_np(dist):
    dist = np.maximum(dist, 0)
    max_exact = REL_BUCKETS // 2
    large = max_exact + (np.log(np.maximum(dist, 1) / max_exact) / np.log(REL_MAX_DIST / max_exact)
                         * (REL_BUCKETS - max_exact)).astype(np.int32)
    large = np.minimum(large, REL_BUCKETS - 1)
    return np.where(dist < max_exact, dist, large).astype(np.int32)


def _attn_bias(rel_bias):
    qi = np.arange(ATTN_BLK)[:, None]
    kj = np.arange(2 * ATTN_BLK)[None, :]
    delta = qi - kj + ATTN_BLK
    out = []
    for gi, (window, dil) in enumerate(ATTN_CONFIGS):
        in_band = (delta >= 0) & (delta <= window // dil)
        bucket = jnp.asarray(_t5_bucket_np(delta * dil).reshape(-1, 1))
        one_hot = (bucket == jnp.arange(REL_BUCKETS)[None, :]).astype(F32)
        tab = jnp.dot(one_hot, rel_bias[:, 4 * gi:4 * gi + 4], precision=HIGHEST).reshape(ATTN_BLK, 2 * ATTN_BLK, 4)
        out.append(jnp.where(jnp.asarray(in_band)[None], tab.transpose(2, 0, 1), NEG))
    return jnp.stack(out)


def _attn_logits(q_ref, k_ref, bias_ref, n, sub_blocks, cur, prev):
    qh = q_ref[cur, :].astype(BF16)
    kband = jnp.concatenate([k_ref[prev, :], k_ref[cur, :]], axis=0).astype(BF16)
    kj = lax.broadcasted_iota(jnp.int32, (1, 2 * ATTN_BLK), 1)
    ok = jnp.logical_or(n % sub_blocks > 0, kj >= ATTN_BLK)
    return jnp.where(ok, _dot(qh, kband, _NT) * 0.125 + bias_ref[...], NEG), kband


def _attn_specs(L):
    seq = pl.BlockSpec((None, None, L, 64), lambda h, s: (s, h, 0, 0))
    tab = pl.BlockSpec((None, ATTN_BLK, 2 * ATTN_BLK), lambda h, s: (h, 0, 0))
    return seq, tab


def _attn_rows(n):
    cur = pl.ds(pl.multiple_of(n * ATTN_BLK, ATTN_BLK), ATTN_BLK)
    prev = pl.ds(pl.multiple_of(jnp.maximum(n - 1, 0) * ATTN_BLK, ATTN_BLK), ATTN_BLK)
    return cur, prev


def _attn_fwd(q, k, v, bias, sub_blocks, *, name):
    nsub, _, L, _ = q.shape
    nblk = L // ATTN_BLK

    def kern(q_ref, k_ref, v_ref, bias_ref, o_ref, l_ref):
        def blk(n, carry):
            cur, prev = _attn_rows(n)
            logits, _ = _attn_logits(q_ref, k_ref, bias_ref, n, sub_blocks, cur, prev)
            vband = jnp.concatenate([v_ref[prev, :], v_ref[cur, :]], axis=0).astype(BF16)
            m = jnp.max(logits, axis=-1, keepdims=True)
            p = jnp.exp(logits - m)
            den = jnp.sum(p, axis=-1, keepdims=True)
            o_ref[cur, :] = _dot((p / den).astype(BF16), vband, _NN)
            l_ref[cur, :] = jnp.broadcast_to(m + jnp.log(den), (ATTN_BLK, 64))
            return carry

        lax.fori_loop(0, nblk, blk, 0)

    seq, tab = _attn_specs(L)
    return pl.pallas_call(
        kern, name=name, grid=(4, nsub), in_specs=[seq, seq, seq, tab],
        out_specs=[seq, seq], out_shape=[jax.ShapeDtypeStruct(q.shape, F32)] * 2,
        compiler_params=_cp(("parallel", "parallel")),
    )(q, k, v, bias)


def _attn_bwd(q, k, v, o, lse, do, dl, bias, sub_blocks, *, name):
    nsub, _, L, _ = q.shape
    nblk = L // ATTN_BLK

    def kern(q_ref, k_ref, v_ref, o_ref, l_ref, do_ref, dl_ref, bias_ref, dq_ref, dk_ref, dv_ref, dbias_ref):
        @pl.when(pl.program_id(1) == 0)
        def _():
            dbias_ref[...] = jnp.zeros_like(dbias_ref)

        dk_ref[...] = jnp.zeros_like(dk_ref)
        dv_ref[...] = jnp.zeros_like(dv_ref)

        def blk(n, carry):
            cur, prev = _attn_rows(n)
            logits, kband = _attn_logits(q_ref, k_ref, bias_ref, n, sub_blocks, cur, prev)
            vband = jnp.concatenate([v_ref[prev, :], v_ref[cur, :]], axis=0).astype(BF16)
            p = jnp.exp(logits - l_ref[cur, 0:1])
            doh = do_ref[cur, :]
            dob = doh.astype(BF16)
            dp = _dot(dob, vband, _NT)
            dd = jnp.sum(doh * o_ref[cur, :], axis=-1, keepdims=True)
            dls = jnp.sum(dl_ref[cur, :], axis=-1, keepdims=True)
            ds = p * (dp - dd + dls)
            dbias_ref[...] += ds
            dsb = (ds * 0.125).astype(BF16)
            dq_ref[cur, :] = _dot(dsb, kband, _NN)
            dkb = _dot(dsb, q_ref[cur, :].astype(BF16), _TN)
            dvb = _dot(p.astype(BF16), dob, _TN)
            dk_ref[prev, :] += dkb[:ATTN_BLK]
            dk_ref[cur, :] += dkb[ATTN_BLK:]
            dv_ref[prev, :] += dvb[:ATTN_BLK]
            dv_ref[cur, :] += dvb[ATTN_BLK:]
            return carry

        lax.fori_loop(0, nblk, blk, 0)

    seq, tab = _attn_specs(L)
    return pl.pallas_call(
        kern, name=name, grid=(4, nsub),
        in_specs=[seq] * 7 + [tab], out_specs=[seq, seq, seq, tab],
        out_shape=[jax.ShapeDtypeStruct(q.shape, F32)] * 3 + [jax.ShapeDtypeStruct((4, ATTN_BLK, 2 * ATTN_BLK), F32)],
        compiler_params=_cp(("arbitrary", "arbitrary")),
    )(q, k, v, o, lse, do, dl, bias)


def _to_sub(t, dil):
    t = t.reshape(NB, SEQ // dil, dil, 4, 64).transpose(0, 3, 2, 1, 4)
    return t.reshape(NB, 4, SEQ, 64)


def _from_sub(t, dil):
    t = t.reshape(NB, 4, dil, SEQ // dil, 64).transpose(0, 3, 2, 1, 4)
    return t.reshape(T, 256)


_ANY = pl.BlockSpec(memory_space=pl.ANY)
_FLIPS = [(0, 0, 1), (1, 0, 0), (0, 1, 0), (1, 1, 0), (1, 0, 1), (0, 1, 1), (1, 1, 1)]


def _me_and_peers():
    x, y, c = lax.axis_index("x"), lax.axis_index("y"), lax.axis_index("c")
    peers = [(1 - x if fx else x, 1 - y if fy else y, 1 - c if fc else c) for fx, fy, fc in _FLIPS]
    return (x, y, c), peers


def _slot(dev):
    return 4 * dev[0] + 2 * dev[1] + dev[2]


def _all_gather(shard, *, name):
    def kern(x_ref, out_ref, send_sems, recv_sems, local_sem):
        x, y, c = lax.axis_index("x"), lax.axis_index("y"), lax.axis_index("c")
        me, sibling = (x, y, c), (x, y, 1 - c)
        chips = [(1 - x, y), (x, 1 - y), (1 - x, 1 - y)]

        def copy(k, block, to, src=None):
            return pltpu.make_async_remote_copy(
                src_ref=out_ref.at[_slot(block)] if src is None else src, dst_ref=out_ref.at[_slot(block)],
                send_sem=send_sems.at[k], recv_sem=recv_sems.at[k], device_id=to, device_id_type=pl.DeviceIdType.MESH)

        mine = pltpu.make_async_copy(x_ref, out_ref.at[_slot(me)], local_sem)
        mine.start()
        first = [copy(0, me, sibling, src=x_ref)]
        first += [copy(1 + j, me, (*chip, c), src=x_ref) for j, chip in enumerate(chips)]
        for cp in first:
            cp.start()
        passed = [copy(4 + j, (*chip, c), sibling) for j, chip in enumerate(chips)]
        for j, chip in enumerate(chips):
            copy(1 + j, (*chip, c), me).wait_recv()
            passed[j].start()
        copy(0, sibling, me).wait_recv()
        for j, chip in enumerate(chips):
            copy(4 + j, (*chip, 1 - c), me).wait_recv()
        for cp in first + passed:
            cp.wait_send()
        mine.wait()

    return pl.pallas_call(
        kern, name=name, in_specs=[_ANY], out_specs=_ANY,
        out_shape=jax.ShapeDtypeStruct((NDEV, *shard.shape), shard.dtype),
        scratch_shapes=[pltpu.SemaphoreType.DMA((7,)), pltpu.SemaphoreType.DMA((7,)), pltpu.SemaphoreType.DMA],
    )(shard)


def _all_to_all(contrib, *, name):
    def kern(x_ref, out_ref, send_sems, recv_sems, local_sem):
        me, peers = _me_and_peers()
        mine = pltpu.make_async_copy(x_ref.at[_slot(me)], out_ref.at[_slot(me)], local_sem)
        mine.start()
        sends = [pltpu.make_async_remote_copy(
            src_ref=x_ref.at[_slot(p)], dst_ref=out_ref.at[_slot(me)], send_sem=send_sems.at[k], recv_sem=recv_sems.at[k],
            device_id=p, device_id_type=pl.DeviceIdType.MESH) for k, p in enumerate(peers)]
        for cp in sends:
            cp.start()
        for k, p in enumerate(peers):
            pltpu.make_async_remote_copy(
                src_ref=x_ref.at[_slot(me)], dst_ref=out_ref.at[_slot(p)], send_sem=send_sems.at[k],
                recv_sem=recv_sems.at[k], device_id=p, device_id_type=pl.DeviceIdType.MESH).wait_recv()
        for cp in sends:
            cp.wait_send()
        mine.wait()

    return pl.pallas_call(
        kern, name=name, in_specs=[_ANY], out_specs=_ANY, out_shape=jax.ShapeDtypeStruct(contrib.shape, contrib.dtype),
        scratch_shapes=[pltpu.SemaphoreType.DMA((7,)), pltpu.SemaphoreType.DMA((7,)), pltpu.SemaphoreType.DMA],
    )(contrib)


def _sum_slots(parts, *, name):
    _, r, c = parts.shape
    tile = _pick(r, (256, 128, 64, 32, 16, 8))

    def kern(p_ref, o_ref):
        acc = p_ref[0].astype(F32)
        for d in range(1, NDEV):
            acc = acc + p_ref[d].astype(F32)
        o_ref[...] = acc

    return pl.pallas_call(
        kern, name=name, grid=(r // tile,), in_specs=[pl.BlockSpec((NDEV, tile, c), lambda i: (0, i, 0))],
        out_specs=pl.BlockSpec((tile, c), lambda i: (i, 0)), out_shape=jax.ShapeDtypeStruct((r, c), F32),
        compiler_params=_cp(("parallel",)),
    )(parts)


BIG = [
    ("ffn1_w13", (1024, 704), 1), ("ffn1_w2", (352, 1024), 0), ("w_in", (1024, 1154), 1), ("p_pool", (768, 128), 1),
    ("p_ssd", (128, 1024), 0), ("p_attn", (256, 128), 1), ("w_out", (128, 1024), 0), ("ffn2_w13", (1024, 704), 1),
    ("ffn2_w2", (352, 1024), 0),
]
BIG_ROWS = [DEPTH * r * c // 1024 for _, (r, c), _ in BIG]
PACK_ROWS = -(-sum(BIG_ROWS) // 256) * 256


def _pack_rows(blocks, lead=()):
    pad = PACK_ROWS - sum(BIG_ROWS)
    return jnp.concatenate(list(blocks) + [jnp.zeros((*lead, pad, 1024), blocks[0].dtype)], axis=len(lead))


def _pack_shards(shards, dtype):
    return _pack_rows([shards[n].astype(dtype).reshape(-1, 1024) for n, _, _ in BIG])


def _unpack_shards(packed):
    out, off = {}, 0
    for (n, (r, c), _), rows in zip(BIG, BIG_ROWS):
        out[n] = packed[off:off + rows].reshape(DEPTH, r, c)
        off += rows
    return out


def _unpack_gathered(g):
    out, off = [{} for _ in range(DEPTH)], 0
    for (n, (r, c), ax), rows in zip(BIG, BIG_ROWS):
        per = rows // DEPTH
        for l in range(DEPTH):
            blk = g[:, off + l * per:off + (l + 1) * per].reshape(NDEV, r, c)
            out[l][n] = blk.transpose(1, 0, 2).reshape(r, NDEV * c) if ax == 1 else blk.reshape(NDEV * r, c)
        off += rows
    return out


def _pack_full_grads(grads, dtype):
    blocks = []
    for (n, (r, c), ax), rows in zip(BIG, BIG_ROWS):
        for l in range(DEPTH):
            gfull = grads[l][n].astype(dtype)
            blk = gfull.reshape(r, NDEV, c).transpose(1, 0, 2) if ax == 1 else gfull.reshape(NDEV, r, c)
            blocks.append(blk.reshape(NDEV, rows // DEPTH, 1024))
    return _pack_rows(blocks, lead=(NDEV,))


class _SmallPack:
    def __init__(self, shapes):
        self.shapes = shapes
        self.rows = [-(-int(np.prod(s)) // 128) for s in shapes]
        self.total = -(-sum(self.rows) // 256) * 256

    def pack(self, arrs):
        parts = []
        for a, s, r in zip(arrs, self.shapes, self.rows):
            assert tuple(a.shape) == tuple(s), (a.shape, s)
            flat = a.astype(F32).reshape(-1)
            parts.append(jnp.pad(flat, (0, r * 128 - flat.shape[0])).reshape(r, 128))
        parts.append(jnp.zeros((self.total - sum(self.rows), 128), F32))
        return jnp.concatenate(parts, axis=0)

    def unpack(self, packed):
        out, off = [], 0
        for s, r in zip(self.shapes, self.rows):
            out.append(packed[off:off + r].reshape(-1)[:int(np.prod(s))].reshape(s))
            off += r
        return out


def _w_in_to_padded(w):
    z = jnp.zeros((*w.shape[:-1], HC - DT0 - 16), w.dtype)
    return jnp.concatenate([w[..., O_Z:O_XBC], w[..., O_XBC:O_DT], w[..., O_G:O_END], w[..., O_Q:O_K], w[..., O_K:O_V],
                            w[..., O_V:O_G], w[..., O_U:O_Z], w[..., O_DT:O_Q], z], axis=-1)


def _w_in_from_padded(g):
    return jnp.concatenate([g[..., U0:DT0], g[..., Z0:XBC0], g[..., XBC0:GATE0], g[..., DT0:DT0 + 16], g[..., Q0:K0],
                            g[..., K0:V0], g[..., V0:U0], g[..., GATE0:Q0]], axis=-1)


def _block_diag(pw):
    out = jnp.zeros((POOL_W, POOL_W), pw.dtype)
    for gi in range(4):
        out = lax.dynamic_update_slice(out, pw[gi], (gi * POOL_GDIM, gi * POOL_GDIM))
    return out


def _expand_heads(v):
    return jnp.repeat(v, 64).reshape(1, D)


def _ffn_fwd(x, x16, w13, w2, g, b, tag):
    h = _mm(x16, w13, name=f"{tag}_h")
    act, h16 = _swiglu_fwd(h, name=f"{tag}_act")
    f = _mm(act, w2, name=f"{tag}_f")
    y, y16 = _ln_fwd(x, f, g, b, scale=FFN_RES, name=f"{tag}_ln")
    return y, y16, (h16, act, f)


def _ffn_bwd(x, x16, w13, w2, g, b, saved, dy, tag):
    h, act, f = saved
    dres, df, dg, db = _ln_bwd(x, f, dy, g, b, scale=FFN_RES, name=f"{tag}_ln_bwd")
    dact = _mm(df, w2, tb=True, name=f"{tag}_dact")
    dw2 = _mm(act.T, df, name=f"{tag}_dw2")
    dh = _swiglu_bwd(h, dact, name=f"{tag}_act_bwd")
    dx = _mm(dh, w13, tb=True, add=dres, name=f"{tag}_dx")
    dw13 = _mm(x16.T, dh, name=f"{tag}_dw13")
    return dx, dw13, dw2, dg[0], db[0]


def _layer_fwd(x0, x0_16, p, bias):
    x1, x1_16, s1 = _ffn_fwd(x0, x0_16, p["ffn1_w13"], p["ffn1_w2"], p["ln1_g"], p["ln1_b"], "ffn1")
    hcat = _mm(x1_16, p["w_in"], name="mix_in")
    pooled = _pool_fwd(hcat, name="pool")
    ya_lin = _mm(pooled, p["pool_bd"], name="pool_lin")
    ya_pre = _affine_fwd(ya_lin, p["pool_b"], p["pool_scale"], name="pool_affine")
    ya = _mm(ya_pre, p["p_pool"], name="pool_out")
    xa = _conv_fwd(hcat, p["conv_w"], p["conv_b"], name="conv")
    dtx = jnp.repeat(hcat[:, DT0:DT0 + SSD_HEADS], 64, axis=1)
    ysc, hs = _ssd_fwd(xa, dtx, p["dt_bias"], p["a_log"], p["d_skip"], name="ssd")
    yb_pre = _gnorm_fwd(ysc, hcat, p["ssd_norm"], name="ssd_norm")
    yb = _mm(yb_pre, p["p_ssd"], name="ssd_out")
    outs, lses = [], []
    for gi, (_, dil) in enumerate(ATTN_CONFIGS):
        q, k, v = (_to_sub(hcat[:, c0 + 256 * gi:c0 + 256 * gi + 256], dil) for c0 in (Q0, K0, V0))
        o, l = _attn_fwd(q, k, v, bias[gi], SEQ // dil // ATTN_BLK, name=f"attn{gi}")
        outs.append(_from_sub(o, dil))
        lses.append(_from_sub(l, dil))
    yc_pre = _amerge_fwd(outs, lses, name="attn_merge")
    yc = _mm(yc_pre, p["p_attn"], name="attn_out")
    merged = _merge_fwd(hcat, ya, yb, yc, p["gate_b"], name="merge")
    mix = _mm(merged, p["w_out"], name="mix_out")
    x2, x2_16 = _ln_fwd(x1, mix, p["ln2_g"], p["ln2_b"], scale=1.0, name="ln2")
    x3, x3_16, s3 = _ffn_fwd(x2, x2_16, p["ffn2_w13"], p["ffn2_w2"], p["ln3_g"], p["ln3_b"], "ffn2")
    saved = dict(x0=x0, x0_16=x0_16, s1=s1, x1=x1, x1_16=x1_16, hcat=hcat, pooled=pooled, ya_lin=ya_lin, ya_pre=ya_pre, ya=ya,
                 xa=xa, hs=hs, ysc=ysc, yb_pre=yb_pre, yb=yb, outs=outs, lses=lses, yc_pre=yc_pre, yc=yc, merged=merged, mix=mix,
                 x2=x2, x2_16=x2_16, s3=s3)
    return x3, x3_16, saved


def _layer_bwd(p, bias, sv, dx3):
    g = {}
    dx2, g["ffn2_w13"], g["ffn2_w2"], g["ln3_g"], g["ln3_b"] = _ffn_bwd(
        sv["x2"], sv["x2_16"], p["ffn2_w13"], p["ffn2_w2"], p["ln3_g"], p["ln3_b"], sv["s3"], dx3, "ffn2")
    hcat = sv["hcat"]
    dres, dmix, dg2, db2 = _ln_bwd(sv["x1"], sv["mix"], dx2, p["ln2_g"], p["ln2_b"], scale=1.0, name="ln2_bwd")
    g["ln2_g"], g["ln2_b"] = dg2[0], db2[0]
    dmerged = _mm(dmix, p["w_out"], tb=True, name="mix_out_dx")
    g["w_out"] = _mm(sv["merged"].T, dmix, name="mix_out_dw")
    dg0, dg1, dg2_, dya, dyb, dyc, gb0, gb1, gb2 = _merge_bwd(hcat, sv["ya"], sv["yb"], sv["yc"], dmerged, p["gate_b"],
                                                               name="merge_bwd")
    g["gate_b"] = jnp.concatenate([gb0, gb1, gb2], axis=0)
    dya_pre = _mm(dya, p["p_pool"], tb=True, name="pool_out_dx")
    g["p_pool"] = _mm(sv["ya_pre"].T, dya, name="pool_out_dw")
    dya_lin, dpb, dps = _affine_bwd(sv["ya_lin"], dya_pre, p["pool_b"], p["pool_scale"], name="pool_affine_bwd")
    g["pool_b"], g["pool_scale"] = dpb[0].reshape(4, POOL_GDIM), dps[0]
    dpooled = _mm(dya_lin, p["pool_bd"], tb=True, name="pool_lin_dx")
    dbd = _mm(sv["pooled"].T, dya_lin, name="pool_lin_dw")
    g["pool_w"] = jnp.stack([dbd[i * POOL_GDIM:(i + 1) * POOL_GDIM, i * POOL_GDIM:(i + 1) * POOL_GDIM] for i in range(4)])
    du = _pool_bwd(dpooled, name="pool_bwd")
    dyb_pre = _mm(dyb, p["p_ssd"], tb=True, name="ssd_out_dx")
    g["p_ssd"] = _mm(sv["yb_pre"].T, dyb, name="ssd_out_dw")
    dysc, dz, dnw = _gnorm_bwd(sv["ysc"], hcat, dyb_pre, p["ssd_norm"], name="ssd_norm_bwd")
    g["ssd_norm"] = dnw[0]
    dtx = jnp.repeat(hcat[:, DT0:DT0 + SSD_HEADS], 64, axis=1)
    dxs, dbm, dcm, ddtx, ddtb, dalog, ddsk = _ssd_bwd(sv["xa"], dtx, p["dt_bias"], p["a_log"], p["d_skip"], sv["hs"], dysc,
                                                     name="ssd_bwd")
    g["dt_bias"], g["a_log"], g["d_skip"] = ddtb[0, ::64], dalog[0, ::64], ddsk[0, ::64]
    dxa = jnp.concatenate([dxs, dbm, dcm], axis=1)
    dxbc, dcw, dcb = _conv_bwd(hcat, p["conv_w"], p["conv_b"], dxa, name="conv_bwd")
    g["conv_w"], g["conv_b"] = dcw, dcb[0]
    ddt = jnp.pad(ddtx[:, ::64], ((0, 0), (0, HC - DT0 - SSD_HEADS))).astype(BF16)
    dyc_pre = _mm(dyc, p["p_attn"], tb=True, name="attn_out_dx")
    g["p_attn"] = _mm(sv["yc_pre"].T, dyc, name="attn_out_dw")
    am = _amerge_bwd(sv["outs"], sv["lses"], dyc_pre, name="attn_merge_bwd")
    dqs, dks, dvs, dbias = [], [], [], []
    for gi, (_, dil) in enumerate(ATTN_CONFIGS):
        q, k, v = (_to_sub(hcat[:, c0 + 256 * gi:c0 + 256 * gi + 256], dil) for c0 in (Q0, K0, V0))
        o, l, do, dl = (_to_sub(t, dil) for t in (sv["outs"][gi], sv["lses"][gi], am[gi], am[3 + gi]))
        dq, dk, dv, dbi = _attn_bwd(q, k, v, o, l, do, dl, bias[gi], SEQ // dil // ATTN_BLK, name=f"attn{gi}_bwd")
        dqs.append(_from_sub(dq, dil).astype(BF16))
        dks.append(_from_sub(dk, dil).astype(BF16))
        dvs.append(_from_sub(dv, dil).astype(BF16))
        dbias.append(dbi)
    g["attn_bias"] = jnp.stack(dbias)
    dhcat = jnp.concatenate([dz, dxbc, dg0, dg1, dg2_, *dqs, *dks, *dvs, du, ddt], axis=1)
    dx1 = _mm(dhcat, p["w_in"], tb=True, add=dres, name="mix_in_dx")
    g["w_in"] = _mm(sv["x1_16"].T, dhcat, name="mix_in_dw")
    dx0, g["ffn1_w13"], g["ffn1_w2"], g["ln1_g"], g["ln1_b"] = _ffn_bwd(
        sv["x0"], sv["x0_16"], p["ffn1_w13"], p["ffn1_w2"], p["ln1_g"], p["ln1_b"], sv["s1"], dx1, "ffn1")
    return dx0, g


def _local_step(x, tgt, lw, rel_bias):
    bias, bias_vjp = jax.vjp(_attn_bias, rel_bias)
    y, y16, saved = x, x.astype(BF16), []
    for p in lw:
        y, y16, sv = _layer_fwd(y, y16, p, bias)
        saved.append(sv)
    dy, sq = _loss_kernel(y, tgt, name="loss")
    grads = [None] * len(lw)
    for l in reversed(range(len(lw))):
        dy, grads[l] = _layer_bwd(lw[l], bias, saved[l], dy)
    (d_rel,) = bias_vjp(sum(g.pop("attn_bias") for g in grads))
    return sq, dy, grads, d_rel


SMALL_REPL = ["ln1_g", "ln1_b", "pool_w", "pool_b", "pool_scale", "conv_b", "dt_bias", "a_log", "d_skip", "ssd_norm",
              "ln2_g", "ln2_b", "ln3_g", "ln3_b", "rel_bias"]
SMALL_SHARD = ["gate_b", "conv_w"]
WEIGHTS = ['ffn1_w13', 'ffn1_w2', 'ln1_g', 'ln1_b', 'w_in', 'gate_b', 'pool_w', 'pool_b', 'pool_scale', 'conv_w', 'conv_b',
           'dt_bias', 'a_log', 'd_skip', 'ssd_norm', 'rel_bias', 'p_pool', 'p_ssd', 'p_attn', 'w_out', 'ln2_g', 'ln2_b',
           'ffn2_w13', 'ffn2_w2', 'ln3_g', 'ln3_b']


def _step(w, m, v, x, tgt):
    dev = 4 * lax.axis_index("x") + 2 * lax.axis_index("y") + lax.axis_index("c")
    gathered = _unpack_gathered(_all_gather(_pack_shards(w, BF16), name="gather_weights"))
    sp_in = _SmallPack([w[n].shape for n in SMALL_SHARD])
    small_g = _all_gather(sp_in.pack([w[n] for n in SMALL_SHARD]), name="gather_small")
    gate_b = jnp.stack([sp_in.unpack(small_g[d])[0] for d in range(NDEV)], axis=2).reshape(DEPTH, 3, D)
    conv_w = jnp.stack([sp_in.unpack(small_g[d])[1] for d in range(NDEV)], axis=2).reshape(DEPTH, 4, 2048)
    lw = []
    for l in range(DEPTH):
        p = dict(gathered[l])
        p["w_in"] = _w_in_to_padded(p["w_in"])
        p["pool_bd"] = _block_diag(w["pool_w"][l]).astype(BF16)
        p["gate_b"], p["conv_w"] = gate_b[l], conv_w[l]
        for n in ("ln1_g", "ln1_b", "ln2_g", "ln2_b", "ln3_g", "ln3_b", "pool_scale", "conv_b", "ssd_norm"):
            p[n] = w[n][l][None, :]
        p["pool_b"] = w["pool_b"][l].reshape(1, POOL_W)
        for n in ("dt_bias", "a_log", "d_skip"):
            p[n] = _expand_heads(w[n][l])
        lw.append(p)
    sq, dx, grads, d_rel = _local_step(x.reshape(T, D), tgt.reshape(T, D), lw, w["rel_bias"])
    loss = lax.psum(jnp.sum(sq) * (0.5 / D), ("x", "y", "c"))
    for g in grads:
        g["w_in"] = _w_in_from_padded(g["w_in"])
    recv = _all_to_all(_pack_full_grads(grads, BF16), name="exchange_grads")
    gbig = _unpack_shards(_sum_slots(recv, name="sum_grads"))
    small_names = SMALL_REPL + SMALL_SHARD
    small = {n: (d_rel if n == "rel_bias" else jnp.stack([g[n] for g in grads])) for n in small_names}
    sp = _SmallPack([small[n].shape for n in small_names])
    gsmall = sp.unpack(_sum_slots(_all_gather(sp.pack([small[n] for n in small_names]), name="gather_small_grads"),
                                  name="sum_small_grads"))
    gout = dict(gbig)
    for n, gv in zip(small_names, gsmall):
        if n in SMALL_SHARD:
            width = w[n].shape[-1]
            gv = lax.dynamic_slice_in_dim(gv, dev * width, width, axis=2)
        gout[n] = gv
    delta, new_m, new_v = {}, {}, {}
    for n, _, _ in BIG:
        shp = w[n].shape
        two_d = lambda a: a.reshape(-1, shp[-1])
        dl, mm_, vv = _adamw(two_d(w[n]), two_d(gout[n]), two_d(m[n]), two_d(v[n]), name=f"adamw_{n}")
        delta[n], new_m[n], new_v[n] = dl.reshape(shp), mm_.reshape(shp), vv.reshape(shp)
    spa = _SmallPack([w[n].shape for n in small_names])
    res = _adamw(spa.pack([w[n] for n in small_names]), spa.pack([gout[n] for n in small_names]),
                 spa.pack([m[n] for n in small_names]), spa.pack([v[n] for n in small_names]), name="adamw_small")
    for out, packed in zip((delta, new_m, new_v), res):
        for n, val in zip(small_names, spa.unpack(packed)):
            out[n] = val
    return (loss, dx.reshape(NB, SEQ, D), *[gout[n] for n in WEIGHTS], *[delta[n] for n in WEIGHTS],
            *[new_m[n] for n in WEIGHTS], *[new_v[n] for n in WEIGHTS])


def kernel(x, ffn1_w13, ffn1_w2, ln1_g, ln1_b, w_in, gate_b, pool_w, pool_b, pool_scale, conv_w, conv_b, dt_bias, a_log, d_skip, ssd_norm, rel_bias, p_pool, p_ssd, p_attn, w_out, ln2_g, ln2_b, ffn2_w13, ffn2_w2, ln3_g, ln3_b, loss_target, m_ffn1_w13, m_ffn1_w2, m_ln1_g, m_ln1_b, m_w_in, m_gate_b, m_pool_w, m_pool_b, m_pool_scale, m_conv_w, m_conv_b, m_dt_bias, m_a_log, m_d_skip, m_ssd_norm, m_rel_bias, m_p_pool, m_p_ssd, m_p_attn, m_w_out, m_ln2_g, m_ln2_b, m_ffn2_w13, m_ffn2_w2, m_ln3_g, m_ln3_b, v_ffn1_w13, v_ffn1_w2, v_ln1_g, v_ln1_b, v_w_in, v_gate_b, v_pool_w, v_pool_b, v_pool_scale, v_conv_w, v_conv_b, v_dt_bias, v_a_log, v_d_skip, v_ssd_norm, v_rel_bias, v_p_pool, v_p_ssd, v_p_attn, v_w_out, v_ln2_g, v_ln2_b, v_ffn2_w13, v_ffn2_w2, v_ln3_g, v_ln3_b):
    given = dict(locals())
    w = {n: given[n] for n in WEIGHTS}
    m = {n: given["m_" + n] for n in WEIGHTS}
    v = {n: given["v_" + n] for n in WEIGHTS}
    return _step(w, m, v, x, loss_target)
```

```python
import functools
import math

import numpy as np
import jax
import jax.numpy as jnp
from jax import lax
from jax.experimental import pallas as pl
from jax.experimental.pallas import tpu as pltpu

F32, BF16 = jnp.float32, jnp.bfloat16
HIGHEST = lax.Precision.HIGHEST

NDEV = 8
DEPTH = 4
D = 1024
SEQ = 2048
NB = 2
T = NB * SEQ
DFF = 2816
POOL_W = 768
POOL_WINDOWS = (2, 4, 8, 16)
POOL_GDIM = 192
SSD_HEADS = 16
CHUNK = 128
NCHUNK = SEQ // CHUNK
ATTN_CONFIGS = ((128, 1), (512, 4), (2048, 16))
ATTN_BLK = 128
REL_BUCKETS = 32
REL_MAX_DIST = 2048
LN_EPS = 1e-5
SSD_EPS = 1e-5
ALPHA = (2.0 * DEPTH) ** 0.25
FFN_RES = 0.5
NEG = -1e30

ADAM_LR, ADAM_B1, ADAM_B2, ADAM_EPS, ADAM_WD, ADAM_STEP = 0.001, 0.9, 0.999, 1e-08, 0.01, 10

Z0, XBC0, GATE0, Q0, K0, V0, U0, DT0, HC = 0, 1024, 3072, 6144, 6912, 7680, 8448, 9216, 9728
O_U, O_Z, O_XBC, O_DT, O_Q, O_K, O_V, O_G, O_END = 0, 768, 1792, 3840, 3856, 4624, 5392, 6160, 9232

VMEM_LIMIT = 56 * 1024 * 1024


def _cp(sem):
    return pltpu.CompilerParams(dimension_semantics=sem, vmem_limit_bytes=VMEM_LIMIT)


def _pick(dim, cands):
    for c in cands:
        if dim % c == 0:
            return c
    return dim


def _mm(a, b, *, tb=False, out_dtype=F32, add=None, add_scale=1.0, out_blocked=False, bn=None, name):
    bk = None
    if a.ndim == 3:
        nkb, M, bk = a.shape
        K = nkb * bk
    else:
        M, K = a.shape
    if b.ndim == 3 and tb:
        assert bk in (None, b.shape[2])
        nkb, N, bk = b.shape
        assert K == nkb * bk, (a.shape, b.shape)
    elif b.ndim == 3:
        nnb, K2, bn = b.shape
        N = nnb * bn
        assert K == K2, (a.shape, b.shape)
    else:
        (N, K2) = b.shape if tb else b.shape[::-1]
        assert K == K2, (a.shape, b.shape, tb)
    bm = _pick(M, (1024, 768, 512, 256))
    if bn is None:
        bn = _pick(N, (1024, 768, 512, 256))
    if bk is None:
        bk = K if K <= 1024 else _pick(K, (1024, 1408, 2432, 512, 256))
    nk = K // bk
    dn = (((1,), (1 if tb else 0,)), ((), ()))

    def kern(*refs):
        a_ref, b_ref = refs[0], refs[1]
        add_ref = refs[2] if add is not None else None
        o_ref = refs[3] if add is not None else refs[2]
        p = lax.dot_general(a_ref[...].astype(BF16), b_ref[...].astype(BF16), dn, preferred_element_type=F32)

        def fin(acc):
            if add_ref is not None:
                acc = acc + add_scale * add_ref[...]
            o_ref[...] = acc.astype(out_dtype)

        if nk == 1:
            fin(p)
        else:
            acc_ref = refs[-1]
            k = pl.program_id(2)

            @pl.when(k == 0)
            def _():
                acc_ref[...] = p

            @pl.when(k > 0)
            def _():
                acc_ref[...] += p

            @pl.when(k == nk - 1)
            def _():
                fin(acc_ref[...])

    if a.ndim == 3:
        a_spec = pl.BlockSpec((None, bm, bk), lambda i, j, k: (k, i, 0))
    else:
        a_spec = pl.BlockSpec((bm, bk), lambda i, j, k: (i, k))
    if b.ndim == 3 and tb:
        b_spec = pl.BlockSpec((None, bn, bk), lambda i, j, k: (k, j, 0))
    elif b.ndim == 3:
        b_spec = pl.BlockSpec((None, bk, bn), lambda i, j, k: (j, k, 0))
    elif tb:
        b_spec = pl.BlockSpec((bn, bk), lambda i, j, k: (j, k))
    else:
        b_spec = pl.BlockSpec((bk, bn), lambda i, j, k: (k, j))
    if out_blocked:
        o_spec, o_shape = pl.BlockSpec((None, bm, bn), lambda i, j, k: (j, i, 0)), (N // bn, M, bn)
    else:
        o_spec, o_shape = pl.BlockSpec((bm, bn), lambda i, j, k: (i, j)), (M, N)
    in_specs = [a_spec, b_spec]
    args = [a, b]
    if add is not None:
        assert add.shape == o_shape
        in_specs.append(o_spec)
        args.append(add)
    return pl.pallas_call(
        kern, name=name, grid=(M // bm, N // bn, nk),
        in_specs=in_specs, out_specs=o_spec,
        out_shape=jax.ShapeDtypeStruct(o_shape, out_dtype),
        scratch_shapes=[pltpu.VMEM((bm, bn), F32)] if nk > 1 else [],
        compiler_params=_cp(("parallel", "parallel", "arbitrary")),
    )(*args)


def _rowwise(fn, rows, pars, outs, accs, *, name, tile, groups=1):
    n_rows = rows[0][0].shape[0]
    nt = n_rows // tile
    n_in = len(rows) + len(pars)
    n_out = len(outs)

    def kern(*refs):
        res = fn(*[r[...] for r in refs[:n_in]])
        for o_ref, val in zip(refs[n_in:n_in + n_out], res[:n_out]):
            o_ref[...] = val.astype(o_ref.dtype)
        i = pl.program_id(1)
        for a_ref, val in zip(refs[n_in + n_out:], res[n_out:]):
            @pl.when(i == 0)
            def _(a_ref=a_ref, val=val):
                a_ref[...] = val

            @pl.when(i > 0)
            def _(a_ref=a_ref, val=val):
                a_ref[...] += val

    in_specs = [pl.BlockSpec((tile, w), lambda g, i, c0=c0: (i, c0 + g)) for (_, w, c0) in rows]
    for p, w in pars:
        if w is None:
            in_specs.append(pl.BlockSpec(p.shape, lambda g, i: (0, 0)))
        else:
            in_specs.append(pl.BlockSpec((p.shape[0], w), lambda g, i: (0, g)))
    out_specs = [pl.BlockSpec((tile, w), lambda g, i: (i, g)) for (_, w, _) in outs]
    out_specs += [pl.BlockSpec((1, w), lambda g, i: (0, g)) for (_, w) in accs]
    out_shape = [jax.ShapeDtypeStruct((n_rows, c), dt) for (c, _, dt) in outs]
    out_shape += [jax.ShapeDtypeStruct((1, c), F32) for (c, _) in accs]
    return pl.pallas_call(
        kern, name=name, grid=(groups, nt), in_specs=in_specs, out_specs=out_specs, out_shape=out_shape,
        compiler_params=_cp(("arbitrary", "arbitrary")),
    )(*[r[0] for r in rows], *[p[0] for p in pars])


def _colsum(v):
    return jnp.sum(v, axis=0, keepdims=True)


def _silu(v):
    return v * jax.nn.sigmoid(v)


def _ln_fn(x, f, g, b, *, scale):
    pre = ALPHA * x + scale * f
    mu = jnp.mean(pre, axis=-1, keepdims=True)
    var = jnp.mean(jnp.square(pre - mu), axis=-1, keepdims=True)
    return (pre - mu) * lax.rsqrt(var + LN_EPS) * g + b


def _ln_fwd(x, f, g, b, *, scale, name):
    fn = lambda x_, f_, g_, b_: [_ln_fn(x_, f_, g_, b_, scale=scale)] * 2
    return _rowwise(fn, [(x, D, 0), (f, D, 0)], [(g, None), (b, None)], [(D, D, F32), (D, D, BF16)], [], name=name, tile=512)


def _ln_bwd(x, f, dy, g, b, *, scale, name):
    def fn(x_, f_, dy_, g_, b_):
        _, vjp = jax.vjp(functools.partial(_ln_fn, scale=scale), x_, f_, g_, b_)
        return list(vjp(dy_))

    return _rowwise(fn, [(x, D, 0), (f, D, 0), (dy, D, 0)], [(g, None), (b, None)],
                    [(D, D, F32), (D, D, BF16)], [(D, D), (D, D)], name=name, tile=512)


FB = 2 * DFF // NDEV
SW_TILE = 512


def _swiglu_specs():
    pair = pl.BlockSpec((2, None, SW_TILE, FB), lambda j, i: (0, j, i, 0))
    one = pl.BlockSpec((None, SW_TILE, FB), lambda j, i: (j, i, 0))
    return pair, one


def _swiglu_fwd(h, *, name):
    def kern(h_ref, act_ref, h16_ref):
        act_ref[...] = (_silu(h_ref[0]) * h_ref[1]).astype(BF16)
        h16_ref[...] = h_ref[...].astype(BF16)

    pair, one = _swiglu_specs()
    act, h16 = pl.pallas_call(
        kern, name=name, grid=(4, T // SW_TILE), in_specs=[pair], out_specs=[one, pair],
        out_shape=[jax.ShapeDtypeStruct((4, T, FB), BF16), jax.ShapeDtypeStruct((2, 4, T, FB), BF16)],
        compiler_params=_cp(("parallel", "parallel")),
    )(h.reshape(2, 4, T, FB))
    return act, h16.reshape(NDEV, T, FB)


def _swiglu_bwd(h16, dact, *, name):
    def kern(h_ref, da_ref, dh_ref):
        a, g, da = h_ref[0].astype(F32), h_ref[1].astype(F32), da_ref[...]
        s = jax.nn.sigmoid(a)
        dh_ref[0] = (da * g * (s * (1.0 + a * (1.0 - s)))).astype(BF16)
        dh_ref[1] = (da * (a * s)).astype(BF16)

    pair, one = _swiglu_specs()
    return pl.pallas_call(
        kern, name=name, grid=(4, T // SW_TILE), in_specs=[pair, one], out_specs=pair,
        out_shape=jax.ShapeDtypeStruct((2, 4, T, FB), BF16), compiler_params=_cp(("parallel", "parallel")),
    )(h16.reshape(2, 4, T, FB), dact).reshape(NDEV, T, FB)


def _affine_fn(y, b, s):
    return (y + b) * s


def _affine_fwd(y, b, s, *, name):
    return _rowwise(lambda y_, b_, s_: [_affine_fn(y_, b_, s_)], [(y, POOL_W, 0)], [(b, None), (s, None)],
                    [(POOL_W, POOL_W, BF16)], [], name=name, tile=512)[0]


def _affine_bwd(y, dyo, b, s, *, name):
    def fn(y_, d_, b_, s_):
        _, vjp = jax.vjp(_affine_fn, y_, b_, s_)
        return list(vjp(d_))

    return _rowwise(fn, [(y, POOL_W, 0), (dyo, POOL_W, 0)], [(b, None), (s, None)],
                    [(POOL_W, POOL_W, BF16)], [(POOL_W, POOL_W), (POOL_W, POOL_W)], name=name, tile=512)


def _gnorm_fn(y, z, w):
    yz = y * _silu(z)
    return yz * lax.rsqrt(jnp.mean(jnp.square(yz), axis=-1, keepdims=True) + SSD_EPS) * w


def _gnorm_fwd(y, hcat, w, *, name):
    return _rowwise(lambda y_, z_, w_: [_gnorm_fn(y_, z_, w_)], [(y, 256, 0), (hcat, 256, Z0 // 256)], [(w, 256)],
                    [(D, 256, BF16)], [], name=name, tile=512, groups=4)[0]


def _gnorm_bwd(y, hcat, dyo, w, *, name):
    def fn(y_, z_, d_, w_):
        _, vjp = jax.vjp(_gnorm_fn, y_, z_, w_)
        return list(vjp(d_))

    return _rowwise(fn, [(y, 256, 0), (hcat, 256, Z0 // 256), (dyo, 256, 0)], [(w, 256)],
                    [(D, 256, F32), (D, 256, BF16)], [(D, 256)], name=name, tile=512, groups=4)


def _merge_fn(g0, g1, g2, ya, yb, yc, b0, b1, b2):
    return jax.nn.sigmoid(g0 + b0) * ya + jax.nn.sigmoid(g1 + b1) * yb + jax.nn.sigmoid(g2 + b2) * yc


def _merge_rows(hcat, ya, yb, yc):
    c = GATE0 // 256
    return [(hcat, 256, c), (hcat, 256, c + 4), (hcat, 256, c + 8), (ya, 256, 0), (yb, 256, 0), (yc, 256, 0)]


def _merge_fwd(hcat, ya, yb, yc, gb, *, name):
    pars = [(gb[0:1], 256), (gb[1:2], 256), (gb[2:3], 256)]
    return _rowwise(lambda *v: [_merge_fn(*v)], _merge_rows(hcat, ya, yb, yc), pars, [(D, 256, BF16)], [],
                    name=name, tile=512, groups=4)[0]


def _merge_bwd(hcat, ya, yb, yc, dm, gb, *, name):
    def fn(g0, g1, g2, ya_, yb_, yc_, dm_, b0, b1, b2):
        _, vjp = jax.vjp(_merge_fn, g0, g1, g2, ya_, yb_, yc_, b0, b1, b2)
        return list(vjp(dm_))

    pars = [(gb[0:1], 256), (gb[1:2], 256), (gb[2:3], 256)]
    return _rowwise(fn, _merge_rows(hcat, ya, yb, yc) + [(dm, 256, 0)], pars,
                    [(D, 256, BF16)] * 6, [(D, 256)] * 3, name=name, tile=512, groups=4)


def _amerge_fn(o0, o1, o2, l0, l1, l2):
    m = jnp.maximum(jnp.maximum(l0, l1), l2)
    e0, e1, e2 = jnp.exp(l0 - m), jnp.exp(l1 - m), jnp.exp(l2 - m)
    return (e0 * o0 + e1 * o1 + e2 * o2) / (e0 + e1 + e2)


def _amerge_fwd(os_, ls_, *, name):
    rows = [(v, 256, 0) for v in (*os_, *ls_)]
    return _rowwise(lambda *v: [_amerge_fn(*v)], rows, [], [(256, 256, BF16)], [], name=name, tile=1024)[0]


def _amerge_bwd(os_, ls_, dy, *, name):
    def fn(*v):
        _, vjp = jax.vjp(_amerge_fn, *v[:6])
        return list(vjp(v[6]))

    rows = [(v, 256, 0) for v in (*os_, *ls_, dy)]
    return _rowwise(fn, rows, [], [(256, 256, F32)] * 6, [], name=name, tile=1024)


def _loss_kernel(y, tgt, *, name):
    def fn(y_, t_):
        e = y_ - t_
        return [e * (1.0 / D), _colsum(e * e)]

    return _rowwise(fn, [(y, D, 0), (tgt, D, 0)], [], [(D, D, F32)], [(D, D)], name=name, tile=512)


def _adam_math(w, g, m, v):
    m2 = ADAM_B1 * m + (1.0 - ADAM_B1) * g
    v2 = ADAM_B2 * v + (1.0 - ADAM_B2) * jnp.square(g)
    m_hat = m2 / (1.0 - ADAM_B1 ** ADAM_STEP)
    v_hat = v2 / (1.0 - ADAM_B2 ** ADAM_STEP)
    return -ADAM_LR * (m_hat / (jnp.sqrt(v_hat) + ADAM_EPS) + ADAM_WD * w), m2, v2


def _adamw(w, g, m, v, *, name):
    r, c = w.shape
    tile = _pick(r, (512, 256, 128, 64, 32, 16, 8))
    fn = lambda *a: list(_adam_math(*a))
    return _rowwise(fn, [(w, c, 0), (g, c, 0), (m, c, 0), (v, c, 0)], [], [(c, c, F32)] * 3, [], name=name, tile=tile)


def _pool_lane_window(j, width):
    lane = lax.broadcasted_iota(jnp.int32, (1, width), 1) + j * width
    grp = lane // POOL_GDIM
    return grp


def _pool_select(grp, vals):
    out = vals[3]
    for gi in (2, 1, 0):
        out = jnp.where(grp == gi, vals[gi], out)
    return out


def _pool_fwd(hcat, *, name):
    width = 256

    def kern(u_ref, o_ref):
        u = u_ref[...]
        t = lax.broadcasted_iota(jnp.int32, (SEQ, 1), 0)
        grp = _pool_lane_window(pl.program_id(1), width)

        def shift(v, k):
            return jnp.where(t >= k, pltpu.roll(v, k, 0), 0.0)

        s2 = u + shift(u, 1)
        s4 = s2 + shift(s2, 2)
        s8 = s4 + shift(s4, 4)
        s16 = s8 + shift(s8, 8)
        win = _pool_select(grp, [jnp.full((1, width), float(w), F32) for w in POOL_WINDOWS])
        cnt = jnp.minimum((t + 1).astype(F32), win)
        o_ref[...] = (_pool_select(grp, [s2, s4, s8, s16]) / cnt - u).astype(o_ref.dtype)

    return pl.pallas_call(
        kern, name=name, grid=(NB, POOL_W // width),
        in_specs=[pl.BlockSpec((SEQ, width), lambda b, j: (b, U0 // width + j))],
        out_specs=pl.BlockSpec((SEQ, width), lambda b, j: (b, j)),
        out_shape=jax.ShapeDtypeStruct((T, POOL_W), BF16), compiler_params=_cp(("parallel", "parallel")),
    )(hcat)


def _pool_bwd(dp, *, name):
    width = 256

    def kern(d_ref, o_ref):
        d = d_ref[...]
        t = lax.broadcasted_iota(jnp.int32, (SEQ, 1), 0)
        grp = _pool_lane_window(pl.program_id(1), width)
        win = _pool_select(grp, [jnp.full((1, width), float(w), F32) for w in POOL_WINDOWS])
        dm = d / jnp.minimum((t + 1).astype(F32), win)

        def shift(v, k):
            return jnp.where(t < SEQ - k, pltpu.roll(v, SEQ - k, 0), 0.0)

        r2 = dm + shift(dm, 1)
        r4 = r2 + shift(r2, 2)
        r8 = r4 + shift(r4, 4)
        r16 = r8 + shift(r8, 8)
        o_ref[...] = (_pool_select(grp, [r2, r4, r8, r16]) - d).astype(o_ref.dtype)

    return pl.pallas_call(
        kern, name=name, grid=(NB, POOL_W // width),
        in_specs=[pl.BlockSpec((SEQ, width), lambda b, j: (b, j))],
        out_specs=pl.BlockSpec((SEQ, width), lambda b, j: (b, j)),
        out_shape=jax.ShapeDtypeStruct((T, POOL_W), BF16), compiler_params=_cp(("parallel", "parallel")),
    )(dp)


CONV_W = 512


def _conv_pre(x, w, b, t):
    pre = w[3:4] * x + b
    for k in (1, 2, 3):
        pre = pre + w[3 - k:4 - k] * jnp.where(t >= k, pltpu.roll(x, k, 0), 0.0)
    return pre


def _conv_fwd(hcat, w, b, *, name):
    def kern(x_ref, w_ref, b_ref, o_ref):
        t = lax.broadcasted_iota(jnp.int32, (SEQ, 1), 0)
        o_ref[...] = _silu(_conv_pre(x_ref[...], w_ref[...], b_ref[...], t))

    return pl.pallas_call(
        kern, name=name, grid=(NB, 2048 // CONV_W),
        in_specs=[pl.BlockSpec((SEQ, CONV_W), lambda s, j: (s, XBC0 // CONV_W + j)),
                  pl.BlockSpec((4, CONV_W), lambda s, j: (0, j)), pl.BlockSpec((1, CONV_W), lambda s, j: (0, j))],
        out_specs=pl.BlockSpec((SEQ, CONV_W), lambda s, j: (s, j)),
        out_shape=jax.ShapeDtypeStruct((T, 2048), F32), compiler_params=_cp(("parallel", "parallel")),
    )(hcat, w, b)


def _conv_bwd(hcat, w, b, dy, *, name):
    def kern(x_ref, w_ref, b_ref, dy_ref, dx_ref, dw_ref, db_ref):
        t = lax.broadcasted_iota(jnp.int32, (SEQ, 1), 0)
        x, w_ = x_ref[...], w_ref[...]
        pre = _conv_pre(x, w_, b_ref[...], t)
        s = jax.nn.sigmoid(pre)
        dpre = dy_ref[...] * (s * (1.0 + pre * (1.0 - s)))
        dx = w_[3:4] * dpre
        dws = [None] * 4
        dws[3] = _colsum(dpre * x)
        for k in (1, 2, 3):
            dx = dx + w_[3 - k:4 - k] * jnp.where(t < SEQ - k, pltpu.roll(dpre, SEQ - k, 0), 0.0)
            dws[3 - k] = _colsum(dpre * jnp.where(t >= k, pltpu.roll(x, k, 0), 0.0))
        dx_ref[...] = dx.astype(dx_ref.dtype)
        db = _colsum(dpre)
        first = pl.program_id(1) == 0

        @pl.when(first)
        def _():
            for k in range(4):
                dw_ref[k:k + 1, :] = dws[k]
            db_ref[...] = db

        @pl.when(jnp.logical_not(first))
        def _():
            for k in range(4):
                dw_ref[k:k + 1, :] += dws[k]
            db_ref[...] += db

    return pl.pallas_call(
        kern, name=name, grid=(2048 // CONV_W, NB),
        in_specs=[pl.BlockSpec((SEQ, CONV_W), lambda j, s: (s, XBC0 // CONV_W + j)),
                  pl.BlockSpec((4, CONV_W), lambda j, s: (0, j)), pl.BlockSpec((1, CONV_W), lambda j, s: (0, j)),
                  pl.BlockSpec((SEQ, CONV_W), lambda j, s: (s, j))],
        out_specs=[pl.BlockSpec((SEQ, CONV_W), lambda j, s: (s, j)), pl.BlockSpec((4, CONV_W), lambda j, s: (0, j)),
                   pl.BlockSpec((1, CONV_W), lambda j, s: (0, j))],
        out_shape=[jax.ShapeDtypeStruct((T, 2048), BF16), jax.ShapeDtypeStruct((4, 2048), F32),
                   jax.ShapeDtypeStruct((1, 2048), F32)],
        compiler_params=_cp(("arbitrary", "arbitrary")),
    )(hcat, w, b, dy)


def _softplus(v):
    return jnp.maximum(v, 0.0) + jnp.log1p(jnp.exp(-jnp.abs(v)))


def _dot(a, b, dims):
    return lax.dot_general(a, b, (dims, ((), ())), preferred_element_type=F32)


_NN, _NT, _TN = ((1,), (0,)), ((1,), (1,)), ((0,), (0,))


def _ssd_specs(order):
    def spec(width, col0):
        return pl.BlockSpec((SEQ, width), lambda i, j, c0=col0: (order(i, j)[0], c0 + order(i, j)[1]))

    def par():
        return pl.BlockSpec((1, 256), lambda i, j: (0, order(i, j)[1]))

    return spec, par


def _ssd_chunk_common(c, xs_ref, b_ref, c_ref, dtx_ref, dtb, a, trif):
    r0 = pl.multiple_of(c * CHUNK, CHUNK)
    rows = pl.ds(r0, CHUNK)
    x = xs_ref[rows, :]
    bb = b_ref[rows, :].astype(BF16)
    cb = c_ref[rows, :].astype(BF16)
    raw = dtx_ref[rows, :] + dtb
    dt = _softplus(raw)
    cs = jnp.dot(trif, dt * a, precision=HIGHEST, preferred_element_type=F32)
    return rows, x, bb, cb, raw, dt, cs


def _head_decay(cs, me, tri):
    cse = jnp.max(jnp.where(me, cs, -jnp.inf), axis=1, keepdims=True)
    csb = jnp.broadcast_to(cse, (CHUNK, CHUNK))
    return jnp.where(tri, jnp.exp(csb - csb.T), 0.0)


def _ssd_fwd(xa, dtx, dtb, alog, dsk, *, name):
    spec, par = _ssd_specs(lambda g, b: (b, g))

    def kern(xs_ref, b_ref, c_ref, dtx_ref, dtb_ref, alog_ref, dsk_ref, y_ref, hs_ref, h_scr):
        row = lax.broadcasted_iota(jnp.int32, (CHUNK, CHUNK), 0)
        col = lax.broadcasted_iota(jnp.int32, (CHUNK, CHUNK), 1)
        tri = row >= col
        trif = tri.astype(F32)
        head = lax.broadcasted_iota(jnp.int32, (1, 256), 1) // 64
        dtb_, a, dsk_ = dtb_ref[...], -jnp.exp(alog_ref[...]), dsk_ref[...]
        h_scr[...] = jnp.zeros_like(h_scr)

        def chunk(c, carry):
            rows, x, bb, cb, _, dt, cs = _ssd_chunk_common(c, xs_ref, b_ref, c_ref, dtx_ref, dtb_, a, trif)
            cs_last = cs[CHUNK - 1:CHUNK, :]
            xdt = x * dt
            xdtb = xdt.astype(BF16)
            g = _dot(cb, bb, _NT)
            hin = h_scr[...]
            hs_ref[rows, :] = hin
            y = jnp.exp(cs) * _dot(cb, hin.astype(BF16), _NN) + dsk_ * x
            for e in range(4):
                me = head == e
                m = (g * _head_decay(cs, me, tri)).astype(BF16)
                y = y + jnp.where(me, _dot(m, xdtb, _NN), 0.0)
            y_ref[rows, :] = y
            st = _dot(bb, (jnp.exp(cs_last - cs) * xdt).astype(BF16), _TN)
            h_scr[...] = hin * jnp.exp(cs_last) + st
            return carry

        lax.fori_loop(0, NCHUNK, chunk, 0)

    return pl.pallas_call(
        kern, name=name, grid=(4, NB),
        in_specs=[spec(256, 0), spec(128, 8), spec(128, 12), spec(256, 0), par(), par(), par()],
        out_specs=[spec(256, 0), spec(256, 0)],
        out_shape=[jax.ShapeDtypeStruct((T, D), F32), jax.ShapeDtypeStruct((T, D), F32)],
        scratch_shapes=[pltpu.VMEM((CHUNK, 256), F32)],
        compiler_params=_cp(("parallel", "parallel")),
    )(xa, xa, xa, dtx, dtb, alog, dsk)


def _ssd_bwd(xa, dtx, dtb, alog, dsk, hs, dy, *, name):
    spec, par = _ssd_specs(lambda g, b: (b, g))

    def kern(xs_ref, b_ref, c_ref, dtx_ref, dtb_ref, alog_ref, dsk_ref, hs_ref, dy_ref,
             dx_ref, db_ref, dc_ref, ddt_ref, ddtb_ref, dalog_ref, ddsk_ref, dh_scr):
        row = lax.broadcasted_iota(jnp.int32, (CHUNK, CHUNK), 0)
        col = lax.broadcasted_iota(jnp.int32, (CHUNK, CHUNK), 1)
        tri = row >= col
        trif = tri.astype(F32)
        trit = (row <= col).astype(F32)
        is_last = lax.broadcasted_iota(jnp.int32, (CHUNK, 1), 0) == CHUNK - 1
        head = lax.broadcasted_iota(jnp.int32, (1, 256), 1) // 64
        dtb_, a, dsk_ = dtb_ref[...], -jnp.exp(alog_ref[...]), dsk_ref[...]
        dh_scr[...] = jnp.zeros_like(dh_scr)

        @pl.when(pl.program_id(1) == 0)
        def _():
            ddtb_ref[...] = jnp.zeros_like(ddtb_ref)
            dalog_ref[...] = jnp.zeros_like(dalog_ref)
            ddsk_ref[...] = jnp.zeros_like(ddsk_ref)

        def hsum(v, me):
            return jnp.sum(jnp.where(me, v, 0.0), axis=1, keepdims=True)

        def chunk(ci, carry):
            c = NCHUNK - 1 - ci
            rows, x, bb, cb, raw, dt, cs = _ssd_chunk_common(c, xs_ref, b_ref, c_ref, dtx_ref, dtb_, a, trif)
            cs_last = cs[CHUNK - 1:CHUNK, :]
            ecs = jnp.exp(cs)
            dsx = jnp.exp(cs_last - cs)
            xdt = x * dt
            xdtb = xdt.astype(BF16)
            g = _dot(cb, bb, _NT)
            hin = hs_ref[rows, :]
            hinb = hin.astype(BF16)
            dy_ = dy_ref[rows, :]
            dyb = dy_.astype(BF16)
            dh = dh_scr[...]
            dhb = dh.astype(BF16)
            bdh = _dot(bb, dhb, _NN)
            yoff = ecs * _dot(cb, hinb, _NN)
            dxdt = dsx * bdh
            t1, t2, t3 = dy_ * yoff, xdt * bdh, dh * hin
            dg = jnp.zeros((CHUNK, CHUNK), F32)
            dcs = jnp.zeros((CHUNK, 256), F32)
            for e in range(4):
                me = head == e
                l_ = _head_decay(cs, me, tri)
                m = g * l_
                dxdt = dxdt + jnp.where(me, _dot(m.astype(BF16), dyb, _TN), 0.0)
                dm = _dot(jnp.where(me, dy_, 0.0).astype(BF16), xdtb, _NT)
                dg = dg + dm * l_
                w = dm * m
                dds = hsum(t2, me)
                dse = jnp.max(jnp.where(me, dsx, -jnp.inf), axis=1, keepdims=True)
                ecl = jnp.max(jnp.where(me, jnp.exp(cs_last), -jnp.inf), axis=1, keepdims=True)
                dcs_e = hsum(t1, me) + jnp.sum(w, axis=1, keepdims=True) - jnp.sum(w.T, axis=1, keepdims=True) - dds * dse
                last = jnp.sum(dds * dse, axis=0, keepdims=True) + ecl * jnp.sum(hsum(t3, me), axis=0, keepdims=True)
                dcs_e = dcs_e + jnp.where(is_last, last, 0.0)
                dcs = dcs + jnp.where(me, dcs_e, 0.0)
            dadt = jnp.dot(trit, dcs, precision=HIGHEST, preferred_element_type=F32)
            ddt = a * dadt
            dxx = dxdt * x
            dyx = dy_ * x
            dsk_acc = jnp.zeros((1, 256), F32)
            for e in range(4):
                me = head == e
                ddt = ddt + jnp.where(me, hsum(dxx, me), 0.0)
                dsk_acc = dsk_acc + jnp.where(me, jnp.sum(hsum(dyx, me), axis=0, keepdims=True), 0.0)
            draw = ddt * jax.nn.sigmoid(raw)
            ddt_ref[rows, :] = draw
            ddtb_ref[...] += _colsum(draw)
            dalog_ref[...] += _colsum(dadt * dt) * a
            ddsk_ref[...] += dsk_acc
            dx_ref[rows, :] = dxdt * dt + dsk_ * dy_
            edy = (ecs * dy_).astype(BF16)
            dgb = dg.astype(BF16)
            dc_ref[rows, :] = _dot(dgb, bb, _NN) + _dot(edy, hinb, _NT)
            db_ref[rows, :] = _dot(dgb, cb, _TN) + _dot((dsx * xdt).astype(BF16), dhb, _NT)
            dh_scr[...] = jnp.exp(cs_last) * dh + _dot(cb, edy, _TN)
            return carry

        lax.fori_loop(0, NCHUNK, chunk, 0)

    return pl.pallas_call(
        kern, name=name, grid=(4, NB),
        in_specs=[spec(256, 0), spec(128, 8), spec(128, 12), spec(256, 0), par(), par(), par(), spec(256, 0), spec(256, 0)],
        out_specs=[spec(256, 0), spec(128, 0), spec(128, 0), spec(256, 0), par(), par(), par()],
        out_shape=[jax.ShapeDtypeStruct((T, D), F32), jax.ShapeDtypeStruct((T, 512), F32), jax.ShapeDtypeStruct((T, 512), F32),
                   jax.ShapeDtypeStruct((T, D), F32), jax.ShapeDtypeStruct((1, D), F32), jax.ShapeDtypeStruct((1, D), F32),
                   jax.ShapeDtypeStruct((1, D), F32)],
        scratch_shapes=[pltpu.VMEM((CHUNK, 256), F32)],
        compiler_params=_cp(("arbitrary", "arbitrary")),
    )(xa, xa, xa, dtx, dtb, alog, dsk, hs, dy)


def _t5_bucket_np(dist):
    dist = np.maximum(dist, 0)
    max_exact = REL_BUCKETS // 2
    large = max_exact + (np.log(np.maximum(dist, 1) / max_exact) / np.log(REL_MAX_DIST / max_exact)
                         * (REL_BUCKETS - max_exact)).astype(np.int32)
    large = np.minimum(large, REL_BUCKETS - 1)
    return np.where(dist < max_exact, dist, large).astype(np.int32)


def _attn_bias(rel_bias):
    qi = np.arange(ATTN_BLK)[:, None]
    kj = np.arange(2 * ATTN_BLK)[None, :]
    delta = qi - kj + ATTN_BLK
    out = []
    for gi, (window, dil) in enumerate(ATTN_CONFIGS):
        in_band = (delta >= 0) & (delta <= window // dil)
        bucket = jnp.asarray(_t5_bucket_np(delta * dil).reshape(-1, 1))
        one_hot = (bucket == jnp.arange(REL_BUCKETS)[None, :]).astype(F32)
        tab = jnp.dot(one_hot, rel_bias[:, 4 * gi:4 * gi + 4], precision=HIGHEST).reshape(ATTN_BLK, 2 * ATTN_BLK, 4)
        out.append(jnp.where(jnp.asarray(in_band)[None], tab.transpose(2, 0, 1), NEG))
    return jnp.stack(out)


def _attn_logits(q_ref, k_ref, bias_ref, n, sub_blocks, cur, prev):
    qh = q_ref[cur, :].astype(BF16)
    kband = jnp.concatenate([k_ref[prev, :], k_ref[cur, :]], axis=0).astype(BF16)
    kj = lax.broadcasted_iota(jnp.int32, (1, 2 * ATTN_BLK), 1)
    ok = jnp.logical_or(n % sub_blocks > 0, kj >= ATTN_BLK)
    return jnp.where(ok, _dot(qh, kband, _NT) * 0.125 + bias_ref[...], NEG), kband


def _attn_specs(L):
    seq = pl.BlockSpec((None, None, L, 64), lambda h, s: (s, h, 0, 0))
    tab = pl.BlockSpec((None, ATTN_BLK, 2 * ATTN_BLK), lambda h, s: (h, 0, 0))
    return seq, tab


def _attn_rows(n):
    cur = pl.ds(pl.multiple_of(n * ATTN_BLK, ATTN_BLK), ATTN_BLK)
    prev = pl.ds(pl.multiple_of(jnp.maximum(n - 1, 0) * ATTN_BLK, ATTN_BLK), ATTN_BLK)
    return cur, prev


def _attn_fwd(q, k, v, bias, sub_blocks, *, name):
    nsub, _, L, _ = q.shape
    nblk = L // ATTN_BLK

    def kern(q_ref, k_ref, v_ref, bias_ref, o_ref, l_ref):
        def blk(n, carry):
            cur, prev = _attn_rows(n)
            logits, _ = _attn_logits(q_ref, k_ref, bias_ref, n, sub_blocks, cur, prev)
            vband = jnp.concatenate([v_ref[prev, :], v_ref[cur, :]], axis=0).astype(BF16)
            m = jnp.max(logits, axis=-1, keepdims=True)
            p = jnp.exp(logits - m)
            den = jnp.sum(p, axis=-1, keepdims=True)
            o_ref[cur, :] = _dot((p / den).astype(BF16), vband, _NN)
            l_ref[cur, :] = jnp.broadcast_to(m + jnp.log(den), (ATTN_BLK, 64))
            return carry

        lax.fori_loop(0, nblk, blk, 0)

    seq, tab = _attn_specs(L)
    return pl.pallas_call(
        kern, name=name, grid=(4, nsub), in_specs=[seq, seq, seq, tab],
        out_specs=[seq, seq], out_shape=[jax.ShapeDtypeStruct(q.shape, F32)] * 2,
        compiler_params=_cp(("parallel", "parallel")),
    )(q, k, v, bias)


def _attn_bwd(q, k, v, o, lse, do, dl, bias, sub_blocks, *, name):
    nsub, _, L, _ = q.shape
    nblk = L // ATTN_BLK

    def kern(q_ref, k_ref, v_ref, o_ref, l_ref, do_ref, dl_ref, bias_ref, dq_ref, dk_ref, dv_ref, dbias_ref):
        @pl.when(pl.program_id(1) == 0)
        def _():
            dbias_ref[...] = jnp.zeros_like(dbias_ref)

        dk_ref[...] = jnp.zeros_like(dk_ref)
        dv_ref[...] = jnp.zeros_like(dv_ref)

        def blk(n, carry):
            cur, prev = _attn_rows(n)
            logits, kband = _attn_logits(q_ref, k_ref, bias_ref, n, sub_blocks, cur, prev)
            vband = jnp.concatenate([v_ref[prev, :], v_ref[cur, :]], axis=0).astype(BF16)
            p = jnp.exp(logits - l_ref[cur, 0:1])
            doh = do_ref[cur, :]
            dob = doh.astype(BF16)
            dp = _dot(dob, vband, _NT)
            dd = jnp.sum(doh * o_ref[cur, :], axis=-1, keepdims=True)
            dls = jnp.sum(dl_ref[cur, :], axis=-1, keepdims=True)
            ds = p * (dp - dd + dls)
            dbias_ref[...] += ds
            dsb = (ds * 0.125).astype(BF16)
            dq_ref[cur, :] = _dot(dsb, kband, _NN)
            dkb = _dot(dsb, q_ref[cur, :].astype(BF16), _TN)
            dvb = _dot(p.astype(BF16), dob, _TN)
            dk_ref[prev, :] += dkb[:ATTN_BLK]
            dk_ref[cur, :] += dkb[ATTN_BLK:]
            dv_ref[prev, :] += dvb[:ATTN_BLK]
            dv_ref[cur, :] += dvb[ATTN_BLK:]
            return carry

        lax.fori_loop(0, nblk, blk, 0)

    seq, tab = _attn_specs(L)
    return pl.pallas_call(
        kern, name=name, grid=(4, nsub),
        in_specs=[seq] * 7 + [tab], out_specs=[seq, seq, seq, tab],
        out_shape=[jax.ShapeDtypeStruct(q.shape, F32)] * 3 + [jax.ShapeDtypeStruct((4, ATTN_BLK, 2 * ATTN_BLK), F32)],
        compiler_params=_cp(("arbitrary", "arbitrary")),
    )(q, k, v, o, lse, do, dl, bias)


def _to_sub(t, dil):
    t = t.reshape(NB, SEQ // dil, dil, 4, 64).transpose(0, 3, 2, 1, 4)
    return t.reshape(NB, 4, SEQ, 64)


def _from_sub(t, dil):
    t = t.reshape(NB, 4, dil, SEQ // dil, 64).transpose(0, 3, 2, 1, 4)
    return t.reshape(T, 256)


_ANY = pl.BlockSpec(memory_space=pl.ANY)
MESH = pl.DeviceIdType.MESH


def _slot(dev):
    return 4 * dev[0] + 2 * dev[1] + dev[2]


def _all_gather(shards, *, name):
    n = len(shards)

    def kern(*refs):
        x_refs, out_refs = refs[:n], refs[n:2 * n]
        send_sems, recv_sems, local_sems = refs[2 * n:]
        x, y, c = lax.axis_index("x"), lax.axis_index("y"), lax.axis_index("c")
        me, sibling = (x, y, c), (x, y, 1 - c)
        chips = [(1 - x, y), (x, 1 - y), (1 - x, 1 - y)]

        def copies(k, block, to, own=False):
            return [pltpu.make_async_remote_copy(
                src_ref=x_refs[a] if own else out_refs[a].at[_slot(block)], dst_ref=out_refs[a].at[_slot(block)],
                send_sem=send_sems.at[k, a], recv_sem=recv_sems.at[k, a], device_id=to, device_id_type=MESH) for a in range(n)]

        mine = [pltpu.make_async_copy(x_refs[a], out_refs[a].at[_slot(me)], local_sems.at[a]) for a in range(n)]
        first = copies(0, me, sibling, own=True)
        for j, chip in enumerate(chips):
            first += copies(1 + j, me, (*chip, c), own=True)
        for cp in mine + first:
            cp.start()
        passed = []
        for j, chip in enumerate(chips):
            for cp in copies(1 + j, (*chip, c), me):
                cp.wait_recv()
            fwd = copies(4 + j, (*chip, c), sibling)
            for cp in fwd:
                cp.start()
            passed += fwd
        for cp in copies(0, sibling, me):
            cp.wait_recv()
        for j, chip in enumerate(chips):
            for cp in copies(4 + j, (*chip, 1 - c), me):
                cp.wait_recv()
        for cp in first + passed:
            cp.wait_send()
        for cp in mine:
            cp.wait()

    return pl.pallas_call(
        kern, name=name, in_specs=[_ANY] * n, out_specs=[_ANY] * n,
        out_shape=[jax.ShapeDtypeStruct((NDEV, *s.shape), s.dtype) for s in shards],
        scratch_shapes=[pltpu.SemaphoreType.DMA((7, n)), pltpu.SemaphoreType.DMA((7, n)), pltpu.SemaphoreType.DMA((n,))],
    )(*shards)


def _exchange_sibling(arrs, *, name):
    n = len(arrs)

    def kern(*refs):
        x_refs, out_refs = refs[:n], refs[n:2 * n]
        send_sems, recv_sems = refs[2 * n:]
        x, y, c = lax.axis_index("x"), lax.axis_index("y"), lax.axis_index("c")
        sends = [pltpu.make_async_remote_copy(
            src_ref=x_refs[a].at[1 - c], dst_ref=out_refs[a], send_sem=send_sems.at[a], recv_sem=recv_sems.at[a],
            device_id=(x, y, 1 - c), device_id_type=MESH) for a in range(n)]
        for cp in sends:
            cp.start()
        for cp in sends:
            cp.wait_recv()
        for cp in sends:
            cp.wait_send()

    return pl.pallas_call(
        kern, name=name, in_specs=[_ANY] * n, out_specs=[_ANY] * n,
        out_shape=[jax.ShapeDtypeStruct(a.shape[1:], a.dtype) for a in arrs],
        scratch_shapes=[pltpu.SemaphoreType.DMA((n,)), pltpu.SemaphoreType.DMA((n,))],
    )(*arrs)


def _exchange_chips(arrs, *, name):
    n = len(arrs)

    def kern(*refs):
        x_refs, out_refs = refs[:n], refs[n:2 * n]
        send_sems, recv_sems, local_sems = refs[2 * n:]
        x, y, c = lax.axis_index("x"), lax.axis_index("y"), lax.axis_index("c")
        my_chip = 2 * x + y
        chips = [(1 - x, y), (x, 1 - y), (1 - x, 1 - y)]
        mine = [pltpu.make_async_copy(x_refs[a].at[my_chip], out_refs[a].at[my_chip], local_sems.at[a]) for a in range(n)]
        sends = [pltpu.make_async_remote_copy(
            src_ref=x_refs[a].at[2 * px + py], dst_ref=out_refs[a].at[my_chip], send_sem=send_sems.at[k, a],
            recv_sem=recv_sems.at[k, a], device_id=(px, py, c), device_id_type=MESH)
            for k, (px, py) in enumerate(chips) for a in range(n)]
        for cp in mine + sends:
            cp.start()
        for k, (px, py) in enumerate(chips):
            for a in range(n):
                pltpu.make_async_remote_copy(
                    src_ref=x_refs[a].at[my_chip], dst_ref=out_refs[a].at[2 * px + py], send_sem=send_sems.at[k, a],
                    recv_sem=recv_sems.at[k, a], device_id=(px, py, c), device_id_type=MESH).wait_recv()
        for cp in sends:
            cp.wait_send()
        for cp in mine:
            cp.wait()

    return pl.pallas_call(
        kern, name=name, in_specs=[_ANY] * n, out_specs=[_ANY] * n,
        out_shape=[jax.ShapeDtypeStruct(a.shape, a.dtype) for a in arrs],
        scratch_shapes=[pltpu.SemaphoreType.DMA((3, n)), pltpu.SemaphoreType.DMA((3, n)), pltpu.SemaphoreType.DMA((n,))],
    )(*arrs)


def _add_own_half(full, other, c_idx, *, name):
    _, r, c = full.shape
    tile = _pick(r, (512, 256, 128, 64, 32, 16))

    def kern(c_ref, f_ref, o_ref, out_ref):
        out_ref[...] = (f_ref[...].astype(F32) + o_ref[...].astype(F32)).astype(BF16)

    return pl.pallas_call(
        kern, name=name,
        grid_spec=pltpu.PrefetchScalarGridSpec(
            num_scalar_prefetch=1, grid=(r // tile,),
            in_specs=[pl.BlockSpec((None, tile, c), lambda i, c_ref: (c_ref[0], i, 0)),
                      pl.BlockSpec((tile, c), lambda i, c_ref: (i, 0))],
            out_specs=pl.BlockSpec((tile, c), lambda i, c_ref: (i, 0))),
        out_shape=jax.ShapeDtypeStruct((r, c), BF16), compiler_params=_cp(("parallel",)),
    )(c_idx, full, other)


def _adamw_reduce(parts, w, m, v, *, name):
    ns, r, c = parts.shape
    tile = _pick(r, (256, 128, 64, 32, 16, 8))

    def kern(p_ref, w_ref, m_ref, v_ref, g_ref, d_ref, m2_ref, v2_ref):
        g = p_ref[0].astype(F32)
        for d in range(1, ns):
            g = g + p_ref[d].astype(F32)
        g_ref[...] = g
        d_ref[...], m2_ref[...], v2_ref[...] = _adam_math(w_ref[...], g, m_ref[...], v_ref[...])

    blk = pl.BlockSpec((tile, c), lambda i: (i, 0))
    return pl.pallas_call(
        kern, name=name, grid=(r // tile,), in_specs=[pl.BlockSpec((ns, tile, c), lambda i: (0, i, 0)), blk, blk, blk],
        out_specs=[blk] * 4, out_shape=[jax.ShapeDtypeStruct((r, c), F32)] * 4, compiler_params=_cp(("parallel",)),
    )(parts, w, m, v)


def _sum_slots(parts, *, name):
    ns, r, c = parts.shape
    tile = _pick(r, (256, 128, 64, 32, 16, 8))

    def kern(p_ref, o_ref):
        acc = p_ref[0].astype(F32)
        for d in range(1, ns):
            acc = acc + p_ref[d].astype(F32)
        o_ref[...] = acc

    return pl.pallas_call(
        kern, name=name, grid=(r // tile,), in_specs=[pl.BlockSpec((ns, tile, c), lambda i: (0, i, 0))],
        out_specs=pl.BlockSpec((tile, c), lambda i: (i, 0)), out_shape=jax.ShapeDtypeStruct((r, c), F32),
        compiler_params=_cp(("parallel",)),
    )(parts)


BIG = [
    ("ffn1_w13", (1024, 704), 1), ("ffn1_w2", (352, 1024), 0), ("w_in", (1024, 1154), 1), ("p_pool", (768, 128), 1),
    ("p_ssd", (128, 1024), 0), ("p_attn", (256, 128), 1), ("w_out", (128, 1024), 0), ("ffn2_w13", (1024, 704), 1),
    ("ffn2_w2", (352, 1024), 0),
]
BLOCKED = ("ffn1_w13", "ffn2_w13")


def _from_blocks(blk, ax):
    _, r, c = blk.shape
    return blk.transpose(1, 0, 2).reshape(r, NDEV * c) if ax == 1 else blk.reshape(NDEV * r, c)


def _to_blocks(full, ax):
    r, c = full.shape
    return full.reshape(r, NDEV, c // NDEV).transpose(1, 0, 2) if ax == 1 else full.reshape(NDEV, r // NDEV, c)


class _SmallPack:
    def __init__(self, shapes):
        self.shapes = shapes
        self.rows = [-(-int(np.prod(s)) // 128) for s in shapes]
        self.total = -(-sum(self.rows) // 256) * 256

    def pack(self, arrs):
        parts = []
        for a, s, r in zip(arrs, self.shapes, self.rows):
            assert tuple(a.shape) == tuple(s), (a.shape, s)
            flat = a.astype(F32).reshape(-1)
            parts.append(jnp.pad(flat, (0, r * 128 - flat.shape[0])).reshape(r, 128))
        parts.append(jnp.zeros((self.total - sum(self.rows), 128), F32))
        return jnp.concatenate(parts, axis=0)

    def unpack(self, packed):
        out, off = [], 0
        for s, r in zip(self.shapes, self.rows):
            out.append(packed[off:off + r].reshape(-1)[:int(np.prod(s))].reshape(s))
            off += r
        return out


def _w_in_to_padded(w):
    z = jnp.zeros((*w.shape[:-1], HC - DT0 - 16), w.dtype)
    return jnp.concatenate([w[..., O_Z:O_XBC], w[..., O_XBC:O_DT], w[..., O_G:O_END], w[..., O_Q:O_K], w[..., O_K:O_V],
                            w[..., O_V:O_G], w[..., O_U:O_Z], w[..., O_DT:O_Q], z], axis=-1)


def _w_in_from_padded(g):
    return jnp.concatenate([g[..., U0:DT0], g[..., Z0:XBC0], g[..., XBC0:GATE0], g[..., DT0:DT0 + 16], g[..., Q0:K0],
                            g[..., K0:V0], g[..., V0:U0], g[..., GATE0:Q0]], axis=-1)


def _block_diag(pw):
    out = jnp.zeros((POOL_W, POOL_W), pw.dtype)
    for gi in range(4):
        out = lax.dynamic_update_slice(out, pw[gi], (gi * POOL_GDIM, gi * POOL_GDIM))
    return out


def _expand_heads(v):
    return jnp.repeat(v, 64).reshape(1, D)


def _ffn_fwd(x, x16, w13, w2, g, b, tag):
    h = _mm(x16, w13, out_blocked=True, name=f"{tag}_h")
    act, h16 = _swiglu_fwd(h, name=f"{tag}_act")
    f = _mm(act, w2, name=f"{tag}_f")
    y, y16 = _ln_fwd(x, f, g, b, scale=FFN_RES, name=f"{tag}_ln")
    return y, y16, (h16, act, f)


def _ffn_bwd(x, x16, w13, w2, g, b, saved, dy, tag):
    h16, act, f = saved
    dres, df, dg, db = _ln_bwd(x, f, dy, g, b, scale=FFN_RES, name=f"{tag}_ln_bwd")
    dact = _mm(df, w2, tb=True, out_blocked=True, bn=FB, name=f"{tag}_dact")
    dw2 = _mm(act.transpose(0, 2, 1).reshape(DFF, T), df, out_dtype=BF16, name=f"{tag}_dw2")
    dh = _swiglu_bwd(h16, dact, name=f"{tag}_act_bwd")
    dx = _mm(dh, w13, tb=True, add=dres, name=f"{tag}_dx")
    dw13 = _mm(x16.T, dh, out_blocked=True, out_dtype=BF16, name=f"{tag}_dw13")
    return dx, dw13, dw2, dg[0], db[0]


def _layer_fwd(x0, x0_16, p, bias):
    x1, x1_16, s1 = _ffn_fwd(x0, x0_16, p["ffn1_w13"], p["ffn1_w2"], p["ln1_g"], p["ln1_b"], "ffn1")
    hcat = _mm(x1_16, p["w_in"], name="mix_in")
    pooled = _pool_fwd(hcat, name="pool")
    ya_lin = _mm(pooled, p["pool_bd"], name="pool_lin")
    ya_pre = _affine_fwd(ya_lin, p["pool_b"], p["pool_scale"], name="pool_affine")
    ya = _mm(ya_pre, p["p_pool"], name="pool_out")
    xa = _conv_fwd(hcat, p["conv_w"], p["conv_b"], name="conv")
    dtx = jnp.repeat(hcat[:, DT0:DT0 + SSD_HEADS], 64, axis=1)
    ysc, hs = _ssd_fwd(xa, dtx, p["dt_bias"], p["a_log"], p["d_skip"], name="ssd")
    yb_pre = _gnorm_fwd(ysc, hcat, p["ssd_norm"], name="ssd_norm")
    yb = _mm(yb_pre, p["p_ssd"], name="ssd_out")
    outs, lses = [], []
    for gi, (_, dil) in enumerate(ATTN_CONFIGS):
        q, k, v = (_to_sub(hcat[:, c0 + 256 * gi:c0 + 256 * gi + 256], dil) for c0 in (Q0, K0, V0))
        o, l = _attn_fwd(q, k, v, bias[gi], SEQ // dil // ATTN_BLK, name=f"attn{gi}")
        outs.append(_from_sub(o, dil))
        lses.append(_from_sub(l, dil))
    yc_pre = _amerge_fwd(outs, lses, name="attn_merge")
    yc = _mm(yc_pre, p["p_attn"], name="attn_out")
    merged = _merge_fwd(hcat, ya, yb, yc, p["gate_b"], name="merge")
    mix = _mm(merged, p["w_out"], name="mix_out")
    x2, x2_16 = _ln_fwd(x1, mix, p["ln2_g"], p["ln2_b"], scale=1.0, name="ln2")
    x3, x3_16, s3 = _ffn_fwd(x2, x2_16, p["ffn2_w13"], p["ffn2_w2"], p["ln3_g"], p["ln3_b"], "ffn2")
    saved = dict(x0=x0, x0_16=x0_16, s1=s1, x1=x1, x1_16=x1_16, hcat=hcat, pooled=pooled, ya_lin=ya_lin, ya_pre=ya_pre, ya=ya,
                 xa=xa, hs=hs, ysc=ysc, yb_pre=yb_pre, yb=yb, outs=outs, lses=lses, yc_pre=yc_pre, yc=yc, merged=merged, mix=mix,
                 x2=x2, x2_16=x2_16, s3=s3)
    return x3, x3_16, saved


def _layer_bwd(p, bias, sv, dx3):
    g = {}
    dx2, g["ffn2_w13"], g["ffn2_w2"], g["ln3_g"], g["ln3_b"] = _ffn_bwd(
        sv["x2"], sv["x2_16"], p["ffn2_w13"], p["ffn2_w2"], p["ln3_g"], p["ln3_b"], sv["s3"], dx3, "ffn2")
    hcat = sv["hcat"]
    dres, dmix, dg2, db2 = _ln_bwd(sv["x1"], sv["mix"], dx2, p["ln2_g"], p["ln2_b"], scale=1.0, name="ln2_bwd")
    g["ln2_g"], g["ln2_b"] = dg2[0], db2[0]
    dmerged = _mm(dmix, p["w_out"], tb=True, name="mix_out_dx")
    g["w_out"] = _mm(sv["merged"].T, dmix, out_dtype=BF16, name="mix_out_dw")
    dg0, dg1, dg2_, dya, dyb, dyc, gb0, gb1, gb2 = _merge_bwd(hcat, sv["ya"], sv["yb"], sv["yc"], dmerged, p["gate_b"],
                                                               name="merge_bwd")
    g["gate_b"] = jnp.concatenate([gb0, gb1, gb2], axis=0)
    dya_pre = _mm(dya, p["p_pool"], tb=True, name="pool_out_dx")
    g["p_pool"] = _mm(sv["ya_pre"].T, dya, out_dtype=BF16, name="pool_out_dw")
    dya_lin, dpb, dps = _affine_bwd(sv["ya_lin"], dya_pre, p["pool_b"], p["pool_scale"], name="pool_affine_bwd")
    g["pool_b"], g["pool_scale"] = dpb[0].reshape(4, POOL_GDIM), dps[0]
    dpooled = _mm(dya_lin, p["pool_bd"], tb=True, name="pool_lin_dx")
    dbd = _mm(sv["pooled"].T, dya_lin, name="pool_lin_dw")
    g["pool_w"] = jnp.stack([dbd[i * POOL_GDIM:(i + 1) * POOL_GDIM, i * POOL_GDIM:(i + 1) * POOL_GDIM] for i in range(4)])
    du = _pool_bwd(dpooled, name="pool_bwd")
    dyb_pre = _mm(dyb, p["p_ssd"], tb=True, name="ssd_out_dx")
    g["p_ssd"] = _mm(sv["yb_pre"].T, dyb, out_dtype=BF16, name="ssd_out_dw")
    dysc, dz, dnw = _gnorm_bwd(sv["ysc"], hcat, dyb_pre, p["ssd_norm"], name="ssd_norm_bwd")
    g["ssd_norm"] = dnw[0]
    dtx = jnp.repeat(hcat[:, DT0:DT0 + SSD_HEADS], 64, axis=1)
    dxs, dbm, dcm, ddtx, ddtb, dalog, ddsk = _ssd_bwd(sv["xa"], dtx, p["dt_bias"], p["a_log"], p["d_skip"], sv["hs"], dysc,
                                                     name="ssd_bwd")
    g["dt_bias"], g["a_log"], g["d_skip"] = ddtb[0, ::64], dalog[0, ::64], ddsk[0, ::64]
    dxa = jnp.concatenate([dxs, dbm, dcm], axis=1)
    dxbc, dcw, dcb = _conv_bwd(hcat, p["conv_w"], p["conv_b"], dxa, name="conv_bwd")
    g["conv_w"], g["conv_b"] = dcw, dcb[0]
    ddt = jnp.pad(ddtx[:, ::64], ((0, 0), (0, HC - DT0 - SSD_HEADS))).astype(BF16)
    dyc_pre = _mm(dyc, p["p_attn"], tb=True, name="attn_out_dx")
    g["p_attn"] = _mm(sv["yc_pre"].T, dyc, out_dtype=BF16, name="attn_out_dw")
    am = _amerge_bwd(sv["outs"], sv["lses"], dyc_pre, name="attn_merge_bwd")
    dqs, dks, dvs, dbias = [], [], [], []
    for gi, (_, dil) in enumerate(ATTN_CONFIGS):
        q, k, v = (_to_sub(hcat[:, c0 + 256 * gi:c0 + 256 * gi + 256], dil) for c0 in (Q0, K0, V0))
        o, l, do, dl = (_to_sub(t, dil) for t in (sv["outs"][gi], sv["lses"][gi], am[gi], am[3 + gi]))
        dq, dk, dv, dbi = _attn_bwd(q, k, v, o, l, do, dl, bias[gi], SEQ // dil // ATTN_BLK, name=f"attn{gi}_bwd")
        dqs.append(_from_sub(dq, dil).astype(BF16))
        dks.append(_from_sub(dk, dil).astype(BF16))
        dvs.append(_from_sub(dv, dil).astype(BF16))
        dbias.append(dbi)
    g["attn_bias"] = jnp.stack(dbias)
    dhcat = jnp.concatenate([dz, dxbc, dg0, dg1, dg2_, *dqs, *dks, *dvs, du, ddt], axis=1)
    dx1 = _mm(dhcat, p["w_in"], tb=True, add=dres, name="mix_in_dx")
    g["w_in"] = _mm(sv["x1_16"].T, dhcat, out_dtype=BF16, name="mix_in_dw")
    dx0, g["ffn1_w13"], g["ffn1_w2"], g["ln1_g"], g["ln1_b"] = _ffn_bwd(
        sv["x0"], sv["x0_16"], p["ffn1_w13"], p["ffn1_w2"], p["ln1_g"], p["ln1_b"], sv["s1"], dx1, "ffn1")
    return dx0, g


def _local_step(x, tgt, lw, rel_bias):
    bias, bias_vjp = jax.vjp(_attn_bias, rel_bias)
    y, y16, saved = x, x.astype(BF16), []
    for p in lw:
        y, y16, sv = _layer_fwd(y, y16, p, bias)
        saved.append(sv)
    dy, sq = _loss_kernel(y, tgt, name="loss")
    grads = [None] * len(lw)
    for l in reversed(range(len(lw))):
        dy, grads[l] = _layer_bwd(lw[l], bias, saved[l], dy)
    (d_rel,) = bias_vjp(sum(g.pop("attn_bias") for g in grads))
    return sq, dy, grads, d_rel


SMALL_REPL = ["ln1_g", "ln1_b", "pool_w", "pool_b", "pool_scale", "conv_b", "dt_bias", "a_log", "d_skip", "ssd_norm",
              "ln2_g", "ln2_b", "ln3_g", "ln3_b", "rel_bias"]
SMALL_SHARD = ["gate_b", "conv_w"]
WEIGHTS = ['ffn1_w13', 'ffn1_w2', 'ln1_g', 'ln1_b', 'w_in', 'gate_b', 'pool_w', 'pool_b', 'pool_scale', 'conv_w', 'conv_b',
           'dt_bias', 'a_log', 'd_skip', 'ssd_norm', 'rel_bias', 'p_pool', 'p_ssd', 'p_attn', 'w_out', 'ln2_g', 'ln2_b',
           'ffn2_w13', 'ffn2_w2', 'ln3_g', 'ln3_b']


def _step(w, m, v, x, tgt):
    my_c = lax.axis_index("c")
    dev = 4 * lax.axis_index("x") + 2 * lax.axis_index("y") + my_c
    gathered = _all_gather([w[n].astype(BF16) for n, _, _ in BIG], name="gather_weights")
    sp_in = _SmallPack([w[n].shape for n in SMALL_SHARD])
    small_g = _all_gather([sp_in.pack([w[n] for n in SMALL_SHARD])], name="gather_small")[0]
    gate_b = jnp.stack([sp_in.unpack(small_g[d])[0] for d in range(NDEV)], axis=2).reshape(DEPTH, 3, D)
    conv_w = jnp.stack([sp_in.unpack(small_g[d])[1] for d in range(NDEV)], axis=2).reshape(DEPTH, 4, 2048)
    lw = []
    for l in range(DEPTH):
        p = {}
        for (n, _, ax), blocks in zip(BIG, gathered):
            p[n] = blocks[:, l] if n in BLOCKED else _from_blocks(blocks[:, l], ax)
        p["w_in"] = _w_in_to_padded(p["w_in"])
        p["pool_bd"] = _block_diag(w["pool_w"][l]).astype(BF16)
        p["gate_b"], p["conv_w"] = gate_b[l], conv_w[l]
        for n in ("ln1_g", "ln1_b", "ln2_g", "ln2_b", "ln3_g", "ln3_b", "pool_scale", "conv_b", "ssd_norm"):
            p[n] = w[n][l][None, :]
        p["pool_b"] = w["pool_b"][l].reshape(1, POOL_W)
        for n in ("dt_bias", "a_log", "d_skip"):
            p[n] = _expand_heads(w[n][l])
        lw.append(p)
    sq, dx, grads, d_rel = _local_step(x.reshape(T, D), tgt.reshape(T, D), lw, w["rel_bias"])
    loss = lax.psum(jnp.sum(sq) * (0.5 / D), ("x", "y", "c"))
    for g in grads:
        g["w_in"] = _w_in_from_padded(g["w_in"])
    halves = []
    for n, (r, c), ax in BIG:
        blocks = jnp.stack([g[n] if n in BLOCKED else _to_blocks(g[n], ax) for g in grads], axis=1)
        halves.append(blocks.reshape(4, 2, DEPTH * r, c).transpose(1, 0, 2, 3))
    from_sibling = _exchange_sibling(halves, name="exchange_grads_cores")
    c_idx = my_c.reshape(1).astype(jnp.int32)
    chip_sums = [_add_own_half(h.reshape(2, -1, h.shape[-1]), o.reshape(-1, o.shape[-1]), c_idx, name=f"add_cores_{n}")
                 .reshape(4, -1, h.shape[-1]) for h, o, (n, _, _) in zip(halves, from_sibling, BIG)]
    by_chip = _exchange_chips(chip_sums, name="exchange_grads_chips")
    small_names = SMALL_REPL + SMALL_SHARD
    small = {n: (d_rel if n == "rel_bias" else jnp.stack([g[n] for g in grads])) for n in small_names}
    sp = _SmallPack([small[n].shape for n in small_names])
    gsmall = sp.unpack(_sum_slots(_all_gather([sp.pack([small[n] for n in small_names])], name="gather_small_grads")[0],
                                  name="sum_small_grads"))
    gout = {}
    for n, gv in zip(small_names, gsmall):
        if n in SMALL_SHARD:
            width = w[n].shape[-1]
            gv = lax.dynamic_slice_in_dim(gv, dev * width, width, axis=2)
        gout[n] = gv
    delta, new_m, new_v = {}, {}, {}
    for (n, _, _), parts in zip(BIG, by_chip):
        shp = w[n].shape
        two_d = lambda a: a.reshape(-1, shp[-1])
        res = _adamw_reduce(parts, two_d(w[n]), two_d(m[n]), two_d(v[n]), name=f"adamw_{n}")
        gout[n], delta[n], new_m[n], new_v[n] = (t.reshape(shp) for t in res)
    spa = _SmallPack([w[n].shape for n in small_names])
    res = _adamw(spa.pack([w[n] for n in small_names]), spa.pack([gout[n] for n in small_names]),
                 spa.pack([m[n] for n in small_names]), spa.pack([v[n] for n in small_names]), name="adamw_small")
    for out, packed in zip((delta, new_m, new_v), res):
        for n, val in zip(small_names, spa.unpack(packed)):
            out[n] = val
    return (loss, dx.reshape(NB, SEQ, D), *[gout[n] for n in WEIGHTS], *[delta[n] for n in WEIGHTS],
            *[new_m[n] for n in WEIGHTS], *[new_v[n] for n in WEIGHTS])


def kernel(x, ffn1_w13, ffn1_w2, ln1_g, ln1_b, w_in, gate_b, pool_w, pool_b, pool_scale, conv_w, conv_b, dt_bias, a_log, d_skip, ssd_norm, rel_bias, p_pool, p_ssd, p_attn, w_out, ln2_g, ln2_b, ffn2_w13, ffn2_w2, ln3_g, ln3_b, loss_target, m_ffn1_w13, m_ffn1_w2, m_ln1_g, m_ln1_b, m_w_in, m_gate_b, m_pool_w, m_pool_b, m_pool_scale, m_conv_w, m_conv_b, m_dt_bias, m_a_log, m_d_skip, m_ssd_norm, m_rel_bias, m_p_pool, m_p_ssd, m_p_attn, m_w_out, m_ln2_g, m_ln2_b, m_ffn2_w13, m_ffn2_w2, m_ln3_g, m_ln3_b, v_ffn1_w13, v_ffn1_w2, v_ln1_g, v_ln1_b, v_w_in, v_gate_b, v_pool_w, v_pool_b, v_pool_scale, v_conv_w, v_conv_b, v_dt_bias, v_a_log, v_d_skip, v_ssd_norm, v_rel_bias, v_p_pool, v_p_ssd, v_p_attn, v_w_out, v_ln2_g, v_ln2_b, v_ffn2_w13, v_ffn2_w2, v_ln3_g, v_ln3_b):
    given = dict(locals())
    w = {n: given[n] for n in WEIGHTS}
    m = {n: given["m_" + n] for n in WEIGHTS}
    v = {n: given["v_" + n] for n in WEIGHTS}
    return _step(w, m, v, x, loss_target)
```

```python
import functools
import math

import numpy as np
import jax
import jax.numpy as jnp
from jax import lax
from jax.experimental import pallas as pl
from jax.experimental.pallas import tpu as pltpu

F32, BF16 = jnp.float32, jnp.bfloat16
HIGHEST = lax.Precision.HIGHEST

NDEV = 8
DEPTH = 4
D = 1024
SEQ = 2048
NB = 2
T = NB * SEQ
DFF = 2816
POOL_W = 768
POOL_WINDOWS = (2, 4, 8, 16)
POOL_GDIM = 192
SSD_HEADS = 16
CHUNK = 128
NCHUNK = SEQ // CHUNK
ATTN_CONFIGS = ((128, 1), (512, 4), (2048, 16))
ATTN_BLK = 128
REL_BUCKETS = 32
REL_MAX_DIST = 2048
LN_EPS = 1e-5
SSD_EPS = 1e-5
ALPHA = (2.0 * DEPTH) ** 0.25
FFN_RES = 0.5
NEG = -1e30

ADAM_LR, ADAM_B1, ADAM_B2, ADAM_EPS, ADAM_WD, ADAM_STEP = 0.001, 0.9, 0.999, 1e-08, 0.01, 10

Z0, XBC0, GATE0, QKV0, U0, DT0, HC = 0, 1024, 3072, 6144, 8448, 9216, 9728
O_U, O_Z, O_XBC, O_DT, O_Q, O_K, O_V, O_G, O_END = 0, 768, 1792, 3840, 3856, 4624, 5392, 6160, 9232

VMEM_LIMIT = 56 * 1024 * 1024


def _cp(sem):
    return pltpu.CompilerParams(dimension_semantics=sem, vmem_limit_bytes=VMEM_LIMIT)


def _pick(dim, cands):
    for c in cands:
        if dim % c == 0:
            return c
    return dim


def _mm(a, b, *, tb=False, out_dtype=F32, add=None, add_scale=1.0, out_blocked=False, bn=None, b_layer=None, name):
    bk = None
    if a.ndim == 3:
        nkb, M, bk = a.shape
        K = nkb * bk
    else:
        M, K = a.shape
    b_blocked = b.ndim >= 3
    b_shape = b.shape if b_layer is None else (b.shape[0], *b.shape[2:])
    if b_blocked and tb:
        assert bk in (None, b_shape[2])
        nkb, N, bk = b_shape
        assert K == nkb * bk, (a.shape, b.shape)
    elif b_blocked:
        nnb, K2, bn = b_shape
        N = nnb * bn
        assert K == K2, (a.shape, b.shape)
    else:
        (N, K2) = b.shape if tb else b.shape[::-1]
        assert K == K2, (a.shape, b.shape, tb)
    bm = _pick(M, (1024, 768, 512, 256))
    if bn is None:
        bn = _pick(N, (1024, 768, 512, 256))
    if bk is None:
        bk = K if K <= 1024 else _pick(K, (1024, 1408, 2432, 512, 256))
    nk = K // bk
    dn = (((1,), (1 if tb else 0,)), ((), ()))

    def kern(*refs):
        a_ref, b_ref = refs[0], refs[1]
        add_ref = refs[2] if add is not None else None
        o_ref = refs[3] if add is not None else refs[2]
        p = lax.dot_general(a_ref[...].astype(BF16), b_ref[...].astype(BF16), dn, preferred_element_type=F32)

        def fin(acc):
            if add_ref is not None:
                acc = acc + add_scale * add_ref[...]
            o_ref[...] = acc.astype(out_dtype)

        if nk == 1:
            fin(p)
        else:
            acc_ref = refs[-1]
            k = pl.program_id(2)

            @pl.when(k == 0)
            def _():
                acc_ref[...] = p

            @pl.when(k > 0)
            def _():
                acc_ref[...] += p

            @pl.when(k == nk - 1)
            def _():
                fin(acc_ref[...])

    if a.ndim == 3:
        a_spec = pl.BlockSpec((None, bm, bk), lambda i, j, k: (k, i, 0))
    else:
        a_spec = pl.BlockSpec((bm, bk), lambda i, j, k: (i, k))
    if b_blocked and b_layer is not None:
        b_spec = (pl.BlockSpec((None, None, bn, bk), lambda i, j, k: (k, b_layer, j, 0)) if tb else
                  pl.BlockSpec((None, None, bk, bn), lambda i, j, k: (j, b_layer, k, 0)))
    elif b_blocked and tb:
        b_spec = pl.BlockSpec((None, bn, bk), lambda i, j, k: (k, j, 0))
    elif b_blocked:
        b_spec = pl.BlockSpec((None, bk, bn), lambda i, j, k: (j, k, 0))
    elif tb:
        b_spec = pl.BlockSpec((bn, bk), lambda i, j, k: (j, k))
    else:
        b_spec = pl.BlockSpec((bk, bn), lambda i, j, k: (k, j))
    if out_blocked:
        o_spec, o_shape = pl.BlockSpec((None, bm, bn), lambda i, j, k: (j, i, 0)), (N // bn, M, bn)
    else:
        o_spec, o_shape = pl.BlockSpec((bm, bn), lambda i, j, k: (i, j)), (M, N)
    in_specs = [a_spec, b_spec]
    args = [a, b]
    if add is not None:
        assert add.shape == o_shape
        in_specs.append(o_spec)
        args.append(add)
    return pl.pallas_call(
        kern, name=name, grid=(M // bm, N // bn, nk),
        in_specs=in_specs, out_specs=o_spec,
        out_shape=jax.ShapeDtypeStruct(o_shape, out_dtype),
        scratch_shapes=[pltpu.VMEM((bm, bn), F32)] if nk > 1 else [],
        compiler_params=_cp(("parallel", "parallel", "arbitrary")),
    )(*args)


def _rowwise(fn, rows, pars, outs, accs, *, name, tile, groups=1):
    n_rows = rows[0][0].shape[0]
    nt = n_rows // tile
    n_in = len(rows) + len(pars)
    n_out = len(outs)

    def kern(*refs):
        res = fn(*[r[...] for r in refs[:n_in]])
        for o_ref, val in zip(refs[n_in:n_in + n_out], res[:n_out]):
            o_ref[...] = val.astype(o_ref.dtype)
        i = pl.program_id(1)
        for a_ref, val in zip(refs[n_in + n_out:], res[n_out:]):
            @pl.when(i == 0)
            def _(a_ref=a_ref, val=val):
                a_ref[...] = val

            @pl.when(i > 0)
            def _(a_ref=a_ref, val=val):
                a_ref[...] += val

    in_specs = [pl.BlockSpec((tile, w), lambda g, i, c0=c0: (i, c0 + g)) for (_, w, c0) in rows]
    for p, w in pars:
        if w is None:
            in_specs.append(pl.BlockSpec(p.shape, lambda g, i: (0, 0)))
        else:
            in_specs.append(pl.BlockSpec((p.shape[0], w), lambda g, i: (0, g)))
    out_specs = [pl.BlockSpec((tile, w), lambda g, i: (i, g)) for (_, w, _) in outs]
    out_specs += [pl.BlockSpec((1, w), lambda g, i: (0, g)) for (_, w) in accs]
    out_shape = [jax.ShapeDtypeStruct((n_rows, c), dt) for (c, _, dt) in outs]
    out_shape += [jax.ShapeDtypeStruct((1, c), F32) for (c, _) in accs]
    return pl.pallas_call(
        kern, name=name, grid=(groups, nt), in_specs=in_specs, out_specs=out_specs, out_shape=out_shape,
        compiler_params=_cp(("arbitrary", "arbitrary")),
    )(*[r[0] for r in rows], *[p[0] for p in pars])


def _colsum(v):
    return jnp.sum(v, axis=0, keepdims=True)


def _silu(v):
    return v * jax.nn.sigmoid(v)


def _ln_fn(x, f, g, b, *, scale):
    pre = ALPHA * x + scale * f
    mu = jnp.mean(pre, axis=-1, keepdims=True)
    var = jnp.mean(jnp.square(pre - mu), axis=-1, keepdims=True)
    return (pre - mu) * lax.rsqrt(var + LN_EPS) * g + b


def _ln_fwd(x, f, g, b, *, scale, name):
    fn = lambda x_, f_, g_, b_: [_ln_fn(x_, f_, g_, b_, scale=scale)] * 2
    return _rowwise(fn, [(x, D, 0), (f, D, 0)], [(g, None), (b, None)], [(D, D, F32), (D, D, BF16)], [], name=name, tile=512)


def _ln_bwd(x, f, dy, g, b, *, scale, name):
    def fn(x_, f_, dy_, g_, b_):
        _, vjp = jax.vjp(functools.partial(_ln_fn, scale=scale), x_, f_, g_, b_)
        return list(vjp(dy_))

    return _rowwise(fn, [(x, D, 0), (f, D, 0), (dy, D, 0)], [(g, None), (b, None)],
                    [(D, D, F32), (D, D, BF16)], [(D, D), (D, D)], name=name, tile=512)


FB = 2 * DFF // NDEV
FFN_BM = 1024
_H_PAIR = pl.BlockSpec((2, None, FFN_BM, FB), lambda i, j: (0, j, i, 0))


def _ffn_up(x16, w13, layer, *, name):
    def kern(x_ref, w_ref, act_ref, h_ref):
        x = x_ref[...]
        ha = jnp.dot(x, w_ref[0], preferred_element_type=F32)
        hg = jnp.dot(x, w_ref[1], preferred_element_type=F32)
        act_ref[...] = (_silu(ha) * hg).astype(BF16)
        h_ref[0] = ha.astype(BF16)
        h_ref[1] = hg.astype(BF16)

    act, h16 = pl.pallas_call(
        kern, name=name, grid=(T // FFN_BM, 4),
        in_specs=[pl.BlockSpec((FFN_BM, D), lambda i, j: (i, 0)),
                  pl.BlockSpec((2, None, None, D, FB), lambda i, j: (0, j, layer, 0, 0))],
        out_specs=[pl.BlockSpec((None, FFN_BM, FB), lambda i, j: (j, i, 0)), _H_PAIR],
        out_shape=[jax.ShapeDtypeStruct((4, T, FB), BF16), jax.ShapeDtypeStruct((2, 4, T, FB), BF16)],
        compiler_params=_cp(("parallel", "parallel")),
    )(x16, w13.reshape(2, 4, DEPTH, D, FB))
    return act, h16.reshape(NDEV, T, FB)


def _ffn_down_bwd(df, w2, h16, *, name):
    def kern(df_ref, w_ref, h_ref, dh_ref):
        da = _dot(df_ref[...], w_ref[...], _NT)
        a, g = h_ref[0].astype(F32), h_ref[1].astype(F32)
        s = jax.nn.sigmoid(a)
        dh_ref[0] = (da * g * (s * (1.0 + a * (1.0 - s)))).astype(BF16)
        dh_ref[1] = (da * (a * s)).astype(BF16)

    return pl.pallas_call(
        kern, name=name, grid=(T // FFN_BM, 4),
        in_specs=[pl.BlockSpec((FFN_BM, D), lambda i, j: (i, 0)), pl.BlockSpec((FB, D), lambda i, j: (j, 0)), _H_PAIR],
        out_specs=_H_PAIR, out_shape=jax.ShapeDtypeStruct((2, 4, T, FB), BF16),
        compiler_params=_cp(("parallel", "parallel")),
    )(df, w2, h16.reshape(2, 4, T, FB)).reshape(NDEV, T, FB)


def _affine_fn(y, b, s):
    return (y + b) * s


def _affine_fwd(y, b, s, *, name):
    return _rowwise(lambda y_, b_, s_: [_affine_fn(y_, b_, s_)], [(y, POOL_W, 0)], [(b, None), (s, None)],
                    [(POOL_W, POOL_W, BF16)], [], name=name, tile=512)[0]


def _affine_bwd(y, dyo, b, s, *, name):
    def fn(y_, d_, b_, s_):
        _, vjp = jax.vjp(_affine_fn, y_, b_, s_)
        return list(vjp(d_))

    return _rowwise(fn, [(y, POOL_W, 0), (dyo, POOL_W, 0)], [(b, None), (s, None)],
                    [(POOL_W, POOL_W, BF16)], [(POOL_W, POOL_W), (POOL_W, POOL_W)], name=name, tile=512)


def _gnorm_fn(y, z, w):
    yz = y * _silu(z)
    return yz * lax.rsqrt(jnp.mean(jnp.square(yz), axis=-1, keepdims=True) + SSD_EPS) * w


def _gnorm_fwd(y, hcat, w, *, name):
    return _rowwise(lambda y_, z_, w_: [_gnorm_fn(y_, z_, w_)], [(y, 256, 0), (hcat, 256, Z0 // 256)], [(w, 256)],
                    [(D, 256, BF16)], [], name=name, tile=512, groups=4)[0]


def _gnorm_bwd(y, hcat, dyo, w, *, name):
    def fn(y_, z_, d_, w_):
        _, vjp = jax.vjp(_gnorm_fn, y_, z_, w_)
        return list(vjp(d_))

    return _rowwise(fn, [(y, 256, 0), (hcat, 256, Z0 // 256), (dyo, 256, 0)], [(w, 256)],
                    [(D, 256, F32), (D, 256, BF16)], [(D, 256)], name=name, tile=512, groups=4)


def _merge_fn(g0, g1, g2, ya, yb, yc, b0, b1, b2):
    return jax.nn.sigmoid(g0 + b0) * ya + jax.nn.sigmoid(g1 + b1) * yb + jax.nn.sigmoid(g2 + b2) * yc


def _merge_rows(hcat, ya, yb, yc):
    c = GATE0 // 256
    return [(hcat, 256, c), (hcat, 256, c + 4), (hcat, 256, c + 8), (ya, 256, 0), (yb, 256, 0), (yc, 256, 0)]


def _merge_fwd(hcat, ya, yb, yc, gb, *, name):
    pars = [(gb[0:1], 256), (gb[1:2], 256), (gb[2:3], 256)]
    return _rowwise(lambda *v: [_merge_fn(*v)], _merge_rows(hcat, ya, yb, yc), pars, [(D, 256, BF16)], [],
                    name=name, tile=512, groups=4)[0]


def _merge_bwd(hcat, ya, yb, yc, dm, gb, *, name):
    def fn(g0, g1, g2, ya_, yb_, yc_, dm_, b0, b1, b2):
        _, vjp = jax.vjp(_merge_fn, g0, g1, g2, ya_, yb_, yc_, b0, b1, b2)
        return list(vjp(dm_))

    pars = [(gb[0:1], 256), (gb[1:2], 256), (gb[2:3], 256)]
    return _rowwise(fn, _merge_rows(hcat, ya, yb, yc) + [(dm, 256, 0)], pars,
                    [(D, 256, BF16)] * 6, [(D, 256)] * 3, name=name, tile=512, groups=4)


def _amerge_fn(o0, o1, o2, l0, l1, l2):
    m = jnp.maximum(jnp.maximum(l0, l1), l2)
    e0, e1, e2 = jnp.exp(l0 - m), jnp.exp(l1 - m), jnp.exp(l2 - m)
    return (e0 * o0 + e1 * o1 + e2 * o2) / (e0 + e1 + e2)


def _amerge_fwd(os_, ls_, *, name):
    rows = [(v, 256, 0) for v in (*os_, *ls_)]
    return _rowwise(lambda *v: [_amerge_fn(*v)], rows, [], [(256, 256, BF16)], [], name=name, tile=1024)[0]


def _amerge_bwd(os_, ls_, dy, *, name):
    def fn(*v):
        _, vjp = jax.vjp(_amerge_fn, *v[:6])
        return list(vjp(v[6]))

    rows = [(v, 256, 0) for v in (*os_, *ls_, dy)]
    return _rowwise(fn, rows, [], [(256, 256, F32)] * 6, [], name=name, tile=1024)


def _loss_kernel(y, tgt, *, name):
    def fn(y_, t_):
        e = y_ - t_
        return [e * (1.0 / D), _colsum(e * e)]

    return _rowwise(fn, [(y, D, 0), (tgt, D, 0)], [], [(D, D, F32)], [(D, D)], name=name, tile=512)


def _adam_math(w, g, m, v):
    m2 = ADAM_B1 * m + (1.0 - ADAM_B1) * g
    v2 = ADAM_B2 * v + (1.0 - ADAM_B2) * jnp.square(g)
    m_hat = m2 / (1.0 - ADAM_B1 ** ADAM_STEP)
    v_hat = v2 / (1.0 - ADAM_B2 ** ADAM_STEP)
    return -ADAM_LR * (m_hat / (jnp.sqrt(v_hat) + ADAM_EPS) + ADAM_WD * w), m2, v2


def _adamw(w, g, m, v, *, name):
    r, c = w.shape
    tile = _pick(r, (512, 256, 128, 64, 32, 16, 8))
    fn = lambda *a: list(_adam_math(*a))
    return _rowwise(fn, [(w, c, 0), (g, c, 0), (m, c, 0), (v, c, 0)], [], [(c, c, F32)] * 3, [], name=name, tile=tile)


def _pool_lane_window(j, width):
    lane = lax.broadcasted_iota(jnp.int32, (1, width), 1) + j * width
    grp = lane // POOL_GDIM
    return grp


def _pool_select(grp, vals):
    out = vals[3]
    for gi in (2, 1, 0):
        out = jnp.where(grp == gi, vals[gi], out)
    return out


def _pool_fwd(hcat, *, name):
    width = 256

    def kern(u_ref, o_ref):
        u = u_ref[...]
        t = lax.broadcasted_iota(jnp.int32, (SEQ, 1), 0)
        grp = _pool_lane_window(pl.program_id(1), width)

        def shift(v, k):
            return jnp.where(t >= k, pltpu.roll(v, k, 0), 0.0)

        s2 = u + shift(u, 1)
        s4 = s2 + shift(s2, 2)
        s8 = s4 + shift(s4, 4)
        s16 = s8 + shift(s8, 8)
        win = _pool_select(grp, [jnp.full((1, width), float(w), F32) for w in POOL_WINDOWS])
        cnt = jnp.minimum((t + 1).astype(F32), win)
        o_ref[...] = (_pool_select(grp, [s2, s4, s8, s16]) / cnt - u).astype(o_ref.dtype)

    return pl.pallas_call(
        kern, name=name, grid=(NB, POOL_W // width),
        in_specs=[pl.BlockSpec((SEQ, width), lambda b, j: (b, U0 // width + j))],
        out_specs=pl.BlockSpec((SEQ, width), lambda b, j: (b, j)),
        out_shape=jax.ShapeDtypeStruct((T, POOL_W), BF16), compiler_params=_cp(("parallel", "parallel")),
    )(hcat)


def _pool_bwd(dp, *, name):
    width = 256

    def kern(d_ref, o_ref):
        d = d_ref[...]
        t = lax.broadcasted_iota(jnp.int32, (SEQ, 1), 0)
        grp = _pool_lane_window(pl.program_id(1), width)
        win = _pool_select(grp, [jnp.full((1, width), float(w), F32) for w in POOL_WINDOWS])
        dm = d / jnp.minimum((t + 1).astype(F32), win)

        def shift(v, k):
            return jnp.where(t < SEQ - k, pltpu.roll(v, SEQ - k, 0), 0.0)

        r2 = dm + shift(dm, 1)
        r4 = r2 + shift(r2, 2)
        r8 = r4 + shift(r4, 4)
        r16 = r8 + shift(r8, 8)
        o_ref[...] = (_pool_select(grp, [r2, r4, r8, r16]) - d).astype(o_ref.dtype)

    return pl.pallas_call(
        kern, name=name, grid=(NB, POOL_W // width),
        in_specs=[pl.BlockSpec((SEQ, width), lambda b, j: (b, j))],
        out_specs=pl.BlockSpec((SEQ, width), lambda b, j: (b, j)),
        out_shape=jax.ShapeDtypeStruct((T, POOL_W), BF16), compiler_params=_cp(("parallel", "parallel")),
    )(dp)


CONV_W = 512


def _conv_pre(x, w, b, t):
    pre = w[3:4] * x + b
    for k in (1, 2, 3):
        pre = pre + w[3 - k:4 - k] * jnp.where(t >= k, pltpu.roll(x, k, 0), 0.0)
    return pre


def _conv_fwd(hcat, w, b, *, name):
    def kern(x_ref, w_ref, b_ref, o_ref):
        t = lax.broadcasted_iota(jnp.int32, (SEQ, 1), 0)
        o_ref[...] = _silu(_conv_pre(x_ref[...], w_ref[...], b_ref[...], t))

    return pl.pallas_call(
        kern, name=name, grid=(NB, 2048 // CONV_W),
        in_specs=[pl.BlockSpec((SEQ, CONV_W), lambda s, j: (s, XBC0 // CONV_W + j)),
                  pl.BlockSpec((4, CONV_W), lambda s, j: (0, j)), pl.BlockSpec((1, CONV_W), lambda s, j: (0, j))],
        out_specs=pl.BlockSpec((SEQ, CONV_W), lambda s, j: (s, j)),
        out_shape=jax.ShapeDtypeStruct((T, 2048), F32), compiler_params=_cp(("parallel", "parallel")),
    )(hcat, w, b)


def _conv_bwd(hcat, w, b, dy, *, name):
    def kern(x_ref, w_ref, b_ref, dy_ref, dx_ref, dw_ref, db_ref):
        t = lax.broadcasted_iota(jnp.int32, (SEQ, 1), 0)
        x, w_ = x_ref[...], w_ref[...]
        pre = _conv_pre(x, w_, b_ref[...], t)
        s = jax.nn.sigmoid(pre)
        dpre = dy_ref[...] * (s * (1.0 + pre * (1.0 - s)))
        dx = w_[3:4] * dpre
        dws = [None] * 4
        dws[3] = _colsum(dpre * x)
        for k in (1, 2, 3):
            dx = dx + w_[3 - k:4 - k] * jnp.where(t < SEQ - k, pltpu.roll(dpre, SEQ - k, 0), 0.0)
            dws[3 - k] = _colsum(dpre * jnp.where(t >= k, pltpu.roll(x, k, 0), 0.0))
        dx_ref[...] = dx.astype(dx_ref.dtype)
        db = _colsum(dpre)
        first = pl.program_id(1) == 0

        @pl.when(first)
        def _():
            for k in range(4):
                dw_ref[k:k + 1, :] = dws[k]
            db_ref[...] = db

        @pl.when(jnp.logical_not(first))
        def _():
            for k in range(4):
                dw_ref[k:k + 1, :] += dws[k]
            db_ref[...] += db

    return pl.pallas_call(
        kern, name=name, grid=(2048 // CONV_W, NB),
        in_specs=[pl.BlockSpec((SEQ, CONV_W), lambda j, s: (s, XBC0 // CONV_W + j)),
                  pl.BlockSpec((4, CONV_W), lambda j, s: (0, j)), pl.BlockSpec((1, CONV_W), lambda j, s: (0, j)),
                  pl.BlockSpec((SEQ, CONV_W), lambda j, s: (s, j))],
        out_specs=[pl.BlockSpec((SEQ, CONV_W), lambda j, s: (s, j)), pl.BlockSpec((4, CONV_W), lambda j, s: (0, j)),
                   pl.BlockSpec((1, CONV_W), lambda j, s: (0, j))],
        out_shape=[jax.ShapeDtypeStruct((T, 2048), BF16), jax.ShapeDtypeStruct((4, 2048), F32),
                   jax.ShapeDtypeStruct((1, 2048), F32)],
        compiler_params=_cp(("arbitrary", "arbitrary")),
    )(hcat, w, b, dy)


def _softplus(v):
    return jnp.maximum(v, 0.0) + jnp.log1p(jnp.exp(-jnp.abs(v)))


def _dot(a, b, dims):
    return lax.dot_general(a, b, (dims, ((), ())), preferred_element_type=F32)


_NN, _NT, _TN = ((1,), (0,)), ((1,), (1,)), ((0,), (0,))


def _ssd_specs(order):
    def spec(width, col0):
        return pl.BlockSpec((SEQ, width), lambda i, j, c0=col0: (order(i, j)[0], c0 + order(i, j)[1]))

    def par():
        return pl.BlockSpec((1, 256), lambda i, j: (0, order(i, j)[1]))

    return spec, par


def _ssd_chunk_common(c, xs_ref, b_ref, c_ref, dtx_ref, dtb, a, trif):
    r0 = pl.multiple_of(c * CHUNK, CHUNK)
    rows = pl.ds(r0, CHUNK)
    x = xs_ref[rows, :]
    bb = b_ref[rows, :].astype(BF16)
    cb = c_ref[rows, :].astype(BF16)
    raw = dtx_ref[rows, :] + dtb
    dt = _softplus(raw)
    cs = jnp.dot(trif, dt * a, precision=HIGHEST, preferred_element_type=F32)
    return rows, x, bb, cb, raw, dt, cs


def _head_decay(cs, me, tri):
    cse = jnp.max(jnp.where(me, cs, -jnp.inf), axis=1, keepdims=True)
    csb = jnp.broadcast_to(cse, (CHUNK, CHUNK))
    return jnp.where(tri, jnp.exp(csb - csb.T), 0.0)


def _ssd_fwd(xa, dtx, dtb, alog, dsk, *, name):
    spec, par = _ssd_specs(lambda g, b: (b, g))

    def kern(xs_ref, b_ref, c_ref, dtx_ref, dtb_ref, alog_ref, dsk_ref, y_ref, hs_ref, h_scr):
        row = lax.broadcasted_iota(jnp.int32, (CHUNK, CHUNK), 0)
        col = lax.broadcasted_iota(jnp.int32, (CHUNK, CHUNK), 1)
        tri = row >= col
        trif = tri.astype(F32)
        head = lax.broadcasted_iota(jnp.int32, (1, 256), 1) // 64
        dtb_, a, dsk_ = dtb_ref[...], -jnp.exp(alog_ref[...]), dsk_ref[...]
        h_scr[...] = jnp.zeros_like(h_scr)

        def chunk(c, carry):
            rows, x, bb, cb, _, dt, cs = _ssd_chunk_common(c, xs_ref, b_ref, c_ref, dtx_ref, dtb_, a, trif)
            cs_last = cs[CHUNK - 1:CHUNK, :]
            xdt = x * dt
            xdtb = xdt.astype(BF16)
            g = _dot(cb, bb, _NT)
            hin = h_scr[...]
            hs_ref[rows, :] = hin
            y = jnp.exp(cs) * _dot(cb, hin.astype(BF16), _NN) + dsk_ * x
            for e in range(4):
                me = head == e
                m = (g * _head_decay(cs, me, tri)).astype(BF16)
                y = y + jnp.where(me, _dot(m, xdtb, _NN), 0.0)
            y_ref[rows, :] = y
            st = _dot(bb, (jnp.exp(cs_last - cs) * xdt).astype(BF16), _TN)
            h_scr[...] = hin * jnp.exp(cs_last) + st
            return carry

        lax.fori_loop(0, NCHUNK, chunk, 0)

    return pl.pallas_call(
        kern, name=name, grid=(4, NB),
        in_specs=[spec(256, 0), spec(128, 8), spec(128, 12), spec(256, 0), par(), par(), par()],
        out_specs=[spec(256, 0), spec(256, 0)],
        out_shape=[jax.ShapeDtypeStruct((T, D), F32), jax.ShapeDtypeStruct((T, D), F32)],
        scratch_shapes=[pltpu.VMEM((CHUNK, 256), F32)],
        compiler_params=_cp(("parallel", "parallel")),
    )(xa, xa, xa, dtx, dtb, alog, dsk)


def _ssd_bwd(xa, dtx, dtb, alog, dsk, hs, dy, *, name):
    spec, par = _ssd_specs(lambda g, b: (b, g))

    def kern(xs_ref, b_ref, c_ref, dtx_ref, dtb_ref, alog_ref, dsk_ref, hs_ref, dy_ref,
             dx_ref, db_ref, dc_ref, ddt_ref, ddtb_ref, dalog_ref, ddsk_ref, dh_scr):
        row = lax.broadcasted_iota(jnp.int32, (CHUNK, CHUNK), 0)
        col = lax.broadcasted_iota(jnp.int32, (CHUNK, CHUNK), 1)
        tri = row >= col
        trif = tri.astype(F32)
        trit = (row <= col).astype(F32)
        is_last = lax.broadcasted_iota(jnp.int32, (CHUNK, 1), 0) == CHUNK - 1
        head = lax.broadcasted_iota(jnp.int32, (1, 256), 1) // 64
        dtb_, a, dsk_ = dtb_ref[...], -jnp.exp(alog_ref[...]), dsk_ref[...]
        dh_scr[...] = jnp.zeros_like(dh_scr)

        @pl.when(pl.program_id(1) == 0)
        def _():
            ddtb_ref[...] = jnp.zeros_like(ddtb_ref)
            dalog_ref[...] = jnp.zeros_like(dalog_ref)
            ddsk_ref[...] = jnp.zeros_like(ddsk_ref)

        def hsum(v, me):
            return jnp.sum(jnp.where(me, v, 0.0), axis=1, keepdims=True)

        def chunk(ci, carry):
            c = NCHUNK - 1 - ci
            rows, x, bb, cb, raw, dt, cs = _ssd_chunk_common(c, xs_ref, b_ref, c_ref, dtx_ref, dtb_, a, trif)
            cs_last = cs[CHUNK - 1:CHUNK, :]
            ecs = jnp.exp(cs)
            dsx = jnp.exp(cs_last - cs)
            xdt = x * dt
            xdtb = xdt.astype(BF16)
            g = _dot(cb, bb, _NT)
            hin = hs_ref[rows, :]
            hinb = hin.astype(BF16)
            dy_ = dy_ref[rows, :]
            dyb = dy_.astype(BF16)
            dh = dh_scr[...]
            dhb = dh.astype(BF16)
            bdh = _dot(bb, dhb, _NN)
            yoff = ecs * _dot(cb, hinb, _NN)
            dxdt = dsx * bdh
            t1, t2, t3 = dy_ * yoff, xdt * bdh, dh * hin
            dg = jnp.zeros((CHUNK, CHUNK), F32)
            dcs = jnp.zeros((CHUNK, 256), F32)
            for e in range(4):
                me = head == e
                l_ = _head_decay(cs, me, tri)
                m = g * l_
                dxdt = dxdt + jnp.where(me, _dot(m.astype(BF16), dyb, _TN), 0.0)
                dm = _dot(jnp.where(me, dy_, 0.0).astype(BF16), xdtb, _NT)
                dg = dg + dm * l_
                w = dm * m
                dds = hsum(t2, me)
                dse = jnp.max(jnp.where(me, dsx, -jnp.inf), axis=1, keepdims=True)
                ecl = jnp.max(jnp.where(me, jnp.exp(cs_last), -jnp.inf), axis=1, keepdims=True)
                dcs_e = hsum(t1, me) + jnp.sum(w, axis=1, keepdims=True) - jnp.sum(w.T, axis=1, keepdims=True) - dds * dse
                last = jnp.sum(dds * dse, axis=0, keepdims=True) + ecl * jnp.sum(hsum(t3, me), axis=0, keepdims=True)
                dcs_e = dcs_e + jnp.where(is_last, last, 0.0)
                dcs = dcs + jnp.where(me, dcs_e, 0.0)
            dadt = jnp.dot(trit, dcs, precision=HIGHEST, preferred_element_type=F32)
            ddt = a * dadt
            dxx = dxdt * x
            dyx = dy_ * x
            dsk_acc = jnp.zeros((1, 256), F32)
            for e in range(4):
                me = head == e
                ddt = ddt + jnp.where(me, hsum(dxx, me), 0.0)
                dsk_acc = dsk_acc + jnp.where(me, jnp.sum(hsum(dyx, me), axis=0, keepdims=True), 0.0)
            draw = ddt * jax.nn.sigmoid(raw)
            ddt_ref[rows, :] = draw
            ddtb_ref[...] += _colsum(draw)
            dalog_ref[...] += _colsum(dadt * dt) * a
            ddsk_ref[...] += dsk_acc
            dx_ref[rows, :] = dxdt * dt + dsk_ * dy_
            edy = (ecs * dy_).astype(BF16)
            dgb = dg.astype(BF16)
            dc_ref[rows, :] = _dot(dgb, bb, _NN) + _dot(edy, hinb, _NT)
            db_ref[rows, :] = _dot(dgb, cb, _TN) + _dot((dsx * xdt).astype(BF16), dhb, _NT)
            dh_scr[...] = jnp.exp(cs_last) * dh + _dot(cb, edy, _TN)
            return carry

        lax.fori_loop(0, NCHUNK, chunk, 0)

    return pl.pallas_call(
        kern, name=name, grid=(4, NB),
        in_specs=[spec(256, 0), spec(128, 8), spec(128, 12), spec(256, 0), par(), par(), par(), spec(256, 0), spec(256, 0)],
        out_specs=[spec(256, 0), spec(128, 0), spec(128, 0), spec(256, 0), par(), par(), par()],
        out_shape=[jax.ShapeDtypeStruct((T, D), F32), jax.ShapeDtypeStruct((T, 512), F32), jax.ShapeDtypeStruct((T, 512), F32),
                   jax.ShapeDtypeStruct((T, D), F32), jax.ShapeDtypeStruct((1, D), F32), jax.ShapeDtypeStruct((1, D), F32),
                   jax.ShapeDtypeStruct((1, D), F32)],
        scratch_shapes=[pltpu.VMEM((CHUNK, 256), F32)],
        compiler_params=_cp(("arbitrary", "arbitrary")),
    )(xa, xa, xa, dtx, dtb, alog, dsk, hs, dy)


def _t5_bucket_np(dist):
    dist = np.maximum(dist, 0)
    max_exact = REL_BUCKETS // 2
    large = max_exact + (np.log(np.maximum(dist, 1) / max_exact) / np.log(REL_MAX_DIST / max_exact)
                         * (REL_BUCKETS - max_exact)).astype(np.int32)
    large = np.minimum(large, REL_BUCKETS - 1)
    return np.where(dist < max_exact, dist, large).astype(np.int32)


def _attn_bias(rel_bias):
    qi = np.arange(ATTN_BLK)[:, None]
    kj = np.arange(2 * ATTN_BLK)[None, :]
    delta = qi - kj + ATTN_BLK
    out = []
    for gi, (window, dil) in enumerate(ATTN_CONFIGS):
        in_band = (delta >= 0) & (delta <= window // dil)
        bucket = jnp.asarray(_t5_bucket_np(delta * dil).reshape(-1, 1))
        one_hot = (bucket == jnp.arange(REL_BUCKETS)[None, :]).astype(F32)
        tab = jnp.dot(one_hot, rel_bias[:, 4 * gi:4 * gi + 4], precision=HIGHEST).reshape(ATTN_BLK, 2 * ATTN_BLK, 4)
        out.append(jnp.where(jnp.asarray(in_band)[None], tab.transpose(2, 0, 1), NEG))
    return jnp.stack(out)


def _attn_band(ref, cur, prev):
    return jnp.concatenate([ref[prev, :], ref[cur, :]], axis=0).astype(BF16)


def _attn_head_masks():
    hi = lax.broadcasted_iota(jnp.int32, (1, 128), 1) >= 64
    return [jnp.logical_not(hi), hi]


def _attn_logits(qm, kband, bias, n, sub_blocks):
    kj = lax.broadcasted_iota(jnp.int32, (1, 2 * ATTN_BLK), 1)
    ok = jnp.logical_or(n % sub_blocks > 0, kj >= ATTN_BLK)
    return jnp.where(ok, _dot(qm, kband, _NT) * 0.125 + bias, NEG)


def _attn_specs(col0):
    qkv = [pl.BlockSpec((SEQ, 128), lambda p, s, c=col0 + 2 * i: (s, c + p)) for i in range(3)]
    seq = pl.BlockSpec((SEQ, 128), lambda p, s: (s, p))
    tab = pl.BlockSpec((2, ATTN_BLK, 2 * ATTN_BLK), lambda p, s: (p, 0, 0))
    return qkv, seq, tab


def _attn_rows(n):
    cur = pl.ds(pl.multiple_of(n * ATTN_BLK, ATTN_BLK), ATTN_BLK)
    prev = pl.ds(pl.multiple_of(jnp.maximum(n - 1, 0) * ATTN_BLK, ATTN_BLK), ATTN_BLK)
    return cur, prev


def _attn_fwd(qkv, col0, bias, sub_blocks, *, name):
    def kern(q_ref, k_ref, v_ref, bias_ref, o_ref, l_ref):
        masks = _attn_head_masks()

        def blk(n, carry):
            cur, prev = _attn_rows(n)
            q = q_ref[cur, :]
            kband, vband = _attn_band(k_ref, cur, prev), _attn_band(v_ref, cur, prev)
            out = jnp.zeros((ATTN_BLK, 128), F32)
            lse = jnp.zeros((ATTN_BLK, 128), F32)
            for hh, mk in enumerate(masks):
                logits = _attn_logits(jnp.where(mk, q, 0.0).astype(BF16), kband, bias_ref[hh], n, sub_blocks)
                m = jnp.max(logits, axis=-1, keepdims=True)
                p = jnp.exp(logits - m)
                den = jnp.sum(p, axis=-1, keepdims=True)
                out = jnp.where(mk, _dot((p / den).astype(BF16), vband, _NN), out)
                lse = jnp.where(mk, m + jnp.log(den), lse)
            o_ref[cur, :] = out
            l_ref[cur, :] = lse
            return carry

        lax.fori_loop(0, SEQ // ATTN_BLK, blk, 0)

    qkv_specs, seq, tab = _attn_specs(col0)
    return pl.pallas_call(
        kern, name=name, grid=(2, NB), in_specs=qkv_specs + [tab],
        out_specs=[seq, seq], out_shape=[jax.ShapeDtypeStruct((T, 256), F32)] * 2,
        compiler_params=_cp(("parallel", "parallel")),
    )(qkv, qkv, qkv, bias)


def _attn_bwd(qkv, col0, o, lse, do, dl, bias, sub_blocks, *, name):
    def kern(q_ref, k_ref, v_ref, o_ref, l_ref, do_ref, dl_ref, bias_ref, dq_ref, dk_ref, dv_ref, dbias_ref, dk_acc, dv_acc):
        masks = _attn_head_masks()

        @pl.when(pl.program_id(1) == 0)
        def _():
            dbias_ref[...] = jnp.zeros_like(dbias_ref)

        dk_acc[...] = jnp.zeros_like(dk_acc)
        dv_acc[...] = jnp.zeros_like(dv_acc)

        def blk(n, carry):
            cur, prev = _attn_rows(n)
            q, o_, lse_, do_, dl_ = q_ref[cur, :], o_ref[cur, :], l_ref[cur, :], do_ref[cur, :], dl_ref[cur, :]
            kband, vband = _attn_band(k_ref, cur, prev), _attn_band(v_ref, cur, prev)
            dq = jnp.zeros((ATTN_BLK, 128), F32)
            dkb = jnp.zeros((2 * ATTN_BLK, 128), F32)
            dvb = jnp.zeros((2 * ATTN_BLK, 128), F32)
            for hh, mk in enumerate(masks):
                qm = jnp.where(mk, q, 0.0).astype(BF16)
                dom = jnp.where(mk, do_, 0.0)
                domb = dom.astype(BF16)
                logits = _attn_logits(qm, kband, bias_ref[hh], n, sub_blocks)
                p = jnp.exp(logits - jnp.max(jnp.where(mk, lse_, -jnp.inf), axis=-1, keepdims=True))
                dd = jnp.sum(dom * o_, axis=-1, keepdims=True)
                dls = jnp.sum(jnp.where(mk, dl_, 0.0), axis=-1, keepdims=True)
                ds = p * (_dot(domb, vband, _NT) - dd + dls)
                dbias_ref[hh] += ds
                dsb = (ds * 0.125).astype(BF16)
                dq = jnp.where(mk, _dot(dsb, kband, _NN), dq)
                dkb = dkb + _dot(dsb, qm, _TN)
                dvb = dvb + _dot(p.astype(BF16), domb, _TN)
            dq_ref[cur, :] = dq.astype(dq_ref.dtype)
            dk_acc[prev, :] += dkb[:ATTN_BLK]
            dk_acc[cur, :] += dkb[ATTN_BLK:]
            dv_acc[prev, :] += dvb[:ATTN_BLK]
            dv_acc[cur, :] += dvb[ATTN_BLK:]
            return carry

        lax.fori_loop(0, SEQ // ATTN_BLK, blk, 0)
        dk_ref[...] = dk_acc[...].astype(dk_ref.dtype)
        dv_ref[...] = dv_acc[...].astype(dv_ref.dtype)

    qkv_specs, seq, tab = _attn_specs(col0)
    return pl.pallas_call(
        kern, name=name, grid=(2, NB),
        in_specs=qkv_specs + [seq] * 4 + [tab], out_specs=[seq, seq, seq, tab],
        out_shape=[jax.ShapeDtypeStruct((T, 256), BF16)] * 3 + [jax.ShapeDtypeStruct((4, ATTN_BLK, 2 * ATTN_BLK), F32)],
        scratch_shapes=[pltpu.VMEM((SEQ, 128), F32), pltpu.VMEM((SEQ, 128), F32)],
        compiler_params=_cp(("arbitrary", "arbitrary")),
    )(qkv, qkv, qkv, o, lse, do, dl, bias)


def _to_sub(t, dil):
    if dil == 1:
        return t
    return t.reshape(NB, SEQ // dil, dil, t.shape[-1]).transpose(0, 2, 1, 3).reshape(T, t.shape[-1])


def _from_sub(t, dil):
    if dil == 1:
        return t
    return t.reshape(NB, dil, SEQ // dil, t.shape[-1]).transpose(0, 2, 1, 3).reshape(T, t.shape[-1])


_ANY = pl.BlockSpec(memory_space=pl.ANY)
MESH = pl.DeviceIdType.MESH


def _slot(dev):
    return 4 * dev[0] + 2 * dev[1] + dev[2]


def _all_gather(shards, *, name):
    n = len(shards)

    def kern(*refs):
        x_refs, out_refs = refs[:n], refs[n:2 * n]
        send_sems, recv_sems, local_sems = refs[2 * n:]
        x, y, c = lax.axis_index("x"), lax.axis_index("y"), lax.axis_index("c")
        me, sibling = (x, y, c), (x, y, 1 - c)
        chips = [(1 - x, y), (x, 1 - y), (1 - x, 1 - y)]

        def copies(k, block, to, own=False):
            return [pltpu.make_async_remote_copy(
                src_ref=x_refs[a] if own else out_refs[a].at[_slot(block)], dst_ref=out_refs[a].at[_slot(block)],
                send_sem=send_sems.at[k, a], recv_sem=recv_sems.at[k, a], device_id=to, device_id_type=MESH) for a in range(n)]

        mine = [pltpu.make_async_copy(x_refs[a], out_refs[a].at[_slot(me)], local_sems.at[a]) for a in range(n)]
        first = copies(0, me, sibling, own=True)
        for j, chip in enumerate(chips):
            first += copies(1 + j, me, (*chip, c), own=True)
        for cp in mine + first:
            cp.start()
        passed = []
        for j, chip in enumerate(chips):
            for cp in copies(1 + j, (*chip, c), me):
                cp.wait_recv()
            fwd = copies(4 + j, (*chip, c), sibling)
            for cp in fwd:
                cp.start()
            passed += fwd
        for cp in copies(0, sibling, me):
            cp.wait_recv()
        for j, chip in enumerate(chips):
            for cp in copies(4 + j, (*chip, 1 - c), me):
                cp.wait_recv()
        for cp in first + passed:
            cp.wait_send()
        for cp in mine:
            cp.wait()

    return pl.pallas_call(
        kern, name=name, in_specs=[_ANY] * n, out_specs=[_ANY] * n,
        out_shape=[jax.ShapeDtypeStruct((NDEV, *s.shape), s.dtype) for s in shards],
        scratch_shapes=[pltpu.SemaphoreType.DMA((7, n)), pltpu.SemaphoreType.DMA((7, n)), pltpu.SemaphoreType.DMA((n,))],
    )(*shards)


def _exchange_sibling(arrs, *, name):
    n = len(arrs)

    def kern(*refs):
        x_refs, out_refs = refs[:n], refs[n:2 * n]
        send_sems, recv_sems = refs[2 * n:]
        x, y, c = lax.axis_index("x"), lax.axis_index("y"), lax.axis_index("c")
        sends = [pltpu.make_async_remote_copy(
            src_ref=x_refs[a].at[1 - c], dst_ref=out_refs[a], send_sem=send_sems.at[a], recv_sem=recv_sems.at[a],
            device_id=(x, y, 1 - c), device_id_type=MESH) for a in range(n)]
        for cp in sends:
            cp.start()
        for cp in sends:
            cp.wait_recv()
        for cp in sends:
            cp.wait_send()

    return pl.pallas_call(
        kern, name=name, in_specs=[_ANY] * n, out_specs=[_ANY] * n,
        out_shape=[jax.ShapeDtypeStruct(a.shape[1:], a.dtype) for a in arrs],
        scratch_shapes=[pltpu.SemaphoreType.DMA((n,)), pltpu.SemaphoreType.DMA((n,))],
    )(*arrs)


def _exchange_chips(arrs, *, name):
    n = len(arrs)

    def kern(*refs):
        x_refs, out_refs = refs[:n], refs[n:2 * n]
        send_sems, recv_sems, local_sems = refs[2 * n:]
        x, y, c = lax.axis_index("x"), lax.axis_index("y"), lax.axis_index("c")
        my_chip = 2 * x + y
        chips = [(1 - x, y), (x, 1 - y), (1 - x, 1 - y)]
        mine = [pltpu.make_async_copy(x_refs[a].at[my_chip], out_refs[a].at[my_chip], local_sems.at[a]) for a in range(n)]
        sends = [pltpu.make_async_remote_copy(
            src_ref=x_refs[a].at[2 * px + py], dst_ref=out_refs[a].at[my_chip], send_sem=send_sems.at[k, a],
            recv_sem=recv_sems.at[k, a], device_id=(px, py, c), device_id_type=MESH)
            for k, (px, py) in enumerate(chips) for a in range(n)]
        for cp in mine + sends:
            cp.start()
        for k, (px, py) in enumerate(chips):
            for a in range(n):
                pltpu.make_async_remote_copy(
                    src_ref=x_refs[a].at[my_chip], dst_ref=out_refs[a].at[2 * px + py], send_sem=send_sems.at[k, a],
                    recv_sem=recv_sems.at[k, a], device_id=(px, py, c), device_id_type=MESH).wait_recv()
        for cp in sends:
            cp.wait_send()
        for cp in mine:
            cp.wait()

    return pl.pallas_call(
        kern, name=name, in_specs=[_ANY] * n, out_specs=[_ANY] * n,
        out_shape=[jax.ShapeDtypeStruct(a.shape, a.dtype) for a in arrs],
        scratch_shapes=[pltpu.SemaphoreType.DMA((3, n)), pltpu.SemaphoreType.DMA((3, n)), pltpu.SemaphoreType.DMA((n,))],
    )(*arrs)


def _add_own_half(full, other, c_idx, *, name):
    _, r, c = full.shape
    tile = _pick(r, (512, 256, 128, 64, 32, 16))

    def kern(c_ref, f_ref, o_ref, out_ref):
        out_ref[...] = (f_ref[...].astype(F32) + o_ref[...].astype(F32)).astype(BF16)

    return pl.pallas_call(
        kern, name=name,
        grid_spec=pltpu.PrefetchScalarGridSpec(
            num_scalar_prefetch=1, grid=(r // tile,),
            in_specs=[pl.BlockSpec((None, tile, c), lambda i, c_ref: (c_ref[0], i, 0)),
                      pl.BlockSpec((tile, c), lambda i, c_ref: (i, 0))],
            out_specs=pl.BlockSpec((tile, c), lambda i, c_ref: (i, 0))),
        out_shape=jax.ShapeDtypeStruct((r, c), BF16), compiler_params=_cp(("parallel",)),
    )(c_idx, full, other)


def _adamw_reduce(parts, w, m, v, *, name):
    ns, r, c = parts.shape
    tile = _pick(r, (256, 128, 64, 32, 16, 8))

    def kern(p_ref, w_ref, m_ref, v_ref, g_ref, d_ref, m2_ref, v2_ref):
        g = p_ref[0].astype(F32)
        for d in range(1, ns):
            g = g + p_ref[d].astype(F32)
        g_ref[...] = g
        d_ref[...], m2_ref[...], v2_ref[...] = _adam_math(w_ref[...], g, m_ref[...], v_ref[...])

    blk = pl.BlockSpec((tile, c), lambda i: (i, 0))
    return pl.pallas_call(
        kern, name=name, grid=(r // tile,), in_specs=[pl.BlockSpec((ns, tile, c), lambda i: (0, i, 0)), blk, blk, blk],
        out_specs=[blk] * 4, out_shape=[jax.ShapeDtypeStruct((r, c), F32)] * 4, compiler_params=_cp(("parallel",)),
    )(parts, w, m, v)


def _sum_slots(parts, *, name):
    ns, r, c = parts.shape
    tile = _pick(r, (256, 128, 64, 32, 16, 8))

    def kern(p_ref, o_ref):
        acc = p_ref[0].astype(F32)
        for d in range(1, ns):
            acc = acc + p_ref[d].astype(F32)
        o_ref[...] = acc

    return pl.pallas_call(
        kern, name=name, grid=(r // tile,), in_specs=[pl.BlockSpec((ns, tile, c), lambda i: (0, i, 0))],
        out_specs=pl.BlockSpec((tile, c), lambda i: (i, 0)), out_shape=jax.ShapeDtypeStruct((r, c), F32),
        compiler_params=_cp(("parallel",)),
    )(parts)


BIG = [
    ("ffn1_w13", (1024, 704), 1), ("ffn1_w2", (352, 1024), 0), ("w_in", (1024, 1154), 1), ("p_pool", (768, 128), 1),
    ("p_ssd", (128, 1024), 0), ("p_attn", (256, 128), 1), ("w_out", (128, 1024), 0), ("ffn2_w13", (1024, 704), 1),
    ("ffn2_w2", (352, 1024), 0),
]
BLOCKED = ("ffn1_w13", "ffn2_w13")


def _from_blocks(blk, ax):
    _, r, c = blk.shape
    return blk.transpose(1, 0, 2).reshape(r, NDEV * c) if ax == 1 else blk.reshape(NDEV * r, c)


def _to_blocks(full, ax):
    r, c = full.shape
    return full.reshape(r, NDEV, c // NDEV).transpose(1, 0, 2) if ax == 1 else full.reshape(NDEV, r // NDEV, c)


class _SmallPack:
    def __init__(self, shapes):
        self.shapes = shapes
        self.rows = [-(-int(np.prod(s)) // 128) for s in shapes]
        self.total = -(-sum(self.rows) // 256) * 256

    def pack(self, arrs):
        parts = []
        for a, s, r in zip(arrs, self.shapes, self.rows):
            assert tuple(a.shape) == tuple(s), (a.shape, s)
            flat = a.astype(F32).reshape(-1)
            parts.append(jnp.pad(flat, (0, r * 128 - flat.shape[0])).reshape(r, 128))
        parts.append(jnp.zeros((self.total - sum(self.rows), 128), F32))
        return jnp.concatenate(parts, axis=0)

    def unpack(self, packed):
        out, off = [], 0
        for s, r in zip(self.shapes, self.rows):
            out.append(packed[off:off + r].reshape(-1)[:int(np.prod(s))].reshape(s))
            off += r
        return out


def _w_in_to_padded(w):
    z = jnp.zeros((*w.shape[:-1], HC - DT0 - 16), w.dtype)
    qkv = [w[..., o + 256 * gi:o + 256 * gi + 256] for gi in range(3) for o in (O_Q, O_K, O_V)]
    return jnp.concatenate([w[..., O_Z:O_XBC], w[..., O_XBC:O_DT], w[..., O_G:O_END], *qkv, w[..., O_U:O_Z], w[..., O_DT:O_Q], z],
                           axis=-1)


def _w_in_from_padded(g):
    qkv = [g[..., QKV0 + 768 * gi + o:QKV0 + 768 * gi + o + 256] for o in (0, 256, 512) for gi in range(3)]
    return jnp.concatenate([g[..., U0:DT0], g[..., Z0:XBC0], g[..., XBC0:GATE0], g[..., DT0:DT0 + 16], *qkv, g[..., GATE0:QKV0]],
                           axis=-1)


def _block_diag(pw):
    out = jnp.zeros((POOL_W, POOL_W), pw.dtype)
    for gi in range(4):
        out = lax.dynamic_update_slice(out, pw[gi], (gi * POOL_GDIM, gi * POOL_GDIM))
    return out


def _expand_heads(v):
    return jnp.repeat(v, 64).reshape(1, D)


def _ffn_fwd(x, x16, w13, layer, w2, g, b, tag):
    act, h16 = _ffn_up(x16, w13, layer, name=f"{tag}_up")
    f = _mm(act, w2, name=f"{tag}_f")
    y, y16 = _ln_fwd(x, f, g, b, scale=FFN_RES, name=f"{tag}_ln")
    return y, y16, (h16, act, f)


def _ffn_bwd(x, x16, w13, layer, w2, g, b, saved, dy, tag):
    h16, act, f = saved
    dres, df, dg, db = _ln_bwd(x, f, dy, g, b, scale=FFN_RES, name=f"{tag}_ln_bwd")
    dw2 = _mm(act.transpose(0, 2, 1).reshape(DFF, T), df, out_dtype=BF16, name=f"{tag}_dw2")
    dh = _ffn_down_bwd(df, w2, h16, name=f"{tag}_dh")
    dx = _mm(dh, w13, tb=True, b_layer=layer, add=dres, name=f"{tag}_dx")
    dw13 = _mm(x16.T, dh, out_blocked=True, out_dtype=BF16, name=f"{tag}_dw13")
    return dx, dw13, dw2, dg[0], db[0]


def _layer_fwd(x0, x0_16, p, bias):
    x1, x1_16, s1 = _ffn_fwd(x0, x0_16, p["ffn1_w13"], p["layer"], p["ffn1_w2"], p["ln1_g"], p["ln1_b"], "ffn1")
    hcat = _mm(x1_16, p["w_in"], name="mix_in")
    pooled = _pool_fwd(hcat, name="pool")
    ya_lin = _mm(pooled, p["pool_bd"], name="pool_lin")
    ya_pre = _affine_fwd(ya_lin, p["pool_b"], p["pool_scale"], name="pool_affine")
    ya = _mm(ya_pre, p["p_pool"], name="pool_out")
    xa = _conv_fwd(hcat, p["conv_w"], p["conv_b"], name="conv")
    dtx = jnp.repeat(hcat[:, DT0:DT0 + SSD_HEADS], 64, axis=1)
    ysc, hs = _ssd_fwd(xa, dtx, p["dt_bias"], p["a_log"], p["d_skip"], name="ssd")
    yb_pre = _gnorm_fwd(ysc, hcat, p["ssd_norm"], name="ssd_norm")
    yb = _mm(yb_pre, p["p_ssd"], name="ssd_out")
    outs, lses, attn_sv = [], [], []
    for gi, (_, dil) in enumerate(ATTN_CONFIGS):
        c0 = QKV0 + 768 * gi
        qkv, col0 = (hcat, c0 // 128) if dil == 1 else (_to_sub(hcat[:, c0:c0 + 768], dil), 0)
        o, l = _attn_fwd(qkv, col0, bias[gi], SEQ // dil // ATTN_BLK, name=f"attn{gi}")
        attn_sv.append((None if dil == 1 else qkv, o, l))
        outs.append(_from_sub(o, dil))
        lses.append(_from_sub(l, dil))
    yc_pre = _amerge_fwd(outs, lses, name="attn_merge")
    yc = _mm(yc_pre, p["p_attn"], name="attn_out")
    merged = _merge_fwd(hcat, ya, yb, yc, p["gate_b"], name="merge")
    mix = _mm(merged, p["w_out"], name="mix_out")
    x2, x2_16 = _ln_fwd(x1, mix, p["ln2_g"], p["ln2_b"], scale=1.0, name="ln2")
    x3, x3_16, s3 = _ffn_fwd(x2, x2_16, p["ffn2_w13"], p["layer"], p["ffn2_w2"], p["ln3_g"], p["ln3_b"], "ffn2")
    saved = dict(x0=x0, x0_16=x0_16, s1=s1, x1=x1, x1_16=x1_16, hcat=hcat, pooled=pooled, ya_lin=ya_lin, ya_pre=ya_pre, ya=ya,
                 xa=xa, hs=hs, ysc=ysc, yb_pre=yb_pre, yb=yb, outs=outs, lses=lses, attn_sv=attn_sv, yc_pre=yc_pre, yc=yc, merged=merged, mix=mix,
                 x2=x2, x2_16=x2_16, s3=s3)
    return x3, x3_16, saved


def _layer_bwd(p, bias, sv, dx3):
    g = {}
    dx2, g["ffn2_w13"], g["ffn2_w2"], g["ln3_g"], g["ln3_b"] = _ffn_bwd(
        sv["x2"], sv["x2_16"], p["ffn2_w13"], p["layer"], p["ffn2_w2"], p["ln3_g"], p["ln3_b"], sv["s3"], dx3, "ffn2")
    hcat = sv["hcat"]
    dres, dmix, dg2, db2 = _ln_bwd(sv["x1"], sv["mix"], dx2, p["ln2_g"], p["ln2_b"], scale=1.0, name="ln2_bwd")
    g["ln2_g"], g["ln2_b"] = dg2[0], db2[0]
    dmerged = _mm(dmix, p["w_out"], tb=True, name="mix_out_dx")
    g["w_out"] = _mm(sv["merged"].T, dmix, out_dtype=BF16, name="mix_out_dw")
    dg0, dg1, dg2_, dya, dyb, dyc, gb0, gb1, gb2 = _merge_bwd(hcat, sv["ya"], sv["yb"], sv["yc"], dmerged, p["gate_b"],
                                                               name="merge_bwd")
    g["gate_b"] = jnp.concatenate([gb0, gb1, gb2], axis=0)
    dya_pre = _mm(dya, p["p_pool"], tb=True, name="pool_out_dx")
    g["p_pool"] = _mm(sv["ya_pre"].T, dya, out_dtype=BF16, name="pool_out_dw")
    dya_lin, dpb, dps = _affine_bwd(sv["ya_lin"], dya_pre, p["pool_b"], p["pool_scale"], name="pool_affine_bwd")
    g["pool_b"], g["pool_scale"] = dpb[0].reshape(4, POOL_GDIM), dps[0]
    dpooled = _mm(dya_lin, p["pool_bd"], tb=True, name="pool_lin_dx")
    dbd = _mm(sv["pooled"].T, dya_lin, name="pool_lin_dw")
    g["pool_w"] = jnp.stack([dbd[i * POOL_GDIM:(i + 1) * POOL_GDIM, i * POOL_GDIM:(i + 1) * POOL_GDIM] for i in range(4)])
    du = _pool_bwd(dpooled, name="pool_bwd")
    dyb_pre = _mm(dyb, p["p_ssd"], tb=True, name="ssd_out_dx")
    g["p_ssd"] = _mm(sv["yb_pre"].T, dyb, out_dtype=BF16, name="ssd_out_dw")
    dysc, dz, dnw = _gnorm_bwd(sv["ysc"], hcat, dyb_pre, p["ssd_norm"], name="ssd_norm_bwd")
    g["ssd_norm"] = dnw[0]
    dtx = jnp.repeat(hcat[:, DT0:DT0 + SSD_HEADS], 64, axis=1)
    dxs, dbm, dcm, ddtx, ddtb, dalog, ddsk = _ssd_bwd(sv["xa"], dtx, p["dt_bias"], p["a_log"], p["d_skip"], sv["hs"], dysc,
                                                     name="ssd_bwd")
    g["dt_bias"], g["a_log"], g["d_skip"] = ddtb[0, ::64], dalog[0, ::64], ddsk[0, ::64]
    dxa = jnp.concatenate([dxs, dbm, dcm], axis=1)
    dxbc, dcw, dcb = _conv_bwd(hcat, p["conv_w"], p["conv_b"], dxa, name="conv_bwd")
    g["conv_w"], g["conv_b"] = dcw, dcb[0]
    ddt = jnp.pad(ddtx[:, ::64], ((0, 0), (0, HC - DT0 - SSD_HEADS))).astype(BF16)
    dyc_pre = _mm(dyc, p["p_attn"], tb=True, name="attn_out_dx")
    g["p_attn"] = _mm(sv["yc_pre"].T, dyc, out_dtype=BF16, name="attn_out_dw")
    am = _amerge_bwd(sv["outs"], sv["lses"], dyc_pre, name="attn_merge_bwd")
    dqkv, dbias = [], []
    for gi, (_, dil) in enumerate(ATTN_CONFIGS):
        qkv_sub, o, l = sv["attn_sv"][gi]
        qkv, col0 = (hcat, (QKV0 + 768 * gi) // 128) if dil == 1 else (qkv_sub, 0)
        dq, dk, dv, dbi = _attn_bwd(qkv, col0, o, l, _to_sub(am[gi], dil), _to_sub(am[3 + gi], dil), bias[gi],
                                    SEQ // dil // ATTN_BLK, name=f"attn{gi}_bwd")
        dqkv += [_from_sub(t, dil) for t in (dq, dk, dv)]
        dbias.append(dbi)
    g["attn_bias"] = jnp.stack(dbias)
    dhcat = jnp.concatenate([dz, dxbc, dg0, dg1, dg2_, *dqkv, du, ddt], axis=1)
    dx1 = _mm(dhcat, p["w_in"], tb=True, add=dres, name="mix_in_dx")
    g["w_in"] = _mm(sv["x1_16"].T, dhcat, out_dtype=BF16, name="mix_in_dw")
    dx0, g["ffn1_w13"], g["ffn1_w2"], g["ln1_g"], g["ln1_b"] = _ffn_bwd(
        sv["x0"], sv["x0_16"], p["ffn1_w13"], p["layer"], p["ffn1_w2"], p["ln1_g"], p["ln1_b"], sv["s1"], dx1, "ffn1")
    return dx0, g


def _local_step(x, tgt, lw, rel_bias):
    bias, bias_vjp = jax.vjp(_attn_bias, rel_bias)
    y, y16, saved = x, x.astype(BF16), []
    for p in lw:
        y, y16, sv = _layer_fwd(y, y16, p, bias)
        saved.append(sv)
    dy, sq = _loss_kernel(y, tgt, name="loss")
    grads = [None] * len(lw)
    for l in reversed(range(len(lw))):
        dy, grads[l] = _layer_bwd(lw[l], bias, saved[l], dy)
    (d_rel,) = bias_vjp(sum(g.pop("attn_bias") for g in grads))
    return sq, dy, grads, d_rel


SMALL_REPL = ["ln1_g", "ln1_b", "pool_w", "pool_b", "pool_scale", "conv_b", "dt_bias", "a_log", "d_skip", "ssd_norm",
              "ln2_g", "ln2_b", "ln3_g", "ln3_b", "rel_bias"]
SMALL_SHARD = ["gate_b", "conv_w"]
WEIGHTS = ['ffn1_w13', 'ffn1_w2', 'ln1_g', 'ln1_b', 'w_in', 'gate_b', 'pool_w', 'pool_b', 'pool_scale', 'conv_w', 'conv_b',
           'dt_bias', 'a_log', 'd_skip', 'ssd_norm', 'rel_bias', 'p_pool', 'p_ssd', 'p_attn', 'w_out', 'ln2_g', 'ln2_b',
           'ffn2_w13', 'ffn2_w2', 'ln3_g', 'ln3_b']


def _step(w, m, v, x, tgt):
    my_c = lax.axis_index("c")
    dev = 4 * lax.axis_index("x") + 2 * lax.axis_index("y") + my_c
    gathered = _all_gather([w[n].astype(BF16) for n, _, _ in BIG], name="gather_weights")
    sp_in = _SmallPack([w[n].shape for n in SMALL_SHARD])
    small_g = _all_gather([sp_in.pack([w[n] for n in SMALL_SHARD])], name="gather_small")[0]
    gate_b = jnp.stack([sp_in.unpack(small_g[d])[0] for d in range(NDEV)], axis=2).reshape(DEPTH, 3, D)
    conv_w = jnp.stack([sp_in.unpack(small_g[d])[1] for d in range(NDEV)], axis=2).reshape(DEPTH, 4, 2048)
    lw = []
    for l in range(DEPTH):
        p = {"layer": l}
        for (n, _, ax), blocks in zip(BIG, gathered):
            p[n] = blocks if n in BLOCKED else _from_blocks(blocks[:, l], ax)
        p["w_in"] = _w_in_to_padded(p["w_in"])
        p["pool_bd"] = _block_diag(w["pool_w"][l]).astype(BF16)
        p["gate_b"], p["conv_w"] = gate_b[l], conv_w[l]
        for n in ("ln1_g", "ln1_b", "ln2_g", "ln2_b", "ln3_g", "ln3_b", "pool_scale", "conv_b", "ssd_norm"):
            p[n] = w[n][l][None, :]
        p["pool_b"] = w["pool_b"][l].reshape(1, POOL_W)
        for n in ("dt_bias", "a_log", "d_skip"):
            p[n] = _expand_heads(w[n][l])
        lw.append(p)
    sq, dx, grads, d_rel = _local_step(x.reshape(T, D), tgt.reshape(T, D), lw, w["rel_bias"])
    loss = lax.psum(jnp.sum(sq) * (0.5 / D), ("x", "y", "c"))
    for g in grads:
        g["w_in"] = _w_in_from_padded(g["w_in"])
    halves = []
    for n, (r, c), ax in BIG:
        per_layer = [(g[n] if n in BLOCKED else _to_blocks(g[n], ax)).reshape(4, 2, r, c) for g in grads]
        halves.append(jnp.stack([jnp.stack([blk[:, h] for blk in per_layer], axis=1) for h in (0, 1)]))
    from_sibling = _exchange_sibling(halves, name="exchange_grads_cores")
    c_idx = my_c.reshape(1).astype(jnp.int32)
    chip_sums = [_add_own_half(h.reshape(2, -1, h.shape[-1]), o.reshape(-1, o.shape[-1]), c_idx, name=f"add_cores_{n}")
                 .reshape(4, -1, h.shape[-1]) for h, o, (n, _, _) in zip(halves, from_sibling, BIG)]
    by_chip = _exchange_chips(chip_sums, name="exchange_grads_chips")
    small_names = SMALL_REPL + SMALL_SHARD
    small = {n: (d_rel if n == "rel_bias" else jnp.stack([g[n] for g in grads])) for n in small_names}
    sp = _SmallPack([small[n].shape for n in small_names])
    gsmall = sp.unpack(_sum_slots(_all_gather([sp.pack([small[n] for n in small_names])], name="gather_small_grads")[0],
                                  name="sum_small_grads"))
    gout = {}
    for n, gv in zip(small_names, gsmall):
        if n in SMALL_SHARD:
            width = w[n].shape[-1]
            gv = lax.dynamic_slice_in_dim(gv, dev * width, width, axis=2)
        gout[n] = gv
    delta, new_m, new_v = {}, {}, {}
    for (n, _, _), parts in zip(BIG, by_chip):
        shp = w[n].shape
        two_d = lambda a: a.reshape(-1, shp[-1])
        res = _adamw_reduce(parts, two_d(w[n]), two_d(m[n]), two_d(v[n]), name=f"adamw_{n}")
        gout[n], delta[n], new_m[n], new_v[n] = (t.reshape(shp) for t in res)
    spa = _SmallPack([w[n].shape for n in small_names])
    res = _adamw(spa.pack([w[n] for n in small_names]), spa.pack([gout[n] for n in small_names]),
                 spa.pack([m[n] for n in small_names]), spa.pack([v[n] for n in small_names]), name="adamw_small")
    for out, packed in zip((delta, new_m, new_v), res):
        for n, val in zip(small_names, spa.unpack(packed)):
            out[n] = val
    return (loss, dx.reshape(NB, SEQ, D), *[gout[n] for n in WEIGHTS], *[delta[n] for n in WEIGHTS],
            *[new_m[n] for n in WEIGHTS], *[new_v[n] for n in WEIGHTS])


def kernel(x, ffn1_w13, ffn1_w2, ln1_g, ln1_b, w_in, gate_b, pool_w, pool_b, pool_scale, conv_w, conv_b, dt_bias, a_log, d_skip, ssd_norm, rel_bias, p_pool, p_ssd, p_attn, w_out, ln2_g, ln2_b, ffn2_w13, ffn2_w2, ln3_g, ln3_b, loss_target, m_ffn1_w13, m_ffn1_w2, m_ln1_g, m_ln1_b, m_w_in, m_gate_b, m_pool_w, m_pool_b, m_pool_scale, m_conv_w, m_conv_b, m_dt_bias, m_a_log, m_d_skip, m_ssd_norm, m_rel_bias, m_p_pool, m_p_ssd, m_p_attn, m_w_out, m_ln2_g, m_ln2_b, m_ffn2_w13, m_ffn2_w2, m_ln3_g, m_ln3_b, v_ffn1_w13, v_ffn1_w2, v_ln1_g, v_ln1_b, v_w_in, v_gate_b, v_pool_w, v_pool_b, v_pool_scale, v_conv_w, v_conv_b, v_dt_bias, v_a_log, v_d_skip, v_ssd_norm, v_rel_bias, v_p_pool, v_p_ssd, v_p_attn, v_w_out, v_ln2_g, v_ln2_b, v_ffn2_w13, v_ffn2_w2, v_ln3_g, v_ln3_b):
    given = dict(locals())
    w = {n: given[n] for n in WEIGHTS}
    m = {n: given["m_" + n] for n in WEIGHTS}
    v = {n: given["v_" + n] for n in WEIGHTS}
    return _step(w, m, v, x, loss_target)
```

```python
import functools
import math

import numpy as np
import jax
import jax.numpy as jnp
from jax import lax
from jax.experimental import pallas as pl
from jax.experimental.pallas import tpu as pltpu

F32, BF16 = jnp.float32, jnp.bfloat16
HIGHEST = lax.Precision.HIGHEST

NDEV = 8
DEPTH = 4
D = 1024
SEQ = 2048
NB = 2
T = NB * SEQ
DFF = 2816
POOL_W = 768
POOL_WINDOWS = (2, 4, 8, 16)
POOL_GDIM = 192
SSD_HEADS = 16
CHUNK = 128
NCHUNK = SEQ // CHUNK
ATTN_CONFIGS = ((128, 1), (512, 4), (2048, 16))
ATTN_BLK = 128
REL_BUCKETS = 32
REL_MAX_DIST = 2048
LN_EPS = 1e-5
SSD_EPS = 1e-5
ALPHA = (2.0 * DEPTH) ** 0.25
FFN_RES = 0.5
NEG = -1e30

ADAM_LR, ADAM_B1, ADAM_B2, ADAM_EPS, ADAM_WD, ADAM_STEP = 0.001, 0.9, 0.999, 1e-08, 0.01, 10

Z0, XBC0, GATE0, QKV0, U0, DT0, HC = 0, 1024, 3072, 6144, 8448, 9216, 9728
O_U, O_Z, O_XBC, O_DT, O_Q, O_K, O_V, O_G, O_END = 0, 768, 1792, 3840, 3856, 4624, 5392, 6160, 9232

VMEM_LIMIT = 56 * 1024 * 1024


def _cp(sem):
    return pltpu.CompilerParams(dimension_semantics=sem, vmem_limit_bytes=VMEM_LIMIT)


def _pick(dim, cands):
    for c in cands:
        if dim % c == 0:
            return c
    return dim


def _mm(a, b, *, tb=False, out_dtype=F32, add=None, add_scale=1.0, out_blocked=False, bn=None, b_layer=None, carry=(), name):
    carry = list(carry)
    nc = len(carry)
    bk = None
    if a.ndim == 3:
        nkb, M, bk = a.shape
        K = nkb * bk
    else:
        M, K = a.shape
    b_blocked = b.ndim >= 3
    b_shape = b.shape if b_layer is None else (b.shape[0], *b.shape[2:])
    if b_blocked and tb:
        assert bk in (None, b_shape[2])
        nkb, N, bk = b_shape
        assert K == nkb * bk, (a.shape, b.shape)
    elif b_blocked:
        nnb, K2, bn = b_shape
        N = nnb * bn
        assert K == K2, (a.shape, b.shape)
    else:
        (N, K2) = b.shape if tb else b.shape[::-1]
        assert K == K2, (a.shape, b.shape, tb)
    bm = _pick(M, (1024, 768, 512, 256))
    if bn is None:
        bn = _pick(N, (1024, 768, 512, 256))
    if bk is None:
        bk = K if K <= 1024 else _pick(K, (1024, 1408, 2432, 512, 256))
    nk = K // bk
    dn = (((1,), (1 if tb else 0,)), ((), ()))

    grid = (M // bm, N // bn, nk)

    def kern(*refs):
        a_ref, b_ref = refs[0], refs[1]
        n_in = 2 + (add is not None)
        add_ref = refs[2] if add is not None else None
        x_refs, o_ref, g_refs = refs[n_in:n_in + nc], refs[n_in + nc], refs[n_in + nc + 1:n_in + 2 * nc + 1]
        scratch = refs[n_in + 2 * nc + 1:]
        acc_ref = scratch[0] if nk > 1 else None
        if nc:
            step = (pl.program_id(0) * grid[1] + pl.program_id(1)) * nk + pl.program_id(2)
            carried = _CarriedGather(x_refs, g_refs, scratch[-3:], step, grid[0] * grid[1] * nk)
            carried.before()
        p = lax.dot_general(a_ref[...].astype(BF16), b_ref[...].astype(BF16), dn, preferred_element_type=F32)

        def fin(acc):
            if add_ref is not None:
                acc = acc + add_scale * add_ref[...]
            o_ref[...] = acc.astype(out_dtype)

        if nk == 1:
            fin(p)
        else:
            k = pl.program_id(2)

            @pl.when(k == 0)
            def _():
                acc_ref[...] = p

            @pl.when(k > 0)
            def _():
                acc_ref[...] += p

            @pl.when(k == nk - 1)
            def _():
                fin(acc_ref[...])

        if nc:
            carried.after()

    if a.ndim == 3:
        a_spec = pl.BlockSpec((None, bm, bk), lambda i, j, k: (k, i, 0))
    else:
        a_spec = pl.BlockSpec((bm, bk), lambda i, j, k: (i, k))
    if b_blocked and b_layer is not None:
        b_spec = (pl.BlockSpec((None, None, bn, bk), lambda i, j, k: (k, b_layer, j, 0)) if tb else
                  pl.BlockSpec((None, None, bk, bn), lambda i, j, k: (j, b_layer, k, 0)))
    elif b_blocked and tb:
        b_spec = pl.BlockSpec((None, bn, bk), lambda i, j, k: (k, j, 0))
    elif b_blocked:
        b_spec = pl.BlockSpec((None, bk, bn), lambda i, j, k: (j, k, 0))
    elif tb:
        b_spec = pl.BlockSpec((bn, bk), lambda i, j, k: (j, k))
    else:
        b_spec = pl.BlockSpec((bk, bn), lambda i, j, k: (k, j))
    if out_blocked:
        o_spec, o_shape = pl.BlockSpec((None, bm, bn), lambda i, j, k: (j, i, 0)), (N // bn, M, bn)
    else:
        o_spec, o_shape = pl.BlockSpec((bm, bn), lambda i, j, k: (i, j)), (M, N)
    in_specs = [a_spec, b_spec]
    args = [a, b]
    if add is not None:
        assert add.shape == o_shape
        in_specs.append(o_spec)
        args.append(add)
    res = pl.pallas_call(
        kern, name=name, grid=grid,
        in_specs=in_specs + [_ANY] * nc, out_specs=[o_spec] + [_ANY] * nc,
        out_shape=[jax.ShapeDtypeStruct(o_shape, out_dtype)] + _gathered_shapes(carry),
        scratch_shapes=([pltpu.VMEM((bm, bn), F32)] if nk > 1 else []) + _gather_sems(nc),
        compiler_params=_cp(("arbitrary",) * 3 if nc else ("parallel", "parallel", "arbitrary")),
    )(*args, *carry)
    return (res[0], res[1:]) if nc else res[0]


def _rowwise(fn, rows, pars, outs, accs, *, name, tile, groups=1):
    n_rows = rows[0][0].shape[0]
    nt = n_rows // tile
    n_in = len(rows) + len(pars)
    n_out = len(outs)

    def kern(*refs):
        res = fn(*[r[...] for r in refs[:n_in]])
        for o_ref, val in zip(refs[n_in:n_in + n_out], res[:n_out]):
            o_ref[...] = val.astype(o_ref.dtype)
        i = pl.program_id(1)
        for a_ref, val in zip(refs[n_in + n_out:], res[n_out:]):
            @pl.when(i == 0)
            def _(a_ref=a_ref, val=val):
                a_ref[...] = val

            @pl.when(i > 0)
            def _(a_ref=a_ref, val=val):
                a_ref[...] += val

    in_specs = [pl.BlockSpec((tile, w), lambda g, i, c0=c0: (i, c0 + g)) for (_, w, c0) in rows]
    for p, w in pars:
        if w is None:
            in_specs.append(pl.BlockSpec(p.shape, lambda g, i: (0, 0)))
        else:
            in_specs.append(pl.BlockSpec((p.shape[0], w), lambda g, i: (0, g)))
    out_specs = [pl.BlockSpec((tile, w), lambda g, i: (i, g)) for (_, w, _) in outs]
    out_specs += [pl.BlockSpec((1, w), lambda g, i: (0, g)) for (_, w) in accs]
    out_shape = [jax.ShapeDtypeStruct((n_rows, c), dt) for (c, _, dt) in outs]
    out_shape += [jax.ShapeDtypeStruct((1, c), F32) for (c, _) in accs]
    return pl.pallas_call(
        kern, name=name, grid=(groups, nt), in_specs=in_specs, out_specs=out_specs, out_shape=out_shape,
        compiler_params=_cp(("arbitrary", "arbitrary")),
    )(*[r[0] for r in rows], *[p[0] for p in pars])


def _colsum(v):
    return jnp.sum(v, axis=0, keepdims=True)


def _silu(v):
    return v * jax.nn.sigmoid(v)


def _ln_fn(x, f, g, b, *, scale):
    pre = ALPHA * x + scale * f
    mu = jnp.mean(pre, axis=-1, keepdims=True)
    var = jnp.mean(jnp.square(pre - mu), axis=-1, keepdims=True)
    return (pre - mu) * lax.rsqrt(var + LN_EPS) * g + b


def _ln_fwd(x, f, g, b, *, scale, name):
    fn = lambda x_, f_, g_, b_: [_ln_fn(x_, f_, g_, b_, scale=scale)] * 2
    return _rowwise(fn, [(x, D, 0), (f, D, 0)], [(g, None), (b, None)], [(D, D, F32), (D, D, BF16)], [], name=name, tile=512)


def _ln_bwd(x, f, dy, g, b, *, scale, name):
    def fn(x_, f_, dy_, g_, b_):
        _, vjp = jax.vjp(functools.partial(_ln_fn, scale=scale), x_, f_, g_, b_)
        return list(vjp(dy_))

    return _rowwise(fn, [(x, D, 0), (f, D, 0), (dy, D, 0)], [(g, None), (b, None)],
                    [(D, D, F32), (D, D, BF16)], [(D, D), (D, D)], name=name, tile=512)


FB = 2 * DFF // NDEV
FFN_BM = 1024
_H_PAIR = pl.BlockSpec((2, None, FFN_BM, FB), lambda i, j: (0, j, i, 0))


def _ffn_up(x16, w13, *, carry=(), name):
    carry = list(carry)
    nc = len(carry)
    grid = (T // FFN_BM, 4)

    def kern(*refs):
        x_ref, w_ref = refs[0], refs[1]
        x_refs, (act_ref, h_ref), g_refs = refs[2:2 + nc], refs[2 + nc:4 + nc], refs[4 + nc:4 + 2 * nc]
        if nc:
            step = pl.program_id(0) * grid[1] + pl.program_id(1)
            carried = _CarriedGather(x_refs, g_refs, refs[4 + 2 * nc:], step, grid[0] * grid[1])
            carried.before()
        x = x_ref[...]
        ha = jnp.dot(x, w_ref[0], preferred_element_type=F32)
        hg = jnp.dot(x, w_ref[1], preferred_element_type=F32)
        act_ref[...] = (_silu(ha) * hg).astype(BF16)
        h_ref[0] = ha.astype(BF16)
        h_ref[1] = hg.astype(BF16)
        if nc:
            carried.after()

    res = pl.pallas_call(
        kern, name=name, grid=grid,
        in_specs=[pl.BlockSpec((FFN_BM, D), lambda i, j: (i, 0)),
                  pl.BlockSpec((2, None, D, FB), lambda i, j: (0, j, 0, 0))] + [_ANY] * nc,
        out_specs=[pl.BlockSpec((None, FFN_BM, FB), lambda i, j: (j, i, 0)), _H_PAIR] + [_ANY] * nc,
        out_shape=[jax.ShapeDtypeStruct((4, T, FB), BF16), jax.ShapeDtypeStruct((2, 4, T, FB), BF16)] + _gathered_shapes(carry),
        scratch_shapes=_gather_sems(nc),
        compiler_params=_cp(("arbitrary", "arbitrary") if nc else ("parallel", "parallel")),
    )(x16, w13.reshape(2, 4, D, FB), *carry)
    return res[0], res[1].reshape(NDEV, T, FB), res[2:]


def _ffn_down_bwd(df, w2, h16, *, name):
    def kern(df_ref, w_ref, h_ref, dh_ref):
        da = _dot(df_ref[...], w_ref[...], _NT)
        a, g = h_ref[0].astype(F32), h_ref[1].astype(F32)
        s = jax.nn.sigmoid(a)
        dh_ref[0] = (da * g * (s * (1.0 + a * (1.0 - s)))).astype(BF16)
        dh_ref[1] = (da * (a * s)).astype(BF16)

    return pl.pallas_call(
        kern, name=name, grid=(T // FFN_BM, 4),
        in_specs=[pl.BlockSpec((FFN_BM, D), lambda i, j: (i, 0)), pl.BlockSpec((FB, D), lambda i, j: (j, 0)), _H_PAIR],
        out_specs=_H_PAIR, out_shape=jax.ShapeDtypeStruct((2, 4, T, FB), BF16),
        compiler_params=_cp(("parallel", "parallel")),
    )(df, w2, h16.reshape(2, 4, T, FB)).reshape(NDEV, T, FB)


def _affine_fn(y, b, s):
    return (y + b) * s


def _affine_fwd(y, b, s, *, name):
    return _rowwise(lambda y_, b_, s_: [_affine_fn(y_, b_, s_)], [(y, POOL_W, 0)], [(b, None), (s, None)],
                    [(POOL_W, POOL_W, BF16)], [], name=name, tile=512)[0]


def _affine_bwd(y, dyo, b, s, *, name):
    def fn(y_, d_, b_, s_):
        _, vjp = jax.vjp(_affine_fn, y_, b_, s_)
        return list(vjp(d_))

    return _rowwise(fn, [(y, POOL_W, 0), (dyo, POOL_W, 0)], [(b, None), (s, None)],
                    [(POOL_W, POOL_W, BF16)], [(POOL_W, POOL_W), (POOL_W, POOL_W)], name=name, tile=512)


def _gnorm_fn(y, z, w):
    yz = y * _silu(z)
    return yz * lax.rsqrt(jnp.mean(jnp.square(yz), axis=-1, keepdims=True) + SSD_EPS) * w


def _gnorm_fwd(y, hcat, w, *, name):
    return _rowwise(lambda y_, z_, w_: [_gnorm_fn(y_, z_, w_)], [(y, 256, 0), (hcat, 256, Z0 // 256)], [(w, 256)],
                    [(D, 256, BF16)], [], name=name, tile=512, groups=4)[0]


def _gnorm_bwd(y, hcat, dyo, w, *, name):
    def fn(y_, z_, d_, w_):
        _, vjp = jax.vjp(_gnorm_fn, y_, z_, w_)
        return list(vjp(d_))

    return _rowwise(fn, [(y, 256, 0), (hcat, 256, Z0 // 256), (dyo, 256, 0)], [(w, 256)],
                    [(D, 256, F32), (D, 256, BF16)], [(D, 256)], name=name, tile=512, groups=4)


def _merge_fn(g0, g1, g2, ya, yb, yc, b0, b1, b2):
    return jax.nn.sigmoid(g0 + b0) * ya + jax.nn.sigmoid(g1 + b1) * yb + jax.nn.sigmoid(g2 + b2) * yc


def _merge_rows(hcat, ya, yb, yc):
    c = GATE0 // 256
    return [(hcat, 256, c), (hcat, 256, c + 4), (hcat, 256, c + 8), (ya, 256, 0), (yb, 256, 0), (yc, 256, 0)]


def _merge_fwd(hcat, ya, yb, yc, gb, *, name):
    pars = [(gb[0:1], 256), (gb[1:2], 256), (gb[2:3], 256)]
    return _rowwise(lambda *v: [_merge_fn(*v)], _merge_rows(hcat, ya, yb, yc), pars, [(D, 256, BF16)], [],
                    name=name, tile=512, groups=4)[0]


def _merge_bwd(hcat, ya, yb, yc, dm, gb, *, name):
    def fn(g0, g1, g2, ya_, yb_, yc_, dm_, b0, b1, b2):
        _, vjp = jax.vjp(_merge_fn, g0, g1, g2, ya_, yb_, yc_, b0, b1, b2)
        return list(vjp(dm_))

    pars = [(gb[0:1], 256), (gb[1:2], 256), (gb[2:3], 256)]
    return _rowwise(fn, _merge_rows(hcat, ya, yb, yc) + [(dm, 256, 0)], pars,
                    [(D, 256, BF16)] * 6, [(D, 256)] * 3, name=name, tile=512, groups=4)


def _amerge_fn(o0, o1, o2, l0, l1, l2):
    m = jnp.maximum(jnp.maximum(l0, l1), l2)
    e0, e1, e2 = jnp.exp(l0 - m), jnp.exp(l1 - m), jnp.exp(l2 - m)
    return (e0 * o0 + e1 * o1 + e2 * o2) / (e0 + e1 + e2)


def _amerge_fwd(os_, ls_, *, name):
    rows = [(v, 256, 0) for v in (*os_, *ls_)]
    return _rowwise(lambda *v: [_amerge_fn(*v)], rows, [], [(256, 256, BF16)], [], name=name, tile=1024)[0]


def _amerge_bwd(os_, ls_, dy, *, name):
    def fn(*v):
        _, vjp = jax.vjp(_amerge_fn, *v[:6])
        return list(vjp(v[6]))

    rows = [(v, 256, 0) for v in (*os_, *ls_, dy)]
    return _rowwise(fn, rows, [], [(256, 256, F32)] * 6, [], name=name, tile=1024)


def _loss_kernel(y, tgt, *, name):
    def fn(y_, t_):
        e = y_ - t_
        return [e * (1.0 / D), _colsum(e * e)]

    return _rowwise(fn, [(y, D, 0), (tgt, D, 0)], [], [(D, D, F32)], [(D, D)], name=name, tile=512)


def _adam_math(w, g, m, v):
    m2 = ADAM_B1 * m + (1.0 - ADAM_B1) * g
    v2 = ADAM_B2 * v + (1.0 - ADAM_B2) * jnp.square(g)
    m_hat = m2 / (1.0 - ADAM_B1 ** ADAM_STEP)
    v_hat = v2 / (1.0 - ADAM_B2 ** ADAM_STEP)
    return -ADAM_LR * (m_hat / (jnp.sqrt(v_hat) + ADAM_EPS) + ADAM_WD * w), m2, v2


def _adamw(w, g, m, v, *, name):
    r, c = w.shape
    tile = _pick(r, (512, 256, 128, 64, 32, 16, 8))
    fn = lambda *a: list(_adam_math(*a))
    return _rowwise(fn, [(w, c, 0), (g, c, 0), (m, c, 0), (v, c, 0)], [], [(c, c, F32)] * 3, [], name=name, tile=tile)


def _pool_lane_window(j, width):
    lane = lax.broadcasted_iota(jnp.int32, (1, width), 1) + j * width
    grp = lane // POOL_GDIM
    return grp


def _pool_select(grp, vals):
    out = vals[3]
    for gi in (2, 1, 0):
        out = jnp.where(grp == gi, vals[gi], out)
    return out


def _pool_fwd(hcat, *, name):
    width = 256

    def kern(u_ref, o_ref):
        u = u_ref[...]
        t = lax.broadcasted_iota(jnp.int32, (SEQ, 1), 0)
        grp = _pool_lane_window(pl.program_id(1), width)

        def shift(v, k):
            return jnp.where(t >= k, pltpu.roll(v, k, 0), 0.0)

        s2 = u + shift(u, 1)
        s4 = s2 + shift(s2, 2)
        s8 = s4 + shift(s4, 4)
        s16 = s8 + shift(s8, 8)
        win = _pool_select(grp, [jnp.full((1, width), float(w), F32) for w in POOL_WINDOWS])
        cnt = jnp.minimum((t + 1).astype(F32), win)
        o_ref[...] = (_pool_select(grp, [s2, s4, s8, s16]) / cnt - u).astype(o_ref.dtype)

    return pl.pallas_call(
        kern, name=name, grid=(NB, POOL_W // width),
        in_specs=[pl.BlockSpec((SEQ, width), lambda b, j: (b, U0 // width + j))],
        out_specs=pl.BlockSpec((SEQ, width), lambda b, j: (b, j)),
        out_shape=jax.ShapeDtypeStruct((T, POOL_W), BF16), compiler_params=_cp(("parallel", "parallel")),
    )(hcat)


def _pool_bwd(dp, *, name):
    width = 256

    def kern(d_ref, o_ref):
        d = d_ref[...]
        t = lax.broadcasted_iota(jnp.int32, (SEQ, 1), 0)
        grp = _pool_lane_window(pl.program_id(1), width)
        win = _pool_select(grp, [jnp.full((1, width), float(w), F32) for w in POOL_WINDOWS])
        dm = d / jnp.minimum((t + 1).astype(F32), win)

        def shift(v, k):
            return jnp.where(t < SEQ - k, pltpu.roll(v, SEQ - k, 0), 0.0)

        r2 = dm + shift(dm, 1)
        r4 = r2 + shift(r2, 2)
        r8 = r4 + shift(r4, 4)
        r16 = r8 + shift(r8, 8)
        o_ref[...] = (_pool_select(grp, [r2, r4, r8, r16]) - d).astype(o_ref.dtype)

    return pl.pallas_call(
        kern, name=name, grid=(NB, POOL_W // width),
        in_specs=[pl.BlockSpec((SEQ, width), lambda b, j: (b, j))],
        out_specs=pl.BlockSpec((SEQ, width), lambda b, j: (b, j)),
        out_shape=jax.ShapeDtypeStruct((T, POOL_W), BF16), compiler_params=_cp(("parallel", "parallel")),
    )(dp)


CONV_W = 512


def _conv_pre(x, w, b, t):
    pre = w[3:4] * x + b
    for k in (1, 2, 3):
        pre = pre + w[3 - k:4 - k] * jnp.where(t >= k, pltpu.roll(x, k, 0), 0.0)
    return pre


def _conv_fwd(hcat, w, b, *, name):
    def kern(x_ref, w_ref, b_ref, o_ref):
        t = lax.broadcasted_iota(jnp.int32, (SEQ, 1), 0)
        o_ref[...] = _silu(_conv_pre(x_ref[...], w_ref[...], b_ref[...], t))

    return pl.pallas_call(
        kern, name=name, grid=(NB, 2048 // CONV_W),
        in_specs=[pl.BlockSpec((SEQ, CONV_W), lambda s, j: (s, XBC0 // CONV_W + j)),
                  pl.BlockSpec((4, CONV_W), lambda s, j: (0, j)), pl.BlockSpec((1, CONV_W), lambda s, j: (0, j))],
        out_specs=pl.BlockSpec((SEQ, CONV_W), lambda s, j: (s, j)),
        out_shape=jax.ShapeDtypeStruct((T, 2048), F32), compiler_params=_cp(("parallel", "parallel")),
    )(hcat, w, b)


def _conv_bwd(hcat, w, b, dy, *, name):
    def kern(x_ref, w_ref, b_ref, dy_ref, dx_ref, dw_ref, db_ref):
        t = lax.broadcasted_iota(jnp.int32, (SEQ, 1), 0)
        x, w_ = x_ref[...], w_ref[...]
        pre = _conv_pre(x, w_, b_ref[...], t)
        s = jax.nn.sigmoid(pre)
        dpre = dy_ref[...] * (s * (1.0 + pre * (1.0 - s)))
        dx = w_[3:4] * dpre
        dws = [None] * 4
        dws[3] = _colsum(dpre * x)
        for k in (1, 2, 3):
            dx = dx + w_[3 - k:4 - k] * jnp.where(t < SEQ - k, pltpu.roll(dpre, SEQ - k, 0), 0.0)
            dws[3 - k] = _colsum(dpre * jnp.where(t >= k, pltpu.roll(x, k, 0), 0.0))
        dx_ref[...] = dx.astype(dx_ref.dtype)
        db = _colsum(dpre)
        first = pl.program_id(1) == 0

        @pl.when(first)
        def _():
            for k in range(4):
                dw_ref[k:k + 1, :] = dws[k]
            db_ref[...] = db

        @pl.when(jnp.logical_not(first))
        def _():
            for k in range(4):
                dw_ref[k:k + 1, :] += dws[k]
            db_ref[...] += db

    return pl.pallas_call(
        kern, name=name, grid=(2048 // CONV_W, NB),
        in_specs=[pl.BlockSpec((SEQ, CONV_W), lambda j, s: (s, XBC0 // CONV_W + j)),
                  pl.BlockSpec((4, CONV_W), lambda j, s: (0, j)), pl.BlockSpec((1, CONV_W), lambda j, s: (0, j)),
                  pl.BlockSpec((SEQ, CONV_W), lambda j, s: (s, j))],
        out_specs=[pl.BlockSpec((SEQ, CONV_W), lambda j, s: (s, j)), pl.BlockSpec((4, CONV_W), lambda j, s: (0, j)),
                   pl.BlockSpec((1, CONV_W), lambda j, s: (0, j))],
        out_shape=[jax.ShapeDtypeStruct((T, 2048), BF16), jax.ShapeDtypeStruct((4, 2048), F32),
                   jax.ShapeDtypeStruct((1, 2048), F32)],
        compiler_params=_cp(("arbitrary", "arbitrary")),
    )(hcat, w, b, dy)


def _softplus(v):
    return jnp.maximum(v, 0.0) + jnp.log1p(jnp.exp(-jnp.abs(v)))


def _dot(a, b, dims):
    return lax.dot_general(a, b, (dims, ((), ())), preferred_element_type=F32)


_NN, _NT, _TN = ((1,), (0,)), ((1,), (1,)), ((0,), (0,))


def _ssd_specs(order):
    def spec(width, col0):
        return pl.BlockSpec((SEQ, width), lambda i, j, c0=col0: (order(i, j)[0], c0 + order(i, j)[1]))

    def par():
        return pl.BlockSpec((1, 256), lambda i, j: (0, order(i, j)[1]))

    return spec, par


def _ssd_chunk_common(c, xs_ref, b_ref, c_ref, dtx_ref, dtb, a, trif):
    r0 = pl.multiple_of(c * CHUNK, CHUNK)
    rows = pl.ds(r0, CHUNK)
    x = xs_ref[rows, :]
    bb = b_ref[rows, :].astype(BF16)
    cb = c_ref[rows, :].astype(BF16)
    raw = dtx_ref[rows, :] + dtb
    dt = _softplus(raw)
    cs = jnp.dot(trif, dt * a, precision=HIGHEST, preferred_element_type=F32)
    return rows, x, bb, cb, raw, dt, cs


def _head_decay(cs, me, tri):
    cse = jnp.max(jnp.where(me, cs, -jnp.inf), axis=1, keepdims=True)
    csb = jnp.broadcast_to(cse, (CHUNK, CHUNK))
    return jnp.where(tri, jnp.exp(csb - csb.T), 0.0)


def _ssd_fwd(xa, dtx, dtb, alog, dsk, *, name):
    spec, par = _ssd_specs(lambda g, b: (b, g))

    def kern(xs_ref, b_ref, c_ref, dtx_ref, dtb_ref, alog_ref, dsk_ref, y_ref, hs_ref, h_scr):
        row = lax.broadcasted_iota(jnp.int32, (CHUNK, CHUNK), 0)
        col = lax.broadcasted_iota(jnp.int32, (CHUNK, CHUNK), 1)
        tri = row >= col
        trif = tri.astype(F32)
        head = lax.broadcasted_iota(jnp.int32, (1, 256), 1) // 64
        dtb_, a, dsk_ = dtb_ref[...], -jnp.exp(alog_ref[...]), dsk_ref[...]
        h_scr[...] = jnp.zeros_like(h_scr)

        def chunk(c, carry):
            rows, x, bb, cb, _, dt, cs = _ssd_chunk_common(c, xs_ref, b_ref, c_ref, dtx_ref, dtb_, a, trif)
            cs_last = cs[CHUNK - 1:CHUNK, :]
            xdt = x * dt
            xdtb = xdt.astype(BF16)
            g = _dot(cb, bb, _NT)
            hin = h_scr[...]
            hs_ref[rows, :] = hin
            y = jnp.exp(cs) * _dot(cb, hin.astype(BF16), _NN) + dsk_ * x
            for e in range(4):
                me = head == e
                m = (g * _head_decay(cs, me, tri)).astype(BF16)
                y = y + jnp.where(me, _dot(m, xdtb, _NN), 0.0)
            y_ref[rows, :] = y
            st = _dot(bb, (jnp.exp(cs_last - cs) * xdt).astype(BF16), _TN)
            h_scr[...] = hin * jnp.exp(cs_last) + st
            return carry

        lax.fori_loop(0, NCHUNK, chunk, 0)

    return pl.pallas_call(
        kern, name=name, grid=(4, NB),
        in_specs=[spec(256, 0), spec(128, 8), spec(128, 12), spec(256, 0), par(), par(), par()],
        out_specs=[spec(256, 0), spec(256, 0)],
        out_shape=[jax.ShapeDtypeStruct((T, D), F32), jax.ShapeDtypeStruct((T, D), F32)],
        scratch_shapes=[pltpu.VMEM((CHUNK, 256), F32)],
        compiler_params=_cp(("parallel", "parallel")),
    )(xa, xa, xa, dtx, dtb, alog, dsk)


def _ssd_bwd(xa, dtx, dtb, alog, dsk, hs, dy, *, name):
    spec, par = _ssd_specs(lambda g, b: (b, g))

    def kern(xs_ref, b_ref, c_ref, dtx_ref, dtb_ref, alog_ref, dsk_ref, hs_ref, dy_ref,
             dx_ref, db_ref, dc_ref, ddt_ref, ddtb_ref, dalog_ref, ddsk_ref, dh_scr):
        row = lax.broadcasted_iota(jnp.int32, (CHUNK, CHUNK), 0)
        col = lax.broadcasted_iota(jnp.int32, (CHUNK, CHUNK), 1)
        tri = row >= col
        trif = tri.astype(F32)
        trit = (row <= col).astype(F32)
        is_last = lax.broadcasted_iota(jnp.int32, (CHUNK, 1), 0) == CHUNK - 1
        head = lax.broadcasted_iota(jnp.int32, (1, 256), 1) // 64
        dtb_, a, dsk_ = dtb_ref[...], -jnp.exp(alog_ref[...]), dsk_ref[...]
        dh_scr[...] = jnp.zeros_like(dh_scr)

        @pl.when(pl.program_id(1) == 0)
        def _():
            ddtb_ref[...] = jnp.zeros_like(ddtb_ref)
            dalog_ref[...] = jnp.zeros_like(dalog_ref)
            ddsk_ref[...] = jnp.zeros_like(ddsk_ref)

        def hsum(v, me):
            return jnp.sum(jnp.where(me, v, 0.0), axis=1, keepdims=True)

        def chunk(ci, carry):
            c = NCHUNK - 1 - ci
            rows, x, bb, cb, raw, dt, cs = _ssd_chunk_common(c, xs_ref, b_ref, c_ref, dtx_ref, dtb_, a, trif)
            cs_last = cs[CHUNK - 1:CHUNK, :]
            ecs = jnp.exp(cs)
            dsx = jnp.exp(cs_last - cs)
            xdt = x * dt
            xdtb = xdt.astype(BF16)
            g = _dot(cb, bb, _NT)
            hin = hs_ref[rows, :]
            hinb = hin.astype(BF16)
            dy_ = dy_ref[rows, :]
            dyb = dy_.astype(BF16)
            dh = dh_scr[...]
            dhb = dh.astype(BF16)
            bdh = _dot(bb, dhb, _NN)
            yoff = ecs * _dot(cb, hinb, _NN)
            dxdt = dsx * bdh
            t1, t2, t3 = dy_ * yoff, xdt * bdh, dh * hin
            dg = jnp.zeros((CHUNK, CHUNK), F32)
            dcs = jnp.zeros((CHUNK, 256), F32)
            for e in range(4):
                me = head == e
                l_ = _head_decay(cs, me, tri)
                m = g * l_
                dxdt = dxdt + jnp.where(me, _dot(m.astype(BF16), dyb, _TN), 0.0)
                dm = _dot(jnp.where(me, dy_, 0.0).astype(BF16), xdtb, _NT)
                dg = dg + dm * l_
                w = dm * m
                dds = hsum(t2, me)
                dse = jnp.max(jnp.where(me, dsx, -jnp.inf), axis=1, keepdims=True)
                ecl = jnp.max(jnp.where(me, jnp.exp(cs_last), -jnp.inf), axis=1, keepdims=True)
                dcs_e = hsum(t1, me) + jnp.sum(w, axis=1, keepdims=True) - jnp.sum(w.T, axis=1, keepdims=True) - dds * dse
                last = jnp.sum(dds * dse, axis=0, keepdims=True) + ecl * jnp.sum(hsum(t3, me), axis=0, keepdims=True)
                dcs_e = dcs_e + jnp.where(is_last, last, 0.0)
                dcs = dcs + jnp.where(me, dcs_e, 0.0)
            dadt = jnp.dot(trit, dcs, precision=HIGHEST, preferred_element_type=F32)
            ddt = a * dadt
            dxx = dxdt * x
            dyx = dy_ * x
            dsk_acc = jnp.zeros((1, 256), F32)
            for e in range(4):
                me = head == e
                ddt = ddt + jnp.where(me, hsum(dxx, me), 0.0)
                dsk_acc = dsk_acc + jnp.where(me, jnp.sum(hsum(dyx, me), axis=0, keepdims=True), 0.0)
            draw = ddt * jax.nn.sigmoid(raw)
            ddt_ref[rows, :] = draw
            ddtb_ref[...] += _colsum(draw)
            dalog_ref[...] += _colsum(dadt * dt) * a
            ddsk_ref[...] += dsk_acc
            dx_ref[rows, :] = dxdt * dt + dsk_ * dy_
            edy = (ecs * dy_).astype(BF16)
            dgb = dg.astype(BF16)
            dc_ref[rows, :] = _dot(dgb, bb, _NN) + _dot(edy, hinb, _NT)
            db_ref[rows, :] = _dot(dgb, cb, _TN) + _dot((dsx * xdt).astype(BF16), dhb, _NT)
            dh_scr[...] = jnp.exp(cs_last) * dh + _dot(cb, edy, _TN)
            return carry

        lax.fori_loop(0, NCHUNK, chunk, 0)

    return pl.pallas_call(
        kern, name=name, grid=(4, NB),
        in_specs=[spec(256, 0), spec(128, 8), spec(128, 12), spec(256, 0), par(), par(), par(), spec(256, 0), spec(256, 0)],
        out_specs=[spec(256, 0), spec(128, 0), spec(128, 0), spec(256, 0), par(), par(), par()],
        out_shape=[jax.ShapeDtypeStruct((T, D), F32), jax.ShapeDtypeStruct((T, 512), F32), jax.ShapeDtypeStruct((T, 512), F32),
                   jax.ShapeDtypeStruct((T, D), F32), jax.ShapeDtypeStruct((1, D), F32), jax.ShapeDtypeStruct((1, D), F32),
                   jax.ShapeDtypeStruct((1, D), F32)],
        scratch_shapes=[pltpu.VMEM((CHUNK, 256), F32)],
        compiler_params=_cp(("arbitrary", "arbitrary")),
    )(xa, xa, xa, dtx, dtb, alog, dsk, hs, dy)


def _t5_bucket_np(dist):
    dist = np.maximum(dist, 0)
    max_exact = REL_BUCKETS // 2
    large = max_exact + (np.log(np.maximum(dist, 1) / max_exact) / np.log(REL_MAX_DIST / max_exact)
                         * (REL_BUCKETS - max_exact)).astype(np.int32)
    large = np.minimum(large, REL_BUCKETS - 1)
    return np.where(dist < max_exact, dist, large).astype(np.int32)


def _attn_bias(rel_bias):
    qi = np.arange(ATTN_BLK)[:, None]
    kj = np.arange(2 * ATTN_BLK)[None, :]
    delta = qi - kj + ATTN_BLK
    out = []
    for gi, (window, dil) in enumerate(ATTN_CONFIGS):
        in_band = (delta >= 0) & (delta <= window // dil)
        bucket = jnp.asarray(_t5_bucket_np(delta * dil).reshape(-1, 1))
        one_hot = (bucket == jnp.arange(REL_BUCKETS)[None, :]).astype(F32)
        tab = jnp.dot(one_hot, rel_bias[:, 4 * gi:4 * gi + 4], precision=HIGHEST).reshape(ATTN_BLK, 2 * ATTN_BLK, 4)
        out.append(jnp.where(jnp.asarray(in_band)[None], tab.transpose(2, 0, 1), NEG))
    return jnp.stack(out)


def _attn_band(ref, cur, prev):
    return jnp.concatenate([ref[prev, :], ref[cur, :]], axis=0).astype(BF16)


def _attn_head_masks():
    hi = lax.broadcasted_iota(jnp.int32, (1, 128), 1) >= 64
    return [jnp.logical_not(hi), hi]


def _attn_logits(qm, kband, bias, n, sub_blocks):
    kj = lax.broadcasted_iota(jnp.int32, (1, 2 * ATTN_BLK), 1)
    ok = jnp.logical_or(n % sub_blocks > 0, kj >= ATTN_BLK)
    return jnp.where(ok, _dot(qm, kband, _NT) * 0.125 + bias, NEG)


def _attn_specs(col0):
    qkv = [pl.BlockSpec((SEQ, 128), lambda p, s, c=col0 + 2 * i: (s, c + p)) for i in range(3)]
    seq = pl.BlockSpec((SEQ, 128), lambda p, s: (s, p))
    tab = pl.BlockSpec((2, ATTN_BLK, 2 * ATTN_BLK), lambda p, s: (p, 0, 0))
    return qkv, seq, tab


def _attn_rows(n):
    cur = pl.ds(pl.multiple_of(n * ATTN_BLK, ATTN_BLK), ATTN_BLK)
    prev = pl.ds(pl.multiple_of(jnp.maximum(n - 1, 0) * ATTN_BLK, ATTN_BLK), ATTN_BLK)
    return cur, prev


def _attn_fwd(qkv, col0, bias, sub_blocks, *, name):
    def kern(q_ref, k_ref, v_ref, bias_ref, o_ref, l_ref):
        masks = _attn_head_masks()

        def blk(n, carry):
            cur, prev = _attn_rows(n)
            q = q_ref[cur, :]
            kband, vband = _attn_band(k_ref, cur, prev), _attn_band(v_ref, cur, prev)
            out = jnp.zeros((ATTN_BLK, 128), F32)
            lse = jnp.zeros((ATTN_BLK, 128), F32)
            for hh, mk in enumerate(masks):
                logits = _attn_logits(jnp.where(mk, q, 0.0).astype(BF16), kband, bias_ref[hh], n, sub_blocks)
                m = jnp.max(logits, axis=-1, keepdims=True)
                p = jnp.exp(logits - m)
                den = jnp.sum(p, axis=-1, keepdims=True)
                out = jnp.where(mk, _dot((p / den).astype(BF16), vband, _NN), out)
                lse = jnp.where(mk, m + jnp.log(den), lse)
            o_ref[cur, :] = out
            l_ref[cur, :] = lse
            return carry

        lax.fori_loop(0, SEQ // ATTN_BLK, blk, 0)

    qkv_specs, seq, tab = _attn_specs(col0)
    return pl.pallas_call(
        kern, name=name, grid=(2, NB), in_specs=qkv_specs + [tab],
        out_specs=[seq, seq], out_shape=[jax.ShapeDtypeStruct((T, 256), F32)] * 2,
        compiler_params=_cp(("parallel", "parallel")),
    )(qkv, qkv, qkv, bias)


def _attn_bwd(qkv, col0, o, lse, do, dl, bias, sub_blocks, *, name):
    def kern(q_ref, k_ref, v_ref, o_ref, l_ref, do_ref, dl_ref, bias_ref, dq_ref, dk_ref, dv_ref, dbias_ref, dk_acc, dv_acc):
        masks = _attn_head_masks()

        @pl.when(pl.program_id(1) == 0)
        def _():
            dbias_ref[...] = jnp.zeros_like(dbias_ref)

        dk_acc[...] = jnp.zeros_like(dk_acc)
        dv_acc[...] = jnp.zeros_like(dv_acc)

        def blk(n, carry):
            cur, prev = _attn_rows(n)
            q, o_, lse_, do_, dl_ = q_ref[cur, :], o_ref[cur, :], l_ref[cur, :], do_ref[cur, :], dl_ref[cur, :]
            kband, vband = _attn_band(k_ref, cur, prev), _attn_band(v_ref, cur, prev)
            dq = jnp.zeros((ATTN_BLK, 128), F32)
            dkb = jnp.zeros((2 * ATTN_BLK, 128), F32)
            dvb = jnp.zeros((2 * ATTN_BLK, 128), F32)
            for hh, mk in enumerate(masks):
                qm = jnp.where(mk, q, 0.0).astype(BF16)
                dom = jnp.where(mk, do_, 0.0)
                domb = dom.astype(BF16)
                logits = _attn_logits(qm, kband, bias_ref[hh], n, sub_blocks)
                p = jnp.exp(logits - jnp.max(jnp.where(mk, lse_, -jnp.inf), axis=-1, keepdims=True))
                dd = jnp.sum(dom * o_, axis=-1, keepdims=True)
                dls = jnp.sum(jnp.where(mk, dl_, 0.0), axis=-1, keepdims=True)
                ds = p * (_dot(domb, vband, _NT) - dd + dls)
                dbias_ref[hh] += ds
                dsb = (ds * 0.125).astype(BF16)
                dq = jnp.where(mk, _dot(dsb, kband, _NN), dq)
                dkb = dkb + _dot(dsb, qm, _TN)
                dvb = dvb + _dot(p.astype(BF16), domb, _TN)
            dq_ref[cur, :] = dq.astype(dq_ref.dtype)
            dk_acc[prev, :] += dkb[:ATTN_BLK]
            dk_acc[cur, :] += dkb[ATTN_BLK:]
            dv_acc[prev, :] += dvb[:ATTN_BLK]
            dv_acc[cur, :] += dvb[ATTN_BLK:]
            return carry

        lax.fori_loop(0, SEQ // ATTN_BLK, blk, 0)
        dk_ref[...] = dk_acc[...].astype(dk_ref.dtype)
        dv_ref[...] = dv_acc[...].astype(dv_ref.dtype)

    qkv_specs, seq, tab = _attn_specs(col0)
    return pl.pallas_call(
        kern, name=name, grid=(2, NB),
        in_specs=qkv_specs + [seq] * 4 + [tab], out_specs=[seq, seq, seq, tab],
        out_shape=[jax.ShapeDtypeStruct((T, 256), BF16)] * 3 + [jax.ShapeDtypeStruct((4, ATTN_BLK, 2 * ATTN_BLK), F32)],
        scratch_shapes=[pltpu.VMEM((SEQ, 128), F32), pltpu.VMEM((SEQ, 128), F32)],
        compiler_params=_cp(("arbitrary", "arbitrary")),
    )(qkv, qkv, qkv, o, lse, do, dl, bias)


def _to_sub(t, dil):
    if dil == 1:
        return t
    return t.reshape(NB, SEQ // dil, dil, t.shape[-1]).transpose(0, 2, 1, 3).reshape(T, t.shape[-1])


def _from_sub(t, dil):
    if dil == 1:
        return t
    return t.reshape(NB, dil, SEQ // dil, t.shape[-1]).transpose(0, 2, 1, 3).reshape(T, t.shape[-1])


_ANY = pl.BlockSpec(memory_space=pl.ANY)
MESH = pl.DeviceIdType.MESH


def _slot(dev):
    return 4 * dev[0] + 2 * dev[1] + dev[2]


class _GatherSteps:
    def __init__(self, x_refs, out_refs, sems):
        self.x_refs, self.out_refs = x_refs, out_refs
        self.send_sems, self.recv_sems, self.local_sems = sems
        self.n = len(x_refs)
        x, y, c = lax.axis_index("x"), lax.axis_index("y"), lax.axis_index("c")
        self.c, self.me, self.sibling = c, (x, y, c), (x, y, 1 - c)
        self.chips = [(1 - x, y), (x, 1 - y), (1 - x, 1 - y)]

    def copies(self, k, block, to, own=False):
        return [pltpu.make_async_remote_copy(
            src_ref=self.x_refs[a] if own else self.out_refs[a].at[_slot(block)], dst_ref=self.out_refs[a].at[_slot(block)],
            send_sem=self.send_sems.at[k, a], recv_sem=self.recv_sems.at[k, a], device_id=to, device_id_type=MESH)
            for a in range(self.n)]

    def mine(self):
        return [pltpu.make_async_copy(self.x_refs[a], self.out_refs[a].at[_slot(self.me)], self.local_sems.at[a])
                for a in range(self.n)]

    def first(self):
        out = self.copies(0, self.me, self.sibling, own=True)
        for j, chip in enumerate(self.chips):
            out += self.copies(1 + j, self.me, (*chip, self.c), own=True)
        return out

    def passed(self, j):
        return self.copies(4 + j, (*self.chips[j], self.c), self.sibling)

    def start(self):
        for cp in self.mine() + self.first():
            cp.start()

    def pass_on(self):
        for j, chip in enumerate(self.chips):
            for cp in self.copies(1 + j, (*chip, self.c), self.me):
                cp.wait_recv()
            for cp in self.passed(j):
                cp.start()

    def finish(self):
        for cp in self.copies(0, self.sibling, self.me):
            cp.wait_recv()
        for j, chip in enumerate(self.chips):
            for cp in self.copies(4 + j, (*chip, 1 - self.c), self.me):
                cp.wait_recv()
        for cp in self.first() + [cp for j in range(3) for cp in self.passed(j)]:
            cp.wait_send()
        for cp in self.mine():
            cp.wait()


class _CarriedGather:
    def __init__(self, x_refs, out_refs, sems, step, n_steps):
        self.steps, self.step, self.n_steps = _GatherSteps(x_refs, out_refs, sems), step, n_steps

    def before(self):
        pl.when(self.step == 0)(self.steps.start)
        pl.when(self.step == self.n_steps // 2)(self.steps.pass_on)

    def after(self):
        pl.when(self.step == self.n_steps - 1)(self.steps.finish)


def _gathered_shapes(shards):
    return [jax.ShapeDtypeStruct((NDEV, *s.shape), s.dtype) for s in shards]


def _gather_sems(n):
    return [pltpu.SemaphoreType.DMA((7, n)), pltpu.SemaphoreType.DMA((7, n)), pltpu.SemaphoreType.DMA((n,))] if n else []


def _all_gather(shards, *, name):
    n = len(shards)

    def kern(*refs):
        steps = _GatherSteps(refs[:n], refs[n:2 * n], refs[2 * n:])
        steps.start()
        steps.pass_on()
        steps.finish()

    return pl.pallas_call(
        kern, name=name, in_specs=[_ANY] * n, out_specs=[_ANY] * n, out_shape=_gathered_shapes(shards),
        scratch_shapes=_gather_sems(n),
    )(*shards)


def _exchange_sibling(arrs, *, name):
    n = len(arrs)

    def kern(*refs):
        x_refs, out_refs = refs[:n], refs[n:2 * n]
        send_sems, recv_sems = refs[2 * n:]
        x, y, c = lax.axis_index("x"), lax.axis_index("y"), lax.axis_index("c")
        sends = [pltpu.make_async_remote_copy(
            src_ref=x_refs[a].at[1 - c], dst_ref=out_refs[a], send_sem=send_sems.at[a], recv_sem=recv_sems.at[a],
            device_id=(x, y, 1 - c), device_id_type=MESH) for a in range(n)]
        for cp in sends:
            cp.start()
        for cp in sends:
            cp.wait_recv()
        for cp in sends:
            cp.wait_send()

    return pl.pallas_call(
        kern, name=name, in_specs=[_ANY] * n, out_specs=[_ANY] * n,
        out_shape=[jax.ShapeDtypeStruct(a.shape[1:], a.dtype) for a in arrs],
        scratch_shapes=[pltpu.SemaphoreType.DMA((n,)), pltpu.SemaphoreType.DMA((n,))],
    )(*arrs)


def _exchange_chips(arrs, *, name):
    n = len(arrs)

    def kern(*refs):
        x_refs, out_refs = refs[:n], refs[n:2 * n]
        send_sems, recv_sems, local_sems = refs[2 * n:]
        x, y, c = lax.axis_index("x"), lax.axis_index("y"), lax.axis_index("c")
        my_chip = 2 * x + y
        chips = [(1 - x, y), (x, 1 - y), (1 - x, 1 - y)]
        mine = [pltpu.make_async_copy(x_refs[a].at[my_chip], out_refs[a].at[my_chip], local_sems.at[a]) for a in range(n)]
        sends = [pltpu.make_async_remote_copy(
            src_ref=x_refs[a].at[2 * px + py], dst_ref=out_refs[a].at[my_chip], send_sem=send_sems.at[k, a],
            recv_sem=recv_sems.at[k, a], device_id=(px, py, c), device_id_type=MESH)
            for k, (px, py) in enumerate(chips) for a in range(n)]
        for cp in mine + sends:
            cp.start()
        for k, (px, py) in enumerate(chips):
            for a in range(n):
                pltpu.make_async_remote_copy(
                    src_ref=x_refs[a].at[my_chip], dst_ref=out_refs[a].at[2 * px + py], send_sem=send_sems.at[k, a],
                    recv_sem=recv_sems.at[k, a], device_id=(px, py, c), device_id_type=MESH).wait_recv()
        for cp in sends:
            cp.wait_send()
        for cp in mine:
            cp.wait()

    return pl.pallas_call(
        kern, name=name, in_specs=[_ANY] * n, out_specs=[_ANY] * n,
        out_shape=[jax.ShapeDtypeStruct(a.shape, a.dtype) for a in arrs],
        scratch_shapes=[pltpu.SemaphoreType.DMA((3, n)), pltpu.SemaphoreType.DMA((3, n)), pltpu.SemaphoreType.DMA((n,))],
    )(*arrs)


def _add_own_half(full, other, c_idx, *, name):
    _, r, c = full.shape
    tile = _pick(r, (512, 256, 128, 64, 32, 16))

    def kern(c_ref, f_ref, o_ref, out_ref):
        out_ref[...] = (f_ref[...].astype(F32) + o_ref[...].astype(F32)).astype(BF16)

    return pl.pallas_call(
        kern, name=name,
        grid_spec=pltpu.PrefetchScalarGridSpec(
            num_scalar_prefetch=1, grid=(r // tile,),
            in_specs=[pl.BlockSpec((None, tile, c), lambda i, c_ref: (c_ref[0], i, 0)),
                      pl.BlockSpec((tile, c), lambda i, c_ref: (i, 0))],
            out_specs=pl.BlockSpec((tile, c), lambda i, c_ref: (i, 0))),
        out_shape=jax.ShapeDtypeStruct((r, c), BF16), compiler_params=_cp(("parallel",)),
    )(c_idx, full, other)


def _adamw_reduce(parts, w, m, v, *, name):
    ns, r, c = parts.shape
    tile = _pick(r, (256, 128, 64, 32, 16, 8))

    def kern(p_ref, w_ref, m_ref, v_ref, g_ref, d_ref, m2_ref, v2_ref):
        g = p_ref[0].astype(F32)
        for d in range(1, ns):
            g = g + p_ref[d].astype(F32)
        g_ref[...] = g
        d_ref[...], m2_ref[...], v2_ref[...] = _adam_math(w_ref[...], g, m_ref[...], v_ref[...])

    blk = pl.BlockSpec((tile, c), lambda i: (i, 0))
    return pl.pallas_call(
        kern, name=name, grid=(r // tile,), in_specs=[pl.BlockSpec((ns, tile, c), lambda i: (0, i, 0)), blk, blk, blk],
        out_specs=[blk] * 4, out_shape=[jax.ShapeDtypeStruct((r, c), F32)] * 4, compiler_params=_cp(("parallel",)),
    )(parts, w, m, v)


def _sum_slots(parts, *, name):
    ns, r, c = parts.shape
    tile = _pick(r, (256, 128, 64, 32, 16, 8))

    def kern(p_ref, o_ref):
        acc = p_ref[0].astype(F32)
        for d in range(1, ns):
            acc = acc + p_ref[d].astype(F32)
        o_ref[...] = acc

    return pl.pallas_call(
        kern, name=name, grid=(r // tile,), in_specs=[pl.BlockSpec((ns, tile, c), lambda i: (0, i, 0))],
        out_specs=pl.BlockSpec((tile, c), lambda i: (i, 0)), out_shape=jax.ShapeDtypeStruct((r, c), F32),
        compiler_params=_cp(("parallel",)),
    )(parts)


BIG = [
    ("ffn1_w13", (1024, 704), 1), ("ffn1_w2", (352, 1024), 0), ("w_in", (1024, 1154), 1), ("p_pool", (768, 128), 1),
    ("p_ssd", (128, 1024), 0), ("p_attn", (256, 128), 1), ("w_out", (128, 1024), 0), ("ffn2_w13", (1024, 704), 1),
    ("ffn2_w2", (352, 1024), 0),
]
BLOCKED = ("ffn1_w13", "ffn2_w13")


def _from_blocks(blk, ax):
    _, r, c = blk.shape
    return blk.transpose(1, 0, 2).reshape(r, NDEV * c) if ax == 1 else blk.reshape(NDEV * r, c)


def _to_blocks(full, ax):
    r, c = full.shape
    return full.reshape(r, NDEV, c // NDEV).transpose(1, 0, 2) if ax == 1 else full.reshape(NDEV, r // NDEV, c)


class _SmallPack:
    def __init__(self, shapes):
        self.shapes = shapes
        self.rows = [-(-int(np.prod(s)) // 128) for s in shapes]
        self.total = -(-sum(self.rows) // 256) * 256

    def pack(self, arrs):
        parts = []
        for a, s, r in zip(arrs, self.shapes, self.rows):
            assert tuple(a.shape) == tuple(s), (a.shape, s)
            flat = a.astype(F32).reshape(-1)
            parts.append(jnp.pad(flat, (0, r * 128 - flat.shape[0])).reshape(r, 128))
        parts.append(jnp.zeros((self.total - sum(self.rows), 128), F32))
        return jnp.concatenate(parts, axis=0)

    def unpack(self, packed):
        out, off = [], 0
        for s, r in zip(self.shapes, self.rows):
            out.append(packed[off:off + r].reshape(-1)[:int(np.prod(s))].reshape(s))
            off += r
        return out


def _w_in_to_padded(w):
    z = jnp.zeros((*w.shape[:-1], HC - DT0 - 16), w.dtype)
    qkv = [w[..., o + 256 * gi:o + 256 * gi + 256] for gi in range(3) for o in (O_Q, O_K, O_V)]
    return jnp.concatenate([w[..., O_Z:O_XBC], w[..., O_XBC:O_DT], w[..., O_G:O_END], *qkv, w[..., O_U:O_Z], w[..., O_DT:O_Q], z],
                           axis=-1)


def _w_in_from_padded(g):
    qkv = [g[..., QKV0 + 768 * gi + o:QKV0 + 768 * gi + o + 256] for o in (0, 256, 512) for gi in range(3)]
    return jnp.concatenate([g[..., U0:DT0], g[..., Z0:XBC0], g[..., XBC0:GATE0], g[..., DT0:DT0 + 16], *qkv, g[..., GATE0:QKV0]],
                           axis=-1)


def _block_diag(pw):
    out = jnp.zeros((POOL_W, POOL_W), pw.dtype)
    for gi in range(4):
        out = lax.dynamic_update_slice(out, pw[gi], (gi * POOL_GDIM, gi * POOL_GDIM))
    return out


def _expand_heads(v):
    return jnp.repeat(v, 64).reshape(1, D)


def _ffn_fwd(x, x16, w13, w2, g, b, tag, carry_up=(), carry_f=()):
    act, h16, got_up = _ffn_up(x16, w13, carry=carry_up, name=f"{tag}_up")
    f = _mm(act, w2, carry=carry_f, name=f"{tag}_f")
    f, got_f = f if carry_f else (f, [])
    y, y16 = _ln_fwd(x, f, g, b, scale=FFN_RES, name=f"{tag}_ln")
    return y, y16, (h16, act, f), list(got_up) + list(got_f)


def _ffn_bwd(x, x16, w13, w2, g, b, saved, dy, tag):
    h16, act, f = saved
    dres, df, dg, db = _ln_bwd(x, f, dy, g, b, scale=FFN_RES, name=f"{tag}_ln_bwd")
    dw2 = _mm(act.transpose(0, 2, 1).reshape(DFF, T), df, out_dtype=BF16, name=f"{tag}_dw2")
    dh = _ffn_down_bwd(df, w2, h16, name=f"{tag}_dh")
    dx = _mm(dh, w13, tb=True, add=dres, name=f"{tag}_dx")
    dw13 = _mm(x16.T, dh, out_blocked=True, out_dtype=BF16, name=f"{tag}_dw13")
    return dx, dw13, dw2, dg[0], db[0]


CARRIERS = {"ffn1_up": ["ffn1_w13"], "ffn1_f": ["ffn1_w2", "p_pool", "p_attn"], "mix_in": ["w_in"], "ffn2_up": ["ffn2_w13"],
            "ffn2_f": ["ffn2_w2", "p_ssd", "w_out"]}


def _layer_fwd(x0, x0_16, p, bias, nxt=None):
    send = {k: ([nxt[n] for n in names] if nxt is not None else []) for k, names in CARRIERS.items()}
    got = {}
    x1, x1_16, s1, g1 = _ffn_fwd(x0, x0_16, p["ffn1_w13"], p["ffn1_w2"], p["ln1_g"], p["ln1_b"], "ffn1",
                                 send["ffn1_up"], send["ffn1_f"])
    got.update(zip(CARRIERS["ffn1_up"] + CARRIERS["ffn1_f"], g1))
    hcat = _mm(x1_16, p["w_in"], carry=send["mix_in"], name="mix_in")
    if nxt is not None:
        hcat, g_in = hcat
        got.update(zip(CARRIERS["mix_in"], g_in))
    pooled = _pool_fwd(hcat, name="pool")
    ya_lin = _mm(pooled, p["pool_bd"], name="pool_lin")
    ya_pre = _affine_fwd(ya_lin, p["pool_b"], p["pool_scale"], name="pool_affine")
    ya = _mm(ya_pre, p["p_pool"], name="pool_out")
    xa = _conv_fwd(hcat, p["conv_w"], p["conv_b"], name="conv")
    dtx = jnp.repeat(hcat[:, DT0:DT0 + SSD_HEADS], 64, axis=1)
    ysc, hs = _ssd_fwd(xa, dtx, p["dt_bias"], p["a_log"], p["d_skip"], name="ssd")
    yb_pre = _gnorm_fwd(ysc, hcat, p["ssd_norm"], name="ssd_norm")
    yb = _mm(yb_pre, p["p_ssd"], name="ssd_out")
    outs, lses, attn_sv = [], [], []
    for gi, (_, dil) in enumerate(ATTN_CONFIGS):
        c0 = QKV0 + 768 * gi
        qkv, col0 = (hcat, c0 // 128) if dil == 1 else (_to_sub(hcat[:, c0:c0 + 768], dil), 0)
        o, l = _attn_fwd(qkv, col0, bias[gi], SEQ // dil // ATTN_BLK, name=f"attn{gi}")
        attn_sv.append((None if dil == 1 else qkv, o, l))
        outs.append(_from_sub(o, dil))
        lses.append(_from_sub(l, dil))
    yc_pre = _amerge_fwd(outs, lses, name="attn_merge")
    yc = _mm(yc_pre, p["p_attn"], name="attn_out")
    merged = _merge_fwd(hcat, ya, yb, yc, p["gate_b"], name="merge")
    mix = _mm(merged, p["w_out"], name="mix_out")
    x2, x2_16 = _ln_fwd(x1, mix, p["ln2_g"], p["ln2_b"], scale=1.0, name="ln2")
    x3, x3_16, s3, g3 = _ffn_fwd(x2, x2_16, p["ffn2_w13"], p["ffn2_w2"], p["ln3_g"], p["ln3_b"], "ffn2",
                                 send["ffn2_up"], send["ffn2_f"])
    got.update(zip(CARRIERS["ffn2_up"] + CARRIERS["ffn2_f"], g3))
    saved = dict(x0=x0, x0_16=x0_16, s1=s1, x1=x1, x1_16=x1_16, hcat=hcat, pooled=pooled, ya_lin=ya_lin, ya_pre=ya_pre, ya=ya,
                 xa=xa, hs=hs, ysc=ysc, yb_pre=yb_pre, yb=yb, outs=outs, lses=lses, attn_sv=attn_sv, yc_pre=yc_pre, yc=yc, merged=merged, mix=mix,
                 x2=x2, x2_16=x2_16, s3=s3)
    return x3, x3_16, saved, got


def _layer_bwd(p, bias, sv, dx3):
    g = {}
    dx2, g["ffn2_w13"], g["ffn2_w2"], g["ln3_g"], g["ln3_b"] = _ffn_bwd(
        sv["x2"], sv["x2_16"], p["ffn2_w13"], p["ffn2_w2"], p["ln3_g"], p["ln3_b"], sv["s3"], dx3, "ffn2")
    hcat = sv["hcat"]
    dres, dmix, dg2, db2 = _ln_bwd(sv["x1"], sv["mix"], dx2, p["ln2_g"], p["ln2_b"], scale=1.0, name="ln2_bwd")
    g["ln2_g"], g["ln2_b"] = dg2[0], db2[0]
    dmerged = _mm(dmix, p["w_out"], tb=True, name="mix_out_dx")
    g["w_out"] = _mm(sv["merged"].T, dmix, out_dtype=BF16, name="mix_out_dw")
    dg0, dg1, dg2_, dya, dyb, dyc, gb0, gb1, gb2 = _merge_bwd(hcat, sv["ya"], sv["yb"], sv["yc"], dmerged, p["gate_b"],
                                                               name="merge_bwd")
    g["gate_b"] = jnp.concatenate([gb0, gb1, gb2], axis=0)
    dya_pre = _mm(dya, p["p_pool"], tb=True, name="pool_out_dx")
    g["p_pool"] = _mm(sv["ya_pre"].T, dya, out_dtype=BF16, name="pool_out_dw")
    dya_lin, dpb, dps = _affine_bwd(sv["ya_lin"], dya_pre, p["pool_b"], p["pool_scale"], name="pool_affine_bwd")
    g["pool_b"], g["pool_scale"] = dpb[0].reshape(4, POOL_GDIM), dps[0]
    dpooled = _mm(dya_lin, p["pool_bd"], tb=True, name="pool_lin_dx")
    dbd = _mm(sv["pooled"].T, dya_lin, name="pool_lin_dw")
    g["pool_w"] = jnp.stack([dbd[i * POOL_GDIM:(i + 1) * POOL_GDIM, i * POOL_GDIM:(i + 1) * POOL_GDIM] for i in range(4)])
    du = _pool_bwd(dpooled, name="pool_bwd")
    dyb_pre = _mm(dyb, p["p_ssd"], tb=True, name="ssd_out_dx")
    g["p_ssd"] = _mm(sv["yb_pre"].T, dyb, out_dtype=BF16, name="ssd_out_dw")
    dysc, dz, dnw = _gnorm_bwd(sv["ysc"], hcat, dyb_pre, p["ssd_norm"], name="ssd_norm_bwd")
    g["ssd_norm"] = dnw[0]
    dtx = jnp.repeat(hcat[:, DT0:DT0 + SSD_HEADS], 64, axis=1)
    dxs, dbm, dcm, ddtx, ddtb, dalog, ddsk = _ssd_bwd(sv["xa"], dtx, p["dt_bias"], p["a_log"], p["d_skip"], sv["hs"], dysc,
                                                     name="ssd_bwd")
    g["dt_bias"], g["a_log"], g["d_skip"] = ddtb[0, ::64], dalog[0, ::64], ddsk[0, ::64]
    dxa = jnp.concatenate([dxs, dbm, dcm], axis=1)
    dxbc, dcw, dcb = _conv_bwd(hcat, p["conv_w"], p["conv_b"], dxa, name="conv_bwd")
    g["conv_w"], g["conv_b"] = dcw, dcb[0]
    ddt = jnp.pad(ddtx[:, ::64], ((0, 0), (0, HC - DT0 - SSD_HEADS))).astype(BF16)
    dyc_pre = _mm(dyc, p["p_attn"], tb=True, name="attn_out_dx")
    g["p_attn"] = _mm(sv["yc_pre"].T, dyc, out_dtype=BF16, name="attn_out_dw")
    am = _amerge_bwd(sv["outs"], sv["lses"], dyc_pre, name="attn_merge_bwd")
    dqkv, dbias = [], []
    for gi, (_, dil) in enumerate(ATTN_CONFIGS):
        qkv_sub, o, l = sv["attn_sv"][gi]
        qkv, col0 = (hcat, (QKV0 + 768 * gi) // 128) if dil == 1 else (qkv_sub, 0)
        dq, dk, dv, dbi = _attn_bwd(qkv, col0, o, l, _to_sub(am[gi], dil), _to_sub(am[3 + gi], dil), bias[gi],
                                    SEQ // dil // ATTN_BLK, name=f"attn{gi}_bwd")
        dqkv += [_from_sub(t, dil) for t in (dq, dk, dv)]
        dbias.append(dbi)
    g["attn_bias"] = jnp.stack(dbias)
    dhcat = jnp.concatenate([dz, dxbc, dg0, dg1, dg2_, *dqkv, du, ddt], axis=1)
    dx1 = _mm(dhcat, p["w_in"], tb=True, add=dres, name="mix_in_dx")
    g["w_in"] = _mm(sv["x1_16"].T, dhcat, out_dtype=BF16, name="mix_in_dw")
    dx0, g["ffn1_w13"], g["ffn1_w2"], g["ln1_g"], g["ln1_b"] = _ffn_bwd(
        sv["x0"], sv["x0_16"], p["ffn1_w13"], p["ffn1_w2"], p["ln1_g"], p["ln1_b"], sv["s1"], dx1, "ffn1")
    return dx0, g


def _local_step(x, tgt, lw, rel_bias, shards=None, prepare=None):
    bias, bias_vjp = jax.vjp(_attn_bias, rel_bias)
    y, y16, saved, lw = x, x.astype(BF16), [], list(lw)
    n_layers = len(lw) if shards is None else len(shards)
    for l in range(n_layers):
        nxt = shards[l + 1] if shards is not None and l + 1 < n_layers else None
        y, y16, sv, got = _layer_fwd(y, y16, lw[l], bias, nxt)
        saved.append(sv)
        if nxt is not None:
            lw.append(prepare(got, l + 1))
    dy, sq = _loss_kernel(y, tgt, name="loss")
    grads = [None] * len(lw)
    for l in reversed(range(len(lw))):
        dy, grads[l] = _layer_bwd(lw[l], bias, saved[l], dy)
    (d_rel,) = bias_vjp(sum(g.pop("attn_bias") for g in grads))
    return sq, dy, grads, d_rel


SMALL_REPL = ["ln1_g", "ln1_b", "pool_w", "pool_b", "pool_scale", "conv_b", "dt_bias", "a_log", "d_skip", "ssd_norm",
              "ln2_g", "ln2_b", "ln3_g", "ln3_b", "rel_bias"]
SMALL_SHARD = ["gate_b", "conv_w"]
WEIGHTS = ['ffn1_w13', 'ffn1_w2', 'ln1_g', 'ln1_b', 'w_in', 'gate_b', 'pool_w', 'pool_b', 'pool_scale', 'conv_w', 'conv_b',
           'dt_bias', 'a_log', 'd_skip', 'ssd_norm', 'rel_bias', 'p_pool', 'p_ssd', 'p_attn', 'w_out', 'ln2_g', 'ln2_b',
           'ffn2_w13', 'ffn2_w2', 'ln3_g', 'ln3_b']


def _step(w, m, v, x, tgt):
    my_c = lax.axis_index("c")
    dev = 4 * lax.axis_index("x") + 2 * lax.axis_index("y") + my_c
    shards = [{n: w[n][l].astype(BF16) for n, _, _ in BIG} for l in range(DEPTH)]
    first = dict(zip([n for n, _, _ in BIG], _all_gather([shards[0][n] for n, _, _ in BIG], name="gather_weights")))
    sp_in = _SmallPack([w[n].shape for n in SMALL_SHARD])
    small_g = _all_gather([sp_in.pack([w[n] for n in SMALL_SHARD])], name="gather_small")[0]
    gate_b = jnp.stack([sp_in.unpack(small_g[d])[0] for d in range(NDEV)], axis=2).reshape(DEPTH, 3, D)
    conv_w = jnp.stack([sp_in.unpack(small_g[d])[1] for d in range(NDEV)], axis=2).reshape(DEPTH, 4, 2048)

    def prepare(blocks, l):
        p = {n: blocks[n] if n in BLOCKED else _from_blocks(blocks[n], ax) for n, _, ax in BIG}
        p["w_in"] = _w_in_to_padded(p["w_in"])
        p["pool_bd"] = _block_diag(w["pool_w"][l]).astype(BF16)
        p["gate_b"], p["conv_w"] = gate_b[l], conv_w[l]
        for n in ("ln1_g", "ln1_b", "ln2_g", "ln2_b", "ln3_g", "ln3_b", "pool_scale", "conv_b", "ssd_norm"):
            p[n] = w[n][l][None, :]
        p["pool_b"] = w["pool_b"][l].reshape(1, POOL_W)
        for n in ("dt_bias", "a_log", "d_skip"):
            p[n] = _expand_heads(w[n][l])
        return p

    sq, dx, grads, d_rel = _local_step(x.reshape(T, D), tgt.reshape(T, D), [prepare(first, 0)], w["rel_bias"], shards, prepare)
    loss = lax.psum(jnp.sum(sq) * (0.5 / D), ("x", "y", "c"))
    for g in grads:
        g["w_in"] = _w_in_from_padded(g["w_in"])
    halves = []
    for n, (r, c), ax in BIG:
        per_layer = [(g[n] if n in BLOCKED else _to_blocks(g[n], ax)).reshape(4, 2, r, c) for g in grads]
        halves.append(jnp.stack([jnp.stack([blk[:, h] for blk in per_layer], axis=1) for h in (0, 1)]))
    from_sibling = _exchange_sibling(halves, name="exchange_grads_cores")
    c_idx = my_c.reshape(1).astype(jnp.int32)
    chip_sums = [_add_own_half(h.reshape(2, -1, h.shape[-1]), o.reshape(-1, o.shape[-1]), c_idx, name=f"add_cores_{n}")
                 .reshape(4, -1, h.shape[-1]) for h, o, (n, _, _) in zip(halves, from_sibling, BIG)]
    by_chip = _exchange_chips(chip_sums, name="exchange_grads_chips")
    small_names = SMALL_REPL + SMALL_SHARD
    small = {n: (d_rel if n == "rel_bias" else jnp.stack([g[n] for g in grads])) for n in small_names}
    sp = _SmallPack([small[n].shape for n in small_names])
    gsmall = sp.unpack(_sum_slots(_all_gather([sp.pack([small[n] for n in small_names])], name="gather_small_grads")[0],
                                  name="sum_small_grads"))
    gout = {}
    for n, gv in zip(small_names, gsmall):
        if n in SMALL_SHARD:
            width = w[n].shape[-1]
            gv = lax.dynamic_slice_in_dim(gv, dev * width, width, axis=2)
        gout[n] = gv
    delta, new_m, new_v = {}, {}, {}
    for (n, _, _), parts in zip(BIG, by_chip):
        shp = w[n].shape
        two_d = lambda a: a.reshape(-1, shp[-1])
        res = _adamw_reduce(parts, two_d(w[n]), two_d(m[n]), two_d(v[n]), name=f"adamw_{n}")
        gout[n], delta[n], new_m[n], new_v[n] = (t.reshape(shp) for t in res)
    spa = _SmallPack([w[n].shape for n in small_names])
    res = _adamw(spa.pack([w[n] for n in small_names]), spa.pack([gout[n] for n in small_names]),
                 spa.pack([m[n] for n in small_names]), spa.pack([v[n] for n in small_names]), name="adamw_small")
    for out, packed in zip((delta, new_m, new_v), res):
        for n, val in zip(small_names, spa.unpack(packed)):
            out[n] = val
    return (loss, dx.reshape(NB, SEQ, D), *[gout[n] for n in WEIGHTS], *[delta[n] for n in WEIGHTS],
            *[new_m[n] for n in WEIGHTS], *[new_v[n] for n in WEIGHTS])


def kernel(x, ffn1_w13, ffn1_w2, ln1_g, ln1_b, w_in, gate_b, pool_w, pool_b, pool_scale, conv_w, conv_b, dt_bias, a_log, d_skip, ssd_norm, rel_bias, p_pool, p_ssd, p_attn, w_out, ln2_g, ln2_b, ffn2_w13, ffn2_w2, ln3_g, ln3_b, loss_target, m_ffn1_w13, m_ffn1_w2, m_ln1_g, m_ln1_b, m_w_in, m_gate_b, m_pool_w, m_pool_b, m_pool_scale, m_conv_w, m_conv_b, m_dt_bias, m_a_log, m_d_skip, m_ssd_norm, m_rel_bias, m_p_pool, m_p_ssd, m_p_attn, m_w_out, m_ln2_g, m_ln2_b, m_ffn2_w13, m_ffn2_w2, m_ln3_g, m_ln3_b, v_ffn1_w13, v_ffn1_w2, v_ln1_g, v_ln1_b, v_w_in, v_gate_b, v_pool_w, v_pool_b, v_pool_scale, v_conv_w, v_conv_b, v_dt_bias, v_a_log, v_d_skip, v_ssd_norm, v_rel_bias, v_p_pool, v_p_ssd, v_p_attn, v_w_out, v_ln2_g, v_ln2_b, v_ffn2_w13, v_ffn2_w2, v_ln3_g, v_ln3_b):
    given = dict(locals())
    w = {n: given[n] for n in WEIGHTS}
    m = {n: given["m_" + n] for n in WEIGHTS}
    v = {n: given["v_" + n] for n in WEIGHTS}
    return _step(w, m, v, x, loss_target)
```

```python
import functools
import math

import numpy as np
import jax
import jax.numpy as jnp
from jax import lax
from jax.experimental import pallas as pl
from jax.experimental.pallas import tpu as pltpu

F32, BF16 = jnp.float32, jnp.bfloat16
HIGHEST = lax.Precision.HIGHEST

NDEV = 8
DEPTH = 4
D = 1024
SEQ = 2048
NB = 2
T = NB * SEQ
DFF = 2816
POOL_W = 768
POOL_WINDOWS = (2, 4, 8, 16)
POOL_GDIM = 192
SSD_HEADS = 16
CHUNK = 128
NCHUNK = SEQ // CHUNK
ATTN_CONFIGS = ((128, 1), (512, 4), (2048, 16))
ATTN_BLK = 128
REL_BUCKETS = 32
REL_MAX_DIST = 2048
LN_EPS = 1e-5
SSD_EPS = 1e-5
ALPHA = (2.0 * DEPTH) ** 0.25
FFN_RES = 0.5
NEG = -1e30

ADAM_LR, ADAM_B1, ADAM_B2, ADAM_EPS, ADAM_WD, ADAM_STEP = 0.001, 0.9, 0.999, 1e-08, 0.01, 10

Z0, XBC0, GATE0, QKV0, U0, DT0, HC = 0, 1024, 3072, 6144, 8448, 9216, 9728
O_U, O_Z, O_XBC, O_DT, O_Q, O_K, O_V, O_G, O_END = 0, 768, 1792, 3840, 3856, 4624, 5392, 6160, 9232

VMEM_LIMIT = 56 * 1024 * 1024


def _cp(sem):
    return pltpu.CompilerParams(dimension_semantics=sem, vmem_limit_bytes=VMEM_LIMIT)


def _pick(dim, cands):
    for c in cands:
        if dim % c == 0:
            return c
    return dim


def _mm(a, b, *, tb=False, out_dtype=F32, add=None, add_scale=1.0, out_blocked=False, bn=None, b_layer=None, carry=None, name):
    nc = carry.n if carry is not None else 0
    bk = None
    if a.ndim == 3:
        nkb, M, bk = a.shape
        K = nkb * bk
    else:
        M, K = a.shape
    b_blocked = b.ndim >= 3
    b_shape = b.shape if b_layer is None else (b.shape[0], *b.shape[2:])
    if b_blocked and tb:
        assert bk in (None, b_shape[2])
        nkb, N, bk = b_shape
        assert K == nkb * bk, (a.shape, b.shape)
    elif b_blocked:
        nnb, K2, bn = b_shape
        N = nnb * bn
        assert K == K2, (a.shape, b.shape)
    else:
        (N, K2) = b.shape if tb else b.shape[::-1]
        assert K == K2, (a.shape, b.shape, tb)
    bm = _pick(M, (1024, 768, 512, 256))
    if bn is None:
        bn = _pick(N, (1024, 768, 512, 256))
    if bk is None:
        bk = K if K <= 1024 else _pick(K, (1024, 1408, 2432, 512, 256))
    nk = K // bk
    dn = (((1,), (1 if tb else 0,)), ((), ()))

    grid = (M // bm, N // bn, nk)

    def kern(*refs):
        a_ref, b_ref = refs[0], refs[1]
        n_in = 2 + (add is not None)
        add_ref = refs[2] if add is not None else None
        x_refs, o_ref, g_refs = refs[n_in:n_in + nc], refs[n_in + nc], refs[n_in + nc + 1:n_in + 2 * nc + 1]
        scratch = refs[n_in + 2 * nc + 1:]
        acc_ref = scratch[0] if nk > 1 else None
        if nc:
            step = (pl.program_id(0) * grid[1] + pl.program_id(1)) * nk + pl.program_id(2)
            carried = _Carried(carry, x_refs, g_refs, scratch[-3:], step, grid[0] * grid[1] * nk)
            carried.before()
        p = lax.dot_general(a_ref[...].astype(BF16), b_ref[...].astype(BF16), dn, preferred_element_type=F32)

        def fin(acc):
            if add_ref is not None:
                acc = acc + add_scale * add_ref[...]
            o_ref[...] = acc.astype(out_dtype)

        if nk == 1:
            fin(p)
        else:
            k = pl.program_id(2)

            @pl.when(k == 0)
            def _():
                acc_ref[...] = p

            @pl.when(k > 0)
            def _():
                acc_ref[...] += p

            @pl.when(k == nk - 1)
            def _():
                fin(acc_ref[...])

        if nc:
            carried.after()

    if a.ndim == 3:
        a_spec = pl.BlockSpec((None, bm, bk), lambda i, j, k: (k, i, 0))
    else:
        a_spec = pl.BlockSpec((bm, bk), lambda i, j, k: (i, k))
    if b_blocked and b_layer is not None:
        b_spec = (pl.BlockSpec((None, None, bn, bk), lambda i, j, k: (k, b_layer, j, 0)) if tb else
                  pl.BlockSpec((None, None, bk, bn), lambda i, j, k: (j, b_layer, k, 0)))
    elif b_blocked and tb:
        b_spec = pl.BlockSpec((None, bn, bk), lambda i, j, k: (k, j, 0))
    elif b_blocked:
        b_spec = pl.BlockSpec((None, bk, bn), lambda i, j, k: (j, k, 0))
    elif tb:
        b_spec = pl.BlockSpec((bn, bk), lambda i, j, k: (j, k))
    else:
        b_spec = pl.BlockSpec((bk, bn), lambda i, j, k: (k, j))
    if out_blocked:
        o_spec, o_shape = pl.BlockSpec((None, bm, bn), lambda i, j, k: (j, i, 0)), (N // bn, M, bn)
    else:
        o_spec, o_shape = pl.BlockSpec((bm, bn), lambda i, j, k: (i, j)), (M, N)
    in_specs = [a_spec, b_spec]
    args = [a, b]
    if add is not None:
        assert add.shape == o_shape
        in_specs.append(o_spec)
        args.append(add)
    res = pl.pallas_call(
        kern, name=name, grid=grid,
        in_specs=in_specs + [_ANY] * nc, out_specs=[o_spec] + [_ANY] * nc,
        out_shape=[jax.ShapeDtypeStruct(o_shape, out_dtype)] + (carry.out_shapes() if nc else []),
        scratch_shapes=([pltpu.VMEM((bm, bn), F32)] if nk > 1 else []) + (carry.sems() if nc else []),
        compiler_params=_cp(("arbitrary",) * 3 if nc else ("parallel", "parallel", "arbitrary")),
    )(*args, *(carry.arrays if nc else []))
    return (res[0], res[1:]) if nc else res[0]


def _rowwise(fn, rows, pars, outs, accs, *, name, tile, groups=1):
    n_rows = rows[0][0].shape[0]
    nt = n_rows // tile
    n_in = len(rows) + len(pars)
    n_out = len(outs)

    def kern(*refs):
        res = fn(*[r[...] for r in refs[:n_in]])
        for o_ref, val in zip(refs[n_in:n_in + n_out], res[:n_out]):
            o_ref[...] = val.astype(o_ref.dtype)
        i = pl.program_id(1)
        for a_ref, val in zip(refs[n_in + n_out:], res[n_out:]):
            @pl.when(i == 0)
            def _(a_ref=a_ref, val=val):
                a_ref[...] = val

            @pl.when(i > 0)
            def _(a_ref=a_ref, val=val):
                a_ref[...] += val

    in_specs = [pl.BlockSpec((tile, w), lambda g, i, c0=c0: (i, c0 + g)) for (_, w, c0) in rows]
    for p, w in pars:
        if w is None:
            in_specs.append(pl.BlockSpec(p.shape, lambda g, i: (0, 0)))
        else:
            in_specs.append(pl.BlockSpec((p.shape[0], w), lambda g, i: (0, g)))
    out_specs = [pl.BlockSpec((tile, w), lambda g, i: (i, g)) for (_, w, _) in outs]
    out_specs += [pl.BlockSpec((1, w), lambda g, i: (0, g)) for (_, w) in accs]
    out_shape = [jax.ShapeDtypeStruct((n_rows, c), dt) for (c, _, dt) in outs]
    out_shape += [jax.ShapeDtypeStruct((1, c), F32) for (c, _) in accs]
    return pl.pallas_call(
        kern, name=name, grid=(groups, nt), in_specs=in_specs, out_specs=out_specs, out_shape=out_shape,
        compiler_params=_cp(("arbitrary", "arbitrary")),
    )(*[r[0] for r in rows], *[p[0] for p in pars])


def _colsum(v):
    return jnp.sum(v, axis=0, keepdims=True)


def _silu(v):
    return v * jax.nn.sigmoid(v)


def _ln_fn(x, f, g, b, *, scale):
    pre = ALPHA * x + scale * f
    mu = jnp.mean(pre, axis=-1, keepdims=True)
    var = jnp.mean(jnp.square(pre - mu), axis=-1, keepdims=True)
    return (pre - mu) * lax.rsqrt(var + LN_EPS) * g + b


def _ln_fwd(x, f, g, b, *, scale, name):
    fn = lambda x_, f_, g_, b_: [_ln_fn(x_, f_, g_, b_, scale=scale)] * 2
    return _rowwise(fn, [(x, D, 0), (f, D, 0)], [(g, None), (b, None)], [(D, D, F32), (D, D, BF16)], [], name=name, tile=512)


def _ln_bwd(x, f, dy, g, b, *, scale, name):
    def fn(x_, f_, dy_, g_, b_):
        _, vjp = jax.vjp(functools.partial(_ln_fn, scale=scale), x_, f_, g_, b_)
        return list(vjp(dy_))

    return _rowwise(fn, [(x, D, 0), (f, D, 0), (dy, D, 0)], [(g, None), (b, None)],
                    [(D, D, F32), (D, D, BF16)], [(D, D), (D, D)], name=name, tile=512)


FB = 2 * DFF // NDEV
FFN_BM = 1024
_H_PAIR = pl.BlockSpec((2, None, FFN_BM, FB), lambda i, j: (0, j, i, 0))


def _ffn_up(x16, w13, *, carry=None, name):
    nc = carry.n if carry is not None else 0
    grid = (T // FFN_BM, 4)

    def kern(*refs):
        x_ref, w_ref = refs[0], refs[1]
        x_refs, (act_ref, h_ref), g_refs = refs[2:2 + nc], refs[2 + nc:4 + nc], refs[4 + nc:4 + 2 * nc]
        if nc:
            step = pl.program_id(0) * grid[1] + pl.program_id(1)
            carried = _Carried(carry, x_refs, g_refs, refs[4 + 2 * nc:], step, grid[0] * grid[1])
            carried.before()
        x = x_ref[...]
        ha = jnp.dot(x, w_ref[0], preferred_element_type=F32)
        hg = jnp.dot(x, w_ref[1], preferred_element_type=F32)
        act_ref[...] = (_silu(ha) * hg).astype(BF16)
        h_ref[0] = ha.astype(BF16)
        h_ref[1] = hg.astype(BF16)
        if nc:
            carried.after()

    res = pl.pallas_call(
        kern, name=name, grid=grid,
        in_specs=[pl.BlockSpec((FFN_BM, D), lambda i, j: (i, 0)),
                  pl.BlockSpec((2, None, D, FB), lambda i, j: (0, j, 0, 0))] + [_ANY] * nc,
        out_specs=[pl.BlockSpec((None, FFN_BM, FB), lambda i, j: (j, i, 0)), _H_PAIR] + [_ANY] * nc,
        out_shape=[jax.ShapeDtypeStruct((4, T, FB), BF16), jax.ShapeDtypeStruct((2, 4, T, FB), BF16)]
        + (carry.out_shapes() if nc else []),
        scratch_shapes=carry.sems() if nc else [],
        compiler_params=_cp(("arbitrary", "arbitrary") if nc else ("parallel", "parallel")),
    )(x16, w13.reshape(2, 4, D, FB), *(carry.arrays if nc else []))
    return res[0], res[1].reshape(NDEV, T, FB), res[2:]


def _ffn_down_bwd(df, w2, h16, *, name):
    def kern(df_ref, w_ref, h_ref, dh_ref):
        da = _dot(df_ref[...], w_ref[...], _NT)
        a, g = h_ref[0].astype(F32), h_ref[1].astype(F32)
        s = jax.nn.sigmoid(a)
        dh_ref[0] = (da * g * (s * (1.0 + a * (1.0 - s)))).astype(BF16)
        dh_ref[1] = (da * (a * s)).astype(BF16)

    return pl.pallas_call(
        kern, name=name, grid=(T // FFN_BM, 4),
        in_specs=[pl.BlockSpec((FFN_BM, D), lambda i, j: (i, 0)), pl.BlockSpec((FB, D), lambda i, j: (j, 0)), _H_PAIR],
        out_specs=_H_PAIR, out_shape=jax.ShapeDtypeStruct((2, 4, T, FB), BF16),
        compiler_params=_cp(("parallel", "parallel")),
    )(df, w2, h16.reshape(2, 4, T, FB)).reshape(NDEV, T, FB)


def _affine_fn(y, b, s):
    return (y + b) * s


def _affine_fwd(y, b, s, *, name):
    return _rowwise(lambda y_, b_, s_: [_affine_fn(y_, b_, s_)], [(y, POOL_W, 0)], [(b, None), (s, None)],
                    [(POOL_W, POOL_W, BF16)], [], name=name, tile=512)[0]


def _affine_bwd(y, dyo, b, s, *, name):
    def fn(y_, d_, b_, s_):
        _, vjp = jax.vjp(_affine_fn, y_, b_, s_)
        return list(vjp(d_))

    return _rowwise(fn, [(y, POOL_W, 0), (dyo, POOL_W, 0)], [(b, None), (s, None)],
                    [(POOL_W, POOL_W, BF16)], [(POOL_W, POOL_W), (POOL_W, POOL_W)], name=name, tile=512)


def _gnorm_fn(y, z, w):
    yz = y * _silu(z)
    return yz * lax.rsqrt(jnp.mean(jnp.square(yz), axis=-1, keepdims=True) + SSD_EPS) * w


def _gnorm_fwd(y, hcat, w, *, name):
    return _rowwise(lambda y_, z_, w_: [_gnorm_fn(y_, z_, w_)], [(y, 256, 0), (hcat, 256, Z0 // 256)], [(w, 256)],
                    [(D, 256, BF16)], [], name=name, tile=512, groups=4)[0]


def _gnorm_bwd(y, hcat, dyo, w, *, name):
    def fn(y_, z_, d_, w_):
        _, vjp = jax.vjp(_gnorm_fn, y_, z_, w_)
        return list(vjp(d_))

    return _rowwise(fn, [(y, 256, 0), (hcat, 256, Z0 // 256), (dyo, 256, 0)], [(w, 256)],
                    [(D, 256, F32), (D, 256, BF16)], [(D, 256)], name=name, tile=512, groups=4)


def _merge_fn(g0, g1, g2, ya, yb, yc, b0, b1, b2):
    return jax.nn.sigmoid(g0 + b0) * ya + jax.nn.sigmoid(g1 + b1) * yb + jax.nn.sigmoid(g2 + b2) * yc


def _merge_rows(hcat, ya, yb, yc):
    c = GATE0 // 256
    return [(hcat, 256, c), (hcat, 256, c + 4), (hcat, 256, c + 8), (ya, 256, 0), (yb, 256, 0), (yc, 256, 0)]


def _merge_fwd(hcat, ya, yb, yc, gb, *, name):
    pars = [(gb[0:1], 256), (gb[1:2], 256), (gb[2:3], 256)]
    return _rowwise(lambda *v: [_merge_fn(*v)], _merge_rows(hcat, ya, yb, yc), pars, [(D, 256, BF16)], [],
                    name=name, tile=512, groups=4)[0]


def _merge_bwd(hcat, ya, yb, yc, dm, gb, *, name):
    def fn(g0, g1, g2, ya_, yb_, yc_, dm_, b0, b1, b2):
        _, vjp = jax.vjp(_merge_fn, g0, g1, g2, ya_, yb_, yc_, b0, b1, b2)
        return list(vjp(dm_))

    pars = [(gb[0:1], 256), (gb[1:2], 256), (gb[2:3], 256)]
    return _rowwise(fn, _merge_rows(hcat, ya, yb, yc) + [(dm, 256, 0)], pars,
                    [(D, 256, BF16)] * 6, [(D, 256)] * 3, name=name, tile=512, groups=4)


def _amerge_fn(o0, o1, o2, l0, l1, l2):
    m = jnp.maximum(jnp.maximum(l0, l1), l2)
    e0, e1, e2 = jnp.exp(l0 - m), jnp.exp(l1 - m), jnp.exp(l2 - m)
    return (e0 * o0 + e1 * o1 + e2 * o2) / (e0 + e1 + e2)


def _amerge_fwd(os_, ls_, *, name):
    rows = [(v, 256, 0) for v in (*os_, *ls_)]
    return _rowwise(lambda *v: [_amerge_fn(*v)], rows, [], [(256, 256, BF16)], [], name=name, tile=1024)[0]


def _amerge_bwd(os_, ls_, dy, *, name):
    def fn(*v):
        _, vjp = jax.vjp(_amerge_fn, *v[:6])
        return list(vjp(v[6]))

    rows = [(v, 256, 0) for v in (*os_, *ls_, dy)]
    return _rowwise(fn, rows, [], [(256, 256, F32)] * 6, [], name=name, tile=1024)


def _loss_kernel(y, tgt, *, name):
    def fn(y_, t_):
        e = y_ - t_
        return [e * (1.0 / D), _colsum(e * e)]

    return _rowwise(fn, [(y, D, 0), (tgt, D, 0)], [], [(D, D, F32)], [(D, D)], name=name, tile=512)


def _adam_math(w, g, m, v):
    m2 = ADAM_B1 * m + (1.0 - ADAM_B1) * g
    v2 = ADAM_B2 * v + (1.0 - ADAM_B2) * jnp.square(g)
    m_hat = m2 / (1.0 - ADAM_B1 ** ADAM_STEP)
    v_hat = v2 / (1.0 - ADAM_B2 ** ADAM_STEP)
    return -ADAM_LR * (m_hat / (jnp.sqrt(v_hat) + ADAM_EPS) + ADAM_WD * w), m2, v2


def _adamw(w, g, m, v, *, name):
    r, c = w.shape
    tile = _pick(r, (512, 256, 128, 64, 32, 16, 8))
    fn = lambda *a: list(_adam_math(*a))
    return _rowwise(fn, [(w, c, 0), (g, c, 0), (m, c, 0), (v, c, 0)], [], [(c, c, F32)] * 3, [], name=name, tile=tile)


def _pool_lane_window(j, width):
    lane = lax.broadcasted_iota(jnp.int32, (1, width), 1) + j * width
    grp = lane // POOL_GDIM
    return grp


def _pool_select(grp, vals):
    out = vals[3]
    for gi in (2, 1, 0):
        out = jnp.where(grp == gi, vals[gi], out)
    return out


def _pool_fwd(hcat, *, name):
    width = 256

    def kern(u_ref, o_ref):
        u = u_ref[...]
        t = lax.broadcasted_iota(jnp.int32, (SEQ, 1), 0)
        grp = _pool_lane_window(pl.program_id(1), width)

        def shift(v, k):
            return jnp.where(t >= k, pltpu.roll(v, k, 0), 0.0)

        s2 = u + shift(u, 1)
        s4 = s2 + shift(s2, 2)
        s8 = s4 + shift(s4, 4)
        s16 = s8 + shift(s8, 8)
        win = _pool_select(grp, [jnp.full((1, width), float(w), F32) for w in POOL_WINDOWS])
        cnt = jnp.minimum((t + 1).astype(F32), win)
        o_ref[...] = (_pool_select(grp, [s2, s4, s8, s16]) / cnt - u).astype(o_ref.dtype)

    return pl.pallas_call(
        kern, name=name, grid=(NB, POOL_W // width),
        in_specs=[pl.BlockSpec((SEQ, width), lambda b, j: (b, U0 // width + j))],
        out_specs=pl.BlockSpec((SEQ, width), lambda b, j: (b, j)),
        out_shape=jax.ShapeDtypeStruct((T, POOL_W), BF16), compiler_params=_cp(("parallel", "parallel")),
    )(hcat)


def _pool_bwd(dp, *, name):
    width = 256

    def kern(d_ref, o_ref):
        d = d_ref[...]
        t = lax.broadcasted_iota(jnp.int32, (SEQ, 1), 0)
        grp = _pool_lane_window(pl.program_id(1), width)
        win = _pool_select(grp, [jnp.full((1, width), float(w), F32) for w in POOL_WINDOWS])
        dm = d / jnp.minimum((t + 1).astype(F32), win)

        def shift(v, k):
            return jnp.where(t < SEQ - k, pltpu.roll(v, SEQ - k, 0), 0.0)

        r2 = dm + shift(dm, 1)
        r4 = r2 + shift(r2, 2)
        r8 = r4 + shift(r4, 4)
        r16 = r8 + shift(r8, 8)
        o_ref[...] = (_pool_select(grp, [r2, r4, r8, r16]) - d).astype(o_ref.dtype)

    return pl.pallas_call(
        kern, name=name, grid=(NB, POOL_W // width),
        in_specs=[pl.BlockSpec((SEQ, width), lambda b, j: (b, j))],
        out_specs=pl.BlockSpec((SEQ, width), lambda b, j: (b, j)),
        out_shape=jax.ShapeDtypeStruct((T, POOL_W), BF16), compiler_params=_cp(("parallel", "parallel")),
    )(dp)


CONV_W = 512


def _conv_pre(x, w, b, t):
    pre = w[3:4] * x + b
    for k in (1, 2, 3):
        pre = pre + w[3 - k:4 - k] * jnp.where(t >= k, pltpu.roll(x, k, 0), 0.0)
    return pre


def _conv_fwd(hcat, w, b, *, name):
    def kern(x_ref, w_ref, b_ref, o_ref):
        t = lax.broadcasted_iota(jnp.int32, (SEQ, 1), 0)
        o_ref[...] = _silu(_conv_pre(x_ref[...], w_ref[...], b_ref[...], t))

    return pl.pallas_call(
        kern, name=name, grid=(NB, 2048 // CONV_W),
        in_specs=[pl.BlockSpec((SEQ, CONV_W), lambda s, j: (s, XBC0 // CONV_W + j)),
                  pl.BlockSpec((4, CONV_W), lambda s, j: (0, j)), pl.BlockSpec((1, CONV_W), lambda s, j: (0, j))],
        out_specs=pl.BlockSpec((SEQ, CONV_W), lambda s, j: (s, j)),
        out_shape=jax.ShapeDtypeStruct((T, 2048), F32), compiler_params=_cp(("parallel", "parallel")),
    )(hcat, w, b)


def _conv_bwd(hcat, w, b, dy, *, name):
    def kern(x_ref, w_ref, b_ref, dy_ref, dx_ref, dw_ref, db_ref):
        t = lax.broadcasted_iota(jnp.int32, (SEQ, 1), 0)
        x, w_ = x_ref[...], w_ref[...]
        pre = _conv_pre(x, w_, b_ref[...], t)
        s = jax.nn.sigmoid(pre)
        dpre = dy_ref[...] * (s * (1.0 + pre * (1.0 - s)))
        dx = w_[3:4] * dpre
        dws = [None] * 4
        dws[3] = _colsum(dpre * x)
        for k in (1, 2, 3):
            dx = dx + w_[3 - k:4 - k] * jnp.where(t < SEQ - k, pltpu.roll(dpre, SEQ - k, 0), 0.0)
            dws[3 - k] = _colsum(dpre * jnp.where(t >= k, pltpu.roll(x, k, 0), 0.0))
        dx_ref[...] = dx.astype(dx_ref.dtype)
        db = _colsum(dpre)
        first = pl.program_id(1) == 0

        @pl.when(first)
        def _():
            for k in range(4):
                dw_ref[k:k + 1, :] = dws[k]
            db_ref[...] = db

        @pl.when(jnp.logical_not(first))
        def _():
            for k in range(4):
                dw_ref[k:k + 1, :] += dws[k]
            db_ref[...] += db

    return pl.pallas_call(
        kern, name=name, grid=(2048 // CONV_W, NB),
        in_specs=[pl.BlockSpec((SEQ, CONV_W), lambda j, s: (s, XBC0 // CONV_W + j)),
                  pl.BlockSpec((4, CONV_W), lambda j, s: (0, j)), pl.BlockSpec((1, CONV_W), lambda j, s: (0, j)),
                  pl.BlockSpec((SEQ, CONV_W), lambda j, s: (s, j))],
        out_specs=[pl.BlockSpec((SEQ, CONV_W), lambda j, s: (s, j)), pl.BlockSpec((4, CONV_W), lambda j, s: (0, j)),
                   pl.BlockSpec((1, CONV_W), lambda j, s: (0, j))],
        out_shape=[jax.ShapeDtypeStruct((T, 2048), BF16), jax.ShapeDtypeStruct((4, 2048), F32),
                   jax.ShapeDtypeStruct((1, 2048), F32)],
        compiler_params=_cp(("arbitrary", "arbitrary")),
    )(hcat, w, b, dy)


def _softplus(v):
    return jnp.maximum(v, 0.0) + jnp.log1p(jnp.exp(-jnp.abs(v)))


def _dot(a, b, dims):
    return lax.dot_general(a, b, (dims, ((), ())), preferred_element_type=F32)


_NN, _NT, _TN = ((1,), (0,)), ((1,), (1,)), ((0,), (0,))


def _ssd_specs(order):
    def spec(width, col0):
        return pl.BlockSpec((SEQ, width), lambda i, j, c0=col0: (order(i, j)[0], c0 + order(i, j)[1]))

    def par():
        return pl.BlockSpec((1, 256), lambda i, j: (0, order(i, j)[1]))

    return spec, par


def _ssd_chunk_common(c, xs_ref, b_ref, c_ref, dtx_ref, dtb, a, trif):
    r0 = pl.multiple_of(c * CHUNK, CHUNK)
    rows = pl.ds(r0, CHUNK)
    x = xs_ref[rows, :]
    bb = b_ref[rows, :].astype(BF16)
    cb = c_ref[rows, :].astype(BF16)
    raw = dtx_ref[rows, :] + dtb
    dt = _softplus(raw)
    cs = jnp.dot(trif, dt * a, precision=HIGHEST, preferred_element_type=F32)
    return rows, x, bb, cb, raw, dt, cs


def _head_decay(cs, me, tri):
    cse = jnp.max(jnp.where(me, cs, -jnp.inf), axis=1, keepdims=True)
    csb = jnp.broadcast_to(cse, (CHUNK, CHUNK))
    return jnp.where(tri, jnp.exp(csb - csb.T), 0.0)


def _ssd_fwd(xa, dtx, dtb, alog, dsk, *, name):
    spec, par = _ssd_specs(lambda g, b: (b, g))

    def kern(xs_ref, b_ref, c_ref, dtx_ref, dtb_ref, alog_ref, dsk_ref, y_ref, hs_ref, h_scr):
        row = lax.broadcasted_iota(jnp.int32, (CHUNK, CHUNK), 0)
        col = lax.broadcasted_iota(jnp.int32, (CHUNK, CHUNK), 1)
        tri = row >= col
        trif = tri.astype(F32)
        head = lax.broadcasted_iota(jnp.int32, (1, 256), 1) // 64
        dtb_, a, dsk_ = dtb_ref[...], -jnp.exp(alog_ref[...]), dsk_ref[...]
        h_scr[...] = jnp.zeros_like(h_scr)

        def chunk(c, carry):
            rows, x, bb, cb, _, dt, cs = _ssd_chunk_common(c, xs_ref, b_ref, c_ref, dtx_ref, dtb_, a, trif)
            cs_last = cs[CHUNK - 1:CHUNK, :]
            xdt = x * dt
            xdtb = xdt.astype(BF16)
            g = _dot(cb, bb, _NT)
            hin = h_scr[...]
            hs_ref[rows, :] = hin
            y = jnp.exp(cs) * _dot(cb, hin.astype(BF16), _NN) + dsk_ * x
            for e in range(4):
                me = head == e
                m = (g * _head_decay(cs, me, tri)).astype(BF16)
                y = y + jnp.where(me, _dot(m, xdtb, _NN), 0.0)
            y_ref[rows, :] = y
            st = _dot(bb, (jnp.exp(cs_last - cs) * xdt).astype(BF16), _TN)
            h_scr[...] = hin * jnp.exp(cs_last) + st
            return carry

        lax.fori_loop(0, NCHUNK, chunk, 0)

    return pl.pallas_call(
        kern, name=name, grid=(4, NB),
        in_specs=[spec(256, 0), spec(128, 8), spec(128, 12), spec(256, 0), par(), par(), par()],
        out_specs=[spec(256, 0), spec(256, 0)],
        out_shape=[jax.ShapeDtypeStruct((T, D), F32), jax.ShapeDtypeStruct((T, D), F32)],
        scratch_shapes=[pltpu.VMEM((CHUNK, 256), F32)],
        compiler_params=_cp(("parallel", "parallel")),
    )(xa, xa, xa, dtx, dtb, alog, dsk)


def _ssd_bwd(xa, dtx, dtb, alog, dsk, hs, dy, *, name):
    spec, par = _ssd_specs(lambda g, b: (b, g))

    def kern(xs_ref, b_ref, c_ref, dtx_ref, dtb_ref, alog_ref, dsk_ref, hs_ref, dy_ref,
             dx_ref, db_ref, dc_ref, ddt_ref, ddtb_ref, dalog_ref, ddsk_ref, dh_scr):
        row = lax.broadcasted_iota(jnp.int32, (CHUNK, CHUNK), 0)
        col = lax.broadcasted_iota(jnp.int32, (CHUNK, CHUNK), 1)
        tri = row >= col
        trif = tri.astype(F32)
        trit = (row <= col).astype(F32)
        is_last = lax.broadcasted_iota(jnp.int32, (CHUNK, 1), 0) == CHUNK - 1
        head = lax.broadcasted_iota(jnp.int32, (1, 256), 1) // 64
        dtb_, a, dsk_ = dtb_ref[...], -jnp.exp(alog_ref[...]), dsk_ref[...]
        dh_scr[...] = jnp.zeros_like(dh_scr)

        @pl.when(pl.program_id(1) == 0)
        def _():
            ddtb_ref[...] = jnp.zeros_like(ddtb_ref)
            dalog_ref[...] = jnp.zeros_like(dalog_ref)
            ddsk_ref[...] = jnp.zeros_like(ddsk_ref)

        def hsum(v, me):
            return jnp.sum(jnp.where(me, v, 0.0), axis=1, keepdims=True)

        def chunk(ci, carry):
            c = NCHUNK - 1 - ci
            rows, x, bb, cb, raw, dt, cs = _ssd_chunk_common(c, xs_ref, b_ref, c_ref, dtx_ref, dtb_, a, trif)
            cs_last = cs[CHUNK - 1:CHUNK, :]
            ecs = jnp.exp(cs)
            dsx = jnp.exp(cs_last - cs)
            xdt = x * dt
            xdtb = xdt.astype(BF16)
            g = _dot(cb, bb, _NT)
            hin = hs_ref[rows, :]
            hinb = hin.astype(BF16)
            dy_ = dy_ref[rows, :]
            dyb = dy_.astype(BF16)
            dh = dh_scr[...]
            dhb = dh.astype(BF16)
            bdh = _dot(bb, dhb, _NN)
            yoff = ecs * _dot(cb, hinb, _NN)
            dxdt = dsx * bdh
            t1, t2, t3 = dy_ * yoff, xdt * bdh, dh * hin
            dg = jnp.zeros((CHUNK, CHUNK), F32)
            dcs = jnp.zeros((CHUNK, 256), F32)
            for e in range(4):
                me = head == e
                l_ = _head_decay(cs, me, tri)
                m = g * l_
                dxdt = dxdt + jnp.where(me, _dot(m.astype(BF16), dyb, _TN), 0.0)
                dm = _dot(jnp.where(me, dy_, 0.0).astype(BF16), xdtb, _NT)
                dg = dg + dm * l_
                w = dm * m
                dds = hsum(t2, me)
                dse = jnp.max(jnp.where(me, dsx, -jnp.inf), axis=1, keepdims=True)
                ecl = jnp.max(jnp.where(me, jnp.exp(cs_last), -jnp.inf), axis=1, keepdims=True)
                dcs_e = hsum(t1, me) + jnp.sum(w, axis=1, keepdims=True) - jnp.sum(w.T, axis=1, keepdims=True) - dds * dse
                last = jnp.sum(dds * dse, axis=0, keepdims=True) + ecl * jnp.sum(hsum(t3, me), axis=0, keepdims=True)
                dcs_e = dcs_e + jnp.where(is_last, last, 0.0)
                dcs = dcs + jnp.where(me, dcs_e, 0.0)
            dadt = jnp.dot(trit, dcs, precision=HIGHEST, preferred_element_type=F32)
            ddt = a * dadt
            dxx = dxdt * x
            dyx = dy_ * x
            dsk_acc = jnp.zeros((1, 256), F32)
            for e in range(4):
                me = head == e
                ddt = ddt + jnp.where(me, hsum(dxx, me), 0.0)
                dsk_acc = dsk_acc + jnp.where(me, jnp.sum(hsum(dyx, me), axis=0, keepdims=True), 0.0)
            draw = ddt * jax.nn.sigmoid(raw)
            ddt_ref[rows, :] = draw
            ddtb_ref[...] += _colsum(draw)
            dalog_ref[...] += _colsum(dadt * dt) * a
            ddsk_ref[...] += dsk_acc
            dx_ref[rows, :] = dxdt * dt + dsk_ * dy_
            edy = (ecs * dy_).astype(BF16)
            dgb = dg.astype(BF16)
            dc_ref[rows, :] = _dot(dgb, bb, _NN) + _dot(edy, hinb, _NT)
            db_ref[rows, :] = _dot(dgb, cb, _TN) + _dot((dsx * xdt).astype(BF16), dhb, _NT)
            dh_scr[...] = jnp.exp(cs_last) * dh + _dot(cb, edy, _TN)
            return carry

        lax.fori_loop(0, NCHUNK, chunk, 0)

    return pl.pallas_call(
        kern, name=name, grid=(4, NB),
        in_specs=[spec(256, 0), spec(128, 8), spec(128, 12), spec(256, 0), par(), par(), par(), spec(256, 0), spec(256, 0)],
        out_specs=[spec(256, 0), spec(128, 0), spec(128, 0), spec(256, 0), par(), par(), par()],
        out_shape=[jax.ShapeDtypeStruct((T, D), F32), jax.ShapeDtypeStruct((T, 512), F32), jax.ShapeDtypeStruct((T, 512), F32),
                   jax.ShapeDtypeStruct((T, D), F32), jax.ShapeDtypeStruct((1, D), F32), jax.ShapeDtypeStruct((1, D), F32),
                   jax.ShapeDtypeStruct((1, D), F32)],
        scratch_shapes=[pltpu.VMEM((CHUNK, 256), F32)],
        compiler_params=_cp(("arbitrary", "arbitrary")),
    )(xa, xa, xa, dtx, dtb, alog, dsk, hs, dy)


def _t5_bucket_np(dist):
    dist = np.maximum(dist, 0)
    max_exact = REL_BUCKETS // 2
    large = max_exact + (np.log(np.maximum(dist, 1) / max_exact) / np.log(REL_MAX_DIST / max_exact)
                         * (REL_BUCKETS - max_exact)).astype(np.int32)
    large = np.minimum(large, REL_BUCKETS - 1)
    return np.where(dist < max_exact, dist, large).astype(np.int32)


def _attn_bias(rel_bias):
    qi = np.arange(ATTN_BLK)[:, None]
    kj = np.arange(2 * ATTN_BLK)[None, :]
    delta = qi - kj + ATTN_BLK
    out = []
    for gi, (window, dil) in enumerate(ATTN_CONFIGS):
        in_band = (delta >= 0) & (delta <= window // dil)
        bucket = jnp.asarray(_t5_bucket_np(delta * dil).reshape(-1, 1))
        one_hot = (bucket == jnp.arange(REL_BUCKETS)[None, :]).astype(F32)
        tab = jnp.dot(one_hot, rel_bias[:, 4 * gi:4 * gi + 4], precision=HIGHEST).reshape(ATTN_BLK, 2 * ATTN_BLK, 4)
        out.append(jnp.where(jnp.asarray(in_band)[None], tab.transpose(2, 0, 1), NEG))
    return jnp.stack(out)


def _attn_band(ref, cur, prev):
    return jnp.concatenate([ref[prev, :], ref[cur, :]], axis=0).astype(BF16)


def _attn_head_masks():
    hi = lax.broadcasted_iota(jnp.int32, (1, 128), 1) >= 64
    return [jnp.logical_not(hi), hi]


def _attn_logits(qm, kband, bias, n, sub_blocks):
    kj = lax.broadcasted_iota(jnp.int32, (1, 2 * ATTN_BLK), 1)
    ok = jnp.logical_or(n % sub_blocks > 0, kj >= ATTN_BLK)
    return jnp.where(ok, _dot(qm, kband, _NT) * 0.125 + bias, NEG)


def _attn_specs(col0):
    qkv = [pl.BlockSpec((SEQ, 128), lambda p, s, c=col0 + 2 * i: (s, c + p)) for i in range(3)]
    seq = pl.BlockSpec((SEQ, 128), lambda p, s: (s, p))
    tab = pl.BlockSpec((2, ATTN_BLK, 2 * ATTN_BLK), lambda p, s: (p, 0, 0))
    return qkv, seq, tab


def _attn_rows(n):
    cur = pl.ds(pl.multiple_of(n * ATTN_BLK, ATTN_BLK), ATTN_BLK)
    prev = pl.ds(pl.multiple_of(jnp.maximum(n - 1, 0) * ATTN_BLK, ATTN_BLK), ATTN_BLK)
    return cur, prev


def _attn_fwd(qkv, col0, bias, sub_blocks, *, name):
    def kern(q_ref, k_ref, v_ref, bias_ref, o_ref, l_ref):
        masks = _attn_head_masks()

        def blk(n, carry):
            cur, prev = _attn_rows(n)
            q = q_ref[cur, :]
            kband, vband = _attn_band(k_ref, cur, prev), _attn_band(v_ref, cur, prev)
            out = jnp.zeros((ATTN_BLK, 128), F32)
            lse = jnp.zeros((ATTN_BLK, 128), F32)
            for hh, mk in enumerate(masks):
                logits = _attn_logits(jnp.where(mk, q, 0.0).astype(BF16), kband, bias_ref[hh], n, sub_blocks)
                m = jnp.max(logits, axis=-1, keepdims=True)
                p = jnp.exp(logits - m)
                den = jnp.sum(p, axis=-1, keepdims=True)
                out = jnp.where(mk, _dot((p / den).astype(BF16), vband, _NN), out)
                lse = jnp.where(mk, m + jnp.log(den), lse)
            o_ref[cur, :] = out
            l_ref[cur, :] = lse
            return carry

        lax.fori_loop(0, SEQ // ATTN_BLK, blk, 0)

    qkv_specs, seq, tab = _attn_specs(col0)
    return pl.pallas_call(
        kern, name=name, grid=(2, NB), in_specs=qkv_specs + [tab],
        out_specs=[seq, seq], out_shape=[jax.ShapeDtypeStruct((T, 256), F32)] * 2,
        compiler_params=_cp(("parallel", "parallel")),
    )(qkv, qkv, qkv, bias)


def _attn_bwd(qkv, col0, o, lse, do, dl, bias, sub_blocks, *, name):
    def kern(q_ref, k_ref, v_ref, o_ref, l_ref, do_ref, dl_ref, bias_ref, dq_ref, dk_ref, dv_ref, dbias_ref, dk_acc, dv_acc):
        masks = _attn_head_masks()

        @pl.when(pl.program_id(1) == 0)
        def _():
            dbias_ref[...] = jnp.zeros_like(dbias_ref)

        dk_acc[...] = jnp.zeros_like(dk_acc)
        dv_acc[...] = jnp.zeros_like(dv_acc)

        def blk(n, carry):
            cur, prev = _attn_rows(n)
            q, o_, lse_, do_, dl_ = q_ref[cur, :], o_ref[cur, :], l_ref[cur, :], do_ref[cur, :], dl_ref[cur, :]
            kband, vband = _attn_band(k_ref, cur, prev), _attn_band(v_ref, cur, prev)
            dq = jnp.zeros((ATTN_BLK, 128), F32)
            dkb = jnp.zeros((2 * ATTN_BLK, 128), F32)
            dvb = jnp.zeros((2 * ATTN_BLK, 128), F32)
            for hh, mk in enumerate(masks):
                qm = jnp.where(mk, q, 0.0).astype(BF16)
                dom = jnp.where(mk, do_, 0.0)
                domb = dom.astype(BF16)
                logits = _attn_logits(qm, kband, bias_ref[hh], n, sub_blocks)
                p = jnp.exp(logits - jnp.max(jnp.where(mk, lse_, -jnp.inf), axis=-1, keepdims=True))
                dd = jnp.sum(dom * o_, axis=-1, keepdims=True)
                dls = jnp.sum(jnp.where(mk, dl_, 0.0), axis=-1, keepdims=True)
                ds = p * (_dot(domb, vband, _NT) - dd + dls)
                dbias_ref[hh] += ds
                dsb = (ds * 0.125).astype(BF16)
                dq = jnp.where(mk, _dot(dsb, kband, _NN), dq)
                dkb = dkb + _dot(dsb, qm, _TN)
                dvb = dvb + _dot(p.astype(BF16), domb, _TN)
            dq_ref[cur, :] = dq.astype(dq_ref.dtype)
            dk_acc[prev, :] += dkb[:ATTN_BLK]
            dk_acc[cur, :] += dkb[ATTN_BLK:]
            dv_acc[prev, :] += dvb[:ATTN_BLK]
            dv_acc[cur, :] += dvb[ATTN_BLK:]
            return carry

        lax.fori_loop(0, SEQ // ATTN_BLK, blk, 0)
        dk_ref[...] = dk_acc[...].astype(dk_ref.dtype)
        dv_ref[...] = dv_acc[...].astype(dv_ref.dtype)

    qkv_specs, seq, tab = _attn_specs(col0)
    return pl.pallas_call(
        kern, name=name, grid=(2, NB),
        in_specs=qkv_specs + [seq] * 4 + [tab], out_specs=[seq, seq, seq, tab],
        out_shape=[jax.ShapeDtypeStruct((T, 256), BF16)] * 3 + [jax.ShapeDtypeStruct((4, ATTN_BLK, 2 * ATTN_BLK), F32)],
        scratch_shapes=[pltpu.VMEM((SEQ, 128), F32), pltpu.VMEM((SEQ, 128), F32)],
        compiler_params=_cp(("arbitrary", "arbitrary")),
    )(qkv, qkv, qkv, o, lse, do, dl, bias)


def _to_sub(t, dil):
    if dil == 1:
        return t
    return t.reshape(NB, SEQ // dil, dil, t.shape[-1]).transpose(0, 2, 1, 3).reshape(T, t.shape[-1])


def _from_sub(t, dil):
    if dil == 1:
        return t
    return t.reshape(NB, dil, SEQ // dil, t.shape[-1]).transpose(0, 2, 1, 3).reshape(T, t.shape[-1])


_ANY = pl.BlockSpec(memory_space=pl.ANY)
MESH = pl.DeviceIdType.MESH


def _slot(dev):
    return 4 * dev[0] + 2 * dev[1] + dev[2]


class _GatherSteps:
    def __init__(self, x_refs, out_refs, sems):
        self.x_refs, self.out_refs = x_refs, out_refs
        self.send_sems, self.recv_sems, self.local_sems = sems
        self.n = len(x_refs)
        x, y, c = lax.axis_index("x"), lax.axis_index("y"), lax.axis_index("c")
        self.c, self.me, self.sibling = c, (x, y, c), (x, y, 1 - c)
        self.chips = [(1 - x, y), (x, 1 - y), (1 - x, 1 - y)]

    def copies(self, k, block, to, own=False):
        return [pltpu.make_async_remote_copy(
            src_ref=self.x_refs[a] if own else self.out_refs[a].at[_slot(block)], dst_ref=self.out_refs[a].at[_slot(block)],
            send_sem=self.send_sems.at[k, a], recv_sem=self.recv_sems.at[k, a], device_id=to, device_id_type=MESH)
            for a in range(self.n)]

    def mine(self):
        return [pltpu.make_async_copy(self.x_refs[a], self.out_refs[a].at[_slot(self.me)], self.local_sems.at[a])
                for a in range(self.n)]

    def first(self):
        out = self.copies(0, self.me, self.sibling, own=True)
        for j, chip in enumerate(self.chips):
            out += self.copies(1 + j, self.me, (*chip, self.c), own=True)
        return out

    def passed(self, j):
        return self.copies(4 + j, (*self.chips[j], self.c), self.sibling)

    def start(self):
        for cp in self.mine() + self.first():
            cp.start()

    def pass_on(self):
        for j, chip in enumerate(self.chips):
            for cp in self.copies(1 + j, (*chip, self.c), self.me):
                cp.wait_recv()
            for cp in self.passed(j):
                cp.start()

    def finish(self):
        for cp in self.copies(0, self.sibling, self.me):
            cp.wait_recv()
        for j, chip in enumerate(self.chips):
            for cp in self.copies(4 + j, (*chip, 1 - self.c), self.me):
                cp.wait_recv()
        for cp in self.first() + [cp for j in range(3) for cp in self.passed(j)]:
            cp.wait_send()
        for cp in self.mine():
            cp.wait()


class _ChipSteps:
    def __init__(self, x_refs, out_refs, sems):
        self.x_refs, self.out_refs = x_refs, out_refs
        self.send_sems, self.recv_sems, self.local_sems = sems
        self.n = len(x_refs)
        x, y, self.c = lax.axis_index("x"), lax.axis_index("y"), lax.axis_index("c")
        self.my_chip = 2 * x + y
        self.chips = [(1 - x, y), (x, 1 - y), (1 - x, 1 - y)]

    def mine(self):
        return [pltpu.make_async_copy(self.x_refs[a].at[self.my_chip], self.out_refs[a].at[self.my_chip], self.local_sems.at[a])
                for a in range(self.n)]

    def copies(self, sending):
        return [pltpu.make_async_remote_copy(
            src_ref=self.x_refs[a].at[2 * px + py if sending else self.my_chip],
            dst_ref=self.out_refs[a].at[self.my_chip if sending else 2 * px + py],
            send_sem=self.send_sems.at[k, a], recv_sem=self.recv_sems.at[k, a], device_id=(px, py, self.c), device_id_type=MESH)
            for k, (px, py) in enumerate(self.chips) for a in range(self.n)]

    def start(self):
        for cp in self.mine() + self.copies(True):
            cp.start()

    def pass_on(self):
        pass

    def finish(self):
        for cp in self.copies(False):
            cp.wait_recv()
        for cp in self.copies(True):
            cp.wait_send()
        for cp in self.mine():
            cp.wait()


class _Carry:
    def __init__(self, kind, arrays):
        self.kind, self.arrays, self.n = kind, list(arrays), len(arrays)
        self.k = {"gather": 7, "chips": 3}[kind]

    def out_shapes(self):
        lead = (NDEV,) if self.kind == "gather" else ()
        return [jax.ShapeDtypeStruct((*lead, *a.shape), a.dtype) for a in self.arrays]

    def sems(self):
        return [pltpu.SemaphoreType.DMA((self.k, self.n)), pltpu.SemaphoreType.DMA((self.k, self.n)),
                pltpu.SemaphoreType.DMA((self.n,))]

    def steps(self, x_refs, out_refs, sems):
        return (_GatherSteps if self.kind == "gather" else _ChipSteps)(x_refs, out_refs, sems)

    def standalone(self, name):
        def kern(*refs):
            steps = self.steps(refs[:self.n], refs[self.n:2 * self.n], refs[2 * self.n:])
            steps.start()
            steps.pass_on()
            steps.finish()

        return pl.pallas_call(kern, name=name, in_specs=[_ANY] * self.n, out_specs=[_ANY] * self.n, out_shape=self.out_shapes(),
                              scratch_shapes=self.sems())(*self.arrays)


class _Carried:
    def __init__(self, carry, x_refs, out_refs, sems, step, n_steps):
        self.steps, self.step, self.n_steps = carry.steps(x_refs, out_refs, sems), step, n_steps

    def before(self):
        pl.when(self.step == 0)(self.steps.start)
        pl.when(self.step == self.n_steps // 2)(self.steps.pass_on)

    def after(self):
        pl.when(self.step == self.n_steps - 1)(self.steps.finish)


def _all_gather(shards, *, name):
    return _Carry("gather", shards).standalone(name)


def _exchange_sibling(arrs, *, name):
    n = len(arrs)

    def kern(*refs):
        x_refs, out_refs = refs[:n], refs[n:2 * n]
        send_sems, recv_sems = refs[2 * n:]
        x, y, c = lax.axis_index("x"), lax.axis_index("y"), lax.axis_index("c")
        sends = [pltpu.make_async_remote_copy(
            src_ref=x_refs[a].at[1 - c], dst_ref=out_refs[a], send_sem=send_sems.at[a], recv_sem=recv_sems.at[a],
            device_id=(x, y, 1 - c), device_id_type=MESH) for a in range(n)]
        for cp in sends:
            cp.start()
        for cp in sends:
            cp.wait_recv()
        for cp in sends:
            cp.wait_send()

    return pl.pallas_call(
        kern, name=name, in_specs=[_ANY] * n, out_specs=[_ANY] * n,
        out_shape=[jax.ShapeDtypeStruct(a.shape[1:], a.dtype) for a in arrs],
        scratch_shapes=[pltpu.SemaphoreType.DMA((n,)), pltpu.SemaphoreType.DMA((n,))],
    )(*arrs)


def _exchange_chips(arrs, *, name):
    return _Carry("chips", arrs).standalone(name)


def _add_own_half(full, other, c_idx, *, name):
    _, r, c = full.shape
    tile = _pick(r, (512, 256, 128, 64, 32, 16))

    def kern(c_ref, f_ref, o_ref, out_ref):
        out_ref[...] = (f_ref[...].astype(F32) + o_ref[...].astype(F32)).astype(BF16)

    return pl.pallas_call(
        kern, name=name,
        grid_spec=pltpu.PrefetchScalarGridSpec(
            num_scalar_prefetch=1, grid=(r // tile,),
            in_specs=[pl.BlockSpec((None, tile, c), lambda i, c_ref: (c_ref[0], i, 0)),
                      pl.BlockSpec((tile, c), lambda i, c_ref: (i, 0))],
            out_specs=pl.BlockSpec((tile, c), lambda i, c_ref: (i, 0))),
        out_shape=jax.ShapeDtypeStruct((r, c), BF16), compiler_params=_cp(("parallel",)),
    )(c_idx, full, other)


def _adamw_reduce(parts, w, m, v, *, name):
    ns, r, c = parts[0].shape
    tile = _pick(r, (256, 128, 64, 32, 16, 8))

    def kern(*refs):
        p_refs, (w_ref, m_ref, v_ref, g_ref, d_ref, m2_ref, v2_ref) = refs[:DEPTH], refs[DEPTH:]
        for l in range(DEPTH):
            @pl.when(pl.program_id(0) == l)
            def _(p_ref=p_refs[l]):
                g = p_ref[0].astype(F32)
                for s in range(1, ns):
                    g = g + p_ref[s].astype(F32)
                g_ref[...] = g
                d_ref[...], m2_ref[...], v2_ref[...] = _adam_math(w_ref[...], g, m_ref[...], v_ref[...])

    p_specs = [pl.BlockSpec((ns, tile, c), lambda l_, i, l=l: (0, jnp.where(l_ == l, i, 0), 0)) for l in range(DEPTH)]
    blk = pl.BlockSpec((None, tile, c), lambda l_, i: (l_, i, 0))
    return pl.pallas_call(
        kern, name=name, grid=(DEPTH, r // tile), in_specs=p_specs + [blk, blk, blk],
        out_specs=[blk] * 4, out_shape=[jax.ShapeDtypeStruct((DEPTH, r, c), F32)] * 4,
        compiler_params=_cp(("arbitrary", "arbitrary")),
    )(*parts, w, m, v)


def _sum_slots(parts, *, name):
    ns, r, c = parts.shape
    tile = _pick(r, (256, 128, 64, 32, 16, 8))

    def kern(p_ref, o_ref):
        acc = p_ref[0].astype(F32)
        for d in range(1, ns):
            acc = acc + p_ref[d].astype(F32)
        o_ref[...] = acc

    return pl.pallas_call(
        kern, name=name, grid=(r // tile,), in_specs=[pl.BlockSpec((ns, tile, c), lambda i: (0, i, 0))],
        out_specs=pl.BlockSpec((tile, c), lambda i: (i, 0)), out_shape=jax.ShapeDtypeStruct((r, c), F32),
        compiler_params=_cp(("parallel",)),
    )(parts)


BIG = [
    ("ffn1_w13", (1024, 704), 1), ("ffn1_w2", (352, 1024), 0), ("w_in", (1024, 1154), 1), ("p_pool", (768, 128), 1),
    ("p_ssd", (128, 1024), 0), ("p_attn", (256, 128), 1), ("w_out", (128, 1024), 0), ("ffn2_w13", (1024, 704), 1),
    ("ffn2_w2", (352, 1024), 0),
]
BLOCKED = ("ffn1_w13", "ffn2_w13")


def _from_blocks(blk, ax):
    _, r, c = blk.shape
    return blk.transpose(1, 0, 2).reshape(r, NDEV * c) if ax == 1 else blk.reshape(NDEV * r, c)


def _to_blocks(full, ax):
    r, c = full.shape
    return full.reshape(r, NDEV, c // NDEV).transpose(1, 0, 2) if ax == 1 else full.reshape(NDEV, r // NDEV, c)


class _SmallPack:
    def __init__(self, shapes):
        self.shapes = shapes
        self.rows = [-(-int(np.prod(s)) // 128) for s in shapes]
        self.total = -(-sum(self.rows) // 256) * 256

    def pack(self, arrs):
        parts = []
        for a, s, r in zip(arrs, self.shapes, self.rows):
            assert tuple(a.shape) == tuple(s), (a.shape, s)
            flat = a.astype(F32).reshape(-1)
            parts.append(jnp.pad(flat, (0, r * 128 - flat.shape[0])).reshape(r, 128))
        parts.append(jnp.zeros((self.total - sum(self.rows), 128), F32))
        return jnp.concatenate(parts, axis=0)

    def unpack(self, packed):
        out, off = [], 0
        for s, r in zip(self.shapes, self.rows):
            out.append(packed[off:off + r].reshape(-1)[:int(np.prod(s))].reshape(s))
            off += r
        return out


def _w_in_to_padded(w):
    z = jnp.zeros((*w.shape[:-1], HC - DT0 - 16), w.dtype)
    qkv = [w[..., o + 256 * gi:o + 256 * gi + 256] for gi in range(3) for o in (O_Q, O_K, O_V)]
    return jnp.concatenate([w[..., O_Z:O_XBC], w[..., O_XBC:O_DT], w[..., O_G:O_END], *qkv, w[..., O_U:O_Z], w[..., O_DT:O_Q], z],
                           axis=-1)


def _w_in_from_padded(g):
    qkv = [g[..., QKV0 + 768 * gi + o:QKV0 + 768 * gi + o + 256] for o in (0, 256, 512) for gi in range(3)]
    return jnp.concatenate([g[..., U0:DT0], g[..., Z0:XBC0], g[..., XBC0:GATE0], g[..., DT0:DT0 + 16], *qkv, g[..., GATE0:QKV0]],
                           axis=-1)


def _block_diag(pw):
    out = jnp.zeros((POOL_W, POOL_W), pw.dtype)
    for gi in range(4):
        out = lax.dynamic_update_slice(out, pw[gi], (gi * POOL_GDIM, gi * POOL_GDIM))
    return out


def _expand_heads(v):
    return jnp.repeat(v, 64).reshape(1, D)


def _ffn_fwd(x, x16, w13, w2, g, b, tag, carry_up=(), carry_f=()):
    act, h16, got_up = _ffn_up(x16, w13, carry=_Carry("gather", carry_up) if carry_up else None, name=f"{tag}_up")
    f = _mm(act, w2, carry=_Carry("gather", carry_f) if carry_f else None, name=f"{tag}_f")
    f, got_f = f if carry_f else (f, [])
    y, y16 = _ln_fwd(x, f, g, b, scale=FFN_RES, name=f"{tag}_ln")
    return y, y16, (h16, act, f), list(got_up) + list(got_f)


def _ffn_bwd(x, x16, w13, w2, g, b, saved, dy, tag, carry_dx=(), carry_dw13=()):
    h16, act, f = saved
    dres, df, dg, db = _ln_bwd(x, f, dy, g, b, scale=FFN_RES, name=f"{tag}_ln_bwd")
    dw2 = _mm(act.transpose(0, 2, 1).reshape(DFF, T), df, out_dtype=BF16, name=f"{tag}_dw2")
    dh = _ffn_down_bwd(df, w2, h16, name=f"{tag}_dh")
    dx = _mm(dh, w13, tb=True, add=dres, carry=_Carry("chips", carry_dx) if carry_dx else None, name=f"{tag}_dx")
    dx, got_dx = dx if carry_dx else (dx, [])
    dw13 = _mm(x16.T, dh, out_blocked=True, out_dtype=BF16, carry=_Carry("chips", carry_dw13) if carry_dw13 else None,
               name=f"{tag}_dw13")
    dw13, got_dw13 = dw13 if carry_dw13 else (dw13, [])
    return dx, dw13, dw2, dg[0], db[0], list(got_dx) + list(got_dw13)


CARRIERS = {"ffn1_up": ["ffn1_w13"], "ffn1_f": ["ffn1_w2", "p_pool", "p_attn"], "mix_in": ["w_in"], "ffn2_up": ["ffn2_w13"],
            "ffn2_f": ["ffn2_w2", "p_ssd", "w_out"]}


def _layer_fwd(x0, x0_16, p, bias, nxt=None):
    send = {k: ([nxt[n] for n in names] if nxt is not None else []) for k, names in CARRIERS.items()}
    got = {}
    x1, x1_16, s1, g1 = _ffn_fwd(x0, x0_16, p["ffn1_w13"], p["ffn1_w2"], p["ln1_g"], p["ln1_b"], "ffn1",
                                 send["ffn1_up"], send["ffn1_f"])
    got.update(zip(CARRIERS["ffn1_up"] + CARRIERS["ffn1_f"], g1))
    hcat = _mm(x1_16, p["w_in"], carry=_Carry("gather", send["mix_in"]) if nxt is not None else None, name="mix_in")
    if nxt is not None:
        hcat, g_in = hcat
        got.update(zip(CARRIERS["mix_in"], g_in))
    pooled = _pool_fwd(hcat, name="pool")
    ya_lin = _mm(pooled, p["pool_bd"], name="pool_lin")
    ya_pre = _affine_fwd(ya_lin, p["pool_b"], p["pool_scale"], name="pool_affine")
    ya = _mm(ya_pre, p["p_pool"], name="pool_out")
    xa = _conv_fwd(hcat, p["conv_w"], p["conv_b"], name="conv")
    dtx = jnp.repeat(hcat[:, DT0:DT0 + SSD_HEADS], 64, axis=1)
    ysc, hs = _ssd_fwd(xa, dtx, p["dt_bias"], p["a_log"], p["d_skip"], name="ssd")
    yb_pre = _gnorm_fwd(ysc, hcat, p["ssd_norm"], name="ssd_norm")
    yb = _mm(yb_pre, p["p_ssd"], name="ssd_out")
    outs, lses, attn_sv = [], [], []
    for gi, (_, dil) in enumerate(ATTN_CONFIGS):
        c0 = QKV0 + 768 * gi
        qkv, col0 = (hcat, c0 // 128) if dil == 1 else (_to_sub(hcat[:, c0:c0 + 768], dil), 0)
        o, l = _attn_fwd(qkv, col0, bias[gi], SEQ // dil // ATTN_BLK, name=f"attn{gi}")
        attn_sv.append((None if dil == 1 else qkv, o, l))
        outs.append(_from_sub(o, dil))
        lses.append(_from_sub(l, dil))
    yc_pre = _amerge_fwd(outs, lses, name="attn_merge")
    yc = _mm(yc_pre, p["p_attn"], name="attn_out")
    merged = _merge_fwd(hcat, ya, yb, yc, p["gate_b"], name="merge")
    mix = _mm(merged, p["w_out"], name="mix_out")
    x2, x2_16 = _ln_fwd(x1, mix, p["ln2_g"], p["ln2_b"], scale=1.0, name="ln2")
    x3, x3_16, s3, g3 = _ffn_fwd(x2, x2_16, p["ffn2_w13"], p["ffn2_w2"], p["ln3_g"], p["ln3_b"], "ffn2",
                                 send["ffn2_up"], send["ffn2_f"])
    got.update(zip(CARRIERS["ffn2_up"] + CARRIERS["ffn2_f"], g3))
    saved = dict(x0=x0, x0_16=x0_16, s1=s1, x1=x1, x1_16=x1_16, hcat=hcat, pooled=pooled, ya_lin=ya_lin, ya_pre=ya_pre, ya=ya,
                 xa=xa, hs=hs, ysc=ysc, yb_pre=yb_pre, yb=yb, outs=outs, lses=lses, attn_sv=attn_sv, yc_pre=yc_pre, yc=yc, merged=merged, mix=mix,
                 x2=x2, x2_16=x2_16, s3=s3)
    return x3, x3_16, saved, got


BWD_CARRIERS = {"ffn2_dx": ["w_in"], "ffn2_dw13": ["ffn1_w13"], "mix_in_dx": ["ffn2_w13"], "mix_in_dw": ["ffn1_w2", "ffn2_w2"],
                "ffn1_dx": ["p_pool", "p_ssd", "p_attn", "w_out"]}


def _layer_bwd(p, bias, sv, dx3, pending=None):
    send = {k: ([pending[n] for n in names] if pending is not None else []) for k, names in BWD_CARRIERS.items()}
    done = {}
    g = {}
    dx2, g["ffn2_w13"], g["ffn2_w2"], g["ln3_g"], g["ln3_b"], got = _ffn_bwd(
        sv["x2"], sv["x2_16"], p["ffn2_w13"], p["ffn2_w2"], p["ln3_g"], p["ln3_b"], sv["s3"], dx3, "ffn2",
        send["ffn2_dx"], send["ffn2_dw13"])
    done.update(zip(BWD_CARRIERS["ffn2_dx"] + BWD_CARRIERS["ffn2_dw13"], got))
    hcat = sv["hcat"]
    dres, dmix, dg2, db2 = _ln_bwd(sv["x1"], sv["mix"], dx2, p["ln2_g"], p["ln2_b"], scale=1.0, name="ln2_bwd")
    g["ln2_g"], g["ln2_b"] = dg2[0], db2[0]
    dmerged = _mm(dmix, p["w_out"], tb=True, name="mix_out_dx")
    g["w_out"] = _mm(sv["merged"].T, dmix, out_dtype=BF16, name="mix_out_dw")
    dg0, dg1, dg2_, dya, dyb, dyc, gb0, gb1, gb2 = _merge_bwd(hcat, sv["ya"], sv["yb"], sv["yc"], dmerged, p["gate_b"],
                                                               name="merge_bwd")
    g["gate_b"] = jnp.concatenate([gb0, gb1, gb2], axis=0)
    dya_pre = _mm(dya, p["p_pool"], tb=True, name="pool_out_dx")
    g["p_pool"] = _mm(sv["ya_pre"].T, dya, out_dtype=BF16, name="pool_out_dw")
    dya_lin, dpb, dps = _affine_bwd(sv["ya_lin"], dya_pre, p["pool_b"], p["pool_scale"], name="pool_affine_bwd")
    g["pool_b"], g["pool_scale"] = dpb[0].reshape(4, POOL_GDIM), dps[0]
    dpooled = _mm(dya_lin, p["pool_bd"], tb=True, name="pool_lin_dx")
    dbd = _mm(sv["pooled"].T, dya_lin, name="pool_lin_dw")
    g["pool_w"] = jnp.stack([dbd[i * POOL_GDIM:(i + 1) * POOL_GDIM, i * POOL_GDIM:(i + 1) * POOL_GDIM] for i in range(4)])
    du = _pool_bwd(dpooled, name="pool_bwd")
    dyb_pre = _mm(dyb, p["p_ssd"], tb=True, name="ssd_out_dx")
    g["p_ssd"] = _mm(sv["yb_pre"].T, dyb, out_dtype=BF16, name="ssd_out_dw")
    dysc, dz, dnw = _gnorm_bwd(sv["ysc"], hcat, dyb_pre, p["ssd_norm"], name="ssd_norm_bwd")
    g["ssd_norm"] = dnw[0]
    dtx = jnp.repeat(hcat[:, DT0:DT0 + SSD_HEADS], 64, axis=1)
    dxs, dbm, dcm, ddtx, ddtb, dalog, ddsk = _ssd_bwd(sv["xa"], dtx, p["dt_bias"], p["a_log"], p["d_skip"], sv["hs"], dysc,
                                                     name="ssd_bwd")
    g["dt_bias"], g["a_log"], g["d_skip"] = ddtb[0, ::64], dalog[0, ::64], ddsk[0, ::64]
    dxa = jnp.concatenate([dxs, dbm, dcm], axis=1)
    dxbc, dcw, dcb = _conv_bwd(hcat, p["conv_w"], p["conv_b"], dxa, name="conv_bwd")
    g["conv_w"], g["conv_b"] = dcw, dcb[0]
    ddt = jnp.pad(ddtx[:, ::64], ((0, 0), (0, HC - DT0 - SSD_HEADS))).astype(BF16)
    dyc_pre = _mm(dyc, p["p_attn"], tb=True, name="attn_out_dx")
    g["p_attn"] = _mm(sv["yc_pre"].T, dyc, out_dtype=BF16, name="attn_out_dw")
    am = _amerge_bwd(sv["outs"], sv["lses"], dyc_pre, name="attn_merge_bwd")
    dqkv, dbias = [], []
    for gi, (_, dil) in enumerate(ATTN_CONFIGS):
        qkv_sub, o, l = sv["attn_sv"][gi]
        qkv, col0 = (hcat, (QKV0 + 768 * gi) // 128) if dil == 1 else (qkv_sub, 0)
        dq, dk, dv, dbi = _attn_bwd(qkv, col0, o, l, _to_sub(am[gi], dil), _to_sub(am[3 + gi], dil), bias[gi],
                                    SEQ // dil // ATTN_BLK, name=f"attn{gi}_bwd")
        dqkv += [_from_sub(t, dil) for t in (dq, dk, dv)]
        dbias.append(dbi)
    g["attn_bias"] = jnp.stack(dbias)
    dhcat = jnp.concatenate([dz, dxbc, dg0, dg1, dg2_, *dqkv, du, ddt], axis=1)
    chips = lambda k: _Carry("chips", send[k]) if pending is not None else None
    dx1 = _mm(dhcat, p["w_in"], tb=True, add=dres, carry=chips("mix_in_dx"), name="mix_in_dx")
    g["w_in"] = _mm(sv["x1_16"].T, dhcat, out_dtype=BF16, carry=chips("mix_in_dw"), name="mix_in_dw")
    if pending is not None:
        (dx1, got_dx), (g["w_in"], got_dw) = dx1, g["w_in"]
        done.update(zip(BWD_CARRIERS["mix_in_dx"] + BWD_CARRIERS["mix_in_dw"], list(got_dx) + list(got_dw)))
    dx0, g["ffn1_w13"], g["ffn1_w2"], g["ln1_g"], g["ln1_b"], got = _ffn_bwd(
        sv["x0"], sv["x0_16"], p["ffn1_w13"], p["ffn1_w2"], p["ln1_g"], p["ln1_b"], sv["s1"], dx1, "ffn1", send["ffn1_dx"])
    done.update(zip(BWD_CARRIERS["ffn1_dx"], got))
    return dx0, g, done


def _local_step(x, tgt, lw, rel_bias, shards=None, prepare=None, chip_sums=None):
    bias, bias_vjp = jax.vjp(_attn_bias, rel_bias)
    y, y16, saved, lw = x, x.astype(BF16), [], list(lw)
    n_layers = len(lw) if shards is None else len(shards)
    for l in range(n_layers):
        nxt = shards[l + 1] if shards is not None and l + 1 < n_layers else None
        y, y16, sv, got = _layer_fwd(y, y16, lw[l], bias, nxt)
        saved.append(sv)
        if nxt is not None:
            lw.append(prepare(got, l + 1))
    dy, sq = _loss_kernel(y, tgt, name="loss")
    grads, sums, pending = [None] * n_layers, [None] * n_layers, None
    for l in reversed(range(n_layers)):
        dy, grads[l], done = _layer_bwd(lw[l], bias, saved[l], dy, pending)
        if pending is not None:
            sums[l + 1] = done
        pending = chip_sums(grads[l]) if chip_sums is not None else None
    if pending is not None:
        names = list(pending)
        sums[0] = dict(zip(names, _exchange_chips([pending[n] for n in names], name="exchange_grads_chips")))
    (d_rel,) = bias_vjp(sum(g.pop("attn_bias") for g in grads))
    return sq, dy, grads, d_rel, sums


SMALL_REPL = ["ln1_g", "ln1_b", "pool_w", "pool_b", "pool_scale", "conv_b", "dt_bias", "a_log", "d_skip", "ssd_norm",
              "ln2_g", "ln2_b", "ln3_g", "ln3_b", "rel_bias"]
SMALL_SHARD = ["gate_b", "conv_w"]
WEIGHTS = ['ffn1_w13', 'ffn1_w2', 'ln1_g', 'ln1_b', 'w_in', 'gate_b', 'pool_w', 'pool_b', 'pool_scale', 'conv_w', 'conv_b',
           'dt_bias', 'a_log', 'd_skip', 'ssd_norm', 'rel_bias', 'p_pool', 'p_ssd', 'p_attn', 'w_out', 'ln2_g', 'ln2_b',
           'ffn2_w13', 'ffn2_w2', 'ln3_g', 'ln3_b']


def _step(w, m, v, x, tgt):
    my_c = lax.axis_index("c")
    dev = 4 * lax.axis_index("x") + 2 * lax.axis_index("y") + my_c
    shards = [{n: w[n][l].astype(BF16) for n, _, _ in BIG} for l in range(DEPTH)]
    first = dict(zip([n for n, _, _ in BIG], _all_gather([shards[0][n] for n, _, _ in BIG], name="gather_weights")))
    sp_in = _SmallPack([w[n].shape for n in SMALL_SHARD])
    small_g = _all_gather([sp_in.pack([w[n] for n in SMALL_SHARD])], name="gather_small")[0]
    gate_b = jnp.stack([sp_in.unpack(small_g[d])[0] for d in range(NDEV)], axis=2).reshape(DEPTH, 3, D)
    conv_w = jnp.stack([sp_in.unpack(small_g[d])[1] for d in range(NDEV)], axis=2).reshape(DEPTH, 4, 2048)

    def prepare(blocks, l):
        p = {n: blocks[n] if n in BLOCKED else _from_blocks(blocks[n], ax) for n, _, ax in BIG}
        p["w_in"] = _w_in_to_padded(p["w_in"])
        p["pool_bd"] = _block_diag(w["pool_w"][l]).astype(BF16)
        p["gate_b"], p["conv_w"] = gate_b[l], conv_w[l]
        for n in ("ln1_g", "ln1_b", "ln2_g", "ln2_b", "ln3_g", "ln3_b", "pool_scale", "conv_b", "ssd_norm"):
            p[n] = w[n][l][None, :]
        p["pool_b"] = w["pool_b"][l].reshape(1, POOL_W)
        for n in ("dt_bias", "a_log", "d_skip"):
            p[n] = _expand_heads(w[n][l])
        return p

    c_idx = my_c.reshape(1).astype(jnp.int32)

    def chip_sums(g):
        halves = []
        for n, (r, c), ax in BIG:
            full = _w_in_from_padded(g[n]) if n == "w_in" else g[n]
            halves.append((full if n in BLOCKED else _to_blocks(full, ax)).reshape(4, 2, r, c).transpose(1, 0, 2, 3))
        from_sibling = _exchange_sibling(halves, name="exchange_grads_cores")
        return {n: _add_own_half(h.reshape(2, 4 * r, c), o.reshape(4 * r, c), c_idx, name=f"add_cores_{n}").reshape(4, r, c)
                for h, o, (n, (r, c), _) in zip(halves, from_sibling, BIG)}

    sq, dx, grads, d_rel, sums = _local_step(x.reshape(T, D), tgt.reshape(T, D), [prepare(first, 0)], w["rel_bias"], shards,
                                             prepare, chip_sums)
    loss = lax.psum(jnp.sum(sq) * (0.5 / D), ("x", "y", "c"))
    small_names = SMALL_REPL + SMALL_SHARD
    small = {n: (d_rel if n == "rel_bias" else jnp.stack([g[n] for g in grads])) for n in small_names}
    sp = _SmallPack([small[n].shape for n in small_names])
    gsmall = sp.unpack(_sum_slots(_all_gather([sp.pack([small[n] for n in small_names])], name="gather_small_grads")[0],
                                  name="sum_small_grads"))
    gout = {}
    for n, gv in zip(small_names, gsmall):
        if n in SMALL_SHARD:
            width = w[n].shape[-1]
            gv = lax.dynamic_slice_in_dim(gv, dev * width, width, axis=2)
        gout[n] = gv
    delta, new_m, new_v = {}, {}, {}
    for n, _, _ in BIG:
        gout[n], delta[n], new_m[n], new_v[n] = _adamw_reduce([s[n] for s in sums], w[n], m[n], v[n], name=f"adamw_{n}")
    spa = _SmallPack([w[n].shape for n in small_names])
    res = _adamw(spa.pack([w[n] for n in small_names]), spa.pack([gout[n] for n in small_names]),
                 spa.pack([m[n] for n in small_names]), spa.pack([v[n] for n in small_names]), name="adamw_small")
    for out, packed in zip((delta, new_m, new_v), res):
        for n, val in zip(small_names, spa.unpack(packed)):
            out[n] = val
    return (loss, dx.reshape(NB, SEQ, D), *[gout[n] for n in WEIGHTS], *[delta[n] for n in WEIGHTS],
            *[new_m[n] for n in WEIGHTS], *[new_v[n] for n in WEIGHTS])


def kernel(x, ffn1_w13, ffn1_w2, ln1_g, ln1_b, w_in, gate_b, pool_w, pool_b, pool_scale, conv_w, conv_b, dt_bias, a_log, d_skip, ssd_norm, rel_bias, p_pool, p_ssd, p_attn, w_out, ln2_g, ln2_b, ffn2_w13, ffn2_w2, ln3_g, ln3_b, loss_target, m_ffn1_w13, m_ffn1_w2, m_ln1_g, m_ln1_b, m_w_in, m_gate_b, m_pool_w, m_pool_b, m_pool_scale, m_conv_w, m_conv_b, m_dt_bias, m_a_log, m_d_skip, m_ssd_norm, m_rel_bias, m_p_pool, m_p_ssd, m_p_attn, m_w_out, m_ln2_g, m_ln2_b, m_ffn2_w13, m_ffn2_w2, m_ln3_g, m_ln3_b, v_ffn1_w13, v_ffn1_w2, v_ln1_g, v_ln1_b, v_w_in, v_gate_b, v_pool_w, v_pool_b, v_pool_scale, v_conv_w, v_conv_b, v_dt_bias, v_a_log, v_d_skip, v_ssd_norm, v_rel_bias, v_p_pool, v_p_ssd, v_p_attn, v_w_out, v_ln2_g, v_ln2_b, v_ffn2_w13, v_ffn2_w2, v_ln3_g, v_ln3_b):
    given = dict(locals())
    w = {n: given[n] for n in WEIGHTS}
    m = {n: given["m_" + n] for n in WEIGHTS}
    v = {n: given["v_" + n] for n in WEIGHTS}
    return _step(w, m, v, x, loss_target)
```

```python
import functools
import math

import numpy as np
import jax
import jax.numpy as jnp
from jax import lax
from jax.experimental import pallas as pl
from jax.experimental.pallas import tpu as pltpu

F32, BF16 = jnp.float32, jnp.bfloat16
HIGHEST = lax.Precision.HIGHEST

NDEV = 8
DEPTH = 4
D = 1024
SEQ = 2048
NB = 2
T = NB * SEQ
DFF = 2816
POOL_W = 768
POOL_WINDOWS = (2, 4, 8, 16)
POOL_GDIM = 192
SSD_HEADS = 16
CHUNK = 128
NCHUNK = SEQ // CHUNK
ATTN_CONFIGS = ((128, 1), (512, 4), (2048, 16))
ATTN_BLK = 128
REL_BUCKETS = 32
REL_MAX_DIST = 2048
LN_EPS = 1e-5
SSD_EPS = 1e-5
ALPHA = (2.0 * DEPTH) ** 0.25
FFN_RES = 0.5
NEG = -1e30

ADAM_LR, ADAM_B1, ADAM_B2, ADAM_EPS, ADAM_WD, ADAM_STEP = 0.001, 0.9, 0.999, 1e-08, 0.01, 10

Z0, XBC0, GATE0, QKV0, U0, DT0, HC = 0, 1024, 3072, 6144, 8448, 9216, 9728
O_U, O_Z, O_XBC, O_DT, O_Q, O_K, O_V, O_G, O_END = 0, 768, 1792, 3840, 3856, 4624, 5392, 6160, 9232

VMEM_LIMIT = 56 * 1024 * 1024


def _cp(sem):
    return pltpu.CompilerParams(dimension_semantics=sem, vmem_limit_bytes=VMEM_LIMIT)


def _pick(dim, cands):
    for c in cands:
        if dim % c == 0:
            return c
    return dim


def _mm(a, b, *, tb=False, out_dtype=F32, add=None, add_scale=1.0, out_blocked=False, bn=None, b_layer=None, carry=None, name):
    nc = carry.n if carry is not None else 0
    bk = None
    if a.ndim == 3:
        nkb, M, bk = a.shape
        K = nkb * bk
    else:
        M, K = a.shape
    b_blocked = b.ndim >= 3
    b_shape = b.shape if b_layer is None else (b.shape[0], *b.shape[2:])
    if b_blocked and tb:
        assert bk in (None, b_shape[2])
        nkb, N, bk = b_shape
        assert K == nkb * bk, (a.shape, b.shape)
    elif b_blocked:
        nnb, K2, bn = b_shape
        N = nnb * bn
        assert K == K2, (a.shape, b.shape)
    else:
        (N, K2) = b.shape if tb else b.shape[::-1]
        assert K == K2, (a.shape, b.shape, tb)
    bm = _pick(M, (1024, 1408, 768, 512, 256))
    if bn is None:
        bn = _pick(N, (2432, 1024, 768, 512, 256))
    if bk is None:
        bk = K if K <= 1024 else _pick(K, (1024, 1408, 2432, 512, 256))
    nk = K // bk
    dn = (((1,), (1 if tb else 0,)), ((), ()))

    grid = (M // bm, N // bn, nk)

    def kern(*refs):
        a_ref, b_ref = refs[0], refs[1]
        n_in = 2 + (add is not None)
        add_ref = refs[2] if add is not None else None
        x_refs, o_ref, g_refs = refs[n_in:n_in + nc], refs[n_in + nc], refs[n_in + nc + 1:n_in + 2 * nc + 1]
        scratch = refs[n_in + 2 * nc + 1:]
        acc_ref = scratch[0] if nk > 1 else None
        if nc:
            step = (pl.program_id(0) * grid[1] + pl.program_id(1)) * nk + pl.program_id(2)
            carried = _Carried(carry, x_refs, g_refs, scratch[-3:], step, grid[0] * grid[1] * nk)
            carried.before()
        p = lax.dot_general(a_ref[...].astype(BF16), b_ref[...].astype(BF16), dn, preferred_element_type=F32)

        def fin(acc):
            if add_ref is not None:
                acc = acc + add_scale * add_ref[...]
            o_ref[...] = acc.astype(out_dtype)

        if nk == 1:
            fin(p)
        else:
            k = pl.program_id(2)

            @pl.when(k == 0)
            def _():
                acc_ref[...] = p

            @pl.when(k > 0)
            def _():
                acc_ref[...] += p

            @pl.when(k == nk - 1)
            def _():
                fin(acc_ref[...])

        if nc:
            carried.after()

    if a.ndim == 3:
        a_spec = pl.BlockSpec((None, bm, bk), lambda i, j, k: (k, i, 0))
    else:
        a_spec = pl.BlockSpec((bm, bk), lambda i, j, k: (i, k))
    if b_blocked and b_layer is not None:
        b_spec = (pl.BlockSpec((None, None, bn, bk), lambda i, j, k: (k, b_layer, j, 0)) if tb else
                  pl.BlockSpec((None, None, bk, bn), lambda i, j, k: (j, b_layer, k, 0)))
    elif b_blocked and tb:
        b_spec = pl.BlockSpec((None, bn, bk), lambda i, j, k: (k, j, 0))
    elif b_blocked:
        b_spec = pl.BlockSpec((None, bk, bn), lambda i, j, k: (j, k, 0))
    elif tb:
        b_spec = pl.BlockSpec((bn, bk), lambda i, j, k: (j, k))
    else:
        b_spec = pl.BlockSpec((bk, bn), lambda i, j, k: (k, j))
    if out_blocked:
        o_spec, o_shape = pl.BlockSpec((None, bm, bn), lambda i, j, k: (j, i, 0)), (N // bn, M, bn)
    else:
        o_spec, o_shape = pl.BlockSpec((bm, bn), lambda i, j, k: (i, j)), (M, N)
    in_specs = [a_spec, b_spec]
    args = [a, b]
    if add is not None:
        assert add.shape == o_shape
        in_specs.append(o_spec)
        args.append(add)
    res = pl.pallas_call(
        kern, name=name, grid=grid,
        in_specs=in_specs + [_ANY] * nc, out_specs=[o_spec] + [_ANY] * nc,
        out_shape=[jax.ShapeDtypeStruct(o_shape, out_dtype)] + (carry.out_shapes() if nc else []),
        scratch_shapes=([pltpu.VMEM((bm, bn), F32)] if nk > 1 else []) + (carry.sems() if nc else []),
        compiler_params=_cp(("arbitrary",) * 3 if nc else ("parallel", "parallel", "arbitrary")),
    )(*args, *(carry.arrays if nc else []))
    return (res[0], res[1:]) if nc else res[0]


def _rowwise(fn, rows, pars, outs, accs, *, name, tile, groups=1):
    n_rows = rows[0][0].shape[0]
    nt = n_rows // tile
    n_in = len(rows) + len(pars)
    n_out = len(outs)
    transposed = [len(o) == 4 and o[3] for o in outs]

    def kern(*refs):
        res = fn(*[r[...] for r in refs[:n_in]])
        for o_ref, val, tr in zip(refs[n_in:n_in + n_out], res[:n_out], transposed):
            o_ref[...] = (val.astype(F32).T if tr else val).astype(o_ref.dtype)
        i = pl.program_id(1)
        for a_ref, val in zip(refs[n_in + n_out:], res[n_out:]):
            @pl.when(i == 0)
            def _(a_ref=a_ref, val=val):
                a_ref[...] = val

            @pl.when(i > 0)
            def _(a_ref=a_ref, val=val):
                a_ref[...] += val

    in_specs = [pl.BlockSpec((tile, w), lambda g, i, c0=c0: (i, c0 + g)) for (_, w, c0) in rows]
    for p, w in pars:
        if w is None:
            in_specs.append(pl.BlockSpec(p.shape, lambda g, i: (0, 0)))
        else:
            in_specs.append(pl.BlockSpec((p.shape[0], w), lambda g, i: (0, g)))
    out_specs = [pl.BlockSpec((o[1], tile), lambda g, i: (g, i)) if tr else pl.BlockSpec((tile, o[1]), lambda g, i: (i, g))
                 for o, tr in zip(outs, transposed)]
    out_specs += [pl.BlockSpec((1, w), lambda g, i: (0, g)) for (_, w) in accs]
    out_shape = [jax.ShapeDtypeStruct((o[0], n_rows) if tr else (n_rows, o[0]), o[2]) for o, tr in zip(outs, transposed)]
    out_shape += [jax.ShapeDtypeStruct((1, c), F32) for (c, _) in accs]
    return pl.pallas_call(
        kern, name=name, grid=(groups, nt), in_specs=in_specs, out_specs=out_specs, out_shape=out_shape,
        compiler_params=_cp(("arbitrary", "arbitrary")),
    )(*[r[0] for r in rows], *[p[0] for p in pars])


def _colsum(v):
    return jnp.sum(v, axis=0, keepdims=True)


def _silu(v):
    return v * jax.nn.sigmoid(v)


def _ln_fn(x, f, g, b, *, scale):
    pre = ALPHA * x + scale * f
    mu = jnp.mean(pre, axis=-1, keepdims=True)
    var = jnp.mean(jnp.square(pre - mu), axis=-1, keepdims=True)
    return (pre - mu) * lax.rsqrt(var + LN_EPS) * g + b


def _ln_fwd(x, f, g, b, *, scale, name):
    fn = lambda x_, f_, g_, b_: [_ln_fn(x_, f_, g_, b_, scale=scale)] * 3
    return _rowwise(fn, [(x, D, 0), (f, D, 0)], [(g, None), (b, None)], [(D, D, F32), (D, D, BF16), (D, D, BF16, True)], [],
                    name=name, tile=512)


def _ln_bwd(x, f, dy, g, b, *, scale, name):
    def fn(x_, f_, dy_, g_, b_):
        _, vjp = jax.vjp(functools.partial(_ln_fn, scale=scale), x_, f_, g_, b_)
        return list(vjp(dy_))

    return _rowwise(fn, [(x, D, 0), (f, D, 0), (dy, D, 0)], [(g, None), (b, None)],
                    [(D, D, F32), (D, D, BF16)], [(D, D), (D, D)], name=name, tile=512)


FB = 2 * DFF // NDEV
FFN_BM = 1024
_H_PAIR = pl.BlockSpec((2, None, FFN_BM, FB), lambda i, j: (0, j, i, 0))


def _ffn_up(x16, w13, *, carry=None, name):
    nc = carry.n if carry is not None else 0
    grid = (T // FFN_BM, 4)

    def kern(*refs):
        x_ref, w_ref = refs[0], refs[1]
        x_refs, (act_ref, act_t_ref, h_ref), g_refs = refs[2:2 + nc], refs[2 + nc:5 + nc], refs[5 + nc:5 + 2 * nc]
        if nc:
            step = pl.program_id(0) * grid[1] + pl.program_id(1)
            carried = _Carried(carry, x_refs, g_refs, refs[5 + 2 * nc:], step, grid[0] * grid[1])
            carried.before()
        x = x_ref[...]
        ha = jnp.dot(x, w_ref[0], preferred_element_type=F32)
        hg = jnp.dot(x, w_ref[1], preferred_element_type=F32)
        act = _silu(ha) * hg
        act_ref[...] = act.astype(BF16)
        act_t_ref[...] = act.T.astype(BF16)
        h_ref[0] = ha.astype(BF16)
        h_ref[1] = hg.astype(BF16)
        if nc:
            carried.after()

    res = pl.pallas_call(
        kern, name=name, grid=grid,
        in_specs=[pl.BlockSpec((FFN_BM, D), lambda i, j: (i, 0)),
                  pl.BlockSpec((2, None, D, FB), lambda i, j: (0, j, 0, 0))] + [_ANY] * nc,
        out_specs=[pl.BlockSpec((None, FFN_BM, FB), lambda i, j: (j, i, 0)), pl.BlockSpec((None, FB, FFN_BM), lambda i, j: (j, 0, i)),
                   _H_PAIR] + [_ANY] * nc,
        out_shape=[jax.ShapeDtypeStruct((4, T, FB), BF16), jax.ShapeDtypeStruct((4, FB, T), BF16),
                   jax.ShapeDtypeStruct((2, 4, T, FB), BF16)]
        + (carry.out_shapes() if nc else []),
        scratch_shapes=carry.sems() if nc else [],
        compiler_params=_cp(("arbitrary", "arbitrary") if nc else ("parallel", "parallel")),
    )(x16, w13.reshape(2, 4, D, FB), *(carry.arrays if nc else []))
    return res[0], res[1].reshape(DFF, T), res[2].reshape(NDEV, T, FB), res[3:]


def _ffn_down_bwd(df, w2, h16, *, name):
    def kern(df_ref, w_ref, h_ref, dh_ref):
        da = _dot(df_ref[...], w_ref[...], _NT)
        a, g = h_ref[0].astype(F32), h_ref[1].astype(F32)
        s = jax.nn.sigmoid(a)
        dh_ref[0] = (da * g * (s * (1.0 + a * (1.0 - s)))).astype(BF16)
        dh_ref[1] = (da * (a * s)).astype(BF16)

    return pl.pallas_call(
        kern, name=name, grid=(T // FFN_BM, 4),
        in_specs=[pl.BlockSpec((FFN_BM, D), lambda i, j: (i, 0)), pl.BlockSpec((FB, D), lambda i, j: (j, 0)), _H_PAIR],
        out_specs=_H_PAIR, out_shape=jax.ShapeDtypeStruct((2, 4, T, FB), BF16),
        compiler_params=_cp(("parallel", "parallel")),
    )(df, w2, h16.reshape(2, 4, T, FB)).reshape(NDEV, T, FB)


def _affine_fn(y, b, s):
    return (y + b) * s


def _affine_fwd(y, b, s, *, name):
    return _rowwise(lambda y_, b_, s_: [_affine_fn(y_, b_, s_)] * 2, [(y, POOL_W, 0)], [(b, None), (s, None)],
                    [(POOL_W, POOL_W, BF16), (POOL_W, POOL_W, BF16, True)], [], name=name, tile=512)


def _affine_bwd(y, dyo, b, s, *, name):
    def fn(y_, d_, b_, s_):
        _, vjp = jax.vjp(_affine_fn, y_, b_, s_)
        return list(vjp(d_))

    return _rowwise(fn, [(y, POOL_W, 0), (dyo, POOL_W, 0)], [(b, None), (s, None)],
                    [(POOL_W, POOL_W, BF16)], [(POOL_W, POOL_W), (POOL_W, POOL_W)], name=name, tile=512)


def _gnorm_fn(y, z, w):
    yz = y * _silu(z)
    return yz * lax.rsqrt(jnp.mean(jnp.square(yz), axis=-1, keepdims=True) + SSD_EPS) * w


def _gnorm_fwd(y, hcat, w, *, name):
    return _rowwise(lambda y_, z_, w_: [_gnorm_fn(y_, z_, w_)] * 2, [(y, 256, 0), (hcat, 256, Z0 // 256)], [(w, 256)],
                    [(D, 256, BF16), (D, 256, BF16, True)], [], name=name, tile=512, groups=4)


def _gnorm_bwd(y, hcat, dyo, w, *, name):
    def fn(y_, z_, d_, w_):
        _, vjp = jax.vjp(_gnorm_fn, y_, z_, w_)
        return list(vjp(d_))

    return _rowwise(fn, [(y, 256, 0), (hcat, 256, Z0 // 256), (dyo, 256, 0)], [(w, 256)],
                    [(D, 256, F32), (D, 256, BF16)], [(D, 256)], name=name, tile=512, groups=4)


def _merge_fn(g0, g1, g2, ya, yb, yc, b0, b1, b2):
    return jax.nn.sigmoid(g0 + b0) * ya + jax.nn.sigmoid(g1 + b1) * yb + jax.nn.sigmoid(g2 + b2) * yc


def _merge_rows(hcat, ya, yb, yc):
    c = GATE0 // 256
    return [(hcat, 256, c), (hcat, 256, c + 4), (hcat, 256, c + 8), (ya, 256, 0), (yb, 256, 0), (yc, 256, 0)]


def _merge_fwd(hcat, ya, yb, yc, gb, *, name):
    pars = [(gb[0:1], 256), (gb[1:2], 256), (gb[2:3], 256)]
    return _rowwise(lambda *v: [_merge_fn(*v)] * 2, _merge_rows(hcat, ya, yb, yc), pars, [(D, 256, BF16), (D, 256, BF16, True)],
                    [], name=name, tile=512, groups=4)


def _merge_bwd(hcat, ya, yb, yc, dm, gb, *, name):
    def fn(g0, g1, g2, ya_, yb_, yc_, dm_, b0, b1, b2):
        _, vjp = jax.vjp(_merge_fn, g0, g1, g2, ya_, yb_, yc_, b0, b1, b2)
        return list(vjp(dm_))

    pars = [(gb[0:1], 256), (gb[1:2], 256), (gb[2:3], 256)]
    return _rowwise(fn, _merge_rows(hcat, ya, yb, yc) + [(dm, 256, 0)], pars,
                    [(D, 256, BF16)] * 6, [(D, 256)] * 3, name=name, tile=512, groups=4)


def _amerge_fn(o0, o1, o2, l0, l1, l2):
    m = jnp.maximum(jnp.maximum(l0, l1), l2)
    e0, e1, e2 = jnp.exp(l0 - m), jnp.exp(l1 - m), jnp.exp(l2 - m)
    return (e0 * o0 + e1 * o1 + e2 * o2) / (e0 + e1 + e2)


def _amerge_fwd(os_, ls_, *, name):
    rows = [(v, 256, 0) for v in (*os_, *ls_)]
    return _rowwise(lambda *v: [_amerge_fn(*v)] * 2, rows, [], [(256, 256, BF16), (256, 256, BF16, True)], [], name=name, tile=1024)


def _amerge_bwd(os_, ls_, dy, *, name):
    def fn(*v):
        _, vjp = jax.vjp(_amerge_fn, *v[:6])
        return list(vjp(v[6]))

    rows = [(v, 256, 0) for v in (*os_, *ls_, dy)]
    return _rowwise(fn, rows, [], [(256, 256, F32)] * 6, [], name=name, tile=1024)


def _loss_kernel(y, tgt, *, name):
    def fn(y_, t_):
        e = y_ - t_
        return [e * (1.0 / D), _colsum(e * e)]

    return _rowwise(fn, [(y, D, 0), (tgt, D, 0)], [], [(D, D, F32)], [(D, D)], name=name, tile=512)


def _adam_math(w, g, m, v):
    m2 = ADAM_B1 * m + (1.0 - ADAM_B1) * g
    v2 = ADAM_B2 * v + (1.0 - ADAM_B2) * jnp.square(g)
    m_hat = m2 / (1.0 - ADAM_B1 ** ADAM_STEP)
    v_hat = v2 / (1.0 - ADAM_B2 ** ADAM_STEP)
    return -ADAM_LR * (m_hat / (jnp.sqrt(v_hat) + ADAM_EPS) + ADAM_WD * w), m2, v2


def _adamw(w, g, m, v, *, name):
    r, c = w.shape
    tile = _pick(r, (512, 256, 128, 64, 32, 16, 8))
    fn = lambda *a: list(_adam_math(*a))
    return _rowwise(fn, [(w, c, 0), (g, c, 0), (m, c, 0), (v, c, 0)], [], [(c, c, F32)] * 3, [], name=name, tile=tile)


def _pool_lane_window(j, width):
    lane = lax.broadcasted_iota(jnp.int32, (1, width), 1) + j * width
    grp = lane // POOL_GDIM
    return grp


def _pool_select(grp, vals):
    out = vals[3]
    for gi in (2, 1, 0):
        out = jnp.where(grp == gi, vals[gi], out)
    return out


def _pool_fwd(hcat, *, name):
    width = 256

    def kern(u_ref, o_ref, ot_ref):
        u = u_ref[...]
        t = lax.broadcasted_iota(jnp.int32, (SEQ, 1), 0)
        grp = _pool_lane_window(pl.program_id(1), width)

        def shift(v, k):
            return jnp.where(t >= k, pltpu.roll(v, k, 0), 0.0)

        s2 = u + shift(u, 1)
        s4 = s2 + shift(s2, 2)
        s8 = s4 + shift(s4, 4)
        s16 = s8 + shift(s8, 8)
        win = _pool_select(grp, [jnp.full((1, width), float(w), F32) for w in POOL_WINDOWS])
        cnt = jnp.minimum((t + 1).astype(F32), win)
        pooled = _pool_select(grp, [s2, s4, s8, s16]) / cnt - u
        o_ref[...] = pooled.astype(o_ref.dtype)
        ot_ref[...] = pooled.T.astype(ot_ref.dtype)

    return pl.pallas_call(
        kern, name=name, grid=(NB, POOL_W // width),
        in_specs=[pl.BlockSpec((SEQ, width), lambda b, j: (b, U0 // width + j))],
        out_specs=[pl.BlockSpec((SEQ, width), lambda b, j: (b, j)), pl.BlockSpec((width, SEQ), lambda b, j: (j, b))],
        out_shape=[jax.ShapeDtypeStruct((T, POOL_W), BF16), jax.ShapeDtypeStruct((POOL_W, T), BF16)],
        compiler_params=_cp(("parallel", "parallel")),
    )(hcat)


def _pool_bwd(dp, *, name):
    width = 256

    def kern(d_ref, o_ref):
        d = d_ref[...]
        t = lax.broadcasted_iota(jnp.int32, (SEQ, 1), 0)
        grp = _pool_lane_window(pl.program_id(1), width)
        win = _pool_select(grp, [jnp.full((1, width), float(w), F32) for w in POOL_WINDOWS])
        dm = d / jnp.minimum((t + 1).astype(F32), win)

        def shift(v, k):
            return jnp.where(t < SEQ - k, pltpu.roll(v, SEQ - k, 0), 0.0)

        r2 = dm + shift(dm, 1)
        r4 = r2 + shift(r2, 2)
        r8 = r4 + shift(r4, 4)
        r16 = r8 + shift(r8, 8)
        o_ref[...] = (_pool_select(grp, [r2, r4, r8, r16]) - d).astype(o_ref.dtype)

    return pl.pallas_call(
        kern, name=name, grid=(NB, POOL_W // width),
        in_specs=[pl.BlockSpec((SEQ, width), lambda b, j: (b, j))],
        out_specs=pl.BlockSpec((SEQ, width), lambda b, j: (b, j)),
        out_shape=jax.ShapeDtypeStruct((T, POOL_W), BF16), compiler_params=_cp(("parallel", "parallel")),
    )(dp)


CONV_W = 512


def _conv_pre(x, w, b, t):
    pre = w[3:4] * x + b
    for k in (1, 2, 3):
        pre = pre + w[3 - k:4 - k] * jnp.where(t >= k, pltpu.roll(x, k, 0), 0.0)
    return pre


def _conv_fwd(hcat, w, b, *, name):
    def kern(x_ref, w_ref, b_ref, o_ref):
        t = lax.broadcasted_iota(jnp.int32, (SEQ, 1), 0)
        o_ref[...] = _silu(_conv_pre(x_ref[...], w_ref[...], b_ref[...], t))

    return pl.pallas_call(
        kern, name=name, grid=(NB, 2048 // CONV_W),
        in_specs=[pl.BlockSpec((SEQ, CONV_W), lambda s, j: (s, XBC0 // CONV_W + j)),
                  pl.BlockSpec((4, CONV_W), lambda s, j: (0, j)), pl.BlockSpec((1, CONV_W), lambda s, j: (0, j))],
        out_specs=pl.BlockSpec((SEQ, CONV_W), lambda s, j: (s, j)),
        out_shape=jax.ShapeDtypeStruct((T, 2048), F32), compiler_params=_cp(("parallel", "parallel")),
    )(hcat, w, b)


def _conv_bwd(hcat, w, b, dy, *, name):
    def kern(x_ref, w_ref, b_ref, dy_ref, dx_ref, dw_ref, db_ref):
        t = lax.broadcasted_iota(jnp.int32, (SEQ, 1), 0)
        x, w_ = x_ref[...], w_ref[...]
        pre = _conv_pre(x, w_, b_ref[...], t)
        s = jax.nn.sigmoid(pre)
        dpre = dy_ref[...] * (s * (1.0 + pre * (1.0 - s)))
        dx = w_[3:4] * dpre
        dws = [None] * 4
        dws[3] = _colsum(dpre * x)
        for k in (1, 2, 3):
            dx = dx + w_[3 - k:4 - k] * jnp.where(t < SEQ - k, pltpu.roll(dpre, SEQ - k, 0), 0.0)
            dws[3 - k] = _colsum(dpre * jnp.where(t >= k, pltpu.roll(x, k, 0), 0.0))
        dx_ref[...] = dx.astype(dx_ref.dtype)
        db = _colsum(dpre)
        first = pl.program_id(1) == 0

        @pl.when(first)
        def _():
            for k in range(4):
                dw_ref[k:k + 1, :] = dws[k]
            db_ref[...] = db

        @pl.when(jnp.logical_not(first))
        def _():
            for k in range(4):
                dw_ref[k:k + 1, :] += dws[k]
            db_ref[...] += db

    return pl.pallas_call(
        kern, name=name, grid=(2048 // CONV_W, NB),
        in_specs=[pl.BlockSpec((SEQ, CONV_W), lambda j, s: (s, XBC0 // CONV_W + j)),
                  pl.BlockSpec((4, CONV_W), lambda j, s: (0, j)), pl.BlockSpec((1, CONV_W), lambda j, s: (0, j)),
                  pl.BlockSpec((SEQ, CONV_W), lambda j, s: (s, j))],
        out_specs=[pl.BlockSpec((SEQ, CONV_W), lambda j, s: (s, j)), pl.BlockSpec((4, CONV_W), lambda j, s: (0, j)),
                   pl.BlockSpec((1, CONV_W), lambda j, s: (0, j))],
        out_shape=[jax.ShapeDtypeStruct((T, 2048), BF16), jax.ShapeDtypeStruct((4, 2048), F32),
                   jax.ShapeDtypeStruct((1, 2048), F32)],
        compiler_params=_cp(("arbitrary", "arbitrary")),
    )(hcat, w, b, dy)


def _softplus(v):
    return jnp.maximum(v, 0.0) + jnp.log1p(jnp.exp(-jnp.abs(v)))


def _dot(a, b, dims):
    return lax.dot_general(a, b, (dims, ((), ())), preferred_element_type=F32)


_NN, _NT, _TN = ((1,), (0,)), ((1,), (1,)), ((0,), (0,))


def _ssd_specs(order):
    def spec(width, col0):
        return pl.BlockSpec((SEQ, width), lambda i, j, c0=col0: (order(i, j)[0], c0 + order(i, j)[1]))

    def par():
        return pl.BlockSpec((1, 256), lambda i, j: (0, order(i, j)[1]))

    return spec, par


def _ssd_chunk_common(c, xs_ref, b_ref, c_ref, dtx_ref, dtb, a, trif):
    r0 = pl.multiple_of(c * CHUNK, CHUNK)
    rows = pl.ds(r0, CHUNK)
    x = xs_ref[rows, :]
    bb = b_ref[rows, :].astype(BF16)
    cb = c_ref[rows, :].astype(BF16)
    raw = dtx_ref[rows, :] + dtb
    dt = _softplus(raw)
    cs = jnp.dot(trif, dt * a, precision=HIGHEST, preferred_element_type=F32)
    return rows, x, bb, cb, raw, dt, cs


def _head_decay(cs, me, tri):
    cse = jnp.max(jnp.where(me, cs, -jnp.inf), axis=1, keepdims=True)
    csb = jnp.broadcast_to(cse, (CHUNK, CHUNK))
    return jnp.where(tri, jnp.exp(csb - csb.T), 0.0)


def _ssd_fwd(xa, dtx, dtb, alog, dsk, *, name):
    spec, par = _ssd_specs(lambda g, b: (b, g))

    def kern(xs_ref, b_ref, c_ref, dtx_ref, dtb_ref, alog_ref, dsk_ref, y_ref, hs_ref, h_scr):
        row = lax.broadcasted_iota(jnp.int32, (CHUNK, CHUNK), 0)
        col = lax.broadcasted_iota(jnp.int32, (CHUNK, CHUNK), 1)
        tri = row >= col
        trif = tri.astype(F32)
        head = lax.broadcasted_iota(jnp.int32, (1, 256), 1) // 64
        dtb_, a, dsk_ = dtb_ref[...], -jnp.exp(alog_ref[...]), dsk_ref[...]
        h_scr[...] = jnp.zeros_like(h_scr)

        def chunk(c, carry):
            rows, x, bb, cb, _, dt, cs = _ssd_chunk_common(c, xs_ref, b_ref, c_ref, dtx_ref, dtb_, a, trif)
            cs_last = cs[CHUNK - 1:CHUNK, :]
            xdt = x * dt
            xdtb = xdt.astype(BF16)
            g = _dot(cb, bb, _NT)
            hin = h_scr[...]
            hs_ref[rows, :] = hin
            y = jnp.exp(cs) * _dot(cb, hin.astype(BF16), _NN) + dsk_ * x
            for e in range(4):
                me = head == e
                m = (g * _head_decay(cs, me, tri)).astype(BF16)
                y = y + jnp.where(me, _dot(m, xdtb, _NN), 0.0)
            y_ref[rows, :] = y
            st = _dot(bb, (jnp.exp(cs_last - cs) * xdt).astype(BF16), _TN)
            h_scr[...] = hin * jnp.exp(cs_last) + st
            return carry

        lax.fori_loop(0, NCHUNK, chunk, 0)

    return pl.pallas_call(
        kern, name=name, grid=(4, NB),
        in_specs=[spec(256, 0), spec(128, 8), spec(128, 12), spec(256, 0), par(), par(), par()],
        out_specs=[spec(256, 0), spec(256, 0)],
        out_shape=[jax.ShapeDtypeStruct((T, D), F32), jax.ShapeDtypeStruct((T, D), F32)],
        scratch_shapes=[pltpu.VMEM((CHUNK, 256), F32)],
        compiler_params=_cp(("parallel", "parallel")),
    )(xa, xa, xa, dtx, dtb, alog, dsk)


def _ssd_bwd(xa, dtx, dtb, alog, dsk, hs, dy, *, name):
    spec, par = _ssd_specs(lambda g, b: (b, g))

    def kern(xs_ref, b_ref, c_ref, dtx_ref, dtb_ref, alog_ref, dsk_ref, hs_ref, dy_ref,
             dx_ref, db_ref, dc_ref, ddt_ref, ddtb_ref, dalog_ref, ddsk_ref, dh_scr):
        row = lax.broadcasted_iota(jnp.int32, (CHUNK, CHUNK), 0)
        col = lax.broadcasted_iota(jnp.int32, (CHUNK, CHUNK), 1)
        tri = row >= col
        trif = tri.astype(F32)
        trit = (row <= col).astype(F32)
        is_last = lax.broadcasted_iota(jnp.int32, (CHUNK, 1), 0) == CHUNK - 1
        head = lax.broadcasted_iota(jnp.int32, (1, 256), 1) // 64
        dtb_, a, dsk_ = dtb_ref[...], -jnp.exp(alog_ref[...]), dsk_ref[...]
        dh_scr[...] = jnp.zeros_like(dh_scr)

        @pl.when(pl.program_id(1) == 0)
        def _():
            ddtb_ref[...] = jnp.zeros_like(ddtb_ref)
            dalog_ref[...] = jnp.zeros_like(dalog_ref)
            ddsk_ref[...] = jnp.zeros_like(ddsk_ref)

        def hsum(v, me):
            return jnp.sum(jnp.where(me, v, 0.0), axis=1, keepdims=True)

        def chunk(ci, carry):
            c = NCHUNK - 1 - ci
            rows, x, bb, cb, raw, dt, cs = _ssd_chunk_common(c, xs_ref, b_ref, c_ref, dtx_ref, dtb_, a, trif)
            cs_last = cs[CHUNK - 1:CHUNK, :]
            ecs = jnp.exp(cs)
            dsx = jnp.exp(cs_last - cs)
            xdt = x * dt
            xdtb = xdt.astype(BF16)
            g = _dot(cb, bb, _NT)
            hin = hs_ref[rows, :]
            hinb = hin.astype(BF16)
            dy_ = dy_ref[rows, :]
            dyb = dy_.astype(BF16)
            dh = dh_scr[...]
            dhb = dh.astype(BF16)
            bdh = _dot(bb, dhb, _NN)
            yoff = ecs * _dot(cb, hinb, _NN)
            dxdt = dsx * bdh
            t1, t2, t3 = dy_ * yoff, xdt * bdh, dh * hin
            dg = jnp.zeros((CHUNK, CHUNK), F32)
            dcs = jnp.zeros((CHUNK, 256), F32)
            for e in range(4):
                me = head == e
                l_ = _head_decay(cs, me, tri)
                m = g * l_
                dxdt = dxdt + jnp.where(me, _dot(m.astype(BF16), dyb, _TN), 0.0)
                dm = _dot(jnp.where(me, dy_, 0.0).astype(BF16), xdtb, _NT)
                dg = dg + dm * l_
                w = dm * m
                dds = hsum(t2, me)
                dse = jnp.max(jnp.where(me, dsx, -jnp.inf), axis=1, keepdims=True)
                ecl = jnp.max(jnp.where(me, jnp.exp(cs_last), -jnp.inf), axis=1, keepdims=True)
                dcs_e = hsum(t1, me) + jnp.sum(w, axis=1, keepdims=True) - jnp.sum(w.T, axis=1, keepdims=True) - dds * dse
                last = jnp.sum(dds * dse, axis=0, keepdims=True) + ecl * jnp.sum(hsum(t3, me), axis=0, keepdims=True)
                dcs_e = dcs_e + jnp.where(is_last, last, 0.0)
                dcs = dcs + jnp.where(me, dcs_e, 0.0)
            dadt = jnp.dot(trit, dcs, precision=HIGHEST, preferred_element_type=F32)
            ddt = a * dadt
            dxx = dxdt * x
            dyx = dy_ * x
            dsk_acc = jnp.zeros((1, 256), F32)
            for e in range(4):
                me = head == e
                ddt = ddt + jnp.where(me, hsum(dxx, me), 0.0)
                dsk_acc = dsk_acc + jnp.where(me, jnp.sum(hsum(dyx, me), axis=0, keepdims=True), 0.0)
            draw = ddt * jax.nn.sigmoid(raw)
            ddt_ref[rows, :] = draw
            ddtb_ref[...] += _colsum(draw)
            dalog_ref[...] += _colsum(dadt * dt) * a
            ddsk_ref[...] += dsk_acc
            dx_ref[rows, :] = dxdt * dt + dsk_ * dy_
            edy = (ecs * dy_).astype(BF16)
            dgb = dg.astype(BF16)
            dc_ref[rows, :] = _dot(dgb, bb, _NN) + _dot(edy, hinb, _NT)
            db_ref[rows, :] = _dot(dgb, cb, _TN) + _dot((dsx * xdt).astype(BF16), dhb, _NT)
            dh_scr[...] = jnp.exp(cs_last) * dh + _dot(cb, edy, _TN)
            return carry

        lax.fori_loop(0, NCHUNK, chunk, 0)

    return pl.pallas_call(
        kern, name=name, grid=(4, NB),
        in_specs=[spec(256, 0), spec(128, 8), spec(128, 12), spec(256, 0), par(), par(), par(), spec(256, 0), spec(256, 0)],
        out_specs=[spec(256, 0), spec(128, 0), spec(128, 0), spec(256, 0), par(), par(), par()],
        out_shape=[jax.ShapeDtypeStruct((T, D), F32), jax.ShapeDtypeStruct((T, 512), F32), jax.ShapeDtypeStruct((T, 512), F32),
                   jax.ShapeDtypeStruct((T, D), F32), jax.ShapeDtypeStruct((1, D), F32), jax.ShapeDtypeStruct((1, D), F32),
                   jax.ShapeDtypeStruct((1, D), F32)],
        scratch_shapes=[pltpu.VMEM((CHUNK, 256), F32)],
        compiler_params=_cp(("arbitrary", "arbitrary")),
    )(xa, xa, xa, dtx, dtb, alog, dsk, hs, dy)


def _t5_bucket_np(dist):
    dist = np.maximum(dist, 0)
    max_exact = REL_BUCKETS // 2
    large = max_exact + (np.log(np.maximum(dist, 1) / max_exact) / np.log(REL_MAX_DIST / max_exact)
                         * (REL_BUCKETS - max_exact)).astype(np.int32)
    large = np.minimum(large, REL_BUCKETS - 1)
    return np.where(dist < max_exact, dist, large).astype(np.int32)


def _attn_bias(rel_bias):
    qi = np.arange(ATTN_BLK)[:, None]
    kj = np.arange(2 * ATTN_BLK)[None, :]
    delta = qi - kj + ATTN_BLK
    out = []
    for gi, (window, dil) in enumerate(ATTN_CONFIGS):
        in_band = (delta >= 0) & (delta <= window // dil)
        bucket = jnp.asarray(_t5_bucket_np(delta * dil).reshape(-1, 1))
        one_hot = (bucket == jnp.arange(REL_BUCKETS)[None, :]).astype(F32)
        tab = jnp.dot(one_hot, rel_bias[:, 4 * gi:4 * gi + 4], precision=HIGHEST).reshape(ATTN_BLK, 2 * ATTN_BLK, 4)
        out.append(jnp.where(jnp.asarray(in_band)[None], tab.transpose(2, 0, 1), NEG))
    return jnp.stack(out)


def _attn_band(ref, cur, prev):
    return jnp.concatenate([ref[prev, :], ref[cur, :]], axis=0).astype(BF16)


def _attn_head_masks():
    hi = lax.broadcasted_iota(jnp.int32, (1, 128), 1) >= 64
    return [jnp.logical_not(hi), hi]


def _attn_logits(qm, kband, bias, n, sub_blocks):
    kj = lax.broadcasted_iota(jnp.int32, (1, 2 * ATTN_BLK), 1)
    ok = jnp.logical_or(n % sub_blocks > 0, kj >= ATTN_BLK)
    return jnp.where(ok, _dot(qm, kband, _NT) * 0.125 + bias, NEG)


def _attn_specs(col0):
    qkv = [pl.BlockSpec((SEQ, 128), lambda p, s, c=col0 + 2 * i: (s, c + p)) for i in range(3)]
    seq = pl.BlockSpec((SEQ, 128), lambda p, s: (s, p))
    tab = pl.BlockSpec((2, ATTN_BLK, 2 * ATTN_BLK), lambda p, s: (p, 0, 0))
    return qkv, seq, tab


def _attn_rows(n):
    cur = pl.ds(pl.multiple_of(n * ATTN_BLK, ATTN_BLK), ATTN_BLK)
    prev = pl.ds(pl.multiple_of(jnp.maximum(n - 1, 0) * ATTN_BLK, ATTN_BLK), ATTN_BLK)
    return cur, prev


def _attn_fwd(qkv, col0, bias, sub_blocks, *, name):
    def kern(q_ref, k_ref, v_ref, bias_ref, o_ref, l_ref):
        masks = _attn_head_masks()

        def blk(n, carry):
            cur, prev = _attn_rows(n)
            q = q_ref[cur, :]
            kband, vband = _attn_band(k_ref, cur, prev), _attn_band(v_ref, cur, prev)
            out = jnp.zeros((ATTN_BLK, 128), F32)
            lse = jnp.zeros((ATTN_BLK, 128), F32)
            for hh, mk in enumerate(masks):
                logits = _attn_logits(jnp.where(mk, q, 0.0).astype(BF16), kband, bias_ref[hh], n, sub_blocks)
                m = jnp.max(logits, axis=-1, keepdims=True)
                p = jnp.exp(logits - m)
                den = jnp.sum(p, axis=-1, keepdims=True)
                out = jnp.where(mk, _dot((p / den).astype(BF16), vband, _NN), out)
                lse = jnp.where(mk, m + jnp.log(den), lse)
            o_ref[cur, :] = out
            l_ref[cur, :] = lse
            return carry

        lax.fori_loop(0, SEQ // ATTN_BLK, blk, 0)

    qkv_specs, seq, tab = _attn_specs(col0)
    return pl.pallas_call(
        kern, name=name, grid=(2, NB), in_specs=qkv_specs + [tab],
        out_specs=[seq, seq], out_shape=[jax.ShapeDtypeStruct((T, 256), F32)] * 2,
        compiler_params=_cp(("parallel", "parallel")),
    )(qkv, qkv, qkv, bias)


def _attn_bwd(qkv, col0, o, lse, do, dl, bias, sub_blocks, *, name):
    def kern(q_ref, k_ref, v_ref, o_ref, l_ref, do_ref, dl_ref, bias_ref, dq_ref, dk_ref, dv_ref, dbias_ref, dk_acc, dv_acc):
        masks = _attn_head_masks()

        @pl.when(pl.program_id(1) == 0)
        def _():
            dbias_ref[...] = jnp.zeros_like(dbias_ref)

        dk_acc[...] = jnp.zeros_like(dk_acc)
        dv_acc[...] = jnp.zeros_like(dv_acc)

        def blk(n, carry):
            cur, prev = _attn_rows(n)
            q, o_, lse_, do_, dl_ = q_ref[cur, :], o_ref[cur, :], l_ref[cur, :], do_ref[cur, :], dl_ref[cur, :]
            kband, vband = _attn_band(k_ref, cur, prev), _attn_band(v_ref, cur, prev)
            dq = jnp.zeros((ATTN_BLK, 128), F32)
            dkb = jnp.zeros((2 * ATTN_BLK, 128), F32)
            dvb = jnp.zeros((2 * ATTN_BLK, 128), F32)
            for hh, mk in enumerate(masks):
                qm = jnp.where(mk, q, 0.0).astype(BF16)
                dom = jnp.where(mk, do_, 0.0)
                domb = dom.astype(BF16)
                logits = _attn_logits(qm, kband, bias_ref[hh], n, sub_blocks)
                p = jnp.exp(logits - jnp.max(jnp.where(mk, lse_, -jnp.inf), axis=-1, keepdims=True))
                dd = jnp.sum(dom * o_, axis=-1, keepdims=True)
                dls = jnp.sum(jnp.where(mk, dl_, 0.0), axis=-1, keepdims=True)
                ds = p * (_dot(domb, vband, _NT) - dd + dls)
                dbias_ref[hh] += ds
                dsb = (ds * 0.125).astype(BF16)
                dq = jnp.where(mk, _dot(dsb, kband, _NN), dq)
                dkb = dkb + _dot(dsb, qm, _TN)
                dvb = dvb + _dot(p.astype(BF16), domb, _TN)
            dq_ref[cur, :] = dq.astype(dq_ref.dtype)
            dk_acc[prev, :] += dkb[:ATTN_BLK]
            dk_acc[cur, :] += dkb[ATTN_BLK:]
            dv_acc[prev, :] += dvb[:ATTN_BLK]
            dv_acc[cur, :] += dvb[ATTN_BLK:]
            return carry

        lax.fori_loop(0, SEQ // ATTN_BLK, blk, 0)
        dk_ref[...] = dk_acc[...].astype(dk_ref.dtype)
        dv_ref[...] = dv_acc[...].astype(dv_ref.dtype)

    qkv_specs, seq, tab = _attn_specs(col0)
    return pl.pallas_call(
        kern, name=name, grid=(2, NB),
        in_specs=qkv_specs + [seq] * 4 + [tab], out_specs=[seq, seq, seq, tab],
        out_shape=[jax.ShapeDtypeStruct((T, 256), BF16)] * 3 + [jax.ShapeDtypeStruct((4, ATTN_BLK, 2 * ATTN_BLK), F32)],
        scratch_shapes=[pltpu.VMEM((SEQ, 128), F32), pltpu.VMEM((SEQ, 128), F32)],
        compiler_params=_cp(("arbitrary", "arbitrary")),
    )(qkv, qkv, qkv, o, lse, do, dl, bias)


def _to_sub(t, dil):
    if dil == 1:
        return t
    return t.reshape(NB, SEQ // dil, dil, t.shape[-1]).transpose(0, 2, 1, 3).reshape(T, t.shape[-1])


def _from_sub(t, dil):
    if dil == 1:
        return t
    return t.reshape(NB, dil, SEQ // dil, t.shape[-1]).transpose(0, 2, 1, 3).reshape(T, t.shape[-1])


_ANY = pl.BlockSpec(memory_space=pl.ANY)
MESH = pl.DeviceIdType.MESH


def _slot(dev):
    return 4 * dev[0] + 2 * dev[1] + dev[2]


class _GatherSteps:
    def __init__(self, x_refs, out_refs, sems):
        self.x_refs, self.out_refs = x_refs, out_refs
        self.send_sems, self.recv_sems, self.local_sems = sems
        self.n = len(x_refs)
        x, y, c = lax.axis_index("x"), lax.axis_index("y"), lax.axis_index("c")
        self.c, self.me, self.sibling = c, (x, y, c), (x, y, 1 - c)
        self.chips = [(1 - x, y), (x, 1 - y), (1 - x, 1 - y)]

    def copies(self, k, block, to, own=False):
        return [pltpu.make_async_remote_copy(
            src_ref=self.x_refs[a] if own else self.out_refs[a].at[_slot(block)], dst_ref=self.out_refs[a].at[_slot(block)],
            send_sem=self.send_sems.at[k, a], recv_sem=self.recv_sems.at[k, a], device_id=to, device_id_type=MESH)
            for a in range(self.n)]

    def mine(self):
        return [pltpu.make_async_copy(self.x_refs[a], self.out_refs[a].at[_slot(self.me)], self.local_sems.at[a])
                for a in range(self.n)]

    def first(self):
        out = self.copies(0, self.me, self.sibling, own=True)
        for j, chip in enumerate(self.chips):
            out += self.copies(1 + j, self.me, (*chip, self.c), own=True)
        return out

    def passed(self, j):
        return self.copies(4 + j, (*self.chips[j], self.c), self.sibling)

    def start(self):
        for cp in self.mine() + self.first():
            cp.start()

    def pass_on(self):
        for j, chip in enumerate(self.chips):
            for cp in self.copies(1 + j, (*chip, self.c), self.me):
                cp.wait_recv()
            for cp in self.passed(j):
                cp.start()

    def finish(self):
        for cp in self.copies(0, self.sibling, self.me):
            cp.wait_recv()
        for j, chip in enumerate(self.chips):
            for cp in self.copies(4 + j, (*chip, 1 - self.c), self.me):
                cp.wait_recv()
        for cp in self.first() + [cp for j in range(3) for cp in self.passed(j)]:
            cp.wait_send()
        for cp in self.mine():
            cp.wait()


class _ChipSteps:
    def __init__(self, x_refs, out_refs, sems):
        self.x_refs, self.out_refs = x_refs, out_refs
        self.send_sems, self.recv_sems, self.local_sems = sems
        self.n = len(x_refs)
        x, y, self.c = lax.axis_index("x"), lax.axis_index("y"), lax.axis_index("c")
        self.my_chip = 2 * x + y
        self.chips = [(1 - x, y), (x, 1 - y), (1 - x, 1 - y)]

    def mine(self):
        return [pltpu.make_async_copy(self.x_refs[a].at[self.my_chip], self.out_refs[a].at[self.my_chip], self.local_sems.at[a])
                for a in range(self.n)]

    def copies(self, sending):
        return [pltpu.make_async_remote_copy(
            src_ref=self.x_refs[a].at[2 * px + py if sending else self.my_chip],
            dst_ref=self.out_refs[a].at[self.my_chip if sending else 2 * px + py],
            send_sem=self.send_sems.at[k, a], recv_sem=self.recv_sems.at[k, a], device_id=(px, py, self.c), device_id_type=MESH)
            for k, (px, py) in enumerate(self.chips) for a in range(self.n)]

    def start(self):
        for cp in self.mine() + self.copies(True):
            cp.start()

    def pass_on(self):
        pass

    def finish(self):
        for cp in self.copies(False):
            cp.wait_recv()
        for cp in self.copies(True):
            cp.wait_send()
        for cp in self.mine():
            cp.wait()


class _Carry:
    def __init__(self, kind, arrays):
        self.kind, self.arrays, self.n = kind, list(arrays), len(arrays)
        self.k = {"gather": 7, "chips": 3}[kind]

    def out_shapes(self):
        lead = (NDEV,) if self.kind == "gather" else ()
        return [jax.ShapeDtypeStruct((*lead, *a.shape), a.dtype) for a in self.arrays]

    def sems(self):
        return [pltpu.SemaphoreType.DMA((self.k, self.n)), pltpu.SemaphoreType.DMA((self.k, self.n)),
                pltpu.SemaphoreType.DMA((self.n,))]

    def steps(self, x_refs, out_refs, sems):
        return (_GatherSteps if self.kind == "gather" else _ChipSteps)(x_refs, out_refs, sems)

    def standalone(self, name):
        def kern(*refs):
            steps = self.steps(refs[:self.n], refs[self.n:2 * self.n], refs[2 * self.n:])
            steps.start()
            steps.pass_on()
            steps.finish()

        return pl.pallas_call(kern, name=name, in_specs=[_ANY] * self.n, out_specs=[_ANY] * self.n, out_shape=self.out_shapes(),
                              scratch_shapes=self.sems())(*self.arrays)


class _Carried:
    def __init__(self, carry, x_refs, out_refs, sems, step, n_steps):
        self.steps, self.step, self.n_steps = carry.steps(x_refs, out_refs, sems), step, n_steps

    def before(self):
        pl.when(self.step == 0)(self.steps.start)
        pl.when(self.step == self.n_steps // 2)(self.steps.pass_on)

    def after(self):
        pl.when(self.step == self.n_steps - 1)(self.steps.finish)


def _all_gather(shards, *, name):
    return _Carry("gather", shards).standalone(name)


def _exchange_sibling(arrs, *, name):
    n = len(arrs)

    def kern(*refs):
        x_refs, out_refs = refs[:n], refs[n:2 * n]
        send_sems, recv_sems = refs[2 * n:]
        x, y, c = lax.axis_index("x"), lax.axis_index("y"), lax.axis_index("c")
        sends = [pltpu.make_async_remote_copy(
            src_ref=x_refs[a].at[p, 1 - c], dst_ref=out_refs[a].at[p], send_sem=send_sems.at[p, a], recv_sem=recv_sems.at[p, a],
            device_id=(x, y, 1 - c), device_id_type=MESH) for p in range(4) for a in range(n)]
        for cp in sends:
            cp.start()
        for cp in sends:
            cp.wait_recv()
        for cp in sends:
            cp.wait_send()

    return pl.pallas_call(
        kern, name=name, in_specs=[_ANY] * n, out_specs=[_ANY] * n,
        out_shape=[jax.ShapeDtypeStruct((4, *a.shape[2:]), a.dtype) for a in arrs],
        scratch_shapes=[pltpu.SemaphoreType.DMA((4, n)), pltpu.SemaphoreType.DMA((4, n))],
    )(*arrs)


def _exchange_chips(arrs, *, name):
    return _Carry("chips", arrs).standalone(name)


def _add_own_half(full, other, c_idx, *, name):
    _, _, r, c = full.shape
    tile = _pick(r, (512, 256, 128, 64, 32, 16))

    def kern(c_ref, f_ref, o_ref, out_ref):
        out_ref[...] = (f_ref[...].astype(F32) + o_ref[...].astype(F32)).astype(BF16)

    blk = pl.BlockSpec((None, tile, c), lambda p, i, c_ref: (p, i, 0))
    return pl.pallas_call(
        kern, name=name,
        grid_spec=pltpu.PrefetchScalarGridSpec(
            num_scalar_prefetch=1, grid=(4, r // tile),
            in_specs=[pl.BlockSpec((None, None, tile, c), lambda p, i, c_ref: (p, c_ref[0], i, 0)), blk], out_specs=blk),
        out_shape=jax.ShapeDtypeStruct((4, r, c), BF16), compiler_params=_cp(("parallel", "parallel")),
    )(c_idx, full, other)


def _adamw_reduce(parts, w, m, v, *, name):
    ns, r, c = parts[0].shape
    tile = _pick(r, (256, 128, 64, 32, 16, 8))

    def kern(*refs):
        p_refs, (w_ref, m_ref, v_ref, g_ref, d_ref, m2_ref, v2_ref) = refs[:DEPTH], refs[DEPTH:]
        for l in range(DEPTH):
            @pl.when(pl.program_id(0) == l)
            def _(p_ref=p_refs[l]):
                g = p_ref[0].astype(F32)
                for s in range(1, ns):
                    g = g + p_ref[s].astype(F32)
                g_ref[...] = g
                d_ref[...], m2_ref[...], v2_ref[...] = _adam_math(w_ref[...], g, m_ref[...], v_ref[...])

    p_specs = [pl.BlockSpec((ns, tile, c), lambda l_, i, l=l: (0, jnp.where(l_ == l, i, 0), 0)) for l in range(DEPTH)]
    blk = pl.BlockSpec((None, tile, c), lambda l_, i: (l_, i, 0))
    return pl.pallas_call(
        kern, name=name, grid=(DEPTH, r // tile), in_specs=p_specs + [blk, blk, blk],
        out_specs=[blk] * 4, out_shape=[jax.ShapeDtypeStruct((DEPTH, r, c), F32)] * 4,
        compiler_params=_cp(("arbitrary", "arbitrary")),
    )(*parts, w, m, v)


def _sum_slots(parts, *, name):
    ns, r, c = parts.shape
    tile = _pick(r, (256, 128, 64, 32, 16, 8))

    def kern(p_ref, o_ref):
        acc = p_ref[0].astype(F32)
        for d in range(1, ns):
            acc = acc + p_ref[d].astype(F32)
        o_ref[...] = acc

    return pl.pallas_call(
        kern, name=name, grid=(r // tile,), in_specs=[pl.BlockSpec((ns, tile, c), lambda i: (0, i, 0))],
        out_specs=pl.BlockSpec((tile, c), lambda i: (i, 0)), out_shape=jax.ShapeDtypeStruct((r, c), F32),
        compiler_params=_cp(("parallel",)),
    )(parts)


BIG = [
    ("ffn1_w13", (1024, 704), 1), ("ffn1_w2", (352, 1024), 0), ("w_in", (1024, 1154), 1), ("p_pool", (768, 128), 1),
    ("p_ssd", (128, 1024), 0), ("p_attn", (256, 128), 1), ("w_out", (128, 1024), 0), ("ffn2_w13", (1024, 704), 1),
    ("ffn2_w2", (352, 1024), 0),
]
BLOCKED = ("ffn1_w13", "ffn2_w13")


def _from_blocks(blk, ax):
    _, r, c = blk.shape
    return blk.transpose(1, 0, 2).reshape(r, NDEV * c) if ax == 1 else blk.reshape(NDEV * r, c)


def _to_blocks(full, ax):
    r, c = full.shape
    return full.reshape(r, NDEV, c // NDEV).transpose(1, 0, 2) if ax == 1 else full.reshape(NDEV, r // NDEV, c)


class _SmallPack:
    def __init__(self, shapes):
        self.shapes = shapes
        self.rows = [-(-int(np.prod(s)) // 128) for s in shapes]
        self.total = -(-sum(self.rows) // 256) * 256

    def pack(self, arrs):
        parts = []
        for a, s, r in zip(arrs, self.shapes, self.rows):
            assert tuple(a.shape) == tuple(s), (a.shape, s)
            flat = a.astype(F32).reshape(-1)
            parts.append(jnp.pad(flat, (0, r * 128 - flat.shape[0])).reshape(r, 128))
        parts.append(jnp.zeros((self.total - sum(self.rows), 128), F32))
        return jnp.concatenate(parts, axis=0)

    def unpack(self, packed):
        out, off = [], 0
        for s, r in zip(self.shapes, self.rows):
            out.append(packed[off:off + r].reshape(-1)[:int(np.prod(s))].reshape(s))
            off += r
        return out


def _w_in_to_padded(w):
    z = jnp.zeros((*w.shape[:-1], HC - DT0 - 16), w.dtype)
    qkv = [w[..., o + 256 * gi:o + 256 * gi + 256] for gi in range(3) for o in (O_Q, O_K, O_V)]
    return jnp.concatenate([w[..., O_Z:O_XBC], w[..., O_XBC:O_DT], w[..., O_G:O_END], *qkv, w[..., O_U:O_Z], w[..., O_DT:O_Q], z],
                           axis=-1)


def _w_in_from_padded(g):
    qkv = [g[..., QKV0 + 768 * gi + o:QKV0 + 768 * gi + o + 256] for o in (0, 256, 512) for gi in range(3)]
    return jnp.concatenate([g[..., U0:DT0], g[..., Z0:XBC0], g[..., XBC0:GATE0], g[..., DT0:DT0 + 16], *qkv, g[..., GATE0:QKV0]],
                           axis=-1)


def _block_diag(pw):
    out = jnp.zeros((POOL_W, POOL_W), pw.dtype)
    for gi in range(4):
        out = lax.dynamic_update_slice(out, pw[gi], (gi * POOL_GDIM, gi * POOL_GDIM))
    return out


def _expand_heads(v):
    return jnp.repeat(v, 64).reshape(1, D)


def _ffn_fwd(x, x16, w13, w2, g, b, tag, carry_up=(), carry_f=()):
    act, act_t, h16, got_up = _ffn_up(x16, w13, carry=_Carry("gather", carry_up) if carry_up else None, name=f"{tag}_up")
    f = _mm(act, w2, carry=_Carry("gather", carry_f) if carry_f else None, name=f"{tag}_f")
    f, got_f = f if carry_f else (f, [])
    y, y16, y16t = _ln_fwd(x, f, g, b, scale=FFN_RES, name=f"{tag}_ln")
    return y, y16, y16t, (h16, act_t, f), list(got_up) + list(got_f)


def _ffn_bwd(x, x16t, w13, w2, g, b, saved, dy, tag, carry_dx=(), carry_dw13=()):
    h16, act_t, f = saved
    dres, df, dg, db = _ln_bwd(x, f, dy, g, b, scale=FFN_RES, name=f"{tag}_ln_bwd")
    dw2 = _mm(act_t, df, out_dtype=BF16, name=f"{tag}_dw2")
    dh = _ffn_down_bwd(df, w2, h16, name=f"{tag}_dh")
    dx = _mm(dh, w13, tb=True, add=dres, carry=_Carry("chips", carry_dx) if carry_dx else None, name=f"{tag}_dx")
    dx, got_dx = dx if carry_dx else (dx, [])
    dw13 = _mm(x16t, dh, out_blocked=True, out_dtype=BF16, carry=_Carry("chips", carry_dw13) if carry_dw13 else None,
               name=f"{tag}_dw13")
    dw13, got_dw13 = dw13 if carry_dw13 else (dw13, [])
    return dx, dw13, dw2, dg[0], db[0], list(got_dx) + list(got_dw13)


CARRIERS = {"ffn1_up": ["ffn1_w13"], "ffn1_f": ["ffn1_w2", "p_pool", "p_attn"], "mix_in": ["w_in"], "ffn2_up": ["ffn2_w13"],
            "ffn2_f": ["ffn2_w2", "p_ssd", "w_out"]}


def _layer_fwd(x0, x0_16, x0_16t, p, bias, nxt=None):
    send = {k: ([nxt[n] for n in names] if nxt is not None else []) for k, names in CARRIERS.items()}
    got = {}
    x1, x1_16, x1_16t, s1, g1 = _ffn_fwd(x0, x0_16, p["ffn1_w13"], p["ffn1_w2"], p["ln1_g"], p["ln1_b"], "ffn1",
                                 send["ffn1_up"], send["ffn1_f"])
    got.update(zip(CARRIERS["ffn1_up"] + CARRIERS["ffn1_f"], g1))
    hcat = _mm(x1_16, p["w_in"], carry=_Carry("gather", send["mix_in"]) if nxt is not None else None, name="mix_in")
    if nxt is not None:
        hcat, g_in = hcat
        got.update(zip(CARRIERS["mix_in"], g_in))
    pooled, pooled_t = _pool_fwd(hcat, name="pool")
    ya_lin = _mm(pooled, p["pool_bd"], name="pool_lin")
    ya_pre, ya_pre_t = _affine_fwd(ya_lin, p["pool_b"], p["pool_scale"], name="pool_affine")
    ya = _mm(ya_pre, p["p_pool"], name="pool_out")
    xa = _conv_fwd(hcat, p["conv_w"], p["conv_b"], name="conv")
    dtx = jnp.repeat(hcat[:, DT0:DT0 + SSD_HEADS], 64, axis=1)
    ysc, hs = _ssd_fwd(xa, dtx, p["dt_bias"], p["a_log"], p["d_skip"], name="ssd")
    yb_pre, yb_pre_t = _gnorm_fwd(ysc, hcat, p["ssd_norm"], name="ssd_norm")
    yb = _mm(yb_pre, p["p_ssd"], name="ssd_out")
    outs, lses, attn_sv = [], [], []
    for gi, (_, dil) in enumerate(ATTN_CONFIGS):
        c0 = QKV0 + 768 * gi
        qkv, col0 = (hcat, c0 // 128) if dil == 1 else (_to_sub(hcat[:, c0:c0 + 768], dil), 0)
        o, l = _attn_fwd(qkv, col0, bias[gi], SEQ // dil // ATTN_BLK, name=f"attn{gi}")
        attn_sv.append((None if dil == 1 else qkv, o, l))
        outs.append(_from_sub(o, dil))
        lses.append(_from_sub(l, dil))
    yc_pre, yc_pre_t = _amerge_fwd(outs, lses, name="attn_merge")
    yc = _mm(yc_pre, p["p_attn"], name="attn_out")
    merged, merged_t = _merge_fwd(hcat, ya, yb, yc, p["gate_b"], name="merge")
    mix = _mm(merged, p["w_out"], name="mix_out")
    x2, x2_16, x2_16t = _ln_fwd(x1, mix, p["ln2_g"], p["ln2_b"], scale=1.0, name="ln2")
    x3, x3_16, x3_16t, s3, g3 = _ffn_fwd(x2, x2_16, p["ffn2_w13"], p["ffn2_w2"], p["ln3_g"], p["ln3_b"], "ffn2",
                                 send["ffn2_up"], send["ffn2_f"])
    got.update(zip(CARRIERS["ffn2_up"] + CARRIERS["ffn2_f"], g3))
    saved = dict(x0=x0, x0_16t=x0_16t, s1=s1, x1=x1, x1_16t=x1_16t, hcat=hcat, pooled_t=pooled_t, ya_lin=ya_lin, ya_pre_t=ya_pre_t,
                 ya=ya, xa=xa, hs=hs, ysc=ysc, yb_pre_t=yb_pre_t, yb=yb, outs=outs, lses=lses, attn_sv=attn_sv, yc_pre_t=yc_pre_t,
                 yc=yc, merged_t=merged_t, mix=mix, x2=x2, x2_16t=x2_16t, s3=s3)
    return x3, x3_16, x3_16t, saved, got


BWD_CARRIERS = {"ffn2_dx": ["w_in"], "ffn2_dw13": ["ffn1_w13"], "mix_in_dx": ["ffn2_w13"], "mix_in_dw": ["ffn1_w2", "ffn2_w2"],
                "ffn1_dx": ["p_pool", "p_ssd", "p_attn", "w_out"]}


def _layer_bwd(p, bias, sv, dx3, pending=None):
    send = {k: ([pending[n] for n in names] if pending is not None else []) for k, names in BWD_CARRIERS.items()}
    done = {}
    g = {}
    dx2, g["ffn2_w13"], g["ffn2_w2"], g["ln3_g"], g["ln3_b"], got = _ffn_bwd(
        sv["x2"], sv["x2_16t"], p["ffn2_w13"], p["ffn2_w2"], p["ln3_g"], p["ln3_b"], sv["s3"], dx3, "ffn2",
        send["ffn2_dx"], send["ffn2_dw13"])
    done.update(zip(BWD_CARRIERS["ffn2_dx"] + BWD_CARRIERS["ffn2_dw13"], got))
    hcat = sv["hcat"]
    dres, dmix, dg2, db2 = _ln_bwd(sv["x1"], sv["mix"], dx2, p["ln2_g"], p["ln2_b"], scale=1.0, name="ln2_bwd")
    g["ln2_g"], g["ln2_b"] = dg2[0], db2[0]
    dmerged = _mm(dmix, p["w_out"], tb=True, name="mix_out_dx")
    g["w_out"] = _mm(sv["merged_t"], dmix, out_dtype=BF16, name="mix_out_dw")
    dg0, dg1, dg2_, dya, dyb, dyc, gb0, gb1, gb2 = _merge_bwd(hcat, sv["ya"], sv["yb"], sv["yc"], dmerged, p["gate_b"],
                                                               name="merge_bwd")
    g["gate_b"] = jnp.concatenate([gb0, gb1, gb2], axis=0)
    dya_pre = _mm(dya, p["p_pool"], tb=True, name="pool_out_dx")
    g["p_pool"] = _mm(sv["ya_pre_t"], dya, out_dtype=BF16, name="pool_out_dw")
    dya_lin, dpb, dps = _affine_bwd(sv["ya_lin"], dya_pre, p["pool_b"], p["pool_scale"], name="pool_affine_bwd")
    g["pool_b"], g["pool_scale"] = dpb[0].reshape(4, POOL_GDIM), dps[0]
    dpooled = _mm(dya_lin, p["pool_bd"], tb=True, name="pool_lin_dx")
    dbd = _mm(sv["pooled_t"], dya_lin, name="pool_lin_dw")
    g["pool_w"] = jnp.stack([dbd[i * POOL_GDIM:(i + 1) * POOL_GDIM, i * POOL_GDIM:(i + 1) * POOL_GDIM] for i in range(4)])
    du = _pool_bwd(dpooled, name="pool_bwd")
    dyb_pre = _mm(dyb, p["p_ssd"], tb=True, name="ssd_out_dx")
    g["p_ssd"] = _mm(sv["yb_pre_t"], dyb, out_dtype=BF16, name="ssd_out_dw")
    dysc, dz, dnw = _gnorm_bwd(sv["ysc"], hcat, dyb_pre, p["ssd_norm"], name="ssd_norm_bwd")
    g["ssd_norm"] = dnw[0]
    dtx = jnp.repeat(hcat[:, DT0:DT0 + SSD_HEADS], 64, axis=1)
    dxs, dbm, dcm, ddtx, ddtb, dalog, ddsk = _ssd_bwd(sv["xa"], dtx, p["dt_bias"], p["a_log"], p["d_skip"], sv["hs"], dysc,
                                                     name="ssd_bwd")
    g["dt_bias"], g["a_log"], g["d_skip"] = ddtb[0, ::64], dalog[0, ::64], ddsk[0, ::64]
    dxa = jnp.concatenate([dxs, dbm, dcm], axis=1)
    dxbc, dcw, dcb = _conv_bwd(hcat, p["conv_w"], p["conv_b"], dxa, name="conv_bwd")
    g["conv_w"], g["conv_b"] = dcw, dcb[0]
    ddt = jnp.pad(ddtx[:, ::64], ((0, 0), (0, HC - DT0 - SSD_HEADS))).astype(BF16)
    dyc_pre = _mm(dyc, p["p_attn"], tb=True, name="attn_out_dx")
    g["p_attn"] = _mm(sv["yc_pre_t"], dyc, out_dtype=BF16, name="attn_out_dw")
    am = _amerge_bwd(sv["outs"], sv["lses"], dyc_pre, name="attn_merge_bwd")
    dqkv, dbias = [], []
    for gi, (_, dil) in enumerate(ATTN_CONFIGS):
        qkv_sub, o, l = sv["attn_sv"][gi]
        qkv, col0 = (hcat, (QKV0 + 768 * gi) // 128) if dil == 1 else (qkv_sub, 0)
        dq, dk, dv, dbi = _attn_bwd(qkv, col0, o, l, _to_sub(am[gi], dil), _to_sub(am[3 + gi], dil), bias[gi],
                                    SEQ // dil // ATTN_BLK, name=f"attn{gi}_bwd")
        dqkv += [_from_sub(t, dil) for t in (dq, dk, dv)]
        dbias.append(dbi)
    g["attn_bias"] = jnp.stack(dbias)
    dhcat = jnp.concatenate([dz, dxbc, dg0, dg1, dg2_, *dqkv, du, ddt], axis=1)
    chips = lambda k: _Carry("chips", send[k]) if pending is not None else None
    dx1 = _mm(dhcat, p["w_in"], tb=True, add=dres, carry=chips("mix_in_dx"), name="mix_in_dx")
    g["w_in"] = _mm(sv["x1_16t"], dhcat, out_dtype=BF16, carry=chips("mix_in_dw"), name="mix_in_dw")
    if pending is not None:
        (dx1, got_dx), (g["w_in"], got_dw) = dx1, g["w_in"]
        done.update(zip(BWD_CARRIERS["mix_in_dx"] + BWD_CARRIERS["mix_in_dw"], list(got_dx) + list(got_dw)))
    dx0, g["ffn1_w13"], g["ffn1_w2"], g["ln1_g"], g["ln1_b"], got = _ffn_bwd(
        sv["x0"], sv["x0_16t"], p["ffn1_w13"], p["ffn1_w2"], p["ln1_g"], p["ln1_b"], sv["s1"], dx1, "ffn1", send["ffn1_dx"])
    done.update(zip(BWD_CARRIERS["ffn1_dx"], got))
    return dx0, g, done


def _local_step(x, tgt, lw, rel_bias, shards=None, prepare=None, chip_sums=None):
    bias, bias_vjp = jax.vjp(_attn_bias, rel_bias)
    y, y16, saved, lw = x, x.astype(BF16), [], list(lw)
    y16t = y16.T
    n_layers = len(lw) if shards is None else len(shards)
    for l in range(n_layers):
        nxt = shards[l + 1] if shards is not None and l + 1 < n_layers else None
        y, y16, y16t, sv, got = _layer_fwd(y, y16, y16t, lw[l], bias, nxt)
        saved.append(sv)
        if nxt is not None:
            lw.append(prepare(got, l + 1))
    dy, sq = _loss_kernel(y, tgt, name="loss")
    grads, sums, pending = [None] * n_layers, [None] * n_layers, None
    for l in reversed(range(n_layers)):
        dy, grads[l], done = _layer_bwd(lw[l], bias, saved[l], dy, pending)
        if pending is not None:
            sums[l + 1] = done
        pending = chip_sums(grads[l]) if chip_sums is not None else None
    if pending is not None:
        names = list(pending)
        sums[0] = dict(zip(names, _exchange_chips([pending[n] for n in names], name="exchange_grads_chips")))
    (d_rel,) = bias_vjp(sum(g.pop("attn_bias") for g in grads))
    return sq, dy, grads, d_rel, sums


SMALL_REPL = ["ln1_g", "ln1_b", "pool_w", "pool_b", "pool_scale", "conv_b", "dt_bias", "a_log", "d_skip", "ssd_norm",
              "ln2_g", "ln2_b", "ln3_g", "ln3_b", "rel_bias"]
SMALL_SHARD = ["gate_b", "conv_w"]
WEIGHTS = ['ffn1_w13', 'ffn1_w2', 'ln1_g', 'ln1_b', 'w_in', 'gate_b', 'pool_w', 'pool_b', 'pool_scale', 'conv_w', 'conv_b',
           'dt_bias', 'a_log', 'd_skip', 'ssd_norm', 'rel_bias', 'p_pool', 'p_ssd', 'p_attn', 'w_out', 'ln2_g', 'ln2_b',
           'ffn2_w13', 'ffn2_w2', 'ln3_g', 'ln3_b']


def _step(w, m, v, x, tgt):
    my_c = lax.axis_index("c")
    dev = 4 * lax.axis_index("x") + 2 * lax.axis_index("y") + my_c
    shards = [{n: w[n][l].astype(BF16) for n, _, _ in BIG} for l in range(DEPTH)]
    first = dict(zip([n for n, _, _ in BIG], _all_gather([shards[0][n] for n, _, _ in BIG], name="gather_weights")))
    sp_in = _SmallPack([w[n].shape for n in SMALL_SHARD])
    small_g = _all_gather([sp_in.pack([w[n] for n in SMALL_SHARD])], name="gather_small")[0]
    gate_b = jnp.stack([sp_in.unpack(small_g[d])[0] for d in range(NDEV)], axis=2).reshape(DEPTH, 3, D)
    conv_w = jnp.stack([sp_in.unpack(small_g[d])[1] for d in range(NDEV)], axis=2).reshape(DEPTH, 4, 2048)

    def prepare(blocks, l):
        p = {n: blocks[n] if n in BLOCKED else _from_blocks(blocks[n], ax) for n, _, ax in BIG}
        p["w_in"] = _w_in_to_padded(p["w_in"])
        p["pool_bd"] = _block_diag(w["pool_w"][l]).astype(BF16)
        p["gate_b"], p["conv_w"] = gate_b[l], conv_w[l]
        for n in ("ln1_g", "ln1_b", "ln2_g", "ln2_b", "ln3_g", "ln3_b", "pool_scale", "conv_b", "ssd_norm"):
            p[n] = w[n][l][None, :]
        p["pool_b"] = w["pool_b"][l].reshape(1, POOL_W)
        for n in ("dt_bias", "a_log", "d_skip"):
            p[n] = _expand_heads(w[n][l])
        return p

    c_idx = my_c.reshape(1).astype(jnp.int32)

    def chip_sums(g):
        parts = []
        for n, (r, c), ax in BIG:
            full = _w_in_from_padded(g[n]) if n == "w_in" else g[n]
            parts.append((full if n in BLOCKED else _to_blocks(full, ax)).reshape(4, 2, r, c))
        from_sibling = _exchange_sibling(parts, name="exchange_grads_cores")
        return {n: _add_own_half(p, o, c_idx, name=f"add_cores_{n}") for p, o, (n, _, _) in zip(parts, from_sibling, BIG)}

    sq, dx, grads, d_rel, sums = _local_step(x.reshape(T, D), tgt.reshape(T, D), [prepare(first, 0)], w["rel_bias"], shards,
                                             prepare, chip_sums)
    loss = lax.psum(jnp.sum(sq) * (0.5 / D), ("x", "y", "c"))
    small_names = SMALL_REPL + SMALL_SHARD
    small = {n: (d_rel if n == "rel_bias" else jnp.stack([g[n] for g in grads])) for n in small_names}
    sp = _SmallPack([small[n].shape for n in small_names])
    gsmall = sp.unpack(_sum_slots(_all_gather([sp.pack([small[n] for n in small_names])], name="gather_small_grads")[0],
                                  name="sum_small_grads"))
    gout = {}
    for n, gv in zip(small_names, gsmall):
        if n in SMALL_SHARD:
            width = w[n].shape[-1]
            gv = lax.dynamic_slice_in_dim(gv, dev * width, width, axis=2)
        gout[n] = gv
    delta, new_m, new_v = {}, {}, {}
    for n, _, _ in BIG:
        gout[n], delta[n], new_m[n], new_v[n] = _adamw_reduce([s[n] for s in sums], w[n], m[n], v[n], name=f"adamw_{n}")
    spa = _SmallPack([w[n].shape for n in small_names])
    res = _adamw(spa.pack([w[n] for n in small_names]), spa.pack([gout[n] for n in small_names]),
                 spa.pack([m[n] for n in small_names]), spa.pack([v[n] for n in small_names]), name="adamw_small")
    for out, packed in zip((delta, new_m, new_v), res):
        for n, val in zip(small_names, spa.unpack(packed)):
            out[n] = val
    return (loss, dx.reshape(NB, SEQ, D), *[gout[n] for n in WEIGHTS], *[delta[n] for n in WEIGHTS],
            *[new_m[n] for n in WEIGHTS], *[new_v[n] for n in WEIGHTS])


def kernel(x, ffn1_w13, ffn1_w2, ln1_g, ln1_b, w_in, gate_b, pool_w, pool_b, pool_scale, conv_w, conv_b, dt_bias, a_log, d_skip, ssd_norm, rel_bias, p_pool, p_ssd, p_attn, w_out, ln2_g, ln2_b, ffn2_w13, ffn2_w2, ln3_g, ln3_b, loss_target, m_ffn1_w13, m_ffn1_w2, m_ln1_g, m_ln1_b, m_w_in, m_gate_b, m_pool_w, m_pool_b, m_pool_scale, m_conv_w, m_conv_b, m_dt_bias, m_a_log, m_d_skip, m_ssd_norm, m_rel_bias, m_p_pool, m_p_ssd, m_p_attn, m_w_out, m_ln2_g, m_ln2_b, m_ffn2_w13, m_ffn2_w2, m_ln3_g, m_ln3_b, v_ffn1_w13, v_ffn1_w2, v_ln1_g, v_ln1_b, v_w_in, v_gate_b, v_pool_w, v_pool_b, v_pool_scale, v_conv_w, v_conv_b, v_dt_bias, v_a_log, v_d_skip, v_ssd_norm, v_rel_bias, v_p_pool, v_p_ssd, v_p_attn, v_w_out, v_ln2_g, v_ln2_b, v_ffn2_w13, v_ffn2_w2, v_ln3_g, v_ln3_b):
    given = dict(locals())
    w = {n: given[n] for n in WEIGHTS}
    m = {n: given["m_" + n] for n in WEIGHTS}
    v = {n: given["v_" + n] for n in WEIGHTS}
    return _step(w, m, v, x, loss_target)
```

```python
import functools
import math

import numpy as np
import jax
import jax.numpy as jnp
from jax import lax
from jax.experimental import pallas as pl
from jax.experimental.pallas import tpu as pltpu

F32, BF16 = jnp.float32, jnp.bfloat16
HIGHEST = lax.Precision.HIGHEST

NDEV = 8
DEPTH = 4
D = 1024
SEQ = 2048
NB = 2
T = NB * SEQ
DFF = 2816
POOL_W = 768
POOL_WINDOWS = (2, 4, 8, 16)
POOL_GDIM = 192
SSD_HEADS = 16
CHUNK = 128
NCHUNK = SEQ // CHUNK
ATTN_CONFIGS = ((128, 1), (512, 4), (2048, 16))
ATTN_BLK = 128
REL_BUCKETS = 32
REL_MAX_DIST = 2048
LN_EPS = 1e-5
SSD_EPS = 1e-5
ALPHA = (2.0 * DEPTH) ** 0.25
FFN_RES = 0.5
NEG = -1e30

ADAM_LR, ADAM_B1, ADAM_B2, ADAM_EPS, ADAM_WD, ADAM_STEP = 0.001, 0.9, 0.999, 1e-08, 0.01, 10

Z0, XBC0, GATE0, QKV0, U0, DT0, HC = 0, 1024, 3072, 6144, 8448, 9216, 9728
O_U, O_Z, O_XBC, O_DT, O_Q, O_K, O_V, O_G, O_END = 0, 768, 1792, 3840, 3856, 4624, 5392, 6160, 9232

VMEM_LIMIT = 56 * 1024 * 1024


def _cp(sem):
    return pltpu.CompilerParams(dimension_semantics=sem, vmem_limit_bytes=VMEM_LIMIT)


def _pick(dim, cands):
    for c in cands:
        if dim % c == 0:
            return c
    return dim


def _mm(a, b, *, tb=False, out_dtype=F32, add=None, add_scale=1.0, out_blocked=False, bn=None, b_layer=None, carry=None, name):
    nc = carry.n if carry is not None else 0
    bk = None
    if a.ndim == 3:
        nkb, M, bk = a.shape
        K = nkb * bk
    else:
        M, K = a.shape
    b_blocked = b.ndim >= 3
    b_shape = b.shape if b_layer is None else (b.shape[0], *b.shape[2:])
    if b_blocked and tb:
        assert bk in (None, b_shape[2])
        nkb, N, bk = b_shape
        assert K == nkb * bk, (a.shape, b.shape)
    elif b_blocked:
        nnb, K2, bn = b_shape
        N = nnb * bn
        assert K == K2, (a.shape, b.shape)
    else:
        (N, K2) = b.shape if tb else b.shape[::-1]
        assert K == K2, (a.shape, b.shape, tb)
    bm = _pick(M, (1024, 1408, 768, 512, 256))
    if bn is None:
        bn = _pick(N, (2432, 1024, 768, 512, 256))
    if bk is None:
        bk = K if K <= 1024 else _pick(K, (1024, 1408, 2432, 512, 256))
    nk = K // bk
    dn = (((1,), (1 if tb else 0,)), ((), ()))

    grid = (M // bm, N // bn, nk)

    def kern(*refs):
        a_ref, b_ref = refs[0], refs[1]
        n_in = 2 + (add is not None)
        add_ref = refs[2] if add is not None else None
        x_refs, o_ref, g_refs = refs[n_in:n_in + nc], refs[n_in + nc], refs[n_in + nc + 1:n_in + 2 * nc + 1]
        scratch = refs[n_in + 2 * nc + 1:]
        acc_ref = scratch[0] if nk > 1 else None
        if nc:
            step = (pl.program_id(0) * grid[1] + pl.program_id(1)) * nk + pl.program_id(2)
            carried = _Carried(carry, x_refs, g_refs, scratch[-3:], step, grid[0] * grid[1] * nk)
            carried.before()
        p = lax.dot_general(a_ref[...].astype(BF16), b_ref[...].astype(BF16), dn, preferred_element_type=F32)

        def fin(acc):
            if add_ref is not None:
                acc = acc + add_scale * add_ref[...]
            o_ref[...] = acc.astype(out_dtype)

        if nk == 1:
            fin(p)
        else:
            k = pl.program_id(2)

            @pl.when(k == 0)
            def _():
                acc_ref[...] = p

            @pl.when(k > 0)
            def _():
                acc_ref[...] += p

            @pl.when(k == nk - 1)
            def _():
                fin(acc_ref[...])

        if nc:
            carried.after()

    if a.ndim == 3:
        a_spec = pl.BlockSpec((None, bm, bk), lambda i, j, k: (k, i, 0))
    else:
        a_spec = pl.BlockSpec((bm, bk), lambda i, j, k: (i, k))
    if b_blocked and b_layer is not None:
        b_spec = (pl.BlockSpec((None, None, bn, bk), lambda i, j, k: (k, b_layer, j, 0)) if tb else
                  pl.BlockSpec((None, None, bk, bn), lambda i, j, k: (j, b_layer, k, 0)))
    elif b_blocked and tb:
        b_spec = pl.BlockSpec((None, bn, bk), lambda i, j, k: (k, j, 0))
    elif b_blocked:
        b_spec = pl.BlockSpec((None, bk, bn), lambda i, j, k: (j, k, 0))
    elif tb:
        b_spec = pl.BlockSpec((bn, bk), lambda i, j, k: (j, k))
    else:
        b_spec = pl.BlockSpec((bk, bn), lambda i, j, k: (k, j))
    if out_blocked:
        o_spec, o_shape = pl.BlockSpec((None, bm, bn), lambda i, j, k: (j, i, 0)), (N // bn, M, bn)
    else:
        o_spec, o_shape = pl.BlockSpec((bm, bn), lambda i, j, k: (i, j)), (M, N)
    in_specs = [a_spec, b_spec]
    args = [a, b]
    if add is not None:
        assert add.shape == o_shape
        in_specs.append(o_spec)
        args.append(add)
    res = pl.pallas_call(
        kern, name=name, grid=grid,
        in_specs=in_specs + [_ANY] * nc, out_specs=[o_spec] + [_ANY] * nc,
        out_shape=[jax.ShapeDtypeStruct(o_shape, out_dtype)] + (carry.out_shapes() if nc else []),
        scratch_shapes=([pltpu.VMEM((bm, bn), F32)] if nk > 1 else []) + (carry.sems() if nc else []),
        compiler_params=_cp(("arbitrary",) * 3 if nc else ("parallel", "parallel", "arbitrary")),
    )(*args, *(carry.arrays if nc else []))
    return (res[0], res[1:]) if nc else res[0]


def _rowwise(fn, rows, pars, outs, accs, *, name, tile, groups=1):
    n_rows = rows[0][0].shape[0]
    nt = n_rows // tile
    n_in = len(rows) + len(pars)
    n_out = len(outs)
    transposed = [len(o) == 4 and o[3] for o in outs]

    def kern(*refs):
        res = fn(*[r[...] for r in refs[:n_in]])
        for o_ref, val, tr in zip(refs[n_in:n_in + n_out], res[:n_out], transposed):
            o_ref[...] = (val.astype(F32).T if tr else val).astype(o_ref.dtype)
        i = pl.program_id(1)
        for a_ref, val in zip(refs[n_in + n_out:], res[n_out:]):
            @pl.when(i == 0)
            def _(a_ref=a_ref, val=val):
                a_ref[...] = val

            @pl.when(i > 0)
            def _(a_ref=a_ref, val=val):
                a_ref[...] += val

    in_specs = [pl.BlockSpec((tile, w), lambda g, i, c0=c0: (i, c0 + g)) for (_, w, c0) in rows]
    for p, w in pars:
        if w is None:
            in_specs.append(pl.BlockSpec(p.shape, lambda g, i: (0, 0)))
        else:
            in_specs.append(pl.BlockSpec((p.shape[0], w), lambda g, i: (0, g)))
    out_specs = [pl.BlockSpec((o[1], tile), lambda g, i: (g, i)) if tr else pl.BlockSpec((tile, o[1]), lambda g, i: (i, g))
                 for o, tr in zip(outs, transposed)]
    out_specs += [pl.BlockSpec((1, w), lambda g, i: (0, g)) for (_, w) in accs]
    out_shape = [jax.ShapeDtypeStruct((o[0], n_rows) if tr else (n_rows, o[0]), o[2]) for o, tr in zip(outs, transposed)]
    out_shape += [jax.ShapeDtypeStruct((1, c), F32) for (c, _) in accs]
    return pl.pallas_call(
        kern, name=name, grid=(groups, nt), in_specs=in_specs, out_specs=out_specs, out_shape=out_shape,
        compiler_params=_cp(("arbitrary", "arbitrary")),
    )(*[r[0] for r in rows], *[p[0] for p in pars])


def _colsum(v):
    return jnp.sum(v, axis=0, keepdims=True)


def _silu(v):
    return v * jax.nn.sigmoid(v)


def _ln_fn(x, f, g, b, *, scale):
    pre = ALPHA * x + scale * f
    mu = jnp.mean(pre, axis=-1, keepdims=True)
    var = jnp.mean(jnp.square(pre - mu), axis=-1, keepdims=True)
    return (pre - mu) * lax.rsqrt(var + LN_EPS) * g + b


def _ln_fwd(x, f, g, b, *, scale, name):
    fn = lambda x_, f_, g_, b_: [_ln_fn(x_, f_, g_, b_, scale=scale)] * 3
    return _rowwise(fn, [(x, D, 0), (f, D, 0)], [(g, None), (b, None)], [(D, D, F32), (D, D, BF16), (D, D, BF16, True)], [],
                    name=name, tile=512)


def _ln_bwd(x, f, dy, g, b, *, scale, name):
    def fn(x_, f_, dy_, g_, b_):
        _, vjp = jax.vjp(functools.partial(_ln_fn, scale=scale), x_, f_, g_, b_)
        return list(vjp(dy_))

    return _rowwise(fn, [(x, D, 0), (f, D, 0), (dy, D, 0)], [(g, None), (b, None)],
                    [(D, D, F32), (D, D, BF16)], [(D, D), (D, D)], name=name, tile=512)


FB = 2 * DFF // NDEV
FFN_BM = 1024
_H_PAIR = pl.BlockSpec((2, None, FFN_BM, FB), lambda i, j: (0, j, i, 0))


def _ffn_up(x16, w13, *, carry=None, name):
    nc = carry.n if carry is not None else 0
    grid = (T // FFN_BM, 4)

    def kern(*refs):
        x_ref, w_ref = refs[0], refs[1]
        x_refs, (act_ref, act_t_ref, h_ref), g_refs = refs[2:2 + nc], refs[2 + nc:5 + nc], refs[5 + nc:5 + 2 * nc]
        if nc:
            step = pl.program_id(0) * grid[1] + pl.program_id(1)
            carried = _Carried(carry, x_refs, g_refs, refs[5 + 2 * nc:], step, grid[0] * grid[1])
            carried.before()
        x = x_ref[...]
        ha = jnp.dot(x, w_ref[0], preferred_element_type=F32)
        hg = jnp.dot(x, w_ref[1], preferred_element_type=F32)
        act = _silu(ha) * hg
        act_ref[...] = act.astype(BF16)
        act_t_ref[...] = act.T.astype(BF16)
        h_ref[0] = ha.astype(BF16)
        h_ref[1] = hg.astype(BF16)
        if nc:
            carried.after()

    res = pl.pallas_call(
        kern, name=name, grid=grid,
        in_specs=[pl.BlockSpec((FFN_BM, D), lambda i, j: (i, 0)),
                  pl.BlockSpec((2, None, D, FB), lambda i, j: (0, j, 0, 0))] + [_ANY] * nc,
        out_specs=[pl.BlockSpec((None, FFN_BM, FB), lambda i, j: (j, i, 0)), pl.BlockSpec((None, FB, FFN_BM), lambda i, j: (j, 0, i)),
                   _H_PAIR] + [_ANY] * nc,
        out_shape=[jax.ShapeDtypeStruct((4, T, FB), BF16), jax.ShapeDtypeStruct((4, FB, T), BF16),
                   jax.ShapeDtypeStruct((2, 4, T, FB), BF16)]
        + (carry.out_shapes() if nc else []),
        scratch_shapes=carry.sems() if nc else [],
        compiler_params=_cp(("arbitrary", "arbitrary") if nc else ("parallel", "parallel")),
    )(x16, w13.reshape(2, 4, D, FB), *(carry.arrays if nc else []))
    return res[0], res[1].reshape(DFF, T), res[2].reshape(NDEV, T, FB), res[3:]


def _ffn_down_bwd(df, w2, h16, *, name):
    def kern(df_ref, w_ref, h_ref, dh_ref):
        da = _dot(df_ref[...], w_ref[...], _NT)
        a, g = h_ref[0].astype(F32), h_ref[1].astype(F32)
        s = jax.nn.sigmoid(a)
        dh_ref[0] = (da * g * (s * (1.0 + a * (1.0 - s)))).astype(BF16)
        dh_ref[1] = (da * (a * s)).astype(BF16)

    return pl.pallas_call(
        kern, name=name, grid=(T // FFN_BM, 4),
        in_specs=[pl.BlockSpec((FFN_BM, D), lambda i, j: (i, 0)), pl.BlockSpec((FB, D), lambda i, j: (j, 0)), _H_PAIR],
        out_specs=_H_PAIR, out_shape=jax.ShapeDtypeStruct((2, 4, T, FB), BF16),
        compiler_params=_cp(("parallel", "parallel")),
    )(df, w2, h16.reshape(2, 4, T, FB)).reshape(NDEV, T, FB)


def _affine_fn(y, b, s):
    return (y + b) * s


def _affine_fwd(y, b, s, *, name):
    return _rowwise(lambda y_, b_, s_: [_affine_fn(y_, b_, s_)] * 2, [(y, POOL_W, 0)], [(b, None), (s, None)],
                    [(POOL_W, POOL_W, BF16), (POOL_W, POOL_W, BF16, True)], [], name=name, tile=512)


def _affine_bwd(y, dyo, b, s, *, name):
    def fn(y_, d_, b_, s_):
        _, vjp = jax.vjp(_affine_fn, y_, b_, s_)
        return list(vjp(d_))

    return _rowwise(fn, [(y, POOL_W, 0), (dyo, POOL_W, 0)], [(b, None), (s, None)],
                    [(POOL_W, POOL_W, BF16)], [(POOL_W, POOL_W), (POOL_W, POOL_W)], name=name, tile=512)


def _gnorm_fn(y, z, w):
    yz = y * _silu(z)
    return yz * lax.rsqrt(jnp.mean(jnp.square(yz), axis=-1, keepdims=True) + SSD_EPS) * w


def _gnorm_fwd(y, hcat, w, *, name):
    return _rowwise(lambda y_, z_, w_: [_gnorm_fn(y_, z_, w_)] * 2, [(y, 256, 0), (hcat, 256, Z0 // 256)], [(w, 256)],
                    [(D, 256, BF16), (D, 256, BF16, True)], [], name=name, tile=512, groups=4)


def _gnorm_bwd(y, hcat, dyo, w, *, name):
    def fn(y_, z_, d_, w_):
        _, vjp = jax.vjp(_gnorm_fn, y_, z_, w_)
        return list(vjp(d_))

    return _rowwise(fn, [(y, 256, 0), (hcat, 256, Z0 // 256), (dyo, 256, 0)], [(w, 256)],
                    [(D, 256, F32), (D, 256, BF16)], [(D, 256)], name=name, tile=512, groups=4)


def _merge_fn(g0, g1, g2, ya, yb, yc, b0, b1, b2):
    return jax.nn.sigmoid(g0 + b0) * ya + jax.nn.sigmoid(g1 + b1) * yb + jax.nn.sigmoid(g2 + b2) * yc


def _merge_rows(hcat, ya, yb, yc):
    c = GATE0 // 256
    return [(hcat, 256, c), (hcat, 256, c + 4), (hcat, 256, c + 8), (ya, 256, 0), (yb, 256, 0), (yc, 256, 0)]


def _merge_fwd(hcat, ya, yb, yc, gb, *, name):
    pars = [(gb[0:1], 256), (gb[1:2], 256), (gb[2:3], 256)]
    return _rowwise(lambda *v: [_merge_fn(*v)] * 2, _merge_rows(hcat, ya, yb, yc), pars, [(D, 256, BF16), (D, 256, BF16, True)],
                    [], name=name, tile=512, groups=4)


def _merge_bwd(hcat, ya, yb, yc, dm, gb, *, name):
    def fn(g0, g1, g2, ya_, yb_, yc_, dm_, b0, b1, b2):
        _, vjp = jax.vjp(_merge_fn, g0, g1, g2, ya_, yb_, yc_, b0, b1, b2)
        return list(vjp(dm_))

    pars = [(gb[0:1], 256), (gb[1:2], 256), (gb[2:3], 256)]
    return _rowwise(fn, _merge_rows(hcat, ya, yb, yc) + [(dm, 256, 0)], pars,
                    [(D, 256, BF16)] * 6, [(D, 256)] * 3, name=name, tile=512, groups=4)


def _amerge_fn(o0, o1, o2, l0, l1, l2):
    m = jnp.maximum(jnp.maximum(l0, l1), l2)
    e0, e1, e2 = jnp.exp(l0 - m), jnp.exp(l1 - m), jnp.exp(l2 - m)
    return (e0 * o0 + e1 * o1 + e2 * o2) / (e0 + e1 + e2)


def _amerge_fwd(os_, ls_, *, name):
    rows = [(v, 256, 0) for v in (*os_, *ls_)]
    return _rowwise(lambda *v: [_amerge_fn(*v)] * 2, rows, [], [(256, 256, BF16), (256, 256, BF16, True)], [], name=name, tile=1024)


def _amerge_bwd(os_, ls_, dy, *, name):
    def fn(*v):
        _, vjp = jax.vjp(_amerge_fn, *v[:6])
        return list(vjp(v[6]))

    rows = [(v, 256, 0) for v in (*os_, *ls_, dy)]
    return _rowwise(fn, rows, [], [(256, 256, F32)] * 6, [], name=name, tile=1024)


def _loss_kernel(y, tgt, *, name):
    def fn(y_, t_):
        e = y_ - t_
        return [e * (1.0 / D), _colsum(e * e)]

    return _rowwise(fn, [(y, D, 0), (tgt, D, 0)], [], [(D, D, F32)], [(D, D)], name=name, tile=512)


def _adam_math(w, g, m, v):
    m2 = ADAM_B1 * m + (1.0 - ADAM_B1) * g
    v2 = ADAM_B2 * v + (1.0 - ADAM_B2) * jnp.square(g)
    m_hat = m2 / (1.0 - ADAM_B1 ** ADAM_STEP)
    v_hat = v2 / (1.0 - ADAM_B2 ** ADAM_STEP)
    return -ADAM_LR * (m_hat / (jnp.sqrt(v_hat) + ADAM_EPS) + ADAM_WD * w), m2, v2


def _adamw(w, g, m, v, *, name):
    r, c = w.shape
    tile = _pick(r, (512, 256, 128, 64, 32, 16, 8))
    fn = lambda *a: list(_adam_math(*a))
    return _rowwise(fn, [(w, c, 0), (g, c, 0), (m, c, 0), (v, c, 0)], [], [(c, c, F32)] * 3, [], name=name, tile=tile)


def _pool_lane_window(j, width):
    lane = lax.broadcasted_iota(jnp.int32, (1, width), 1) + j * width
    grp = lane // POOL_GDIM
    return grp


def _pool_select(grp, vals):
    out = vals[3]
    for gi in (2, 1, 0):
        out = jnp.where(grp == gi, vals[gi], out)
    return out


def _pool_fwd(hcat, *, name):
    width = 256

    def kern(u_ref, o_ref, ot_ref):
        u = u_ref[...]
        t = lax.broadcasted_iota(jnp.int32, (SEQ, 1), 0)
        grp = _pool_lane_window(pl.program_id(1), width)

        def shift(v, k):
            return jnp.where(t >= k, pltpu.roll(v, k, 0), 0.0)

        s2 = u + shift(u, 1)
        s4 = s2 + shift(s2, 2)
        s8 = s4 + shift(s4, 4)
        s16 = s8 + shift(s8, 8)
        win = _pool_select(grp, [jnp.full((1, width), float(w), F32) for w in POOL_WINDOWS])
        cnt = jnp.minimum((t + 1).astype(F32), win)
        pooled = _pool_select(grp, [s2, s4, s8, s16]) / cnt - u
        o_ref[...] = pooled.astype(o_ref.dtype)
        ot_ref[...] = pooled.T.astype(ot_ref.dtype)

    return pl.pallas_call(
        kern, name=name, grid=(NB, POOL_W // width),
        in_specs=[pl.BlockSpec((SEQ, width), lambda b, j: (b, U0 // width + j))],
        out_specs=[pl.BlockSpec((SEQ, width), lambda b, j: (b, j)), pl.BlockSpec((width, SEQ), lambda b, j: (j, b))],
        out_shape=[jax.ShapeDtypeStruct((T, POOL_W), BF16), jax.ShapeDtypeStruct((POOL_W, T), BF16)],
        compiler_params=_cp(("parallel", "parallel")),
    )(hcat)


def _pool_bwd(dp, *, name):
    width = 256

    def kern(d_ref, o_ref):
        d = d_ref[...]
        t = lax.broadcasted_iota(jnp.int32, (SEQ, 1), 0)
        grp = _pool_lane_window(pl.program_id(1), width)
        win = _pool_select(grp, [jnp.full((1, width), float(w), F32) for w in POOL_WINDOWS])
        dm = d / jnp.minimum((t + 1).astype(F32), win)

        def shift(v, k):
            return jnp.where(t < SEQ - k, pltpu.roll(v, SEQ - k, 0), 0.0)

        r2 = dm + shift(dm, 1)
        r4 = r2 + shift(r2, 2)
        r8 = r4 + shift(r4, 4)
        r16 = r8 + shift(r8, 8)
        o_ref[...] = (_pool_select(grp, [r2, r4, r8, r16]) - d).astype(o_ref.dtype)

    return pl.pallas_call(
        kern, name=name, grid=(NB, POOL_W // width),
        in_specs=[pl.BlockSpec((SEQ, width), lambda b, j: (b, j))],
        out_specs=pl.BlockSpec((SEQ, width), lambda b, j: (b, j)),
        out_shape=jax.ShapeDtypeStruct((T, POOL_W), BF16), compiler_params=_cp(("parallel", "parallel")),
    )(dp)


CONV_W = 512


def _conv_pre(x, w, b, t):
    pre = w[3:4] * x + b
    for k in (1, 2, 3):
        pre = pre + w[3 - k:4 - k] * jnp.where(t >= k, pltpu.roll(x, k, 0), 0.0)
    return pre


def _conv_fwd(hcat, w, b, *, name):
    def kern(x_ref, w_ref, b_ref, o_ref):
        t = lax.broadcasted_iota(jnp.int32, (SEQ, 1), 0)
        o_ref[...] = _silu(_conv_pre(x_ref[...], w_ref[...], b_ref[...], t))

    return pl.pallas_call(
        kern, name=name, grid=(NB, 2048 // CONV_W),
        in_specs=[pl.BlockSpec((SEQ, CONV_W), lambda s, j: (s, XBC0 // CONV_W + j)),
                  pl.BlockSpec((4, CONV_W), lambda s, j: (0, j)), pl.BlockSpec((1, CONV_W), lambda s, j: (0, j))],
        out_specs=pl.BlockSpec((SEQ, CONV_W), lambda s, j: (s, j)),
        out_shape=jax.ShapeDtypeStruct((T, 2048), F32), compiler_params=_cp(("parallel", "parallel")),
    )(hcat, w, b)


def _conv_bwd(hcat, w, b, dy, *, name):
    def kern(x_ref, w_ref, b_ref, dy_ref, dx_ref, dw_ref, db_ref):
        t = lax.broadcasted_iota(jnp.int32, (SEQ, 1), 0)
        x, w_ = x_ref[...], w_ref[...]
        pre = _conv_pre(x, w_, b_ref[...], t)
        s = jax.nn.sigmoid(pre)
        dpre = dy_ref[...] * (s * (1.0 + pre * (1.0 - s)))
        dx = w_[3:4] * dpre
        dws = [None] * 4
        dws[3] = _colsum(dpre * x)
        for k in (1, 2, 3):
            dx = dx + w_[3 - k:4 - k] * jnp.where(t < SEQ - k, pltpu.roll(dpre, SEQ - k, 0), 0.0)
            dws[3 - k] = _colsum(dpre * jnp.where(t >= k, pltpu.roll(x, k, 0), 0.0))
        dx_ref[...] = dx.astype(dx_ref.dtype)
        db = _colsum(dpre)
        first = pl.program_id(1) == 0

        @pl.when(first)
        def _():
            for k in range(4):
                dw_ref[k:k + 1, :] = dws[k]
            db_ref[...] = db

        @pl.when(jnp.logical_not(first))
        def _():
            for k in range(4):
                dw_ref[k:k + 1, :] += dws[k]
            db_ref[...] += db

    return pl.pallas_call(
        kern, name=name, grid=(2048 // CONV_W, NB),
        in_specs=[pl.BlockSpec((SEQ, CONV_W), lambda j, s: (s, XBC0 // CONV_W + j)),
                  pl.BlockSpec((4, CONV_W), lambda j, s: (0, j)), pl.BlockSpec((1, CONV_W), lambda j, s: (0, j)),
                  pl.BlockSpec((SEQ, CONV_W), lambda j, s: (s, j))],
        out_specs=[pl.BlockSpec((SEQ, CONV_W), lambda j, s: (s, j)), pl.BlockSpec((4, CONV_W), lambda j, s: (0, j)),
                   pl.BlockSpec((1, CONV_W), lambda j, s: (0, j))],
        out_shape=[jax.ShapeDtypeStruct((T, 2048), BF16), jax.ShapeDtypeStruct((4, 2048), F32),
                   jax.ShapeDtypeStruct((1, 2048), F32)],
        compiler_params=_cp(("arbitrary", "arbitrary")),
    )(hcat, w, b, dy)


def _softplus(v):
    return jnp.maximum(v, 0.0) + jnp.log1p(jnp.exp(-jnp.abs(v)))


def _dot(a, b, dims):
    return lax.dot_general(a, b, (dims, ((), ())), preferred_element_type=F32)


_NN, _NT, _TN = ((1,), (0,)), ((1,), (1,)), ((0,), (0,))


def _ssd_specs(order):
    def spec(width, col0):
        return pl.BlockSpec((SEQ, width), lambda i, j, c0=col0: (order(i, j)[0], c0 + order(i, j)[1]))

    def par():
        return pl.BlockSpec((1, 256), lambda i, j: (0, order(i, j)[1]))

    return spec, par


def _ssd_chunk_common(c, xs_ref, b_ref, c_ref, dtx_ref, dtb, a, trif):
    r0 = pl.multiple_of(c * CHUNK, CHUNK)
    rows = pl.ds(r0, CHUNK)
    x = xs_ref[rows, :]
    bb = b_ref[rows, :].astype(BF16)
    cb = c_ref[rows, :].astype(BF16)
    raw = dtx_ref[rows, :] + dtb
    dt = _softplus(raw)
    cs = jnp.dot(trif, dt * a, precision=HIGHEST, preferred_element_type=F32)
    return rows, x, bb, cb, raw, dt, cs


def _head_decay(cs, me, tri):
    cse = jnp.max(jnp.where(me, cs, -jnp.inf), axis=1, keepdims=True)
    csb = jnp.broadcast_to(cse, (CHUNK, CHUNK))
    return jnp.where(tri, jnp.exp(csb - csb.T), 0.0)


def _ssd_fwd(xa, dtx, dtb, alog, dsk, *, carry=None, name):
    spec, par = _ssd_specs(lambda g, b: (b, g))
    nc = carry.n if carry is not None else 0

    def kern(*refs):
        xs_ref, b_ref, c_ref, dtx_ref, dtb_ref, alog_ref, dsk_ref = refs[:7]
        x_refs, (y_ref, hs_ref), g_refs, h_scr = refs[7:7 + nc], refs[7 + nc:9 + nc], refs[9 + nc:9 + 2 * nc], refs[9 + 2 * nc]
        if nc:
            carried = _Carried(carry, x_refs, g_refs, refs[10 + 2 * nc:], pl.program_id(0) * NB + pl.program_id(1), 4 * NB)
            carried.before()
        row = lax.broadcasted_iota(jnp.int32, (CHUNK, CHUNK), 0)
        col = lax.broadcasted_iota(jnp.int32, (CHUNK, CHUNK), 1)
        tri = row >= col
        trif = tri.astype(F32)
        head = lax.broadcasted_iota(jnp.int32, (1, 256), 1) // 64
        dtb_, a, dsk_ = dtb_ref[...], -jnp.exp(alog_ref[...]), dsk_ref[...]
        h_scr[...] = jnp.zeros_like(h_scr)

        def chunk(c, carry):
            rows, x, bb, cb, _, dt, cs = _ssd_chunk_common(c, xs_ref, b_ref, c_ref, dtx_ref, dtb_, a, trif)
            cs_last = cs[CHUNK - 1:CHUNK, :]
            xdt = x * dt
            xdtb = xdt.astype(BF16)
            g = _dot(cb, bb, _NT)
            hin = h_scr[...]
            hs_ref[rows, :] = hin
            y = jnp.exp(cs) * _dot(cb, hin.astype(BF16), _NN) + dsk_ * x
            for e in range(4):
                me = head == e
                m = (g * _head_decay(cs, me, tri)).astype(BF16)
                y = y + jnp.where(me, _dot(m, xdtb, _NN), 0.0)
            y_ref[rows, :] = y
            st = _dot(bb, (jnp.exp(cs_last - cs) * xdt).astype(BF16), _TN)
            h_scr[...] = hin * jnp.exp(cs_last) + st
            return carry

        lax.fori_loop(0, NCHUNK, chunk, 0)
        if nc:
            carried.after()

    res = pl.pallas_call(
        kern, name=name, grid=(4, NB),
        in_specs=[spec(256, 0), spec(128, 8), spec(128, 12), spec(256, 0), par(), par(), par()] + [_ANY] * nc,
        out_specs=[spec(256, 0), spec(256, 0)] + [_ANY] * nc,
        out_shape=[jax.ShapeDtypeStruct((T, D), F32), jax.ShapeDtypeStruct((T, D), F32)] + (carry.out_shapes() if nc else []),
        scratch_shapes=[pltpu.VMEM((CHUNK, 256), F32)] + (carry.sems() if nc else []),
        compiler_params=_cp(("arbitrary", "arbitrary") if nc else ("parallel", "parallel")),
    )(xa, xa, xa, dtx, dtb, alog, dsk, *(carry.arrays if nc else []))
    return res[0], res[1], res[2:]


def _ssd_bwd(xa, dtx, dtb, alog, dsk, hs, dy, *, name):
    spec, par = _ssd_specs(lambda g, b: (b, g))

    def kern(xs_ref, b_ref, c_ref, dtx_ref, dtb_ref, alog_ref, dsk_ref, hs_ref, dy_ref,
             dx_ref, db_ref, dc_ref, ddt_ref, ddtb_ref, dalog_ref, ddsk_ref, dh_scr):
        row = lax.broadcasted_iota(jnp.int32, (CHUNK, CHUNK), 0)
        col = lax.broadcasted_iota(jnp.int32, (CHUNK, CHUNK), 1)
        tri = row >= col
        trif = tri.astype(F32)
        trit = (row <= col).astype(F32)
        is_last = lax.broadcasted_iota(jnp.int32, (CHUNK, 1), 0) == CHUNK - 1
        head = lax.broadcasted_iota(jnp.int32, (1, 256), 1) // 64
        dtb_, a, dsk_ = dtb_ref[...], -jnp.exp(alog_ref[...]), dsk_ref[...]
        dh_scr[...] = jnp.zeros_like(dh_scr)

        @pl.when(pl.program_id(1) == 0)
        def _():
            ddtb_ref[...] = jnp.zeros_like(ddtb_ref)
            dalog_ref[...] = jnp.zeros_like(dalog_ref)
            ddsk_ref[...] = jnp.zeros_like(ddsk_ref)

        def hsum(v, me):
            return jnp.sum(jnp.where(me, v, 0.0), axis=1, keepdims=True)

        def chunk(ci, carry):
            c = NCHUNK - 1 - ci
            rows, x, bb, cb, raw, dt, cs = _ssd_chunk_common(c, xs_ref, b_ref, c_ref, dtx_ref, dtb_, a, trif)
            cs_last = cs[CHUNK - 1:CHUNK, :]
            ecs = jnp.exp(cs)
            dsx = jnp.exp(cs_last - cs)
            xdt = x * dt
            xdtb = xdt.astype(BF16)
            g = _dot(cb, bb, _NT)
            hin = hs_ref[rows, :]
            hinb = hin.astype(BF16)
            dy_ = dy_ref[rows, :]
            dyb = dy_.astype(BF16)
            dh = dh_scr[...]
            dhb = dh.astype(BF16)
            bdh = _dot(bb, dhb, _NN)
            yoff = ecs * _dot(cb, hinb, _NN)
            dxdt = dsx * bdh
            t1, t2, t3 = dy_ * yoff, xdt * bdh, dh * hin
            dg = jnp.zeros((CHUNK, CHUNK), F32)
            dcs = jnp.zeros((CHUNK, 256), F32)
            for e in range(4):
                me = head == e
                l_ = _head_decay(cs, me, tri)
                m = g * l_
                dxdt = dxdt + jnp.where(me, _dot(m.astype(BF16), dyb, _TN), 0.0)
                dm = _dot(jnp.where(me, dy_, 0.0).astype(BF16), xdtb, _NT)
                dg = dg + dm * l_
                w = dm * m
                dds = hsum(t2, me)
                dse = jnp.max(jnp.where(me, dsx, -jnp.inf), axis=1, keepdims=True)
                ecl = jnp.max(jnp.where(me, jnp.exp(cs_last), -jnp.inf), axis=1, keepdims=True)
                dcs_e = hsum(t1, me) + jnp.sum(w, axis=1, keepdims=True) - jnp.sum(w.T, axis=1, keepdims=True) - dds * dse
                last = jnp.sum(dds * dse, axis=0, keepdims=True) + ecl * jnp.sum(hsum(t3, me), axis=0, keepdims=True)
                dcs_e = dcs_e + jnp.where(is_last, last, 0.0)
                dcs = dcs + jnp.where(me, dcs_e, 0.0)
            dadt = jnp.dot(trit, dcs, precision=HIGHEST, preferred_element_type=F32)
            ddt = a * dadt
            dxx = dxdt * x
            dyx = dy_ * x
            dsk_acc = jnp.zeros((1, 256), F32)
            for e in range(4):
                me = head == e
                ddt = ddt + jnp.where(me, hsum(dxx, me), 0.0)
                dsk_acc = dsk_acc + jnp.where(me, jnp.sum(hsum(dyx, me), axis=0, keepdims=True), 0.0)
            draw = ddt * jax.nn.sigmoid(raw)
            ddt_ref[rows, :] = draw
            ddtb_ref[...] += _colsum(draw)
            dalog_ref[...] += _colsum(dadt * dt) * a
            ddsk_ref[...] += dsk_acc
            dx_ref[rows, :] = dxdt * dt + dsk_ * dy_
            edy = (ecs * dy_).astype(BF16)
            dgb = dg.astype(BF16)
            dc_ref[rows, :] = _dot(dgb, bb, _NN) + _dot(edy, hinb, _NT)
            db_ref[rows, :] = _dot(dgb, cb, _TN) + _dot((dsx * xdt).astype(BF16), dhb, _NT)
            dh_scr[...] = jnp.exp(cs_last) * dh + _dot(cb, edy, _TN)
            return carry

        lax.fori_loop(0, NCHUNK, chunk, 0)

    return pl.pallas_call(
        kern, name=name, grid=(4, NB),
        in_specs=[spec(256, 0), spec(128, 8), spec(128, 12), spec(256, 0), par(), par(), par(), spec(256, 0), spec(256, 0)],
        out_specs=[spec(256, 0), spec(128, 0), spec(128, 0), spec(256, 0), par(), par(), par()],
        out_shape=[jax.ShapeDtypeStruct((T, D), F32), jax.ShapeDtypeStruct((T, 512), F32), jax.ShapeDtypeStruct((T, 512), F32),
                   jax.ShapeDtypeStruct((T, D), F32), jax.ShapeDtypeStruct((1, D), F32), jax.ShapeDtypeStruct((1, D), F32),
                   jax.ShapeDtypeStruct((1, D), F32)],
        scratch_shapes=[pltpu.VMEM((CHUNK, 256), F32)],
        compiler_params=_cp(("arbitrary", "arbitrary")),
    )(xa, xa, xa, dtx, dtb, alog, dsk, hs, dy)


def _t5_bucket_np(dist):
    dist = np.maximum(dist, 0)
    max_exact = REL_BUCKETS // 2
    large = max_exact + (np.log(np.maximum(dist, 1) / max_exact) / np.log(REL_MAX_DIST / max_exact)
                         * (REL_BUCKETS - max_exact)).astype(np.int32)
    large = np.minimum(large, REL_BUCKETS - 1)
    return np.where(dist < max_exact, dist, large).astype(np.int32)


def _attn_bias(rel_bias):
    qi = np.arange(ATTN_BLK)[:, None]
    kj = np.arange(2 * ATTN_BLK)[None, :]
    delta = qi - kj + ATTN_BLK
    out = []
    for gi, (window, dil) in enumerate(ATTN_CONFIGS):
        in_band = (delta >= 0) & (delta <= window // dil)
        bucket = jnp.asarray(_t5_bucket_np(delta * dil).reshape(-1, 1))
        one_hot = (bucket == jnp.arange(REL_BUCKETS)[None, :]).astype(F32)
        tab = jnp.dot(one_hot, rel_bias[:, 4 * gi:4 * gi + 4], precision=HIGHEST).reshape(ATTN_BLK, 2 * ATTN_BLK, 4)
        out.append(jnp.where(jnp.asarray(in_band)[None], tab.transpose(2, 0, 1), NEG))
    return jnp.stack(out)


def _attn_band(ref, cur, prev):
    return jnp.concatenate([ref[prev, :], ref[cur, :]], axis=0).astype(BF16)


def _attn_head_masks():
    hi = lax.broadcasted_iota(jnp.int32, (1, 128), 1) >= 64
    return [jnp.logical_not(hi), hi]


def _attn_logits(qm, kband, bias, n, sub_blocks):
    kj = lax.broadcasted_iota(jnp.int32, (1, 2 * ATTN_BLK), 1)
    ok = jnp.logical_or(n % sub_blocks > 0, kj >= ATTN_BLK)
    return jnp.where(ok, _dot(qm, kband, _NT) * 0.125 + bias, NEG)


def _attn_specs(col0):
    qkv = [pl.BlockSpec((SEQ, 128), lambda p, s, c=col0 + 2 * i: (s, c + p)) for i in range(3)]
    seq = pl.BlockSpec((SEQ, 128), lambda p, s: (s, p))
    tab = pl.BlockSpec((2, ATTN_BLK, 2 * ATTN_BLK), lambda p, s: (p, 0, 0))
    return qkv, seq, tab


def _attn_rows(n):
    cur = pl.ds(pl.multiple_of(n * ATTN_BLK, ATTN_BLK), ATTN_BLK)
    prev = pl.ds(pl.multiple_of(jnp.maximum(n - 1, 0) * ATTN_BLK, ATTN_BLK), ATTN_BLK)
    return cur, prev


def _attn_fwd(qkv, col0, bias, sub_blocks, *, carry=None, name):
    nc = carry.n if carry is not None else 0

    def kern(*refs):
        q_ref, k_ref, v_ref, bias_ref = refs[:4]
        x_refs, (o_ref, l_ref), g_refs = refs[4:4 + nc], refs[4 + nc:6 + nc], refs[6 + nc:6 + 2 * nc]
        if nc:
            carried = _Carried(carry, x_refs, g_refs, refs[6 + 2 * nc:], pl.program_id(0) * NB + pl.program_id(1), 2 * NB)
            carried.before()
        masks = _attn_head_masks()

        def blk(n, carry):
            cur, prev = _attn_rows(n)
            q = q_ref[cur, :]
            kband, vband = _attn_band(k_ref, cur, prev), _attn_band(v_ref, cur, prev)
            out = jnp.zeros((ATTN_BLK, 128), F32)
            lse = jnp.zeros((ATTN_BLK, 128), F32)
            for hh, mk in enumerate(masks):
                logits = _attn_logits(jnp.where(mk, q, 0.0).astype(BF16), kband, bias_ref[hh], n, sub_blocks)
                m = jnp.max(logits, axis=-1, keepdims=True)
                p = jnp.exp(logits - m)
                den = jnp.sum(p, axis=-1, keepdims=True)
                out = jnp.where(mk, _dot((p / den).astype(BF16), vband, _NN), out)
                lse = jnp.where(mk, m + jnp.log(den), lse)
            o_ref[cur, :] = out
            l_ref[cur, :] = lse
            return carry

        lax.fori_loop(0, SEQ // ATTN_BLK, blk, 0)
        if nc:
            carried.after()

    qkv_specs, seq, tab = _attn_specs(col0)
    res = pl.pallas_call(
        kern, name=name, grid=(2, NB), in_specs=qkv_specs + [tab] + [_ANY] * nc,
        out_specs=[seq, seq] + [_ANY] * nc,
        out_shape=[jax.ShapeDtypeStruct((T, 256), F32)] * 2 + (carry.out_shapes() if nc else []),
        scratch_shapes=carry.sems() if nc else [],
        compiler_params=_cp(("arbitrary", "arbitrary") if nc else ("parallel", "parallel")),
    )(qkv, qkv, qkv, bias, *(carry.arrays if nc else []))
    return res[0], res[1], res[2:]


def _attn_bwd(qkv, col0, o, lse, do, dl, bias, sub_blocks, *, name):
    def kern(q_ref, k_ref, v_ref, o_ref, l_ref, do_ref, dl_ref, bias_ref, dq_ref, dk_ref, dv_ref, dbias_ref, dk_acc, dv_acc):
        masks = _attn_head_masks()

        @pl.when(pl.program_id(1) == 0)
        def _():
            dbias_ref[...] = jnp.zeros_like(dbias_ref)

        dk_acc[...] = jnp.zeros_like(dk_acc)
        dv_acc[...] = jnp.zeros_like(dv_acc)

        def blk(n, carry):
            cur, prev = _attn_rows(n)
            q, o_, lse_, do_, dl_ = q_ref[cur, :], o_ref[cur, :], l_ref[cur, :], do_ref[cur, :], dl_ref[cur, :]
            kband, vband = _attn_band(k_ref, cur, prev), _attn_band(v_ref, cur, prev)
            dq = jnp.zeros((ATTN_BLK, 128), F32)
            dkb = jnp.zeros((2 * ATTN_BLK, 128), F32)
            dvb = jnp.zeros((2 * ATTN_BLK, 128), F32)
            for hh, mk in enumerate(masks):
                qm = jnp.where(mk, q, 0.0).astype(BF16)
                dom = jnp.where(mk, do_, 0.0)
                domb = dom.astype(BF16)
                logits = _attn_logits(qm, kband, bias_ref[hh], n, sub_blocks)
                p = jnp.exp(logits - jnp.max(jnp.where(mk, lse_, -jnp.inf), axis=-1, keepdims=True))
                dd = jnp.sum(dom * o_, axis=-1, keepdims=True)
                dls = jnp.sum(jnp.where(mk, dl_, 0.0), axis=-1, keepdims=True)
                ds = p * (_dot(domb, vband, _NT) - dd + dls)
                dbias_ref[hh] += ds
                dsb = (ds * 0.125).astype(BF16)
                dq = jnp.where(mk, _dot(dsb, kband, _NN), dq)
                dkb = dkb + _dot(dsb, qm, _TN)
                dvb = dvb + _dot(p.astype(BF16), domb, _TN)
            dq_ref[cur, :] = dq.astype(dq_ref.dtype)
            dk_acc[prev, :] += dkb[:ATTN_BLK]
            dk_acc[cur, :] += dkb[ATTN_BLK:]
            dv_acc[prev, :] += dvb[:ATTN_BLK]
            dv_acc[cur, :] += dvb[ATTN_BLK:]
            return carry

        lax.fori_loop(0, SEQ // ATTN_BLK, blk, 0)
        dk_ref[...] = dk_acc[...].astype(dk_ref.dtype)
        dv_ref[...] = dv_acc[...].astype(dv_ref.dtype)

    qkv_specs, seq, tab = _attn_specs(col0)
    return pl.pallas_call(
        kern, name=name, grid=(2, NB),
        in_specs=qkv_specs + [seq] * 4 + [tab], out_specs=[seq, seq, seq, tab],
        out_shape=[jax.ShapeDtypeStruct((T, 256), BF16)] * 3 + [jax.ShapeDtypeStruct((4, ATTN_BLK, 2 * ATTN_BLK), F32)],
        scratch_shapes=[pltpu.VMEM((SEQ, 128), F32), pltpu.VMEM((SEQ, 128), F32)],
        compiler_params=_cp(("arbitrary", "arbitrary")),
    )(qkv, qkv, qkv, o, lse, do, dl, bias)


def _to_sub(t, dil):
    if dil == 1:
        return t
    return t.reshape(NB, SEQ // dil, dil, t.shape[-1]).transpose(0, 2, 1, 3).reshape(T, t.shape[-1])


def _from_sub(t, dil):
    if dil == 1:
        return t
    return t.reshape(NB, dil, SEQ // dil, t.shape[-1]).transpose(0, 2, 1, 3).reshape(T, t.shape[-1])


_ANY = pl.BlockSpec(memory_space=pl.ANY)
MESH = pl.DeviceIdType.MESH


def _slot(dev):
    return 4 * dev[0] + 2 * dev[1] + dev[2]


class _GatherSteps:
    def __init__(self, x_refs, out_refs, sems):
        self.x_refs, self.out_refs = x_refs, out_refs
        self.send_sems, self.recv_sems, self.local_sems = sems
        self.n = len(x_refs)
        x, y, c = lax.axis_index("x"), lax.axis_index("y"), lax.axis_index("c")
        self.c, self.me, self.sibling = c, (x, y, c), (x, y, 1 - c)
        self.chips = [(1 - x, y), (x, 1 - y), (1 - x, 1 - y)]

    def copies(self, k, block, to, own=False):
        return [pltpu.make_async_remote_copy(
            src_ref=self.x_refs[a] if own else self.out_refs[a].at[_slot(block)], dst_ref=self.out_refs[a].at[_slot(block)],
            send_sem=self.send_sems.at[k, a], recv_sem=self.recv_sems.at[k, a], device_id=to, device_id_type=MESH)
            for a in range(self.n)]

    def mine(self):
        return [pltpu.make_async_copy(self.x_refs[a], self.out_refs[a].at[_slot(self.me)], self.local_sems.at[a])
                for a in range(self.n)]

    def first(self):
        out = self.copies(0, self.me, self.sibling, own=True)
        for j, chip in enumerate(self.chips):
            out += self.copies(1 + j, self.me, (*chip, self.c), own=True)
        return out

    def passed(self, j):
        return self.copies(4 + j, (*self.chips[j], self.c), self.sibling)

    def start(self):
        for cp in self.mine() + self.first():
            cp.start()

    def pass_on(self):
        for j, chip in enumerate(self.chips):
            for cp in self.copies(1 + j, (*chip, self.c), self.me):
                cp.wait_recv()
            for cp in self.passed(j):
                cp.start()

    def finish(self):
        for cp in self.copies(0, self.sibling, self.me):
            cp.wait_recv()
        for j, chip in enumerate(self.chips):
            for cp in self.copies(4 + j, (*chip, 1 - self.c), self.me):
                cp.wait_recv()
        for cp in self.first() + [cp for j in range(3) for cp in self.passed(j)]:
            cp.wait_send()
        for cp in self.mine():
            cp.wait()


class _ChipSteps:
    def __init__(self, x_refs, out_refs, sems):
        self.x_refs, self.out_refs = x_refs, out_refs
        self.send_sems, self.recv_sems, self.local_sems = sems
        self.n = len(x_refs)
        x, y, self.c = lax.axis_index("x"), lax.axis_index("y"), lax.axis_index("c")
        self.my_chip = 2 * x + y
        self.chips = [(1 - x, y), (x, 1 - y), (1 - x, 1 - y)]

    def mine(self):
        return [pltpu.make_async_copy(self.x_refs[a].at[self.my_chip], self.out_refs[a].at[self.my_chip], self.local_sems.at[a])
                for a in range(self.n)]

    def copies(self, sending):
        return [pltpu.make_async_remote_copy(
            src_ref=self.x_refs[a].at[2 * px + py if sending else self.my_chip],
            dst_ref=self.out_refs[a].at[self.my_chip if sending else 2 * px + py],
            send_sem=self.send_sems.at[k, a], recv_sem=self.recv_sems.at[k, a], device_id=(px, py, self.c), device_id_type=MESH)
            for k, (px, py) in enumerate(self.chips) for a in range(self.n)]

    def start(self):
        for cp in self.mine() + self.copies(True):
            cp.start()

    def pass_on(self):
        pass

    def finish(self):
        for cp in self.copies(False):
            cp.wait_recv()
        for cp in self.copies(True):
            cp.wait_send()
        for cp in self.mine():
            cp.wait()


class _Carry:
    def __init__(self, kind, arrays):
        self.kind, self.arrays, self.n = kind, list(arrays), len(arrays)
        self.k = {"gather": 7, "chips": 3}[kind]

    def out_shapes(self):
        lead = (NDEV,) if self.kind == "gather" else ()
        return [jax.ShapeDtypeStruct((*lead, *a.shape), a.dtype) for a in self.arrays]

    def sems(self):
        return [pltpu.SemaphoreType.DMA((self.k, self.n)), pltpu.SemaphoreType.DMA((self.k, self.n)),
                pltpu.SemaphoreType.DMA((self.n,))]

    def steps(self, x_refs, out_refs, sems):
        return (_GatherSteps if self.kind == "gather" else _ChipSteps)(x_refs, out_refs, sems)

    def standalone(self, name):
        def kern(*refs):
            steps = self.steps(refs[:self.n], refs[self.n:2 * self.n], refs[2 * self.n:])
            steps.start()
            steps.pass_on()
            steps.finish()

        return pl.pallas_call(kern, name=name, in_specs=[_ANY] * self.n, out_specs=[_ANY] * self.n, out_shape=self.out_shapes(),
                              scratch_shapes=self.sems())(*self.arrays)


class _Carried:
    def __init__(self, carry, x_refs, out_refs, sems, step, n_steps):
        self.steps, self.step, self.n_steps = carry.steps(x_refs, out_refs, sems), step, n_steps

    def before(self):
        pl.when(self.step == 0)(self.steps.start)
        pl.when(self.step == self.n_steps // 2)(self.steps.pass_on)

    def after(self):
        pl.when(self.step == self.n_steps - 1)(self.steps.finish)


def _all_gather(shards, *, name):
    return _Carry("gather", shards).standalone(name)


def _exchange_sibling(arrs, *, name):
    n = len(arrs)

    def kern(*refs):
        x_refs, out_refs = refs[:n], refs[n:2 * n]
        send_sems, recv_sems = refs[2 * n:]
        x, y, c = lax.axis_index("x"), lax.axis_index("y"), lax.axis_index("c")
        sends = [pltpu.make_async_remote_copy(
            src_ref=x_refs[a].at[p, 1 - c], dst_ref=out_refs[a].at[p], send_sem=send_sems.at[p, a], recv_sem=recv_sems.at[p, a],
            device_id=(x, y, 1 - c), device_id_type=MESH) for p in range(4) for a in range(n)]
        for cp in sends:
            cp.start()
        for cp in sends:
            cp.wait_recv()
        for cp in sends:
            cp.wait_send()

    return pl.pallas_call(
        kern, name=name, in_specs=[_ANY] * n, out_specs=[_ANY] * n,
        out_shape=[jax.ShapeDtypeStruct((4, *a.shape[2:]), a.dtype) for a in arrs],
        scratch_shapes=[pltpu.SemaphoreType.DMA((4, n)), pltpu.SemaphoreType.DMA((4, n))],
    )(*arrs)


def _exchange_chips(arrs, *, name):
    return _Carry("chips", arrs).standalone(name)


def _add_own_half(full, other, c_idx, *, name):
    _, _, r, c = full.shape
    tile = r if r <= 512 else _pick(r, (512, 256))

    def kern(c_ref, f_ref, o_ref, out_ref):
        out_ref[...] = (f_ref[...].astype(F32) + o_ref[...].astype(F32)).astype(BF16)

    blk = pl.BlockSpec((None, tile, c), lambda p, i, c_ref: (p, i, 0))
    return pl.pallas_call(
        kern, name=name,
        grid_spec=pltpu.PrefetchScalarGridSpec(
            num_scalar_prefetch=1, grid=(4, r // tile),
            in_specs=[pl.BlockSpec((None, None, tile, c), lambda p, i, c_ref: (p, c_ref[0], i, 0)), blk], out_specs=blk),
        out_shape=jax.ShapeDtypeStruct((4, r, c), BF16), compiler_params=_cp(("parallel", "parallel")),
    )(c_idx, full, other)


def _adamw_reduce(parts, w, m, v, *, name):
    ns, r, c = parts[0].shape
    tile = _pick(r, (256, 128, 64, 32, 16, 8))

    def kern(*refs):
        p_refs, (w_ref, m_ref, v_ref, g_ref, d_ref, m2_ref, v2_ref) = refs[:DEPTH], refs[DEPTH:]
        for l in range(DEPTH):
            @pl.when(pl.program_id(0) == l)
            def _(p_ref=p_refs[l]):
                g = p_ref[0].astype(F32)
                for s in range(1, ns):
                    g = g + p_ref[s].astype(F32)
                g_ref[...] = g
                d_ref[...], m2_ref[...], v2_ref[...] = _adam_math(w_ref[...], g, m_ref[...], v_ref[...])

    p_specs = [pl.BlockSpec((ns, tile, c), lambda l_, i, l=l: (0, jnp.where(l_ == l, i, 0), 0)) for l in range(DEPTH)]
    blk = pl.BlockSpec((None, tile, c), lambda l_, i: (l_, i, 0))
    return pl.pallas_call(
        kern, name=name, grid=(DEPTH, r // tile), in_specs=p_specs + [blk, blk, blk],
        out_specs=[blk] * 4, out_shape=[jax.ShapeDtypeStruct((DEPTH, r, c), F32)] * 4,
        compiler_params=_cp(("arbitrary", "arbitrary")),
    )(*parts, w, m, v)


def _sum_slots(parts, *, name):
    ns, r, c = parts.shape
    tile = _pick(r, (256, 128, 64, 32, 16, 8))

    def kern(p_ref, o_ref):
        acc = p_ref[0].astype(F32)
        for d in range(1, ns):
            acc = acc + p_ref[d].astype(F32)
        o_ref[...] = acc

    return pl.pallas_call(
        kern, name=name, grid=(r // tile,), in_specs=[pl.BlockSpec((ns, tile, c), lambda i: (0, i, 0))],
        out_specs=pl.BlockSpec((tile, c), lambda i: (i, 0)), out_shape=jax.ShapeDtypeStruct((r, c), F32),
        compiler_params=_cp(("parallel",)),
    )(parts)


BIG = [
    ("ffn1_w13", (1024, 704), 1), ("ffn1_w2", (352, 1024), 0), ("w_in", (1024, 1154), 1), ("p_pool", (768, 128), 1),
    ("p_ssd", (128, 1024), 0), ("p_attn", (256, 128), 1), ("w_out", (128, 1024), 0), ("ffn2_w13", (1024, 704), 1),
    ("ffn2_w2", (352, 1024), 0),
]
BLOCKED = ("ffn1_w13", "ffn2_w13")


def _from_blocks(blk, ax):
    _, r, c = blk.shape
    return blk.transpose(1, 0, 2).reshape(r, NDEV * c) if ax == 1 else blk.reshape(NDEV * r, c)


def _to_blocks(full, ax):
    r, c = full.shape
    return full.reshape(r, NDEV, c // NDEV).transpose(1, 0, 2) if ax == 1 else full.reshape(NDEV, r // NDEV, c)


class _SmallPack:
    def __init__(self, shapes):
        self.shapes = shapes
        self.rows = [-(-int(np.prod(s)) // 128) for s in shapes]
        self.total = -(-sum(self.rows) // 256) * 256

    def pack(self, arrs):
        parts = []
        for a, s, r in zip(arrs, self.shapes, self.rows):
            assert tuple(a.shape) == tuple(s), (a.shape, s)
            flat = a.astype(F32).reshape(-1)
            parts.append(jnp.pad(flat, (0, r * 128 - flat.shape[0])).reshape(r, 128))
        parts.append(jnp.zeros((self.total - sum(self.rows), 128), F32))
        return jnp.concatenate(parts, axis=0)

    def unpack(self, packed):
        out, off = [], 0
        for s, r in zip(self.shapes, self.rows):
            out.append(packed[off:off + r].reshape(-1)[:int(np.prod(s))].reshape(s))
            off += r
        return out


W_IN_SEGMENTS = ([(O_Z, O_XBC), (O_XBC, O_DT), (O_G, O_END)] + [(o + 256 * gi, o + 256 * gi + 256) for gi in range(3)
                                                              for o in (O_Q, O_K, O_V)] + [(O_U, O_Z), (O_DT, O_Q)])
W_IN_SHARD = O_END // NDEV


def _w_in_to_padded(w):
    z = jnp.zeros((*w.shape[:-1], HC - DT0 - 16), w.dtype)
    return jnp.concatenate([w[..., a:b] for a, b in W_IN_SEGMENTS] + [z], axis=-1)


def _w_in_blocks_to_padded(blocks):
    pieces = []
    for a, b in W_IN_SEGMENTS:
        for d in range(a // W_IN_SHARD, (b - 1) // W_IN_SHARD + 1):
            pieces.append(blocks[d][:, max(a - W_IN_SHARD * d, 0):min(b - W_IN_SHARD * d, W_IN_SHARD)])
    pieces.append(jnp.zeros((blocks.shape[1], HC - DT0 - 16), blocks.dtype))
    return jnp.concatenate(pieces, axis=1)


def _w_in_padded_to_blocks(g):
    starts = np.cumsum([0] + [b - a for a, b in W_IN_SEGMENTS])
    by_origin = sorted(zip(W_IN_SEGMENTS, starts), key=lambda t: t[0][0])
    blocks = []
    for d in range(NDEV):
        lo_d, hi_d = W_IN_SHARD * d, W_IN_SHARD * (d + 1)
        pieces = [g[:, int(s) + max(a, lo_d) - a:int(s) + min(b, hi_d) - a] for (a, b), s in by_origin if max(a, lo_d) < min(b, hi_d)]
        blocks.append(jnp.concatenate(pieces, axis=1))
    return jnp.stack(blocks)


def _block_diag(pw):
    out = jnp.zeros((POOL_W, POOL_W), pw.dtype)
    for gi in range(4):
        out = lax.dynamic_update_slice(out, pw[gi], (gi * POOL_GDIM, gi * POOL_GDIM))
    return out


def _expand_heads(v):
    return jnp.repeat(v, 64).reshape(1, D)


def _ffn_fwd(x, x16, w13, w2, g, b, tag, carry_up=(), carry_f=()):
    act, act_t, h16, got_up = _ffn_up(x16, w13, carry=_Carry("gather", carry_up) if carry_up else None, name=f"{tag}_up")
    f = _mm(act, w2, carry=_Carry("gather", carry_f) if carry_f else None, name=f"{tag}_f")
    f, got_f = f if carry_f else (f, [])
    y, y16, y16t = _ln_fwd(x, f, g, b, scale=FFN_RES, name=f"{tag}_ln")
    return y, y16, y16t, (h16, act_t, f), list(got_up) + list(got_f)


def _ffn_bwd(x, x16t, w13, w2, g, b, saved, dy, tag, carry_dx=(), carry_dw13=()):
    h16, act_t, f = saved
    dres, df, dg, db = _ln_bwd(x, f, dy, g, b, scale=FFN_RES, name=f"{tag}_ln_bwd")
    dw2 = _mm(act_t, df, out_dtype=BF16, name=f"{tag}_dw2")
    dh = _ffn_down_bwd(df, w2, h16, name=f"{tag}_dh")
    dx = _mm(dh, w13, tb=True, add=dres, carry=_Carry("chips", carry_dx) if carry_dx else None, name=f"{tag}_dx")
    dx, got_dx = dx if carry_dx else (dx, [])
    dw13 = _mm(x16t, dh, out_blocked=True, out_dtype=BF16, carry=_Carry("chips", carry_dw13) if carry_dw13 else None,
               name=f"{tag}_dw13")
    dw13, got_dw13 = dw13 if carry_dw13 else (dw13, [])
    return dx, dw13, dw2, dg[0], db[0], list(got_dx) + list(got_dw13)


CARRIERS = {"ffn1_up": ["ffn1_w2"], "mix_in": ["ffn1_w13"], "ssd": ["w_in"], "attn0": ["ffn2_w2"],
            "attn1": ["p_pool", "p_ssd", "p_attn", "w_out"], "attn2": [("ffn2_w13", 0)], "ffn2_up": [("ffn2_w13", 1)]}


def _layer_fwd(x0, x0_16, x0_16t, p, bias, nxt=None):
    def piece(spec):
        if isinstance(spec, tuple):
            half = nxt[spec[0]].shape[0] // 2
            return nxt[spec[0]][half * spec[1]:half * (spec[1] + 1)]
        return nxt[spec]

    send = {k: ([piece(s) for s in specs] if nxt is not None else []) for k, specs in CARRIERS.items()}
    gather = lambda k: _Carry("gather", send[k]) if nxt is not None else None
    got = {}
    x1, x1_16, x1_16t, s1, g1 = _ffn_fwd(x0, x0_16, p["ffn1_w13"], p["ffn1_w2"], p["ln1_g"], p["ln1_b"], "ffn1", send["ffn1_up"])
    got.update(zip(CARRIERS["ffn1_up"], g1))
    hcat = _mm(x1_16, p["w_in"], carry=gather("mix_in"), name="mix_in")
    if nxt is not None:
        hcat, g_in = hcat
        got.update(zip(CARRIERS["mix_in"], g_in))
    pooled, pooled_t = _pool_fwd(hcat, name="pool")
    ya_lin = _mm(pooled, p["pool_bd"], name="pool_lin")
    ya_pre, ya_pre_t = _affine_fwd(ya_lin, p["pool_b"], p["pool_scale"], name="pool_affine")
    ya = _mm(ya_pre, p["p_pool"], name="pool_out")
    xa = _conv_fwd(hcat, p["conv_w"], p["conv_b"], name="conv")
    dtx = jnp.repeat(hcat[:, DT0:DT0 + SSD_HEADS], 64, axis=1)
    ysc, hs, g_ssd = _ssd_fwd(xa, dtx, p["dt_bias"], p["a_log"], p["d_skip"], carry=gather("ssd"), name="ssd")
    got.update(zip(CARRIERS["ssd"], g_ssd))
    yb_pre, yb_pre_t = _gnorm_fwd(ysc, hcat, p["ssd_norm"], name="ssd_norm")
    yb = _mm(yb_pre, p["p_ssd"], name="ssd_out")
    outs, lses, attn_sv = [], [], []
    for gi, (_, dil) in enumerate(ATTN_CONFIGS):
        c0 = QKV0 + 768 * gi
        qkv, col0 = (hcat, c0 // 128) if dil == 1 else (_to_sub(hcat[:, c0:c0 + 768], dil), 0)
        o, l, g_at = _attn_fwd(qkv, col0, bias[gi], SEQ // dil // ATTN_BLK, carry=gather(f"attn{gi}"), name=f"attn{gi}")
        got.update(zip(CARRIERS[f"attn{gi}"], g_at))
        attn_sv.append((None if dil == 1 else qkv, o, l))
        outs.append(_from_sub(o, dil))
        lses.append(_from_sub(l, dil))
    yc_pre, yc_pre_t = _amerge_fwd(outs, lses, name="attn_merge")
    yc = _mm(yc_pre, p["p_attn"], name="attn_out")
    merged, merged_t = _merge_fwd(hcat, ya, yb, yc, p["gate_b"], name="merge")
    mix = _mm(merged, p["w_out"], name="mix_out")
    x2, x2_16, x2_16t = _ln_fwd(x1, mix, p["ln2_g"], p["ln2_b"], scale=1.0, name="ln2")
    x3, x3_16, x3_16t, s3, g3 = _ffn_fwd(x2, x2_16, p["ffn2_w13"], p["ffn2_w2"], p["ln3_g"], p["ln3_b"], "ffn2", send["ffn2_up"])
    got.update(zip(CARRIERS["ffn2_up"], g3))
    for n in {s[0] for s in got if isinstance(s, tuple)}:
        got[n] = jnp.concatenate([got.pop((n, 0)), got.pop((n, 1))], axis=1)
    saved = dict(x0=x0, x0_16t=x0_16t, s1=s1, x1=x1, x1_16t=x1_16t, hcat=hcat, pooled_t=pooled_t, ya_lin=ya_lin, ya_pre_t=ya_pre_t,
                 ya=ya, xa=xa, hs=hs, ysc=ysc, yb_pre_t=yb_pre_t, yb=yb, outs=outs, lses=lses, attn_sv=attn_sv, yc_pre_t=yc_pre_t,
                 yc=yc, merged_t=merged_t, mix=mix, x2=x2, x2_16t=x2_16t, s3=s3)
    return x3, x3_16, x3_16t, saved, got


BWD_CARRIERS = {"ffn2_dx": ["w_in"], "ffn2_dw13": ["ffn1_w13"], "mix_in_dx": ["ffn2_w13"], "mix_in_dw": ["ffn1_w2", "ffn2_w2"],
                "ffn1_dx": ["p_pool", "p_ssd", "p_attn", "w_out"]}


def _layer_bwd(p, bias, sv, dx3, pending=None):
    send = {k: ([pending[n] for n in names] if pending is not None else []) for k, names in BWD_CARRIERS.items()}
    done = {}
    g = {}
    dx2, g["ffn2_w13"], g["ffn2_w2"], g["ln3_g"], g["ln3_b"], got = _ffn_bwd(
        sv["x2"], sv["x2_16t"], p["ffn2_w13"], p["ffn2_w2"], p["ln3_g"], p["ln3_b"], sv["s3"], dx3, "ffn2",
        send["ffn2_dx"], send["ffn2_dw13"])
    done.update(zip(BWD_CARRIERS["ffn2_dx"] + BWD_CARRIERS["ffn2_dw13"], got))
    hcat = sv["hcat"]
    dres, dmix, dg2, db2 = _ln_bwd(sv["x1"], sv["mix"], dx2, p["ln2_g"], p["ln2_b"], scale=1.0, name="ln2_bwd")
    g["ln2_g"], g["ln2_b"] = dg2[0], db2[0]
    dmerged = _mm(dmix, p["w_out"], tb=True, name="mix_out_dx")
    g["w_out"] = _mm(sv["merged_t"], dmix, out_dtype=BF16, name="mix_out_dw")
    dg0, dg1, dg2_, dya, dyb, dyc, gb0, gb1, gb2 = _merge_bwd(hcat, sv["ya"], sv["yb"], sv["yc"], dmerged, p["gate_b"],
                                                               name="merge_bwd")
    g["gate_b"] = jnp.concatenate([gb0, gb1, gb2], axis=0)
    dya_pre = _mm(dya, p["p_pool"], tb=True, name="pool_out_dx")
    g["p_pool"] = _mm(sv["ya_pre_t"], dya, out_dtype=BF16, name="pool_out_dw")
    dya_lin, dpb, dps = _affine_bwd(sv["ya_lin"], dya_pre, p["pool_b"], p["pool_scale"], name="pool_affine_bwd")
    g["pool_b"], g["pool_scale"] = dpb[0].reshape(4, POOL_GDIM), dps[0]
    dpooled = _mm(dya_lin, p["pool_bd"], tb=True, name="pool_lin_dx")
    dbd = _mm(sv["pooled_t"], dya_lin, name="pool_lin_dw")
    g["pool_w"] = jnp.stack([dbd[i * POOL_GDIM:(i + 1) * POOL_GDIM, i * POOL_GDIM:(i + 1) * POOL_GDIM] for i in range(4)])
    du = _pool_bwd(dpooled, name="pool_bwd")
    dyb_pre = _mm(dyb, p["p_ssd"], tb=True, name="ssd_out_dx")
    g["p_ssd"] = _mm(sv["yb_pre_t"], dyb, out_dtype=BF16, name="ssd_out_dw")
    dysc, dz, dnw = _gnorm_bwd(sv["ysc"], hcat, dyb_pre, p["ssd_norm"], name="ssd_norm_bwd")
    g["ssd_norm"] = dnw[0]
    dtx = jnp.repeat(hcat[:, DT0:DT0 + SSD_HEADS], 64, axis=1)
    dxs, dbm, dcm, ddtx, ddtb, dalog, ddsk = _ssd_bwd(sv["xa"], dtx, p["dt_bias"], p["a_log"], p["d_skip"], sv["hs"], dysc,
                                                     name="ssd_bwd")
    g["dt_bias"], g["a_log"], g["d_skip"] = ddtb[0, ::64], dalog[0, ::64], ddsk[0, ::64]
    dxa = jnp.concatenate([dxs, dbm, dcm], axis=1)
    dxbc, dcw, dcb = _conv_bwd(hcat, p["conv_w"], p["conv_b"], dxa, name="conv_bwd")
    g["conv_w"], g["conv_b"] = dcw, dcb[0]
    ddt = jnp.pad(ddtx[:, ::64], ((0, 0), (0, HC - DT0 - SSD_HEADS))).astype(BF16)
    dyc_pre = _mm(dyc, p["p_attn"], tb=True, name="attn_out_dx")
    g["p_attn"] = _mm(sv["yc_pre_t"], dyc, out_dtype=BF16, name="attn_out_dw")
    am = _amerge_bwd(sv["outs"], sv["lses"], dyc_pre, name="attn_merge_bwd")
    dqkv, dbias = [], []
    for gi, (_, dil) in enumerate(ATTN_CONFIGS):
        qkv_sub, o, l = sv["attn_sv"][gi]
        qkv, col0 = (hcat, (QKV0 + 768 * gi) // 128) if dil == 1 else (qkv_sub, 0)
        dq, dk, dv, dbi = _attn_bwd(qkv, col0, o, l, _to_sub(am[gi], dil), _to_sub(am[3 + gi], dil), bias[gi],
                                    SEQ // dil // ATTN_BLK, name=f"attn{gi}_bwd")
        dqkv += [_from_sub(t, dil) for t in (dq, dk, dv)]
        dbias.append(dbi)
    g["attn_bias"] = jnp.stack(dbias)
    dhcat = jnp.concatenate([dz, dxbc, dg0, dg1, dg2_, *dqkv, du, ddt], axis=1)
    chips = lambda k: _Carry("chips", send[k]) if pending is not None else None
    dx1 = _mm(dhcat, p["w_in"], tb=True, add=dres, carry=chips("mix_in_dx"), name="mix_in_dx")
    g["w_in"] = _mm(sv["x1_16t"], dhcat, out_dtype=BF16, carry=chips("mix_in_dw"), name="mix_in_dw")
    if pending is not None:
        (dx1, got_dx), (g["w_in"], got_dw) = dx1, g["w_in"]
        done.update(zip(BWD_CARRIERS["mix_in_dx"] + BWD_CARRIERS["mix_in_dw"], list(got_dx) + list(got_dw)))
    dx0, g["ffn1_w13"], g["ffn1_w2"], g["ln1_g"], g["ln1_b"], got = _ffn_bwd(
        sv["x0"], sv["x0_16t"], p["ffn1_w13"], p["ffn1_w2"], p["ln1_g"], p["ln1_b"], sv["s1"], dx1, "ffn1", send["ffn1_dx"])
    done.update(zip(BWD_CARRIERS["ffn1_dx"], got))
    return dx0, g, done


def _local_step(x, tgt, lw, rel_bias, shards=None, prepare=None, chip_sums=None):
    bias, bias_vjp = jax.vjp(_attn_bias, rel_bias)
    y, y16, saved, lw = x, x.astype(BF16), [], list(lw)
    y16t = y16.T
    n_layers = len(lw) if shards is None else len(shards)
    for l in range(n_layers):
        nxt = shards[l + 1] if shards is not None and l + 1 < n_layers else None
        y, y16, y16t, sv, got = _layer_fwd(y, y16, y16t, lw[l], bias, nxt)
        saved.append(sv)
        if nxt is not None:
            lw.append(prepare(got, l + 1))
    dy, sq = _loss_kernel(y, tgt, name="loss")
    grads, sums, pending = [None] * n_layers, [None] * n_layers, None
    for l in reversed(range(n_layers)):
        dy, grads[l], done = _layer_bwd(lw[l], bias, saved[l], dy, pending)
        if pending is not None:
            sums[l + 1] = done
        pending = chip_sums(grads[l]) if chip_sums is not None else None
    if pending is not None:
        names = list(pending)
        sums[0] = dict(zip(names, _exchange_chips([pending[n] for n in names], name="exchange_grads_chips")))
    (d_rel,) = bias_vjp(sum(g.pop("attn_bias") for g in grads))
    return sq, dy, grads, d_rel, sums


SMALL_REPL = ["ln1_g", "ln1_b", "pool_w", "pool_b", "pool_scale", "conv_b", "dt_bias", "a_log", "d_skip", "ssd_norm",
              "ln2_g", "ln2_b", "ln3_g", "ln3_b", "rel_bias"]
SMALL_SHARD = ["gate_b", "conv_w"]
WEIGHTS = ['ffn1_w13', 'ffn1_w2', 'ln1_g', 'ln1_b', 'w_in', 'gate_b', 'pool_w', 'pool_b', 'pool_scale', 'conv_w', 'conv_b',
           'dt_bias', 'a_log', 'd_skip', 'ssd_norm', 'rel_bias', 'p_pool', 'p_ssd', 'p_attn', 'w_out', 'ln2_g', 'ln2_b',
           'ffn2_w13', 'ffn2_w2', 'ln3_g', 'ln3_b']


def _step(w, m, v, x, tgt):
    my_c = lax.axis_index("c")
    dev = 4 * lax.axis_index("x") + 2 * lax.axis_index("y") + my_c
    shards = [{n: w[n][l].astype(BF16) for n, _, _ in BIG} for l in range(DEPTH)]
    first = dict(zip([n for n, _, _ in BIG], _all_gather([shards[0][n] for n, _, _ in BIG], name="gather_weights")))
    sp_in = _SmallPack([w[n].shape for n in SMALL_SHARD])
    small_g = _all_gather([sp_in.pack([w[n] for n in SMALL_SHARD])], name="gather_small")[0]
    gate_b = jnp.stack([sp_in.unpack(small_g[d])[0] for d in range(NDEV)], axis=2).reshape(DEPTH, 3, D)
    conv_w = jnp.stack([sp_in.unpack(small_g[d])[1] for d in range(NDEV)], axis=2).reshape(DEPTH, 4, 2048)

    def prepare(blocks, l):
        p = {n: blocks[n] if n in BLOCKED else _from_blocks(blocks[n], ax) for n, _, ax in BIG if n != "w_in"}
        p["w_in"] = _w_in_blocks_to_padded(blocks["w_in"])
        p["pool_bd"] = _block_diag(w["pool_w"][l]).astype(BF16)
        p["gate_b"], p["conv_w"] = gate_b[l], conv_w[l]
        for n in ("ln1_g", "ln1_b", "ln2_g", "ln2_b", "ln3_g", "ln3_b", "pool_scale", "conv_b", "ssd_norm"):
            p[n] = w[n][l][None, :]
        p["pool_b"] = w["pool_b"][l].reshape(1, POOL_W)
        for n in ("dt_bias", "a_log", "d_skip"):
            p[n] = _expand_heads(w[n][l])
        return p

    c_idx = my_c.reshape(1).astype(jnp.int32)

    def chip_sums(g):
        parts = []
        for n, (r, c), ax in BIG:
            blocks = g[n] if n in BLOCKED else _w_in_padded_to_blocks(g[n]) if n == "w_in" else _to_blocks(g[n], ax)
            parts.append(blocks.reshape(4, 2, r, c))
        from_sibling = _exchange_sibling(parts, name="exchange_grads_cores")
        return {n: _add_own_half(p, o, c_idx, name=f"add_cores_{n}") for p, o, (n, _, _) in zip(parts, from_sibling, BIG)}

    sq, dx, grads, d_rel, sums = _local_step(x.reshape(T, D), tgt.reshape(T, D), [prepare(first, 0)], w["rel_bias"], shards,
                                             prepare, chip_sums)
    loss = lax.psum(jnp.sum(sq) * (0.5 / D), ("x", "y", "c"))
    small_names = SMALL_REPL + SMALL_SHARD
    small = {n: (d_rel if n == "rel_bias" else jnp.stack([g[n] for g in grads])) for n in small_names}
    sp = _SmallPack([small[n].shape for n in small_names])
    gsmall = sp.unpack(_sum_slots(_all_gather([sp.pack([small[n] for n in small_names])], name="gather_small_grads")[0],
                                  name="sum_small_grads"))
    gout = {}
    for n, gv in zip(small_names, gsmall):
        if n in SMALL_SHARD:
            width = w[n].shape[-1]
            gv = lax.dynamic_slice_in_dim(gv, dev * width, width, axis=2)
        gout[n] = gv
    delta, new_m, new_v = {}, {}, {}
    for n, _, _ in BIG:
        gout[n], delta[n], new_m[n], new_v[n] = _adamw_reduce([s[n] for s in sums], w[n], m[n], v[n], name=f"adamw_{n}")
    spa = _SmallPack([w[n].shape for n in small_names])
    res = _adamw(spa.pack([w[n] for n in small_names]), spa.pack([gout[n] for n in small_names]),
                 spa.pack([m[n] for n in small_names]), spa.pack([v[n] for n in small_names]), name="adamw_small")
    for out, packed in zip((delta, new_m, new_v), res):
        for n, val in zip(small_names, spa.unpack(packed)):
            out[n] = val
    return (loss, dx.reshape(NB, SEQ, D), *[gout[n] for n in WEIGHTS], *[delta[n] for n in WEIGHTS],
            *[new_m[n] for n in WEIGHTS], *[new_v[n] for n in WEIGHTS])


def kernel(x, ffn1_w13, ffn1_w2, ln1_g, ln1_b, w_in, gate_b, pool_w, pool_b, pool_scale, conv_w, conv_b, dt_bias, a_log, d_skip, ssd_norm, rel_bias, p_pool, p_ssd, p_attn, w_out, ln2_g, ln2_b, ffn2_w13, ffn2_w2, ln3_g, ln3_b, loss_target, m_ffn1_w13, m_ffn1_w2, m_ln1_g, m_ln1_b, m_w_in, m_gate_b, m_pool_w, m_pool_b, m_pool_scale, m_conv_w, m_conv_b, m_dt_bias, m_a_log, m_d_skip, m_ssd_norm, m_rel_bias, m_p_pool, m_p_ssd, m_p_attn, m_w_out, m_ln2_g, m_ln2_b, m_ffn2_w13, m_ffn2_w2, m_ln3_g, m_ln3_b, v_ffn1_w13, v_ffn1_w2, v_ln1_g, v_ln1_b, v_w_in, v_gate_b, v_pool_w, v_pool_b, v_pool_scale, v_conv_w, v_conv_b, v_dt_bias, v_a_log, v_d_skip, v_ssd_norm, v_rel_bias, v_p_pool, v_p_ssd, v_p_attn, v_w_out, v_ln2_g, v_ln2_b, v_ffn2_w13, v_ffn2_w2, v_ln3_g, v_ln3_b):
    given = dict(locals())
    w = {n: given[n] for n in WEIGHTS}
    m = {n: given["m_" + n] for n in WEIGHTS}
    v = {n: given["v_" + n] for n in WEIGHTS}
    return _step(w, m, v, x, loss_target)
```

```python
import functools

import numpy as np
import jax
import jax.numpy as jnp
from jax import lax
from jax.experimental import pallas as pl
from jax.experimental.pallas import tpu as pltpu

F32, BF16 = jnp.float32, jnp.bfloat16
HIGHEST = lax.Precision.HIGHEST

NDEV = 8
DEPTH = 4
D = 1024
SEQ = 2048
NB = 2
T = NB * SEQ
DFF = 2816
POOL_W = 768
POOL_WINDOWS = (2, 4, 8, 16)
POOL_GDIM = 192
SSD_HEADS = 16
CHUNK = 128
NCHUNK = SEQ // CHUNK
ATTN_CONFIGS = ((128, 1), (512, 4), (2048, 16))
ATTN_BLK = 128
REL_BUCKETS = 32
REL_MAX_DIST = 2048
LN_EPS = 1e-5
SSD_EPS = 1e-5
ALPHA = (2.0 * DEPTH) ** 0.25
FFN_RES = 0.5
NEG = -1e30

ADAM_LR, ADAM_B1, ADAM_B2, ADAM_EPS, ADAM_WD, ADAM_STEP = 0.001, 0.9, 0.999, 1e-08, 0.01, 10

Z0, XBC0, GATE0, QKV0, U0, DT0, HC = 0, 1024, 3072, 6144, 8448, 9216, 9728
O_U, O_Z, O_XBC, O_DT, O_Q, O_K, O_V, O_G, O_END = 0, 768, 1792, 3840, 3856, 4624, 5392, 6160, 9232

VMEM_LIMIT = 56 * 1024 * 1024


def _cp(sem):
    return pltpu.CompilerParams(dimension_semantics=sem, vmem_limit_bytes=VMEM_LIMIT)


def _pick(dim, cands):
    for c in cands:
        if dim % c == 0:
            return c
    return dim


def _mm(a, b, *, tb=False, out_dtype=F32, add=None, add_scale=1.0, out_blocked=False, bn=None, b_layer=None, carry=None, name):
    nc = carry.n if carry is not None else 0
    bk = None
    if a.ndim == 3:
        nkb, M, bk = a.shape
        K = nkb * bk
    else:
        M, K = a.shape
    b_blocked = b.ndim >= 3
    b_shape = b.shape if b_layer is None else (b.shape[0], *b.shape[2:])
    if b_blocked and tb:
        assert bk in (None, b_shape[2])
        nkb, N, bk = b_shape
        assert K == nkb * bk, (a.shape, b.shape)
    elif b_blocked:
        nnb, K2, bn = b_shape
        N = nnb * bn
        assert K == K2, (a.shape, b.shape)
    else:
        (N, K2) = b.shape if tb else b.shape[::-1]
        assert K == K2, (a.shape, b.shape, tb)
    bm = _pick(M, (1024, 1408, 768, 512, 256))
    if bn is None:
        bn = _pick(N, (2432, 1024, 768, 512, 256))
    if bk is None:
        bk = K if K <= 1024 else _pick(K, (1024, 1408, 2432, 512, 256))
    nk = K // bk
    dn = (((1,), (1 if tb else 0,)), ((), ()))

    grid = (M // bm, N // bn, nk)

    def kern(*refs):
        a_ref, b_ref = refs[0], refs[1]
        n_in = 2 + (add is not None)
        add_ref = refs[2] if add is not None else None
        x_refs, o_ref, g_refs = refs[n_in:n_in + nc], refs[n_in + nc], refs[n_in + nc + 1:n_in + 2 * nc + 1]
        scratch = refs[n_in + 2 * nc + 1:]
        acc_ref = scratch[0] if nk > 1 else None
        if nc:
            step = (pl.program_id(0) * grid[1] + pl.program_id(1)) * nk + pl.program_id(2)
            carried = _Carried(carry, x_refs, g_refs, scratch[-3:], step, grid[0] * grid[1] * nk)
            carried.before()
        p = lax.dot_general(a_ref[...].astype(BF16), b_ref[...].astype(BF16), dn, preferred_element_type=F32)

        def fin(acc):
            if add_ref is not None:
                acc = acc + add_scale * add_ref[...]
            o_ref[...] = acc.astype(out_dtype)

        if nk == 1:
            fin(p)
        else:
            k = pl.program_id(2)

            @pl.when(k == 0)
            def _():
                acc_ref[...] = p

            @pl.when(k > 0)
            def _():
                acc_ref[...] += p

            @pl.when(k == nk - 1)
            def _():
                fin(acc_ref[...])

        if nc:
            carried.after()

    if a.ndim == 3:
        a_spec = pl.BlockSpec((None, bm, bk), lambda i, j, k: (k, i, 0))
    else:
        a_spec = pl.BlockSpec((bm, bk), lambda i, j, k: (i, k))
    if b_blocked and b_layer is not None:
        b_spec = (pl.BlockSpec((None, None, bn, bk), lambda i, j, k: (k, b_layer, j, 0)) if tb else
                  pl.BlockSpec((None, None, bk, bn), lambda i, j, k: (j, b_layer, k, 0)))
    elif b_blocked and tb:
        b_spec = pl.BlockSpec((None, bn, bk), lambda i, j, k: (k, j, 0))
    elif b_blocked:
        b_spec = pl.BlockSpec((None, bk, bn), lambda i, j, k: (j, k, 0))
    elif tb:
        b_spec = pl.BlockSpec((bn, bk), lambda i, j, k: (j, k))
    else:
        b_spec = pl.BlockSpec((bk, bn), lambda i, j, k: (k, j))
    if out_blocked:
        o_spec, o_shape = pl.BlockSpec((None, bm, bn), lambda i, j, k: (j, i, 0)), (N // bn, M, bn)
    else:
        o_spec, o_shape = pl.BlockSpec((bm, bn), lambda i, j, k: (i, j)), (M, N)
    in_specs = [a_spec, b_spec]
    args = [a, b]
    if add is not None:
        assert add.shape == o_shape
        in_specs.append(o_spec)
        args.append(add)
    res = pl.pallas_call(
        kern, name=name, grid=grid,
        in_specs=in_specs + [_ANY] * nc, out_specs=[o_spec] + [_ANY] * nc,
        out_shape=[jax.ShapeDtypeStruct(o_shape, out_dtype)] + (carry.out_shapes() if nc else []),
        scratch_shapes=([pltpu.VMEM((bm, bn), F32)] if nk > 1 else []) + (carry.sems() if nc else []),
        compiler_params=_cp(("arbitrary",) * 3 if nc else ("parallel", "parallel", "arbitrary")),
    )(*args, *(carry.arrays if nc else []))
    return (res[0], res[1:]) if nc else res[0]


def _rowwise(fn, rows, pars, outs, accs, *, name, tile, groups=1):
    n_rows = rows[0][0].shape[0]
    nt = n_rows // tile
    n_in = len(rows) + len(pars)
    n_out = len(outs)
    transposed = [len(o) == 4 and o[3] for o in outs]

    def kern(*refs):
        res = fn(*[r[...] for r in refs[:n_in]])
        for o_ref, val, tr in zip(refs[n_in:n_in + n_out], res[:n_out], transposed):
            o_ref[...] = (val.astype(F32).T if tr else val).astype(o_ref.dtype)
        i = pl.program_id(1)
        for a_ref, val in zip(refs[n_in + n_out:], res[n_out:]):
            @pl.when(i == 0)
            def _(a_ref=a_ref, val=val):
                a_ref[...] = val

            @pl.when(i > 0)
            def _(a_ref=a_ref, val=val):
                a_ref[...] += val

    in_specs = [pl.BlockSpec((tile, w), lambda g, i, c0=c0: (i, c0 + g)) for (_, w, c0) in rows]
    for p, w in pars:
        if w is None:
            in_specs.append(pl.BlockSpec(p.shape, lambda g, i: (0, 0)))
        else:
            in_specs.append(pl.BlockSpec((p.shape[0], w), lambda g, i: (0, g)))
    out_specs = [pl.BlockSpec((o[1], tile), lambda g, i: (g, i)) if tr else pl.BlockSpec((tile, o[1]), lambda g, i: (i, g))
                 for o, tr in zip(outs, transposed)]
    out_specs += [pl.BlockSpec((1, w), lambda g, i: (0, g)) for (_, w) in accs]
    out_shape = [jax.ShapeDtypeStruct((o[0], n_rows) if tr else (n_rows, o[0]), o[2]) for o, tr in zip(outs, transposed)]
    out_shape += [jax.ShapeDtypeStruct((1, c), F32) for (c, _) in accs]
    return pl.pallas_call(
        kern, name=name, grid=(groups, nt), in_specs=in_specs, out_specs=out_specs, out_shape=out_shape,
        compiler_params=_cp(("arbitrary", "arbitrary")),
    )(*[r[0] for r in rows], *[p[0] for p in pars])


def _colsum(v):
    return jnp.sum(v, axis=0, keepdims=True)


def _silu(v):
    return v * jax.nn.sigmoid(v)


def _ln_fn(x, f, g, b, *, scale):
    pre = ALPHA * x + scale * f
    mu = jnp.mean(pre, axis=-1, keepdims=True)
    var = jnp.mean(jnp.square(pre - mu), axis=-1, keepdims=True)
    return (pre - mu) * lax.rsqrt(var + LN_EPS) * g + b


def _ln_fwd(x, f, g, b, *, scale, name):
    fn = lambda x_, f_, g_, b_: [_ln_fn(x_, f_, g_, b_, scale=scale)] * 3
    return _rowwise(fn, [(x, D, 0), (f, D, 0)], [(g, None), (b, None)], [(D, D, F32), (D, D, BF16), (D, D, BF16, True)], [],
                    name=name, tile=512)


def _ln_bwd(x, f, dy, g, b, *, scale, name):
    def fn(x_, f_, dy_, g_, b_):
        _, vjp = jax.vjp(functools.partial(_ln_fn, scale=scale), x_, f_, g_, b_)
        return list(vjp(dy_))

    return _rowwise(fn, [(x, D, 0), (f, D, 0), (dy, D, 0)], [(g, None), (b, None)],
                    [(D, D, F32), (D, D, BF16)], [(D, D), (D, D)], name=name, tile=512)


FB = 2 * DFF // NDEV
FFN_BM = 1024
_H_PAIR = pl.BlockSpec((2, None, FFN_BM, FB), lambda i, j: (0, j, i, 0))


def _ffn_up(x16, w13, *, carry=None, name):
    nc = carry.n if carry is not None else 0
    grid = (T // FFN_BM, 4)

    def kern(*refs):
        x_ref, w_ref = refs[0], refs[1]
        x_refs, (act_ref, act_t_ref, h_ref), g_refs = refs[2:2 + nc], refs[2 + nc:5 + nc], refs[5 + nc:5 + 2 * nc]
        if nc:
            step = pl.program_id(0) * grid[1] + pl.program_id(1)
            carried = _Carried(carry, x_refs, g_refs, refs[5 + 2 * nc:], step, grid[0] * grid[1])
            carried.before()
        x = x_ref[...]
        ha = jnp.dot(x, w_ref[0], preferred_element_type=F32)
        hg = jnp.dot(x, w_ref[1], preferred_element_type=F32)
        act = _silu(ha) * hg
        act_ref[...] = act.astype(BF16)
        act_t_ref[...] = act.T.astype(BF16)
        h_ref[0] = ha.astype(BF16)
        h_ref[1] = hg.astype(BF16)
        if nc:
            carried.after()

    res = pl.pallas_call(
        kern, name=name, grid=grid,
        in_specs=[pl.BlockSpec((FFN_BM, D), lambda i, j: (i, 0)),
                  pl.BlockSpec((2, None, D, FB), lambda i, j: (0, j, 0, 0))] + [_ANY] * nc,
        out_specs=[pl.BlockSpec((None, FFN_BM, FB), lambda i, j: (j, i, 0)), pl.BlockSpec((None, FB, FFN_BM), lambda i, j: (j, 0, i)),
                   _H_PAIR] + [_ANY] * nc,
        out_shape=[jax.ShapeDtypeStruct((4, T, FB), BF16), jax.ShapeDtypeStruct((4, FB, T), BF16),
                   jax.ShapeDtypeStruct((2, 4, T, FB), BF16)]
        + (carry.out_shapes() if nc else []),
        scratch_shapes=carry.sems() if nc else [],
        compiler_params=_cp(("arbitrary", "arbitrary") if nc else ("parallel", "parallel")),
    )(x16, w13.reshape(2, 4, D, FB), *(carry.arrays if nc else []))
    return res[0], res[1].reshape(DFF, T), res[2].reshape(NDEV, T, FB), res[3:]


def _ffn_down_bwd(df, w2, h16, *, name):
    def kern(df_ref, w_ref, h_ref, dh_ref):
        da = _dot(df_ref[...], w_ref[...], _NT)
        a, g = h_ref[0].astype(F32), h_ref[1].astype(F32)
        s = jax.nn.sigmoid(a)
        dh_ref[0] = (da * g * (s * (1.0 + a * (1.0 - s)))).astype(BF16)
        dh_ref[1] = (da * (a * s)).astype(BF16)

    return pl.pallas_call(
        kern, name=name, grid=(T // FFN_BM, 4),
        in_specs=[pl.BlockSpec((FFN_BM, D), lambda i, j: (i, 0)), pl.BlockSpec((FB, D), lambda i, j: (j, 0)), _H_PAIR],
        out_specs=_H_PAIR, out_shape=jax.ShapeDtypeStruct((2, 4, T, FB), BF16),
        compiler_params=_cp(("parallel", "parallel")),
    )(df, w2, h16.reshape(2, 4, T, FB)).reshape(NDEV, T, FB)


def _affine_fn(y, b, s):
    return (y + b) * s


def _affine_fwd(y, b, s, *, name):
    return _rowwise(lambda y_, b_, s_: [_affine_fn(y_, b_, s_)] * 2, [(y, POOL_W, 0)], [(b, None), (s, None)],
                    [(POOL_W, POOL_W, BF16), (POOL_W, POOL_W, BF16, True)], [], name=name, tile=512)


def _affine_bwd(y, dyo, b, s, *, name):
    def fn(y_, d_, b_, s_):
        _, vjp = jax.vjp(_affine_fn, y_, b_, s_)
        return list(vjp(d_))

    return _rowwise(fn, [(y, POOL_W, 0), (dyo, POOL_W, 0)], [(b, None), (s, None)],
                    [(POOL_W, POOL_W, BF16)], [(POOL_W, POOL_W), (POOL_W, POOL_W)], name=name, tile=512)


def _gnorm_fn(y, z, w):
    yz = y * _silu(z)
    return yz * lax.rsqrt(jnp.mean(jnp.square(yz), axis=-1, keepdims=True) + SSD_EPS) * w


def _gnorm_fwd(y, hcat, w, *, name):
    return _rowwise(lambda y_, z_, w_: [_gnorm_fn(y_, z_, w_)] * 2, [(y, 256, 0), (hcat, 256, Z0 // 256)], [(w, 256)],
                    [(D, 256, BF16), (D, 256, BF16, True)], [], name=name, tile=512, groups=4)


def _gnorm_bwd(y, hcat, dyo, w, *, name):
    def fn(y_, z_, d_, w_):
        _, vjp = jax.vjp(_gnorm_fn, y_, z_, w_)
        return list(vjp(d_))

    return _rowwise(fn, [(y, 256, 0), (hcat, 256, Z0 // 256), (dyo, 256, 0)], [(w, 256)],
                    [(D, 256, F32), (D, 256, BF16)], [(D, 256)], name=name, tile=512, groups=4)


def _merge_fn(g0, g1, g2, ya, yb, yc, b0, b1, b2):
    return jax.nn.sigmoid(g0 + b0) * ya + jax.nn.sigmoid(g1 + b1) * yb + jax.nn.sigmoid(g2 + b2) * yc


def _merge_rows(hcat, ya, yb, yc):
    c = GATE0 // 256
    return [(hcat, 256, c), (hcat, 256, c + 4), (hcat, 256, c + 8), (ya, 256, 0), (yb, 256, 0), (yc, 256, 0)]


def _merge_fwd(hcat, ya, yb, yc, gb, *, name):
    pars = [(gb[0:1], 256), (gb[1:2], 256), (gb[2:3], 256)]
    return _rowwise(lambda *v: [_merge_fn(*v)] * 2, _merge_rows(hcat, ya, yb, yc), pars, [(D, 256, BF16), (D, 256, BF16, True)],
                    [], name=name, tile=512, groups=4)


def _merge_bwd(hcat, ya, yb, yc, dm, gb, *, name):
    def fn(g0, g1, g2, ya_, yb_, yc_, dm_, b0, b1, b2):
        _, vjp = jax.vjp(_merge_fn, g0, g1, g2, ya_, yb_, yc_, b0, b1, b2)
        return list(vjp(dm_))

    pars = [(gb[0:1], 256), (gb[1:2], 256), (gb[2:3], 256)]
    return _rowwise(fn, _merge_rows(hcat, ya, yb, yc) + [(dm, 256, 0)], pars,
                    [(D, 256, BF16)] * 6, [(D, 256)] * 3, name=name, tile=512, groups=4)


def _amerge_fn(o0, o1, o2, l0, l1, l2):
    m = jnp.maximum(jnp.maximum(l0, l1), l2)
    e0, e1, e2 = jnp.exp(l0 - m), jnp.exp(l1 - m), jnp.exp(l2 - m)
    return (e0 * o0 + e1 * o1 + e2 * o2) / (e0 + e1 + e2)


def _amerge_fwd(os_, ls_, *, name):
    rows = [(v, 256, 0) for v in (*os_, *ls_)]
    return _rowwise(lambda *v: [_amerge_fn(*v)] * 2, rows, [], [(256, 256, BF16), (256, 256, BF16, True)], [], name=name, tile=1024)


def _amerge_bwd(os_, ls_, dy, *, name):
    def fn(*v):
        _, vjp = jax.vjp(_amerge_fn, *v[:6])
        return list(vjp(v[6]))

    rows = [(v, 256, 0) for v in (*os_, *ls_, dy)]
    return _rowwise(fn, rows, [], [(256, 256, F32)] * 6, [], name=name, tile=1024)


def _loss_kernel(y, tgt, *, name):
    def fn(y_, t_):
        e = y_ - t_
        return [e * (1.0 / D), _colsum(e * e)]

    return _rowwise(fn, [(y, D, 0), (tgt, D, 0)], [], [(D, D, F32)], [(D, D)], name=name, tile=512)


def _adam_math(w, g, m, v):
    m2 = ADAM_B1 * m + (1.0 - ADAM_B1) * g
    v2 = ADAM_B2 * v + (1.0 - ADAM_B2) * jnp.square(g)
    m_hat = m2 / (1.0 - ADAM_B1 ** ADAM_STEP)
    v_hat = v2 / (1.0 - ADAM_B2 ** ADAM_STEP)
    return -ADAM_LR * (m_hat / (jnp.sqrt(v_hat) + ADAM_EPS) + ADAM_WD * w), m2, v2


def _adamw(w, g, m, v, *, name):
    r, c = w.shape
    tile = _pick(r, (512, 256, 128, 64, 32, 16, 8))
    fn = lambda *a: list(_adam_math(*a))
    return _rowwise(fn, [(w, c, 0), (g, c, 0), (m, c, 0), (v, c, 0)], [], [(c, c, F32)] * 3, [], name=name, tile=tile)


def _pool_lane_window(j, width):
    lane = lax.broadcasted_iota(jnp.int32, (1, width), 1) + j * width
    grp = lane // POOL_GDIM
    return grp


def _pool_select(grp, vals):
    out = vals[3]
    for gi in (2, 1, 0):
        out = jnp.where(grp == gi, vals[gi], out)
    return out


def _pool_fwd(hcat, *, name):
    width = 256

    def kern(u_ref, o_ref, ot_ref):
        u = u_ref[...]
        t = lax.broadcasted_iota(jnp.int32, (SEQ, 1), 0)
        grp = _pool_lane_window(pl.program_id(1), width)

        def shift(v, k):
            return jnp.where(t >= k, pltpu.roll(v, k, 0), 0.0)

        s2 = u + shift(u, 1)
        s4 = s2 + shift(s2, 2)
        s8 = s4 + shift(s4, 4)
        s16 = s8 + shift(s8, 8)
        win = _pool_select(grp, [jnp.full((1, width), float(w), F32) for w in POOL_WINDOWS])
        cnt = jnp.minimum((t + 1).astype(F32), win)
        pooled = _pool_select(grp, [s2, s4, s8, s16]) / cnt - u
        o_ref[...] = pooled.astype(o_ref.dtype)
        ot_ref[...] = pooled.T.astype(ot_ref.dtype)

    return pl.pallas_call(
        kern, name=name, grid=(NB, POOL_W // width),
        in_specs=[pl.BlockSpec((SEQ, width), lambda b, j: (b, U0 // width + j))],
        out_specs=[pl.BlockSpec((SEQ, width), lambda b, j: (b, j)), pl.BlockSpec((width, SEQ), lambda b, j: (j, b))],
        out_shape=[jax.ShapeDtypeStruct((T, POOL_W), BF16), jax.ShapeDtypeStruct((POOL_W, T), BF16)],
        compiler_params=_cp(("parallel", "parallel")),
    )(hcat)


def _pool_bwd(dp, *, name):
    width = 256

    def kern(d_ref, o_ref):
        d = d_ref[...]
        t = lax.broadcasted_iota(jnp.int32, (SEQ, 1), 0)
        grp = _pool_lane_window(pl.program_id(1), width)
        win = _pool_select(grp, [jnp.full((1, width), float(w), F32) for w in POOL_WINDOWS])
        dm = d / jnp.minimum((t + 1).astype(F32), win)

        def shift(v, k):
            return jnp.where(t < SEQ - k, pltpu.roll(v, SEQ - k, 0), 0.0)

        r2 = dm + shift(dm, 1)
        r4 = r2 + shift(r2, 2)
        r8 = r4 + shift(r4, 4)
        r16 = r8 + shift(r8, 8)
        o_ref[...] = (_pool_select(grp, [r2, r4, r8, r16]) - d).astype(o_ref.dtype)

    return pl.pallas_call(
        kern, name=name, grid=(NB, POOL_W // width),
        in_specs=[pl.BlockSpec((SEQ, width), lambda b, j: (b, j))],
        out_specs=pl.BlockSpec((SEQ, width), lambda b, j: (b, j)),
        out_shape=jax.ShapeDtypeStruct((T, POOL_W), BF16), compiler_params=_cp(("parallel", "parallel")),
    )(dp)


CONV_W = 512


def _conv_pre(x, w, b, t):
    pre = w[3:4] * x + b
    for k in (1, 2, 3):
        pre = pre + w[3 - k:4 - k] * jnp.where(t >= k, pltpu.roll(x, k, 0), 0.0)
    return pre


def _conv_fwd(hcat, w, b, *, name):
    def kern(x_ref, w_ref, b_ref, o_ref):
        t = lax.broadcasted_iota(jnp.int32, (SEQ, 1), 0)
        o_ref[...] = _silu(_conv_pre(x_ref[...], w_ref[...], b_ref[...], t))

    return pl.pallas_call(
        kern, name=name, grid=(NB, 2048 // CONV_W),
        in_specs=[pl.BlockSpec((SEQ, CONV_W), lambda s, j: (s, XBC0 // CONV_W + j)),
                  pl.BlockSpec((4, CONV_W), lambda s, j: (0, j)), pl.BlockSpec((1, CONV_W), lambda s, j: (0, j))],
        out_specs=pl.BlockSpec((SEQ, CONV_W), lambda s, j: (s, j)),
        out_shape=jax.ShapeDtypeStruct((T, 2048), F32), compiler_params=_cp(("parallel", "parallel")),
    )(hcat, w, b)


def _conv_bwd(hcat, w, b, dy, *, name):
    def kern(x_ref, w_ref, b_ref, dy_ref, dx_ref, dw_ref, db_ref):
        t = lax.broadcasted_iota(jnp.int32, (SEQ, 1), 0)
        x, w_ = x_ref[...], w_ref[...]
        pre = _conv_pre(x, w_, b_ref[...], t)
        s = jax.nn.sigmoid(pre)
        dpre = dy_ref[...] * (s * (1.0 + pre * (1.0 - s)))
        dx = w_[3:4] * dpre
        dws = [None] * 4
        dws[3] = _colsum(dpre * x)
        for k in (1, 2, 3):
            dx = dx + w_[3 - k:4 - k] * jnp.where(t < SEQ - k, pltpu.roll(dpre, SEQ - k, 0), 0.0)
            dws[3 - k] = _colsum(dpre * jnp.where(t >= k, pltpu.roll(x, k, 0), 0.0))
        dx_ref[...] = dx.astype(dx_ref.dtype)
        db = _colsum(dpre)
        first = pl.program_id(1) == 0

        @pl.when(first)
        def _():
            for k in range(4):
                dw_ref[k:k + 1, :] = dws[k]
            db_ref[...] = db

        @pl.when(jnp.logical_not(first))
        def _():
            for k in range(4):
                dw_ref[k:k + 1, :] += dws[k]
            db_ref[...] += db

    return pl.pallas_call(
        kern, name=name, grid=(2048 // CONV_W, NB),
        in_specs=[pl.BlockSpec((SEQ, CONV_W), lambda j, s: (s, XBC0 // CONV_W + j)),
                  pl.BlockSpec((4, CONV_W), lambda j, s: (0, j)), pl.BlockSpec((1, CONV_W), lambda j, s: (0, j)),
                  pl.BlockSpec((SEQ, CONV_W), lambda j, s: (s, j))],
        out_specs=[pl.BlockSpec((SEQ, CONV_W), lambda j, s: (s, j)), pl.BlockSpec((4, CONV_W), lambda j, s: (0, j)),
                   pl.BlockSpec((1, CONV_W), lambda j, s: (0, j))],
        out_shape=[jax.ShapeDtypeStruct((T, 2048), BF16), jax.ShapeDtypeStruct((4, 2048), F32),
                   jax.ShapeDtypeStruct((1, 2048), F32)],
        compiler_params=_cp(("arbitrary", "arbitrary")),
    )(hcat, w, b, dy)


def _softplus(v):
    return jnp.maximum(v, 0.0) + jnp.log1p(jnp.exp(-jnp.abs(v)))


def _dot(a, b, dims):
    return lax.dot_general(a, b, (dims, ((), ())), preferred_element_type=F32)


_NN, _NT, _TN = ((1,), (0,)), ((1,), (1,)), ((0,), (0,))


def _ssd_specs(order):
    def spec(width, col0):
        return pl.BlockSpec((SEQ, width), lambda i, j, c0=col0: (order(i, j)[0], c0 + order(i, j)[1]))

    def par():
        return pl.BlockSpec((1, 256), lambda i, j: (0, order(i, j)[1]))

    return spec, par


def _ssd_chunk_common(c, xs_ref, b_ref, c_ref, dtx_ref, dtb, a, trif):
    r0 = pl.multiple_of(c * CHUNK, CHUNK)
    rows = pl.ds(r0, CHUNK)
    x = xs_ref[rows, :]
    bb = b_ref[rows, :].astype(BF16)
    cb = c_ref[rows, :].astype(BF16)
    raw = dtx_ref[rows, :] + dtb
    dt = _softplus(raw)
    cs = jnp.dot(trif, dt * a, precision=HIGHEST, preferred_element_type=F32)
    return rows, x, bb, cb, raw, dt, cs


def _head_decay(cs, me, tri):
    cse = jnp.max(jnp.where(me, cs, -jnp.inf), axis=1, keepdims=True)
    csb = jnp.broadcast_to(cse, (CHUNK, CHUNK))
    return jnp.where(tri, jnp.exp(csb - csb.T), 0.0)


def _ssd_fwd(xa, dtx, dtb, alog, dsk, *, carry=None, name):
    spec, par = _ssd_specs(lambda g, b: (b, g))
    nc = carry.n if carry is not None else 0

    def kern(*refs):
        xs_ref, b_ref, c_ref, dtx_ref, dtb_ref, alog_ref, dsk_ref = refs[:7]
        x_refs, (y_ref, hs_ref), g_refs, h_scr = refs[7:7 + nc], refs[7 + nc:9 + nc], refs[9 + nc:9 + 2 * nc], refs[9 + 2 * nc]
        if nc:
            carried = _Carried(carry, x_refs, g_refs, refs[10 + 2 * nc:], pl.program_id(0) * NB + pl.program_id(1), 4 * NB)
            carried.before()
        row = lax.broadcasted_iota(jnp.int32, (CHUNK, CHUNK), 0)
        col = lax.broadcasted_iota(jnp.int32, (CHUNK, CHUNK), 1)
        tri = row >= col
        trif = tri.astype(F32)
        head = lax.broadcasted_iota(jnp.int32, (1, 256), 1) // 64
        dtb_, a, dsk_ = dtb_ref[...], -jnp.exp(alog_ref[...]), dsk_ref[...]
        h_scr[...] = jnp.zeros_like(h_scr)

        def chunk(c, carry):
            rows, x, bb, cb, _, dt, cs = _ssd_chunk_common(c, xs_ref, b_ref, c_ref, dtx_ref, dtb_, a, trif)
            cs_last = cs[CHUNK - 1:CHUNK, :]
            xdt = x * dt
            xdtb = xdt.astype(BF16)
            g = _dot(cb, bb, _NT)
            hin = h_scr[...]
            hs_ref[rows, :] = hin
            y = jnp.exp(cs) * _dot(cb, hin.astype(BF16), _NN) + dsk_ * x
            for e in range(4):
                me = head == e
                m = (g * _head_decay(cs, me, tri)).astype(BF16)
                y = y + jnp.where(me, _dot(m, xdtb, _NN), 0.0)
            y_ref[rows, :] = y
            st = _dot(bb, (jnp.exp(cs_last - cs) * xdt).astype(BF16), _TN)
            h_scr[...] = hin * jnp.exp(cs_last) + st
            return carry

        lax.fori_loop(0, NCHUNK, chunk, 0)
        if nc:
            carried.after()

    res = pl.pallas_call(
        kern, name=name, grid=(4, NB),
        in_specs=[spec(256, 0), spec(128, 8), spec(128, 12), spec(256, 0), par(), par(), par()] + [_ANY] * nc,
        out_specs=[spec(256, 0), spec(256, 0)] + [_ANY] * nc,
        out_shape=[jax.ShapeDtypeStruct((T, D), F32), jax.ShapeDtypeStruct((T, D), F32)] + (carry.out_shapes() if nc else []),
        scratch_shapes=[pltpu.VMEM((CHUNK, 256), F32)] + (carry.sems() if nc else []),
        compiler_params=_cp(("arbitrary", "arbitrary") if nc else ("parallel", "parallel")),
    )(xa, xa, xa, dtx, dtb, alog, dsk, *(carry.arrays if nc else []))
    return res[0], res[1], res[2:]


def _ssd_bwd(xa, dtx, dtb, alog, dsk, hs, dy, *, name):
    spec, par = _ssd_specs(lambda g, b: (b, g))

    def kern(xs_ref, b_ref, c_ref, dtx_ref, dtb_ref, alog_ref, dsk_ref, hs_ref, dy_ref,
             dx_ref, db_ref, dc_ref, ddt_ref, ddtb_ref, dalog_ref, ddsk_ref, dh_scr):
        row = lax.broadcasted_iota(jnp.int32, (CHUNK, CHUNK), 0)
        col = lax.broadcasted_iota(jnp.int32, (CHUNK, CHUNK), 1)
        tri = row >= col
        trif = tri.astype(F32)
        trit = (row <= col).astype(F32)
        is_last = lax.broadcasted_iota(jnp.int32, (CHUNK, 1), 0) == CHUNK - 1
        head = lax.broadcasted_iota(jnp.int32, (1, 256), 1) // 64
        dtb_, a, dsk_ = dtb_ref[...], -jnp.exp(alog_ref[...]), dsk_ref[...]
        dh_scr[...] = jnp.zeros_like(dh_scr)

        @pl.when(pl.program_id(1) == 0)
        def _():
            ddtb_ref[...] = jnp.zeros_like(ddtb_ref)
            dalog_ref[...] = jnp.zeros_like(dalog_ref)
            ddsk_ref[...] = jnp.zeros_like(ddsk_ref)

        def hsum(v, me):
            return jnp.sum(jnp.where(me, v, 0.0), axis=1, keepdims=True)

        def chunk(ci, carry):
            c = NCHUNK - 1 - ci
            rows, x, bb, cb, raw, dt, cs = _ssd_chunk_common(c, xs_ref, b_ref, c_ref, dtx_ref, dtb_, a, trif)
            cs_last = cs[CHUNK - 1:CHUNK, :]
            ecs = jnp.exp(cs)
            dsx = jnp.exp(cs_last - cs)
            xdt = x * dt
            xdtb = xdt.astype(BF16)
            g = _dot(cb, bb, _NT)
            hin = hs_ref[rows, :]
            hinb = hin.astype(BF16)
            dy_ = dy_ref[rows, :]
            dyb = dy_.astype(BF16)
            dh = dh_scr[...]
            dhb = dh.astype(BF16)
            bdh = _dot(bb, dhb, _NN)
            yoff = ecs * _dot(cb, hinb, _NN)
            dxdt = dsx * bdh
            t1, t2, t3 = dy_ * yoff, xdt * bdh, dh * hin
            dg = jnp.zeros((CHUNK, CHUNK), F32)
            dcs = jnp.zeros((CHUNK, 256), F32)
            for e in range(4):
                me = head == e
                l_ = _head_decay(cs, me, tri)
                m = g * l_
                dxdt = dxdt + jnp.where(me, _dot(m.astype(BF16), dyb, _TN), 0.0)
                dm = _dot(jnp.where(me, dy_, 0.0).astype(BF16), xdtb, _NT)
                dg = dg + dm * l_
                w = dm * m
                dds = hsum(t2, me)
                dse = jnp.max(jnp.where(me, dsx, -jnp.inf), axis=1, keepdims=True)
                ecl = jnp.max(jnp.where(me, jnp.exp(cs_last), -jnp.inf), axis=1, keepdims=True)
                dcs_e = hsum(t1, me) + jnp.sum(w, axis=1, keepdims=True) - jnp.sum(w.T, axis=1, keepdims=True) - dds * dse
                last = jnp.sum(dds * dse, axis=0, keepdims=True) + ecl * jnp.sum(hsum(t3, me), axis=0, keepdims=True)
                dcs_e = dcs_e + jnp.where(is_last, last, 0.0)
                dcs = dcs + jnp.where(me, dcs_e, 0.0)
            dadt = jnp.dot(trit, dcs, precision=HIGHEST, preferred_element_type=F32)
            ddt = a * dadt
            dxx = dxdt * x
            dyx = dy_ * x
            dsk_acc = jnp.zeros((1, 256), F32)
            for e in range(4):
                me = head == e
                ddt = ddt + jnp.where(me, hsum(dxx, me), 0.0)
                dsk_acc = dsk_acc + jnp.where(me, jnp.sum(hsum(dyx, me), axis=0, keepdims=True), 0.0)
            draw = ddt * jax.nn.sigmoid(raw)
            ddt_ref[rows, :] = draw
            ddtb_ref[...] += _colsum(draw)
            dalog_ref[...] += _colsum(dadt * dt) * a
            ddsk_ref[...] += dsk_acc
            dx_ref[rows, :] = dxdt * dt + dsk_ * dy_
            edy = (ecs * dy_).astype(BF16)
            dgb = dg.astype(BF16)
            dc_ref[rows, :] = _dot(dgb, bb, _NN) + _dot(edy, hinb, _NT)
            db_ref[rows, :] = _dot(dgb, cb, _TN) + _dot((dsx * xdt).astype(BF16), dhb, _NT)
            dh_scr[...] = jnp.exp(cs_last) * dh + _dot(cb, edy, _TN)
            return carry

        lax.fori_loop(0, NCHUNK, chunk, 0)

    return pl.pallas_call(
        kern, name=name, grid=(4, NB),
        in_specs=[spec(256, 0), spec(128, 8), spec(128, 12), spec(256, 0), par(), par(), par(), spec(256, 0), spec(256, 0)],
        out_specs=[spec(256, 0), spec(128, 0), spec(128, 0), spec(256, 0), par(), par(), par()],
        out_shape=[jax.ShapeDtypeStruct((T, D), F32), jax.ShapeDtypeStruct((T, 512), F32), jax.ShapeDtypeStruct((T, 512), F32),
                   jax.ShapeDtypeStruct((T, D), F32), jax.ShapeDtypeStruct((1, D), F32), jax.ShapeDtypeStruct((1, D), F32),
                   jax.ShapeDtypeStruct((1, D), F32)],
        scratch_shapes=[pltpu.VMEM((CHUNK, 256), F32)],
        compiler_params=_cp(("arbitrary", "arbitrary")),
    )(xa, xa, xa, dtx, dtb, alog, dsk, hs, dy)


def _t5_bucket_np(dist):
    dist = np.maximum(dist, 0)
    max_exact = REL_BUCKETS // 2
    large = max_exact + (np.log(np.maximum(dist, 1) / max_exact) / np.log(REL_MAX_DIST / max_exact)
                         * (REL_BUCKETS - max_exact)).astype(np.int32)
    large = np.minimum(large, REL_BUCKETS - 1)
    return np.where(dist < max_exact, dist, large).astype(np.int32)


def _attn_bias(rel_bias):
    qi = np.arange(ATTN_BLK)[:, None]
    kj = np.arange(2 * ATTN_BLK)[None, :]
    delta = qi - kj + ATTN_BLK
    out = []
    for gi, (window, dil) in enumerate(ATTN_CONFIGS):
        in_band = (delta >= 0) & (delta <= window // dil)
        bucket = jnp.asarray(_t5_bucket_np(delta * dil).reshape(-1, 1))
        one_hot = (bucket == jnp.arange(REL_BUCKETS)[None, :]).astype(F32)
        tab = jnp.dot(one_hot, rel_bias[:, 4 * gi:4 * gi + 4], precision=HIGHEST).reshape(ATTN_BLK, 2 * ATTN_BLK, 4)
        out.append(jnp.where(jnp.asarray(in_band)[None], tab.transpose(2, 0, 1), NEG))
    return jnp.stack(out)


def _attn_band(ref, cur, prev):
    return jnp.concatenate([ref[prev, :], ref[cur, :]], axis=0).astype(BF16)


def _attn_head_masks():
    hi = lax.broadcasted_iota(jnp.int32, (1, 128), 1) >= 64
    return [jnp.logical_not(hi), hi]


def _attn_logits(qm, kband, bias, n, sub_blocks):
    kj = lax.broadcasted_iota(jnp.int32, (1, 2 * ATTN_BLK), 1)
    ok = jnp.logical_or(n % sub_blocks > 0, kj >= ATTN_BLK)
    return jnp.where(ok, _dot(qm, kband, _NT) * 0.125 + bias, NEG)


def _attn_specs(col0):
    qkv = [pl.BlockSpec((SEQ, 128), lambda p, s, c=col0 + 2 * i: (s, c + p)) for i in range(3)]
    seq = pl.BlockSpec((SEQ, 128), lambda p, s: (s, p))
    tab = pl.BlockSpec((2, ATTN_BLK, 2 * ATTN_BLK), lambda p, s: (p, 0, 0))
    return qkv, seq, tab


def _attn_rows(n):
    cur = pl.ds(pl.multiple_of(n * ATTN_BLK, ATTN_BLK), ATTN_BLK)
    prev = pl.ds(pl.multiple_of(jnp.maximum(n - 1, 0) * ATTN_BLK, ATTN_BLK), ATTN_BLK)
    return cur, prev


def _attn_fwd(qkv, col0, bias, sub_blocks, *, carry=None, name):
    nc = carry.n if carry is not None else 0

    def kern(*refs):
        q_ref, k_ref, v_ref, bias_ref = refs[:4]
        x_refs, (o_ref, l_ref), g_refs = refs[4:4 + nc], refs[4 + nc:6 + nc], refs[6 + nc:6 + 2 * nc]
        if nc:
            carried = _Carried(carry, x_refs, g_refs, refs[6 + 2 * nc:], pl.program_id(0) * NB + pl.program_id(1), 2 * NB)
            carried.before()
        masks = _attn_head_masks()

        def blk(n, carry):
            cur, prev = _attn_rows(n)
            q = q_ref[cur, :]
            kband, vband = _attn_band(k_ref, cur, prev), _attn_band(v_ref, cur, prev)
            out = jnp.zeros((ATTN_BLK, 128), F32)
            lse = jnp.zeros((ATTN_BLK, 128), F32)
            for hh, mk in enumerate(masks):
                logits = _attn_logits(jnp.where(mk, q, 0.0).astype(BF16), kband, bias_ref[hh], n, sub_blocks)
                m = jnp.max(logits, axis=-1, keepdims=True)
                p = jnp.exp(logits - m)
                den = jnp.sum(p, axis=-1, keepdims=True)
                out = jnp.where(mk, _dot((p / den).astype(BF16), vband, _NN), out)
                lse = jnp.where(mk, m + jnp.log(den), lse)
            o_ref[cur, :] = out
            l_ref[cur, :] = lse
            return carry

        lax.fori_loop(0, SEQ // ATTN_BLK, blk, 0)
        if nc:
            carried.after()

    qkv_specs, seq, tab = _attn_specs(col0)
    res = pl.pallas_call(
        kern, name=name, grid=(2, NB), in_specs=qkv_specs + [tab] + [_ANY] * nc,
        out_specs=[seq, seq] + [_ANY] * nc,
        out_shape=[jax.ShapeDtypeStruct((T, 256), F32)] * 2 + (carry.out_shapes() if nc else []),
        scratch_shapes=carry.sems() if nc else [],
        compiler_params=_cp(("arbitrary", "arbitrary") if nc else ("parallel", "parallel")),
    )(qkv, qkv, qkv, bias, *(carry.arrays if nc else []))
    return res[0], res[1], res[2:]


def _attn_bwd(qkv, col0, o, lse, do, dl, bias, sub_blocks, *, name):
    def kern(q_ref, k_ref, v_ref, o_ref, l_ref, do_ref, dl_ref, bias_ref, dq_ref, dk_ref, dv_ref, dbias_ref, dk_acc, dv_acc):
        masks = _attn_head_masks()

        @pl.when(pl.program_id(1) == 0)
        def _():
            dbias_ref[...] = jnp.zeros_like(dbias_ref)

        dk_acc[...] = jnp.zeros_like(dk_acc)
        dv_acc[...] = jnp.zeros_like(dv_acc)

        def blk(n, carry):
            cur, prev = _attn_rows(n)
            q, o_, lse_, do_, dl_ = q_ref[cur, :], o_ref[cur, :], l_ref[cur, :], do_ref[cur, :], dl_ref[cur, :]
            kband, vband = _attn_band(k_ref, cur, prev), _attn_band(v_ref, cur, prev)
            dq = jnp.zeros((ATTN_BLK, 128), F32)
            dkb = jnp.zeros((2 * ATTN_BLK, 128), F32)
            dvb = jnp.zeros((2 * ATTN_BLK, 128), F32)
            for hh, mk in enumerate(masks):
                qm = jnp.where(mk, q, 0.0).astype(BF16)
                dom = jnp.where(mk, do_, 0.0)
                domb = dom.astype(BF16)
                logits = _attn_logits(qm, kband, bias_ref[hh], n, sub_blocks)
                p = jnp.exp(logits - jnp.max(jnp.where(mk, lse_, -jnp.inf), axis=-1, keepdims=True))
                dd = jnp.sum(dom * o_, axis=-1, keepdims=True)
                dls = jnp.sum(jnp.where(mk, dl_, 0.0), axis=-1, keepdims=True)
                ds = p * (_dot(domb, vband, _NT) - dd + dls)
                dbias_ref[hh] += ds
                dsb = (ds * 0.125).astype(BF16)
                dq = jnp.where(mk, _dot(dsb, kband, _NN), dq)
                dkb = dkb + _dot(dsb, qm, _TN)
                dvb = dvb + _dot(p.astype(BF16), domb, _TN)
            dq_ref[cur, :] = dq.astype(dq_ref.dtype)
            dk_acc[prev, :] += dkb[:ATTN_BLK]
            dk_acc[cur, :] += dkb[ATTN_BLK:]
            dv_acc[prev, :] += dvb[:ATTN_BLK]
            dv_acc[cur, :] += dvb[ATTN_BLK:]
            return carry

        lax.fori_loop(0, SEQ // ATTN_BLK, blk, 0)
        dk_ref[...] = dk_acc[...].astype(dk_ref.dtype)
        dv_ref[...] = dv_acc[...].astype(dv_ref.dtype)

    qkv_specs, seq, tab = _attn_specs(col0)
    return pl.pallas_call(
        kern, name=name, grid=(2, NB),
        in_specs=qkv_specs + [seq] * 4 + [tab], out_specs=[seq, seq, seq, tab],
        out_shape=[jax.ShapeDtypeStruct((T, 256), BF16)] * 3 + [jax.ShapeDtypeStruct((4, ATTN_BLK, 2 * ATTN_BLK), F32)],
        scratch_shapes=[pltpu.VMEM((SEQ, 128), F32), pltpu.VMEM((SEQ, 128), F32)],
        compiler_params=_cp(("arbitrary", "arbitrary")),
    )(qkv, qkv, qkv, o, lse, do, dl, bias)


def _to_sub(t, dil):
    if dil == 1:
        return t
    return t.reshape(NB, SEQ // dil, dil, t.shape[-1]).transpose(0, 2, 1, 3).reshape(T, t.shape[-1])


def _from_sub(t, dil):
    if dil == 1:
        return t
    return t.reshape(NB, dil, SEQ // dil, t.shape[-1]).transpose(0, 2, 1, 3).reshape(T, t.shape[-1])


_ANY = pl.BlockSpec(memory_space=pl.ANY)
MESH = pl.DeviceIdType.MESH


def _slot(dev):
    return 4 * dev[0] + 2 * dev[1] + dev[2]


class _GatherSteps:
    def __init__(self, x_refs, out_refs, sems):
        self.x_refs, self.out_refs = x_refs, out_refs
        self.send_sems, self.recv_sems, self.local_sems = sems
        self.n = len(x_refs)
        x, y, c = lax.axis_index("x"), lax.axis_index("y"), lax.axis_index("c")
        self.c, self.me, self.sibling = c, (x, y, c), (x, y, 1 - c)
        self.chips = [(1 - x, y), (x, 1 - y), (1 - x, 1 - y)]

    def copies(self, k, block, to, own=False):
        return [pltpu.make_async_remote_copy(
            src_ref=self.x_refs[a] if own else self.out_refs[a].at[_slot(block)], dst_ref=self.out_refs[a].at[_slot(block)],
            send_sem=self.send_sems.at[k, a], recv_sem=self.recv_sems.at[k, a], device_id=to, device_id_type=MESH)
            for a in range(self.n)]

    def mine(self):
        return [pltpu.make_async_copy(self.x_refs[a], self.out_refs[a].at[_slot(self.me)], self.local_sems.at[a])
                for a in range(self.n)]

    def first(self):
        out = self.copies(0, self.me, self.sibling, own=True)
        for j, chip in enumerate(self.chips):
            out += self.copies(1 + j, self.me, (*chip, self.c), own=True)
        return out

    def passed(self, j):
        return self.copies(4 + j, (*self.chips[j], self.c), self.sibling)

    def start(self):
        for cp in self.mine() + self.first():
            cp.start()

    def pass_on(self):
        for j, chip in enumerate(self.chips):
            for cp in self.copies(1 + j, (*chip, self.c), self.me):
                cp.wait_recv()
            for cp in self.passed(j):
                cp.start()

    def finish(self):
        for cp in self.copies(0, self.sibling, self.me):
            cp.wait_recv()
        for j, chip in enumerate(self.chips):
            for cp in self.copies(4 + j, (*chip, 1 - self.c), self.me):
                cp.wait_recv()
        for cp in self.first() + [cp for j in range(3) for cp in self.passed(j)]:
            cp.wait_send()
        for cp in self.mine():
            cp.wait()


class _ChipSteps:
    def __init__(self, x_refs, out_refs, sems):
        self.x_refs, self.out_refs = x_refs, out_refs
        self.send_sems, self.recv_sems, self.local_sems = sems
        self.n = len(x_refs)
        x, y, self.c = lax.axis_index("x"), lax.axis_index("y"), lax.axis_index("c")
        self.my_chip = 2 * x + y
        self.chips = [(1 - x, y), (x, 1 - y), (1 - x, 1 - y)]

    def mine(self):
        return [pltpu.make_async_copy(self.x_refs[a].at[self.my_chip], self.out_refs[a].at[self.my_chip], self.local_sems.at[a])
                for a in range(self.n)]

    def copies(self, sending):
        return [pltpu.make_async_remote_copy(
            src_ref=self.x_refs[a].at[2 * px + py if sending else self.my_chip],
            dst_ref=self.out_refs[a].at[self.my_chip if sending else 2 * px + py],
            send_sem=self.send_sems.at[k, a], recv_sem=self.recv_sems.at[k, a], device_id=(px, py, self.c), device_id_type=MESH)
            for k, (px, py) in enumerate(self.chips) for a in range(self.n)]

    def start(self):
        for cp in self.mine() + self.copies(True):
            cp.start()

    def pass_on(self):
        pass

    def finish(self):
        for cp in self.copies(False):
            cp.wait_recv()
        for cp in self.copies(True):
            cp.wait_send()
        for cp in self.mine():
            cp.wait()


class _Carry:
    def __init__(self, kind, arrays):
        self.kind, self.arrays, self.n = kind, list(arrays), len(arrays)
        self.k = {"gather": 7, "chips": 3}[kind]

    def out_shapes(self):
        lead = (NDEV,) if self.kind == "gather" else ()
        return [jax.ShapeDtypeStruct((*lead, *a.shape), a.dtype) for a in self.arrays]

    def sems(self):
        return [pltpu.SemaphoreType.DMA((self.k, self.n)), pltpu.SemaphoreType.DMA((self.k, self.n)),
                pltpu.SemaphoreType.DMA((self.n,))]

    def steps(self, x_refs, out_refs, sems):
        return (_GatherSteps if self.kind == "gather" else _ChipSteps)(x_refs, out_refs, sems)

    def standalone(self, name):
        def kern(*refs):
            steps = self.steps(refs[:self.n], refs[self.n:2 * self.n], refs[2 * self.n:])
            steps.start()
            steps.pass_on()
            steps.finish()

        return pl.pallas_call(kern, name=name, in_specs=[_ANY] * self.n, out_specs=[_ANY] * self.n, out_shape=self.out_shapes(),
                              scratch_shapes=self.sems())(*self.arrays)


class _Carried:
    def __init__(self, carry, x_refs, out_refs, sems, step, n_steps):
        self.steps, self.step, self.n_steps = carry.steps(x_refs, out_refs, sems), step, n_steps

    def before(self):
        pl.when(self.step == 0)(self.steps.start)
        pl.when(self.step == self.n_steps // 2)(self.steps.pass_on)

    def after(self):
        pl.when(self.step == self.n_steps - 1)(self.steps.finish)


def _all_gather(shards, *, name):
    return _Carry("gather", shards).standalone(name)


def _exchange_sibling(arrs, *, name):
    n = len(arrs)

    def kern(*refs):
        x_refs, out_refs = refs[:n], refs[n:2 * n]
        send_sems, recv_sems = refs[2 * n:]
        x, y, c = lax.axis_index("x"), lax.axis_index("y"), lax.axis_index("c")
        sends = [pltpu.make_async_remote_copy(
            src_ref=x_refs[a].at[p, 1 - c], dst_ref=out_refs[a].at[p], send_sem=send_sems.at[p, a], recv_sem=recv_sems.at[p, a],
            device_id=(x, y, 1 - c), device_id_type=MESH) for p in range(4) for a in range(n)]
        for cp in sends:
            cp.start()
        for cp in sends:
            cp.wait_recv()
        for cp in sends:
            cp.wait_send()

    return pl.pallas_call(
        kern, name=name, in_specs=[_ANY] * n, out_specs=[_ANY] * n,
        out_shape=[jax.ShapeDtypeStruct((4, *a.shape[2:]), a.dtype) for a in arrs],
        scratch_shapes=[pltpu.SemaphoreType.DMA((4, n)), pltpu.SemaphoreType.DMA((4, n))],
    )(*arrs)


def _exchange_chips(arrs, *, name):
    return _Carry("chips", arrs).standalone(name)


def _add_own_half(full, other, c_idx, *, name):
    _, _, r, c = full.shape
    tile = r if r <= 512 else _pick(r, (512, 256))

    def kern(c_ref, f_ref, o_ref, out_ref):
        out_ref[...] = (f_ref[...].astype(F32) + o_ref[...].astype(F32)).astype(BF16)

    blk = pl.BlockSpec((None, tile, c), lambda p, i, c_ref: (p, i, 0))
    return pl.pallas_call(
        kern, name=name,
        grid_spec=pltpu.PrefetchScalarGridSpec(
            num_scalar_prefetch=1, grid=(4, r // tile),
            in_specs=[pl.BlockSpec((None, None, tile, c), lambda p, i, c_ref: (p, c_ref[0], i, 0)), blk], out_specs=blk),
        out_shape=jax.ShapeDtypeStruct((4, r, c), BF16), compiler_params=_cp(("parallel", "parallel")),
    )(c_idx, full, other)


def _adamw_reduce(parts, w, m, v, *, name):
    ns, r, c = parts[0].shape
    tile = _pick(r, (256, 128, 64, 32, 16, 8))

    def kern(*refs):
        p_refs, (w_ref, m_ref, v_ref, g_ref, d_ref, m2_ref, v2_ref) = refs[:DEPTH], refs[DEPTH:]
        for l in range(DEPTH):
            @pl.when(pl.program_id(0) == l)
            def _(p_ref=p_refs[l]):
                g = p_ref[0].astype(F32)
                for s in range(1, ns):
                    g = g + p_ref[s].astype(F32)
                g_ref[...] = g
                d_ref[...], m2_ref[...], v2_ref[...] = _adam_math(w_ref[...], g, m_ref[...], v_ref[...])

    p_specs = [pl.BlockSpec((ns, tile, c), lambda l_, i, l=l: (0, jnp.where(l_ == l, i, 0), 0)) for l in range(DEPTH)]
    blk = pl.BlockSpec((None, tile, c), lambda l_, i: (l_, i, 0))
    return pl.pallas_call(
        kern, name=name, grid=(DEPTH, r // tile), in_specs=p_specs + [blk, blk, blk],
        out_specs=[blk] * 4, out_shape=[jax.ShapeDtypeStruct((DEPTH, r, c), F32)] * 4,
        compiler_params=_cp(("arbitrary", "arbitrary")),
    )(*parts, w, m, v)


def _sum_slots(parts, *, name):
    ns, r, c = parts.shape
    tile = _pick(r, (256, 128, 64, 32, 16, 8))

    def kern(p_ref, o_ref):
        acc = p_ref[0].astype(F32)
        for d in range(1, ns):
            acc = acc + p_ref[d].astype(F32)
        o_ref[...] = acc

    return pl.pallas_call(
        kern, name=name, grid=(r // tile,), in_specs=[pl.BlockSpec((ns, tile, c), lambda i: (0, i, 0))],
        out_specs=pl.BlockSpec((tile, c), lambda i: (i, 0)), out_shape=jax.ShapeDtypeStruct((r, c), F32),
        compiler_params=_cp(("parallel",)),
    )(parts)


BIG = [
    ("ffn1_w13", (1024, 704), 1), ("ffn1_w2", (352, 1024), 0), ("w_in", (1024, 1154), 1), ("p_pool", (768, 128), 1),
    ("p_ssd", (128, 1024), 0), ("p_attn", (256, 128), 1), ("w_out", (128, 1024), 0), ("ffn2_w13", (1024, 704), 1),
    ("ffn2_w2", (352, 1024), 0),
]
BLOCKED = ("ffn1_w13", "ffn2_w13")


def _from_blocks(blk, ax):
    _, r, c = blk.shape
    return blk.transpose(1, 0, 2).reshape(r, NDEV * c) if ax == 1 else blk.reshape(NDEV * r, c)


def _to_blocks(full, ax):
    r, c = full.shape
    return full.reshape(r, NDEV, c // NDEV).transpose(1, 0, 2) if ax == 1 else full.reshape(NDEV, r // NDEV, c)


class _SmallPack:
    def __init__(self, shapes):
        self.shapes = shapes
        self.rows = [-(-int(np.prod(s)) // 128) for s in shapes]
        self.total = -(-sum(self.rows) // 256) * 256

    def pack(self, arrs):
        parts = []
        for a, s, r in zip(arrs, self.shapes, self.rows):
            assert tuple(a.shape) == tuple(s), (a.shape, s)
            flat = a.astype(F32).reshape(-1)
            parts.append(jnp.pad(flat, (0, r * 128 - flat.shape[0])).reshape(r, 128))
        parts.append(jnp.zeros((self.total - sum(self.rows), 128), F32))
        return jnp.concatenate(parts, axis=0)

    def unpack(self, packed):
        out, off = [], 0
        for s, r in zip(self.shapes, self.rows):
            out.append(packed[off:off + r].reshape(-1)[:int(np.prod(s))].reshape(s))
            off += r
        return out


W_IN_SEGMENTS = ([(O_Z, O_XBC), (O_XBC, O_DT), (O_G, O_END)] + [(o + 256 * gi, o + 256 * gi + 256) for gi in range(3)
                                                              for o in (O_Q, O_K, O_V)] + [(O_U, O_Z), (O_DT, O_Q)])
W_IN_SHARD = O_END // NDEV


def _w_in_to_padded(w):
    z = jnp.zeros((*w.shape[:-1], HC - DT0 - 16), w.dtype)
    return jnp.concatenate([w[..., a:b] for a, b in W_IN_SEGMENTS] + [z], axis=-1)


def _w_in_blocks_to_padded(blocks):
    pieces = []
    for a, b in W_IN_SEGMENTS:
        for d in range(a // W_IN_SHARD, (b - 1) // W_IN_SHARD + 1):
            pieces.append(blocks[d][:, max(a - W_IN_SHARD * d, 0):min(b - W_IN_SHARD * d, W_IN_SHARD)])
    pieces.append(jnp.zeros((blocks.shape[1], HC - DT0 - 16), blocks.dtype))
    return jnp.concatenate(pieces, axis=1)


def _w_in_padded_to_blocks(g):
    starts = np.cumsum([0] + [b - a for a, b in W_IN_SEGMENTS])
    by_origin = sorted(zip(W_IN_SEGMENTS, starts), key=lambda t: t[0][0])
    blocks = []
    for d in range(NDEV):
        lo_d, hi_d = W_IN_SHARD * d, W_IN_SHARD * (d + 1)
        pieces = [g[:, int(s) + max(a, lo_d) - a:int(s) + min(b, hi_d) - a] for (a, b), s in by_origin if max(a, lo_d) < min(b, hi_d)]
        blocks.append(jnp.concatenate(pieces, axis=1))
    return jnp.stack(blocks)


def _block_diag(pw):
    same_group = jnp.eye(4, dtype=pw.dtype)[:, None, :, None]
    return (same_group * pw[:, :, None, :]).reshape(POOL_W, POOL_W)


def _expand_heads(v):
    return jnp.repeat(v, 64).reshape(1, D)


def _ffn_fwd(x, x16, w13, w2, g, b, tag, carry_up=(), carry_f=()):
    act, act_t, h16, got_up = _ffn_up(x16, w13, carry=_Carry("gather", carry_up) if carry_up else None, name=f"{tag}_up")
    f = _mm(act, w2, carry=_Carry("gather", carry_f) if carry_f else None, name=f"{tag}_f")
    f, got_f = f if carry_f else (f, [])
    y, y16, y16t = _ln_fwd(x, f, g, b, scale=FFN_RES, name=f"{tag}_ln")
    return y, y16, y16t, (h16, act_t, f), list(got_up) + list(got_f)


def _ffn_bwd(x, x16t, w13, w2, g, b, saved, dy, tag, carry_dx=(), carry_dw13=()):
    h16, act_t, f = saved
    dres, df, dg, db = _ln_bwd(x, f, dy, g, b, scale=FFN_RES, name=f"{tag}_ln_bwd")
    dw2 = _mm(act_t, df, out_dtype=BF16, name=f"{tag}_dw2")
    dh = _ffn_down_bwd(df, w2, h16, name=f"{tag}_dh")
    dx = _mm(dh, w13, tb=True, add=dres, carry=_Carry("chips", carry_dx) if carry_dx else None, name=f"{tag}_dx")
    dx, got_dx = dx if carry_dx else (dx, [])
    dw13 = _mm(x16t, dh, out_blocked=True, out_dtype=BF16, carry=_Carry("chips", carry_dw13) if carry_dw13 else None,
               name=f"{tag}_dw13")
    dw13, got_dw13 = dw13 if carry_dw13 else (dw13, [])
    return dx, dw13, dw2, dg[0], db[0], list(got_dx) + list(got_dw13)


CARRIERS = {"ffn1_up": ["ffn1_w2"], "mix_in": ["ffn1_w13"], "ssd": ["w_in"], "attn0": ["ffn2_w2"],
            "attn1": ["p_pool", "p_ssd", "p_attn", "w_out"], "attn2": [("ffn2_w13", 0)], "ffn2_up": [("ffn2_w13", 1)]}


def _layer_fwd(x0, x0_16, x0_16t, p, bias, nxt=None):
    def piece(spec):
        if isinstance(spec, tuple):
            half = nxt[spec[0]].shape[0] // 2
            return nxt[spec[0]][half * spec[1]:half * (spec[1] + 1)]
        return nxt[spec]

    send = {k: ([piece(s) for s in specs] if nxt is not None else []) for k, specs in CARRIERS.items()}
    gather = lambda k: _Carry("gather", send[k]) if nxt is not None else None
    got = {}
    x1, x1_16, x1_16t, s1, g1 = _ffn_fwd(x0, x0_16, p["ffn1_w13"], p["ffn1_w2"], p["ln1_g"], p["ln1_b"], "ffn1", send["ffn1_up"])
    got.update(zip(CARRIERS["ffn1_up"], g1))
    hcat = _mm(x1_16, p["w_in"], carry=gather("mix_in"), name="mix_in")
    if nxt is not None:
        hcat, g_in = hcat
        got.update(zip(CARRIERS["mix_in"], g_in))
    pooled, pooled_t = _pool_fwd(hcat, name="pool")
    ya_lin = _mm(pooled, p["pool_bd"], name="pool_lin")
    ya_pre, ya_pre_t = _affine_fwd(ya_lin, p["pool_b"], p["pool_scale"], name="pool_affine")
    ya = _mm(ya_pre, p["p_pool"], name="pool_out")
    xa = _conv_fwd(hcat, p["conv_w"], p["conv_b"], name="conv")
    dtx = jnp.repeat(hcat[:, DT0:DT0 + SSD_HEADS], 64, axis=1)
    ysc, hs, g_ssd = _ssd_fwd(xa, dtx, p["dt_bias"], p["a_log"], p["d_skip"], carry=gather("ssd"), name="ssd")
    got.update(zip(CARRIERS["ssd"], g_ssd))
    yb_pre, yb_pre_t = _gnorm_fwd(ysc, hcat, p["ssd_norm"], name="ssd_norm")
    yb = _mm(yb_pre, p["p_ssd"], name="ssd_out")
    outs, lses, attn_sv = [], [], []
    for gi, (_, dil) in enumerate(ATTN_CONFIGS):
        c0 = QKV0 + 768 * gi
        qkv, col0 = (hcat, c0 // 128) if dil == 1 else (_to_sub(hcat[:, c0:c0 + 768], dil), 0)
        o, l, g_at = _attn_fwd(qkv, col0, bias[gi], SEQ // dil // ATTN_BLK, carry=gather(f"attn{gi}"), name=f"attn{gi}")
        got.update(zip(CARRIERS[f"attn{gi}"], g_at))
        attn_sv.append((None if dil == 1 else qkv, o, l))
        outs.append(_from_sub(o, dil))
        lses.append(_from_sub(l, dil))
    yc_pre, yc_pre_t = _amerge_fwd(outs, lses, name="attn_merge")
    yc = _mm(yc_pre, p["p_attn"], name="attn_out")
    merged, merged_t = _merge_fwd(hcat, ya, yb, yc, p["gate_b"], name="merge")
    mix = _mm(merged, p["w_out"], name="mix_out")
    x2, x2_16, x2_16t = _ln_fwd(x1, mix, p["ln2_g"], p["ln2_b"], scale=1.0, name="ln2")
    x3, x3_16, x3_16t, s3, g3 = _ffn_fwd(x2, x2_16, p["ffn2_w13"], p["ffn2_w2"], p["ln3_g"], p["ln3_b"], "ffn2", send["ffn2_up"])
    got.update(zip(CARRIERS["ffn2_up"], g3))
    for n in {s[0] for s in got if isinstance(s, tuple)}:
        got[n] = jnp.concatenate([got.pop((n, 0)), got.pop((n, 1))], axis=1)
    saved = dict(x0=x0, x0_16t=x0_16t, s1=s1, x1=x1, x1_16t=x1_16t, hcat=hcat, pooled_t=pooled_t, ya_lin=ya_lin, ya_pre_t=ya_pre_t,
                 ya=ya, xa=xa, hs=hs, ysc=ysc, yb_pre_t=yb_pre_t, yb=yb, outs=outs, lses=lses, attn_sv=attn_sv, yc_pre_t=yc_pre_t,
                 yc=yc, merged_t=merged_t, mix=mix, x2=x2, x2_16t=x2_16t, s3=s3)
    return x3, x3_16, x3_16t, saved, got


BWD_CARRIERS = {"ffn2_dx": ["w_in"], "ffn2_dw13": ["ffn1_w13"], "mix_in_dx": ["ffn2_w13"], "mix_in_dw": ["ffn1_w2", "ffn2_w2"],
                "ffn1_dx": ["p_pool", "p_ssd", "p_attn", "w_out"]}


def _layer_bwd(p, bias, sv, dx3, pending=None):
    send = {k: ([pending[n] for n in names] if pending is not None else []) for k, names in BWD_CARRIERS.items()}
    done = {}
    g = {}
    dx2, g["ffn2_w13"], g["ffn2_w2"], g["ln3_g"], g["ln3_b"], got = _ffn_bwd(
        sv["x2"], sv["x2_16t"], p["ffn2_w13"], p["ffn2_w2"], p["ln3_g"], p["ln3_b"], sv["s3"], dx3, "ffn2",
        send["ffn2_dx"], send["ffn2_dw13"])
    done.update(zip(BWD_CARRIERS["ffn2_dx"] + BWD_CARRIERS["ffn2_dw13"], got))
    hcat = sv["hcat"]
    dres, dmix, dg2, db2 = _ln_bwd(sv["x1"], sv["mix"], dx2, p["ln2_g"], p["ln2_b"], scale=1.0, name="ln2_bwd")
    g["ln2_g"], g["ln2_b"] = dg2[0], db2[0]
    dmerged = _mm(dmix, p["w_out"], tb=True, name="mix_out_dx")
    g["w_out"] = _mm(sv["merged_t"], dmix, out_dtype=BF16, name="mix_out_dw")
    dg0, dg1, dg2_, dya, dyb, dyc, gb0, gb1, gb2 = _merge_bwd(hcat, sv["ya"], sv["yb"], sv["yc"], dmerged, p["gate_b"],
                                                               name="merge_bwd")
    g["gate_b"] = jnp.concatenate([gb0, gb1, gb2], axis=0)
    dya_pre = _mm(dya, p["p_pool"], tb=True, name="pool_out_dx")
    g["p_pool"] = _mm(sv["ya_pre_t"], dya, out_dtype=BF16, name="pool_out_dw")
    dya_lin, dpb, dps = _affine_bwd(sv["ya_lin"], dya_pre, p["pool_b"], p["pool_scale"], name="pool_affine_bwd")
    g["pool_b"], g["pool_scale"] = dpb[0].reshape(4, POOL_GDIM), dps[0]
    dpooled = _mm(dya_lin, p["pool_bd"], tb=True, name="pool_lin_dx")
    dbd = _mm(sv["pooled_t"], dya_lin, name="pool_lin_dw")
    g["pool_w"] = jnp.stack([dbd[i * POOL_GDIM:(i + 1) * POOL_GDIM, i * POOL_GDIM:(i + 1) * POOL_GDIM] for i in range(4)])
    du = _pool_bwd(dpooled, name="pool_bwd")
    dyb_pre = _mm(dyb, p["p_ssd"], tb=True, name="ssd_out_dx")
    g["p_ssd"] = _mm(sv["yb_pre_t"], dyb, out_dtype=BF16, name="ssd_out_dw")
    dysc, dz, dnw = _gnorm_bwd(sv["ysc"], hcat, dyb_pre, p["ssd_norm"], name="ssd_norm_bwd")
    g["ssd_norm"] = dnw[0]
    dtx = jnp.repeat(hcat[:, DT0:DT0 + SSD_HEADS], 64, axis=1)
    dxs, dbm, dcm, ddtx, ddtb, dalog, ddsk = _ssd_bwd(sv["xa"], dtx, p["dt_bias"], p["a_log"], p["d_skip"], sv["hs"], dysc,
                                                     name="ssd_bwd")
    g["dt_bias"], g["a_log"], g["d_skip"] = ddtb[0, ::64], dalog[0, ::64], ddsk[0, ::64]
    dxa = jnp.concatenate([dxs, dbm, dcm], axis=1)
    dxbc, dcw, dcb = _conv_bwd(hcat, p["conv_w"], p["conv_b"], dxa, name="conv_bwd")
    g["conv_w"], g["conv_b"] = dcw, dcb[0]
    ddt = jnp.pad(ddtx[:, ::64], ((0, 0), (0, HC - DT0 - SSD_HEADS))).astype(BF16)
    dyc_pre = _mm(dyc, p["p_attn"], tb=True, name="attn_out_dx")
    g["p_attn"] = _mm(sv["yc_pre_t"], dyc, out_dtype=BF16, name="attn_out_dw")
    am = _amerge_bwd(sv["outs"], sv["lses"], dyc_pre, name="attn_merge_bwd")
    dqkv, dbias = [], []
    for gi, (_, dil) in enumerate(ATTN_CONFIGS):
        qkv_sub, o, l = sv["attn_sv"][gi]
        qkv, col0 = (hcat, (QKV0 + 768 * gi) // 128) if dil == 1 else (qkv_sub, 0)
        dq, dk, dv, dbi = _attn_bwd(qkv, col0, o, l, _to_sub(am[gi], dil), _to_sub(am[3 + gi], dil), bias[gi],
                                    SEQ // dil // ATTN_BLK, name=f"attn{gi}_bwd")
        dqkv += [_from_sub(t, dil) for t in (dq, dk, dv)]
        dbias.append(dbi)
    g["attn_bias"] = jnp.stack(dbias)
    dhcat = jnp.concatenate([dz, dxbc, dg0, dg1, dg2_, *dqkv, du, ddt], axis=1)
    chips = lambda k: _Carry("chips", send[k]) if pending is not None else None
    dx1 = _mm(dhcat, p["w_in"], tb=True, add=dres, carry=chips("mix_in_dx"), name="mix_in_dx")
    g["w_in"] = _mm(sv["x1_16t"], dhcat, out_dtype=BF16, carry=chips("mix_in_dw"), name="mix_in_dw")
    if pending is not None:
        (dx1, got_dx), (g["w_in"], got_dw) = dx1, g["w_in"]
        done.update(zip(BWD_CARRIERS["mix_in_dx"] + BWD_CARRIERS["mix_in_dw"], list(got_dx) + list(got_dw)))
    dx0, g["ffn1_w13"], g["ffn1_w2"], g["ln1_g"], g["ln1_b"], got = _ffn_bwd(
        sv["x0"], sv["x0_16t"], p["ffn1_w13"], p["ffn1_w2"], p["ln1_g"], p["ln1_b"], sv["s1"], dx1, "ffn1", send["ffn1_dx"])
    done.update(zip(BWD_CARRIERS["ffn1_dx"], got))
    return dx0, g, done


def _local_step(x, tgt, lw, rel_bias, shards=None, prepare=None, chip_sums=None):
    bias, bias_vjp = jax.vjp(_attn_bias, rel_bias)
    y, y16, saved, lw = x, x.astype(BF16), [], list(lw)
    y16t = y16.T
    n_layers = len(lw) if shards is None else len(shards)
    for l in range(n_layers):
        nxt = shards[l + 1] if shards is not None and l + 1 < n_layers else None
        y, y16, y16t, sv, got = _layer_fwd(y, y16, y16t, lw[l], bias, nxt)
        saved.append(sv)
        if nxt is not None:
            lw.append(prepare(got, l + 1))
    dy, sq = _loss_kernel(y, tgt, name="loss")
    grads, sums, pending = [None] * n_layers, [None] * n_layers, None
    for l in reversed(range(n_layers)):
        dy, grads[l], done = _layer_bwd(lw[l], bias, saved[l], dy, pending)
        if pending is not None:
            sums[l + 1] = done
        pending = chip_sums(grads[l]) if chip_sums is not None else None
    if pending is not None:
        names = list(pending)
        sums[0] = dict(zip(names, _exchange_chips([pending[n] for n in names], name="exchange_grads_chips")))
    (d_rel,) = bias_vjp(sum(g.pop("attn_bias") for g in grads))
    return sq, dy, grads, d_rel, sums


SMALL_REPL = ["ln1_g", "ln1_b", "pool_w", "pool_b", "pool_scale", "conv_b", "dt_bias", "a_log", "d_skip", "ssd_norm",
              "ln2_g", "ln2_b", "ln3_g", "ln3_b", "rel_bias"]
SMALL_SHARD = ["gate_b", "conv_w"]
WEIGHTS = ['ffn1_w13', 'ffn1_w2', 'ln1_g', 'ln1_b', 'w_in', 'gate_b', 'pool_w', 'pool_b', 'pool_scale', 'conv_w', 'conv_b',
           'dt_bias', 'a_log', 'd_skip', 'ssd_norm', 'rel_bias', 'p_pool', 'p_ssd', 'p_attn', 'w_out', 'ln2_g', 'ln2_b',
           'ffn2_w13', 'ffn2_w2', 'ln3_g', 'ln3_b']


def _step(w, m, v, x, tgt):
    my_c = lax.axis_index("c")
    dev = 4 * lax.axis_index("x") + 2 * lax.axis_index("y") + my_c
    shards = [{n: w[n][l].astype(BF16) for n, _, _ in BIG} for l in range(DEPTH)]
    first = dict(zip([n for n, _, _ in BIG], _all_gather([shards[0][n] for n, _, _ in BIG], name="gather_weights")))
    sp_in = _SmallPack([w[n].shape for n in SMALL_SHARD])
    small_g = _all_gather([sp_in.pack([w[n] for n in SMALL_SHARD])], name="gather_small")[0]
    gate_b = jnp.stack([sp_in.unpack(small_g[d])[0] for d in range(NDEV)], axis=2).reshape(DEPTH, 3, D)
    conv_w = jnp.stack([sp_in.unpack(small_g[d])[1] for d in range(NDEV)], axis=2).reshape(DEPTH, 4, 2048)

    def prepare(blocks, l):
        p = {n: blocks[n] if n in BLOCKED else _from_blocks(blocks[n], ax) for n, _, ax in BIG if n != "w_in"}
        p["w_in"] = _w_in_blocks_to_padded(blocks["w_in"])
        p["pool_bd"] = _block_diag(w["pool_w"][l]).astype(BF16)
        p["gate_b"], p["conv_w"] = gate_b[l], conv_w[l]
        for n in ("ln1_g", "ln1_b", "ln2_g", "ln2_b", "ln3_g", "ln3_b", "pool_scale", "conv_b", "ssd_norm"):
            p[n] = w[n][l][None, :]
        p["pool_b"] = w["pool_b"][l].reshape(1, POOL_W)
        for n in ("dt_bias", "a_log", "d_skip"):
            p[n] = _expand_heads(w[n][l])
        return p

    c_idx = my_c.reshape(1).astype(jnp.int32)

    def chip_sums(g):
        parts = []
        for n, (r, c), ax in BIG:
            blocks = g[n] if n in BLOCKED else _w_in_padded_to_blocks(g[n]) if n == "w_in" else _to_blocks(g[n], ax)
            parts.append(blocks.reshape(4, 2, r, c))
        from_sibling = _exchange_sibling(parts, name="exchange_grads_cores")
        return {n: _add_own_half(p, o, c_idx, name=f"add_cores_{n}") for p, o, (n, _, _) in zip(parts, from_sibling, BIG)}

    sq, dx, grads, d_rel, sums = _local_step(x.reshape(T, D), tgt.reshape(T, D), [prepare(first, 0)], w["rel_bias"], shards,
                                             prepare, chip_sums)
    loss = lax.psum(jnp.sum(sq) * (0.5 / D), ("x", "y", "c"))
    small_names = SMALL_REPL + SMALL_SHARD
    small = {n: (d_rel if n == "rel_bias" else jnp.stack([g[n] for g in grads])) for n in small_names}
    sp = _SmallPack([small[n].shape for n in small_names])
    gsmall = sp.unpack(_sum_slots(_all_gather([sp.pack([small[n] for n in small_names])], name="gather_small_grads")[0],
                                  name="sum_small_grads"))
    gout = {}
    for n, gv in zip(small_names, gsmall):
        if n in SMALL_SHARD:
            width = w[n].shape[-1]
            gv = lax.dynamic_slice_in_dim(gv, dev * width, width, axis=2)
        gout[n] = gv
    delta, new_m, new_v = {}, {}, {}
    for n, _, _ in BIG:
        gout[n], delta[n], new_m[n], new_v[n] = _adamw_reduce([s[n] for s in sums], w[n], m[n], v[n], name=f"adamw_{n}")
    spa = _SmallPack([w[n].shape for n in small_names])
    res = _adamw(spa.pack([w[n] for n in small_names]), spa.pack([gout[n] for n in small_names]),
                 spa.pack([m[n] for n in small_names]), spa.pack([v[n] for n in small_names]), name="adamw_small")
    for out, packed in zip((delta, new_m, new_v), res):
        for n, val in zip(small_names, spa.unpack(packed)):
            out[n] = val
    return (loss, dx.reshape(NB, SEQ, D), *[gout[n] for n in WEIGHTS], *[delta[n] for n in WEIGHTS],
            *[new_m[n] for n in WEIGHTS], *[new_v[n] for n in WEIGHTS])


def kernel(x, ffn1_w13, ffn1_w2, ln1_g, ln1_b, w_in, gate_b, pool_w, pool_b, pool_scale, conv_w, conv_b, dt_bias, a_log, d_skip, ssd_norm, rel_bias, p_pool, p_ssd, p_attn, w_out, ln2_g, ln2_b, ffn2_w13, ffn2_w2, ln3_g, ln3_b, loss_target, m_ffn1_w13, m_ffn1_w2, m_ln1_g, m_ln1_b, m_w_in, m_gate_b, m_pool_w, m_pool_b, m_pool_scale, m_conv_w, m_conv_b, m_dt_bias, m_a_log, m_d_skip, m_ssd_norm, m_rel_bias, m_p_pool, m_p_ssd, m_p_attn, m_w_out, m_ln2_g, m_ln2_b, m_ffn2_w13, m_ffn2_w2, m_ln3_g, m_ln3_b, v_ffn1_w13, v_ffn1_w2, v_ln1_g, v_ln1_b, v_w_in, v_gate_b, v_pool_w, v_pool_b, v_pool_scale, v_conv_w, v_conv_b, v_dt_bias, v_a_log, v_d_skip, v_ssd_norm, v_rel_bias, v_p_pool, v_p_ssd, v_p_attn, v_w_out, v_ln2_g, v_ln2_b, v_ffn2_w13, v_ffn2_w2, v_ln3_g, v_ln3_b):
    given = dict(locals())
    w = {n: given[n] for n in WEIGHTS}
    m = {n: given["m_" + n] for n in WEIGHTS}
    v = {n: given["v_" + n] for n in WEIGHTS}
    return _step(w, m, v, x, loss_target)
```

```python
import functools

import numpy as np
import jax
import jax.numpy as jnp
from jax import lax
from jax.experimental import pallas as pl
from jax.experimental.pallas import tpu as pltpu

F32, BF16 = jnp.float32, jnp.bfloat16
HIGHEST = lax.Precision.HIGHEST

NDEV = 8
DEPTH = 4
D = 1024
SEQ = 2048
NB = 2
T = NB * SEQ
DFF = 2816
POOL_W = 768
POOL_WINDOWS = (2, 4, 8, 16)
POOL_GDIM = 192
SSD_HEADS = 16
CHUNK = 128
NCHUNK = SEQ // CHUNK
ATTN_CONFIGS = ((128, 1), (512, 4), (2048, 16))
ATTN_BLK = 128
REL_BUCKETS = 32
REL_MAX_DIST = 2048
LN_EPS = 1e-5
SSD_EPS = 1e-5
ALPHA = (2.0 * DEPTH) ** 0.25
FFN_RES = 0.5
NEG = -1e30

ADAM_LR, ADAM_B1, ADAM_B2, ADAM_EPS, ADAM_WD, ADAM_STEP = 0.001, 0.9, 0.999, 1e-08, 0.01, 10

Z0, XBC0, GATE0, QKV0, U0, DT0, HC = 0, 1024, 3072, 6144, 8448, 9216, 9728
O_U, O_Z, O_XBC, O_DT, O_Q, O_K, O_V, O_G, O_END = 0, 768, 1792, 3840, 3856, 4624, 5392, 6160, 9232

VMEM_LIMIT = 56 * 1024 * 1024


def _cp(sem):
    return pltpu.CompilerParams(dimension_semantics=sem, vmem_limit_bytes=VMEM_LIMIT)


def _pick(dim, cands):
    for c in cands:
        if dim % c == 0:
            return c
    return dim


def _mm(a, b, *, tb=False, out_dtype=F32, add=None, add_scale=1.0, out_blocked=False, bn=None, b_layer=None, carry=None, name):
    nc = carry.n if carry is not None else 0
    bk = None
    if a.ndim == 3:
        nkb, M, bk = a.shape
        K = nkb * bk
    else:
        M, K = a.shape
    b_blocked = b.ndim >= 3
    b_shape = b.shape if b_layer is None else (b.shape[0], *b.shape[2:])
    if b_blocked and tb:
        assert bk in (None, b_shape[2])
        nkb, N, bk = b_shape
        assert K == nkb * bk, (a.shape, b.shape)
    elif b_blocked:
        nnb, K2, bn = b_shape
        N = nnb * bn
        assert K == K2, (a.shape, b.shape)
    else:
        (N, K2) = b.shape if tb else b.shape[::-1]
        assert K == K2, (a.shape, b.shape, tb)
    bm = _pick(M, (1024, 1408, 768, 512, 256))
    if bn is None:
        bn = _pick(N, (2432, 1024, 768, 512, 256))
    if bk is None:
        bk = K if K <= 1024 else _pick(K, (1024, 1408, 2432, 512, 256))
    nk = K // bk
    dn = (((1,), (1 if tb else 0,)), ((), ()))

    grid = (M // bm, N // bn, nk)

    def kern(*refs):
        a_ref, b_ref = refs[0], refs[1]
        n_in = 2 + (add is not None)
        add_ref = refs[2] if add is not None else None
        x_refs, o_ref, g_refs = refs[n_in:n_in + nc], refs[n_in + nc], refs[n_in + nc + 1:n_in + 2 * nc + 1]
        scratch = refs[n_in + 2 * nc + 1:]
        acc_ref = scratch[0] if nk > 1 else None
        if nc:
            step = (pl.program_id(0) * grid[1] + pl.program_id(1)) * nk + pl.program_id(2)
            carried = _Carried(carry, x_refs, g_refs, scratch[-3:], step, grid[0] * grid[1] * nk)
            carried.before()
        p = lax.dot_general(a_ref[...].astype(BF16), b_ref[...].astype(BF16), dn, preferred_element_type=F32)

        def fin(acc):
            if add_ref is not None:
                acc = acc + add_scale * add_ref[...]
            o_ref[...] = acc.astype(out_dtype)

        if nk == 1:
            fin(p)
        else:
            k = pl.program_id(2)

            @pl.when(k == 0)
            def _():
                acc_ref[...] = p

            @pl.when(k > 0)
            def _():
                acc_ref[...] += p

            @pl.when(k == nk - 1)
            def _():
                fin(acc_ref[...])

        if nc:
            carried.after()

    if a.ndim == 3:
        a_spec = pl.BlockSpec((None, bm, bk), lambda i, j, k: (k, i, 0))
    else:
        a_spec = pl.BlockSpec((bm, bk), lambda i, j, k: (i, k))
    if b_blocked and b_layer is not None:
        b_spec = (pl.BlockSpec((None, None, bn, bk), lambda i, j, k: (k, b_layer, j, 0)) if tb else
                  pl.BlockSpec((None, None, bk, bn), lambda i, j, k: (j, b_layer, k, 0)))
    elif b_blocked and tb:
        b_spec = pl.BlockSpec((None, bn, bk), lambda i, j, k: (k, j, 0))
    elif b_blocked:
        b_spec = pl.BlockSpec((None, bk, bn), lambda i, j, k: (j, k, 0))
    elif tb:
        b_spec = pl.BlockSpec((bn, bk), lambda i, j, k: (j, k))
    else:
        b_spec = pl.BlockSpec((bk, bn), lambda i, j, k: (k, j))
    if out_blocked:
        o_spec, o_shape = pl.BlockSpec((None, bm, bn), lambda i, j, k: (j, i, 0)), (N // bn, M, bn)
    else:
        o_spec, o_shape = pl.BlockSpec((bm, bn), lambda i, j, k: (i, j)), (M, N)
    in_specs = [a_spec, b_spec]
    args = [a, b]
    if add is not None:
        assert add.shape == o_shape
        in_specs.append(o_spec)
        args.append(add)
    res = pl.pallas_call(
        kern, name=name, grid=grid,
        in_specs=in_specs + [_ANY] * nc, out_specs=[o_spec] + [_ANY] * nc,
        out_shape=[jax.ShapeDtypeStruct(o_shape, out_dtype)] + (carry.out_shapes() if nc else []),
        scratch_shapes=([pltpu.VMEM((bm, bn), F32)] if nk > 1 else []) + (carry.sems() if nc else []),
        compiler_params=_cp(("arbitrary",) * 3 if nc else ("parallel", "parallel", "arbitrary")),
    )(*args, *(carry.arrays if nc else []))
    return (res[0], res[1:]) if nc else res[0]


def _rowwise(fn, rows, pars, outs, accs, *, name, tile, groups=1):
    n_rows = rows[0][0].shape[0]
    nt = n_rows // tile
    n_in = len(rows) + len(pars)
    n_out = len(outs)
    transposed = [len(o) == 4 and o[3] for o in outs]

    def kern(*refs):
        res = fn(*[r[...] for r in refs[:n_in]])
        for o_ref, val, tr in zip(refs[n_in:n_in + n_out], res[:n_out], transposed):
            o_ref[...] = (val.astype(F32).T if tr else val).astype(o_ref.dtype)
        i = pl.program_id(1)
        for a_ref, val in zip(refs[n_in + n_out:], res[n_out:]):
            @pl.when(i == 0)
            def _(a_ref=a_ref, val=val):
                a_ref[...] = val

            @pl.when(i > 0)
            def _(a_ref=a_ref, val=val):
                a_ref[...] += val

    in_specs = [pl.BlockSpec((tile, w), lambda g, i, c0=c0: (i, c0 + g)) for (_, w, c0) in rows]
    for p, w in pars:
        if w is None:
            in_specs.append(pl.BlockSpec(p.shape, lambda g, i: (0, 0)))
        else:
            in_specs.append(pl.BlockSpec((p.shape[0], w), lambda g, i: (0, g)))
    out_specs = [pl.BlockSpec((o[1], tile), lambda g, i: (g, i)) if tr else pl.BlockSpec((tile, o[1]), lambda g, i: (i, g))
                 for o, tr in zip(outs, transposed)]
    out_specs += [pl.BlockSpec((1, w), lambda g, i: (0, g)) for (_, w) in accs]
    out_shape = [jax.ShapeDtypeStruct((o[0], n_rows) if tr else (n_rows, o[0]), o[2]) for o, tr in zip(outs, transposed)]
    out_shape += [jax.ShapeDtypeStruct((1, c), F32) for (c, _) in accs]
    return pl.pallas_call(
        kern, name=name, grid=(groups, nt), in_specs=in_specs, out_specs=out_specs, out_shape=out_shape,
        compiler_params=_cp(("arbitrary", "arbitrary")),
    )(*[r[0] for r in rows], *[p[0] for p in pars])


def _colsum(v):
    return jnp.sum(v, axis=0, keepdims=True)


def _silu(v):
    return v * jax.nn.sigmoid(v)


def _ln_fn(x, f, g, b, *, scale):
    pre = ALPHA * x + scale * f
    mu = jnp.mean(pre, axis=-1, keepdims=True)
    var = jnp.mean(jnp.square(pre - mu), axis=-1, keepdims=True)
    return (pre - mu) * lax.rsqrt(var + LN_EPS) * g + b


def _ln_fwd(x, f, g, b, *, scale, name):
    fn = lambda x_, f_, g_, b_: [_ln_fn(x_, f_, g_, b_, scale=scale)] * 3
    return _rowwise(fn, [(x, D, 0), (f, D, 0)], [(g, None), (b, None)], [(D, D, F32), (D, D, BF16), (D, D, BF16, True)], [],
                    name=name, tile=512)


def _ln_bwd(x, f, dy, g, b, *, scale, name):
    def fn(x_, f_, dy_, g_, b_):
        _, vjp = jax.vjp(functools.partial(_ln_fn, scale=scale), x_, f_, g_, b_)
        return list(vjp(dy_))

    return _rowwise(fn, [(x, D, 0), (f, D, 0), (dy, D, 0)], [(g, None), (b, None)],
                    [(D, D, F32), (D, D, BF16)], [(D, D), (D, D)], name=name, tile=512)


FB = 2 * DFF // NDEV
FFN_BM = 1024
_H_PAIR = pl.BlockSpec((2, None, FFN_BM, FB), lambda i, j: (0, j, i, 0))


def _ffn_up(x16, w13, *, carry=None, name):
    nc = carry.n if carry is not None else 0
    grid = (T // FFN_BM, 4)

    def kern(*refs):
        x_ref, w_ref = refs[0], refs[1]
        x_refs, (act_ref, act_t_ref, h_ref), g_refs = refs[2:2 + nc], refs[2 + nc:5 + nc], refs[5 + nc:5 + 2 * nc]
        if nc:
            step = pl.program_id(0) * grid[1] + pl.program_id(1)
            carried = _Carried(carry, x_refs, g_refs, refs[5 + 2 * nc:], step, grid[0] * grid[1])
            carried.before()
        x = x_ref[...]
        ha = jnp.dot(x, w_ref[0], preferred_element_type=F32)
        hg = jnp.dot(x, w_ref[1], preferred_element_type=F32)
        act = _silu(ha) * hg
        act_ref[...] = act.astype(BF16)
        act_t_ref[...] = act.T.astype(BF16)
        h_ref[0] = ha.astype(BF16)
        h_ref[1] = hg.astype(BF16)
        if nc:
            carried.after()

    res = pl.pallas_call(
        kern, name=name, grid=grid,
        in_specs=[pl.BlockSpec((FFN_BM, D), lambda i, j: (i, 0)),
                  pl.BlockSpec((2, None, D, FB), lambda i, j: (0, j, 0, 0))] + [_ANY] * nc,
        out_specs=[pl.BlockSpec((None, FFN_BM, FB), lambda i, j: (j, i, 0)), pl.BlockSpec((None, FB, FFN_BM), lambda i, j: (j, 0, i)),
                   _H_PAIR] + [_ANY] * nc,
        out_shape=[jax.ShapeDtypeStruct((4, T, FB), BF16), jax.ShapeDtypeStruct((4, FB, T), BF16),
                   jax.ShapeDtypeStruct((2, 4, T, FB), BF16)]
        + (carry.out_shapes() if nc else []),
        scratch_shapes=carry.sems() if nc else [],
        compiler_params=_cp(("arbitrary", "arbitrary") if nc else ("parallel", "parallel")),
    )(x16, w13.reshape(2, 4, D, FB), *(carry.arrays if nc else []))
    return res[0], res[1].reshape(DFF, T), res[2].reshape(NDEV, T, FB), res[3:]


def _ffn_down_bwd(df, w2, h16, *, name):
    def kern(df_ref, w_ref, h_ref, dh_ref):
        da = _dot(df_ref[...], w_ref[...], _NT)
        a, g = h_ref[0].astype(F32), h_ref[1].astype(F32)
        s = jax.nn.sigmoid(a)
        dh_ref[0] = (da * g * (s * (1.0 + a * (1.0 - s)))).astype(BF16)
        dh_ref[1] = (da * (a * s)).astype(BF16)

    return pl.pallas_call(
        kern, name=name, grid=(T // FFN_BM, 4),
        in_specs=[pl.BlockSpec((FFN_BM, D), lambda i, j: (i, 0)), pl.BlockSpec((FB, D), lambda i, j: (j, 0)), _H_PAIR],
        out_specs=_H_PAIR, out_shape=jax.ShapeDtypeStruct((2, 4, T, FB), BF16),
        compiler_params=_cp(("parallel", "parallel")),
    )(df, w2, h16.reshape(2, 4, T, FB)).reshape(NDEV, T, FB)


def _affine_fn(y, b, s):
    return (y + b) * s


def _affine_fwd(y, b, s, *, name):
    return _rowwise(lambda y_, b_, s_: [_affine_fn(y_, b_, s_)] * 2, [(y, POOL_W, 0)], [(b, None), (s, None)],
                    [(POOL_W, POOL_W, BF16), (POOL_W, POOL_W, BF16, True)], [], name=name, tile=512)


def _affine_bwd(y, dyo, b, s, *, name):
    def fn(y_, d_, b_, s_):
        _, vjp = jax.vjp(_affine_fn, y_, b_, s_)
        return list(vjp(d_))

    return _rowwise(fn, [(y, POOL_W, 0), (dyo, POOL_W, 0)], [(b, None), (s, None)],
                    [(POOL_W, POOL_W, BF16)], [(POOL_W, POOL_W), (POOL_W, POOL_W)], name=name, tile=512)


def _gnorm_fn(y, z, w):
    yz = y * _silu(z)
    return yz * lax.rsqrt(jnp.mean(jnp.square(yz), axis=-1, keepdims=True) + SSD_EPS) * w


def _gnorm_fwd(y, hcat, w, *, name):
    return _rowwise(lambda y_, z_, w_: [_gnorm_fn(y_, z_, w_)] * 2, [(y, 256, 0), (hcat, 256, Z0 // 256)], [(w, 256)],
                    [(D, 256, BF16), (D, 256, BF16, True)], [], name=name, tile=512, groups=4)


def _gnorm_bwd(y, hcat, dyo, w, *, name):
    def fn(y_, z_, d_, w_):
        _, vjp = jax.vjp(_gnorm_fn, y_, z_, w_)
        return list(vjp(d_))

    return _rowwise(fn, [(y, 256, 0), (hcat, 256, Z0 // 256), (dyo, 256, 0)], [(w, 256)],
                    [(D, 256, F32), (D, 256, BF16)], [(D, 256)], name=name, tile=512, groups=4)


def _merge_fn(g0, g1, g2, ya, yb, yc, b0, b1, b2):
    return jax.nn.sigmoid(g0 + b0) * ya + jax.nn.sigmoid(g1 + b1) * yb + jax.nn.sigmoid(g2 + b2) * yc


def _merge_rows(hcat, ya, yb, yc):
    c = GATE0 // 256
    return [(hcat, 256, c), (hcat, 256, c + 4), (hcat, 256, c + 8), (ya, 256, 0), (yb, 256, 0), (yc, 256, 0)]


def _merge_fwd(hcat, ya, yb, yc, gb, *, name):
    pars = [(gb[0:1], 256), (gb[1:2], 256), (gb[2:3], 256)]
    return _rowwise(lambda *v: [_merge_fn(*v)] * 2, _merge_rows(hcat, ya, yb, yc), pars, [(D, 256, BF16), (D, 256, BF16, True)],
                    [], name=name, tile=512, groups=4)


def _merge_bwd(hcat, ya, yb, yc, dm, gb, *, name):
    def fn(g0, g1, g2, ya_, yb_, yc_, dm_, b0, b1, b2):
        _, vjp = jax.vjp(_merge_fn, g0, g1, g2, ya_, yb_, yc_, b0, b1, b2)
        return list(vjp(dm_))

    pars = [(gb[0:1], 256), (gb[1:2], 256), (gb[2:3], 256)]
    return _rowwise(fn, _merge_rows(hcat, ya, yb, yc) + [(dm, 256, 0)], pars,
                    [(D, 256, BF16)] * 6, [(D, 256)] * 3, name=name, tile=512, groups=4)


def _amerge_fn(o0, o1, o2, l0, l1, l2):
    m = jnp.maximum(jnp.maximum(l0, l1), l2)
    e0, e1, e2 = jnp.exp(l0 - m), jnp.exp(l1 - m), jnp.exp(l2 - m)
    return (e0 * o0 + e1 * o1 + e2 * o2) / (e0 + e1 + e2)


def _amerge_fwd(os_, ls_, *, name):
    rows = [(v, 256, 0) for v in (*os_, *ls_)]
    return _rowwise(lambda *v: [_amerge_fn(*v)] * 2, rows, [], [(256, 256, BF16), (256, 256, BF16, True)], [], name=name, tile=1024)


def _amerge_bwd(os_, ls_, dy, *, name):
    def fn(*v):
        _, vjp = jax.vjp(_amerge_fn, *v[:6])
        return list(vjp(v[6]))

    rows = [(v, 256, 0) for v in (*os_, *ls_, dy)]
    return _rowwise(fn, rows, [], [(256, 256, F32)] * 6, [], name=name, tile=1024)


def _loss_kernel(y, tgt, *, name):
    def fn(y_, t_):
        e = y_ - t_
        return [e * (1.0 / D), _colsum(e * e)]

    return _rowwise(fn, [(y, D, 0), (tgt, D, 0)], [], [(D, D, F32)], [(D, D)], name=name, tile=512)


def _adam_math(w, g, m, v):
    m2 = ADAM_B1 * m + (1.0 - ADAM_B1) * g
    v2 = ADAM_B2 * v + (1.0 - ADAM_B2) * jnp.square(g)
    m_hat = m2 / (1.0 - ADAM_B1 ** ADAM_STEP)
    v_hat = v2 / (1.0 - ADAM_B2 ** ADAM_STEP)
    return -ADAM_LR * (m_hat / (jnp.sqrt(v_hat) + ADAM_EPS) + ADAM_WD * w), m2, v2


def _adamw(w, g, m, v, *, name):
    r, c = w.shape
    tile = _pick(r, (512, 256, 128, 64, 32, 16, 8))
    fn = lambda *a: list(_adam_math(*a))
    return _rowwise(fn, [(w, c, 0), (g, c, 0), (m, c, 0), (v, c, 0)], [], [(c, c, F32)] * 3, [], name=name, tile=tile)


def _pool_lane_window(j, width):
    lane = lax.broadcasted_iota(jnp.int32, (1, width), 1) + j * width
    grp = lane // POOL_GDIM
    return grp


def _pool_select(grp, vals):
    out = vals[3]
    for gi in (2, 1, 0):
        out = jnp.where(grp == gi, vals[gi], out)
    return out


def _pool_fwd(hcat, *, name):
    width = 256

    def kern(u_ref, o_ref, ot_ref):
        u = u_ref[...]
        t = lax.broadcasted_iota(jnp.int32, (SEQ, 1), 0)
        grp = _pool_lane_window(pl.program_id(1), width)

        def shift(v, k):
            return jnp.where(t >= k, pltpu.roll(v, k, 0), 0.0)

        s2 = u + shift(u, 1)
        s4 = s2 + shift(s2, 2)
        s8 = s4 + shift(s4, 4)
        s16 = s8 + shift(s8, 8)
        win = _pool_select(grp, [jnp.full((1, width), float(w), F32) for w in POOL_WINDOWS])
        cnt = jnp.minimum((t + 1).astype(F32), win)
        pooled = _pool_select(grp, [s2, s4, s8, s16]) / cnt - u
        o_ref[...] = pooled.astype(o_ref.dtype)
        ot_ref[...] = pooled.T.astype(ot_ref.dtype)

    return pl.pallas_call(
        kern, name=name, grid=(NB, POOL_W // width),
        in_specs=[pl.BlockSpec((SEQ, width), lambda b, j: (b, U0 // width + j))],
        out_specs=[pl.BlockSpec((SEQ, width), lambda b, j: (b, j)), pl.BlockSpec((width, SEQ), lambda b, j: (j, b))],
        out_shape=[jax.ShapeDtypeStruct((T, POOL_W), BF16), jax.ShapeDtypeStruct((POOL_W, T), BF16)],
        compiler_params=_cp(("parallel", "parallel")),
    )(hcat)


def _pool_bwd(dp, *, name):
    width = 256

    def kern(d_ref, o_ref):
        d = d_ref[...]
        t = lax.broadcasted_iota(jnp.int32, (SEQ, 1), 0)
        grp = _pool_lane_window(pl.program_id(1), width)
        win = _pool_select(grp, [jnp.full((1, width), float(w), F32) for w in POOL_WINDOWS])
        dm = d / jnp.minimum((t + 1).astype(F32), win)

        def shift(v, k):
            return jnp.where(t < SEQ - k, pltpu.roll(v, SEQ - k, 0), 0.0)

        r2 = dm + shift(dm, 1)
        r4 = r2 + shift(r2, 2)
        r8 = r4 + shift(r4, 4)
        r16 = r8 + shift(r8, 8)
        o_ref[...] = (_pool_select(grp, [r2, r4, r8, r16]) - d).astype(o_ref.dtype)

    return pl.pallas_call(
        kern, name=name, grid=(NB, POOL_W // width),
        in_specs=[pl.BlockSpec((SEQ, width), lambda b, j: (b, j))],
        out_specs=pl.BlockSpec((SEQ, width), lambda b, j: (b, j)),
        out_shape=jax.ShapeDtypeStruct((T, POOL_W), BF16), compiler_params=_cp(("parallel", "parallel")),
    )(dp)


CONV_W = 512


def _conv_pre(x, w, b, t):
    pre = w[3:4] * x + b
    for k in (1, 2, 3):
        pre = pre + w[3 - k:4 - k] * jnp.where(t >= k, pltpu.roll(x, k, 0), 0.0)
    return pre


def _conv_fwd(hcat, w, b, *, name):
    def kern(x_ref, w_ref, b_ref, o_ref):
        t = lax.broadcasted_iota(jnp.int32, (SEQ, 1), 0)
        o_ref[...] = _silu(_conv_pre(x_ref[...], w_ref[...], b_ref[...], t))

    return pl.pallas_call(
        kern, name=name, grid=(NB, 2048 // CONV_W),
        in_specs=[pl.BlockSpec((SEQ, CONV_W), lambda s, j: (s, XBC0 // CONV_W + j)),
                  pl.BlockSpec((4, CONV_W), lambda s, j: (0, j)), pl.BlockSpec((1, CONV_W), lambda s, j: (0, j))],
        out_specs=pl.BlockSpec((SEQ, CONV_W), lambda s, j: (s, j)),
        out_shape=jax.ShapeDtypeStruct((T, 2048), F32), compiler_params=_cp(("parallel", "parallel")),
    )(hcat, w, b)


def _conv_bwd(hcat, w, b, dy, *, name):
    def kern(x_ref, w_ref, b_ref, dy_ref, dx_ref, dw_ref, db_ref):
        t = lax.broadcasted_iota(jnp.int32, (SEQ, 1), 0)
        x, w_ = x_ref[...], w_ref[...]
        pre = _conv_pre(x, w_, b_ref[...], t)
        s = jax.nn.sigmoid(pre)
        dpre = dy_ref[...] * (s * (1.0 + pre * (1.0 - s)))
        dx = w_[3:4] * dpre
        dws = [None] * 4
        dws[3] = _colsum(dpre * x)
        for k in (1, 2, 3):
            dx = dx + w_[3 - k:4 - k] * jnp.where(t < SEQ - k, pltpu.roll(dpre, SEQ - k, 0), 0.0)
            dws[3 - k] = _colsum(dpre * jnp.where(t >= k, pltpu.roll(x, k, 0), 0.0))
        dx_ref[...] = dx.astype(dx_ref.dtype)
        db = _colsum(dpre)
        first = pl.program_id(1) == 0

        @pl.when(first)
        def _():
            for k in range(4):
                dw_ref[k:k + 1, :] = dws[k]
            db_ref[...] = db

        @pl.when(jnp.logical_not(first))
        def _():
            for k in range(4):
                dw_ref[k:k + 1, :] += dws[k]
            db_ref[...] += db

    return pl.pallas_call(
        kern, name=name, grid=(2048 // CONV_W, NB),
        in_specs=[pl.BlockSpec((SEQ, CONV_W), lambda j, s: (s, XBC0 // CONV_W + j)),
                  pl.BlockSpec((4, CONV_W), lambda j, s: (0, j)), pl.BlockSpec((1, CONV_W), lambda j, s: (0, j)),
                  pl.BlockSpec((SEQ, CONV_W), lambda j, s: (s, j))],
        out_specs=[pl.BlockSpec((SEQ, CONV_W), lambda j, s: (s, j)), pl.BlockSpec((4, CONV_W), lambda j, s: (0, j)),
                   pl.BlockSpec((1, CONV_W), lambda j, s: (0, j))],
        out_shape=[jax.ShapeDtypeStruct((T, 2048), BF16), jax.ShapeDtypeStruct((4, 2048), F32),
                   jax.ShapeDtypeStruct((1, 2048), F32)],
        compiler_params=_cp(("arbitrary", "arbitrary")),
    )(hcat, w, b, dy)


def _softplus(v):
    return jnp.maximum(v, 0.0) + jnp.log1p(jnp.exp(-jnp.abs(v)))


def _dot(a, b, dims):
    return lax.dot_general(a, b, (dims, ((), ())), preferred_element_type=F32)


_NN, _NT, _TN = ((1,), (0,)), ((1,), (1,)), ((0,), (0,))


def _ssd_specs(order):
    def spec(width, col0):
        return pl.BlockSpec((SEQ, width), lambda i, j, c0=col0: (order(i, j)[0], c0 + order(i, j)[1]))

    def par():
        return pl.BlockSpec((1, 256), lambda i, j: (0, order(i, j)[1]))

    return spec, par


def _ssd_chunk_common(c, xs_ref, b_ref, c_ref, dtx_ref, dtb, a, trif):
    r0 = pl.multiple_of(c * CHUNK, CHUNK)
    rows = pl.ds(r0, CHUNK)
    x = xs_ref[rows, :]
    bb = b_ref[rows, :].astype(BF16)
    cb = c_ref[rows, :].astype(BF16)
    raw = dtx_ref[rows, :] + dtb
    dt = _softplus(raw)
    cs = jnp.dot(trif, dt * a, precision=HIGHEST, preferred_element_type=F32)
    return rows, x, bb, cb, raw, dt, cs


def _head_decay(cs, me, tri):
    cse = jnp.max(jnp.where(me, cs, -jnp.inf), axis=1, keepdims=True)
    csb = jnp.broadcast_to(cse, (CHUNK, CHUNK))
    return jnp.where(tri, jnp.exp(csb - csb.T), 0.0)


def _ssd_fwd(xa, dtx, dtb, alog, dsk, *, carry=None, name):
    spec, par = _ssd_specs(lambda g, b: (b, g))
    nc = carry.n if carry is not None else 0

    def kern(*refs):
        xs_ref, b_ref, c_ref, dtx_ref, dtb_ref, alog_ref, dsk_ref = refs[:7]
        x_refs, (y_ref, hs_ref), g_refs, h_scr = refs[7:7 + nc], refs[7 + nc:9 + nc], refs[9 + nc:9 + 2 * nc], refs[9 + 2 * nc]
        if nc:
            carried = _Carried(carry, x_refs, g_refs, refs[10 + 2 * nc:], pl.program_id(0) * NB + pl.program_id(1), 4 * NB)
            carried.before()
        row = lax.broadcasted_iota(jnp.int32, (CHUNK, CHUNK), 0)
        col = lax.broadcasted_iota(jnp.int32, (CHUNK, CHUNK), 1)
        tri = row >= col
        trif = tri.astype(F32)
        head = lax.broadcasted_iota(jnp.int32, (1, 256), 1) // 64
        dtb_, a, dsk_ = dtb_ref[...], -jnp.exp(alog_ref[...]), dsk_ref[...]
        h_scr[...] = jnp.zeros_like(h_scr)

        def chunk(c, carry):
            rows, x, bb, cb, _, dt, cs = _ssd_chunk_common(c, xs_ref, b_ref, c_ref, dtx_ref, dtb_, a, trif)
            cs_last = cs[CHUNK - 1:CHUNK, :]
            xdt = x * dt
            xdtb = xdt.astype(BF16)
            g = _dot(cb, bb, _NT)
            hin = h_scr[...]
            hs_ref[rows, :] = hin
            y = jnp.exp(cs) * _dot(cb, hin.astype(BF16), _NN) + dsk_ * x
            for e in range(4):
                me = head == e
                m = (g * _head_decay(cs, me, tri)).astype(BF16)
                y = y + jnp.where(me, _dot(m, xdtb, _NN), 0.0)
            y_ref[rows, :] = y
            st = _dot(bb, (jnp.exp(cs_last - cs) * xdt).astype(BF16), _TN)
            h_scr[...] = hin * jnp.exp(cs_last) + st
            return carry

        lax.fori_loop(0, NCHUNK, chunk, 0)
        if nc:
            carried.after()

    res = pl.pallas_call(
        kern, name=name, grid=(4, NB),
        in_specs=[spec(256, 0), spec(128, 8), spec(128, 12), spec(256, 0), par(), par(), par()] + [_ANY] * nc,
        out_specs=[spec(256, 0), spec(256, 0)] + [_ANY] * nc,
        out_shape=[jax.ShapeDtypeStruct((T, D), F32), jax.ShapeDtypeStruct((T, D), F32)] + (carry.out_shapes() if nc else []),
        scratch_shapes=[pltpu.VMEM((CHUNK, 256), F32)] + (carry.sems() if nc else []),
        compiler_params=_cp(("arbitrary", "arbitrary") if nc else ("parallel", "parallel")),
    )(xa, xa, xa, dtx, dtb, alog, dsk, *(carry.arrays if nc else []))
    return res[0], res[1], res[2:]


def _ssd_bwd(xa, dtx, dtb, alog, dsk, hs, dy, *, name):
    spec, par = _ssd_specs(lambda g, b: (b, g))

    def kern(xs_ref, b_ref, c_ref, dtx_ref, dtb_ref, alog_ref, dsk_ref, hs_ref, dy_ref,
             dx_ref, db_ref, dc_ref, ddt_ref, ddtb_ref, dalog_ref, ddsk_ref, dh_scr):
        row = lax.broadcasted_iota(jnp.int32, (CHUNK, CHUNK), 0)
        col = lax.broadcasted_iota(jnp.int32, (CHUNK, CHUNK), 1)
        tri = row >= col
        trif = tri.astype(F32)
        trit = (row <= col).astype(F32)
        is_last = lax.broadcasted_iota(jnp.int32, (CHUNK, 1), 0) == CHUNK - 1
        head = lax.broadcasted_iota(jnp.int32, (1, 256), 1) // 64
        dtb_, a, dsk_ = dtb_ref[...], -jnp.exp(alog_ref[...]), dsk_ref[...]
        dh_scr[...] = jnp.zeros_like(dh_scr)

        @pl.when(pl.program_id(1) == 0)
        def _():
            ddtb_ref[...] = jnp.zeros_like(ddtb_ref)
            dalog_ref[...] = jnp.zeros_like(dalog_ref)
            ddsk_ref[...] = jnp.zeros_like(ddsk_ref)

        def hsum(v, me):
            return jnp.sum(jnp.where(me, v, 0.0), axis=1, keepdims=True)

        def chunk(ci, carry):
            c = NCHUNK - 1 - ci
            rows, x, bb, cb, raw, dt, cs = _ssd_chunk_common(c, xs_ref, b_ref, c_ref, dtx_ref, dtb_, a, trif)
            cs_last = cs[CHUNK - 1:CHUNK, :]
            ecs = jnp.exp(cs)
            dsx = jnp.exp(cs_last - cs)
            xdt = x * dt
            xdtb = xdt.astype(BF16)
            g = _dot(cb, bb, _NT)
            hin = hs_ref[rows, :]
            hinb = hin.astype(BF16)
            dy_ = dy_ref[rows, :]
            dyb = dy_.astype(BF16)
            dh = dh_scr[...]
            dhb = dh.astype(BF16)
            bdh = _dot(bb, dhb, _NN)
            yoff = ecs * _dot(cb, hinb, _NN)
            dxdt = dsx * bdh
            t1, t2, t3 = dy_ * yoff, xdt * bdh, dh * hin
            dg = jnp.zeros((CHUNK, CHUNK), F32)
            dcs = jnp.zeros((CHUNK, 256), F32)
            for e in range(4):
                me = head == e
                l_ = _head_decay(cs, me, tri)
                m = g * l_
                dxdt = dxdt + jnp.where(me, _dot(m.astype(BF16), dyb, _TN), 0.0)
                dm = _dot(jnp.where(me, dy_, 0.0).astype(BF16), xdtb, _NT)
                dg = dg + dm * l_
                w = dm * m
                dds = hsum(t2, me)
                dse = jnp.max(jnp.where(me, dsx, -jnp.inf), axis=1, keepdims=True)
                ecl = jnp.max(jnp.where(me, jnp.exp(cs_last), -jnp.inf), axis=1, keepdims=True)
                dcs_e = hsum(t1, me) + jnp.sum(w, axis=1, keepdims=True) - jnp.sum(w.T, axis=1, keepdims=True) - dds * dse
                last = jnp.sum(dds * dse, axis=0, keepdims=True) + ecl * jnp.sum(hsum(t3, me), axis=0, keepdims=True)
                dcs_e = dcs_e + jnp.where(is_last, last, 0.0)
                dcs = dcs + jnp.where(me, dcs_e, 0.0)
            dadt = jnp.dot(trit, dcs, precision=HIGHEST, preferred_element_type=F32)
            ddt = a * dadt
            dxx = dxdt * x
            dyx = dy_ * x
            dsk_acc = jnp.zeros((1, 256), F32)
            for e in range(4):
                me = head == e
                ddt = ddt + jnp.where(me, hsum(dxx, me), 0.0)
                dsk_acc = dsk_acc + jnp.where(me, jnp.sum(hsum(dyx, me), axis=0, keepdims=True), 0.0)
            draw = ddt * jax.nn.sigmoid(raw)
            ddt_ref[rows, :] = draw
            ddtb_ref[...] += _colsum(draw)
            dalog_ref[...] += _colsum(dadt * dt) * a
            ddsk_ref[...] += dsk_acc
            dx_ref[rows, :] = dxdt * dt + dsk_ * dy_
            edy = (ecs * dy_).astype(BF16)
            dgb = dg.astype(BF16)
            dc_ref[rows, :] = _dot(dgb, bb, _NN) + _dot(edy, hinb, _NT)
            db_ref[rows, :] = _dot(dgb, cb, _TN) + _dot((dsx * xdt).astype(BF16), dhb, _NT)
            dh_scr[...] = jnp.exp(cs_last) * dh + _dot(cb, edy, _TN)
            return carry

        lax.fori_loop(0, NCHUNK, chunk, 0)

    return pl.pallas_call(
        kern, name=name, grid=(4, NB),
        in_specs=[spec(256, 0), spec(128, 8), spec(128, 12), spec(256, 0), par(), par(), par(), spec(256, 0), spec(256, 0)],
        out_specs=[spec(256, 0), spec(128, 0), spec(128, 0), spec(256, 0), par(), par(), par()],
        out_shape=[jax.ShapeDtypeStruct((T, D), F32), jax.ShapeDtypeStruct((T, 512), F32), jax.ShapeDtypeStruct((T, 512), F32),
                   jax.ShapeDtypeStruct((T, D), F32), jax.ShapeDtypeStruct((1, D), F32), jax.ShapeDtypeStruct((1, D), F32),
                   jax.ShapeDtypeStruct((1, D), F32)],
        scratch_shapes=[pltpu.VMEM((CHUNK, 256), F32)],
        compiler_params=_cp(("arbitrary", "arbitrary")),
    )(xa, xa, xa, dtx, dtb, alog, dsk, hs, dy)


def _t5_bucket_np(dist):
    dist = np.maximum(dist, 0)
    max_exact = REL_BUCKETS // 2
    large = max_exact + (np.log(np.maximum(dist, 1) / max_exact) / np.log(REL_MAX_DIST / max_exact)
                         * (REL_BUCKETS - max_exact)).astype(np.int32)
    large = np.minimum(large, REL_BUCKETS - 1)
    return np.where(dist < max_exact, dist, large).astype(np.int32)


def _attn_bias(rel_bias):
    qi = np.arange(ATTN_BLK)[:, None]
    kj = np.arange(2 * ATTN_BLK)[None, :]
    delta = qi - kj + ATTN_BLK
    out = []
    for gi, (window, dil) in enumerate(ATTN_CONFIGS):
        in_band = (delta >= 0) & (delta <= window // dil)
        bucket = jnp.asarray(_t5_bucket_np(delta * dil).reshape(-1, 1))
        one_hot = (bucket == jnp.arange(REL_BUCKETS)[None, :]).astype(F32)
        tab = jnp.dot(one_hot, rel_bias[:, 4 * gi:4 * gi + 4], precision=HIGHEST).reshape(ATTN_BLK, 2 * ATTN_BLK, 4)
        out.append(jnp.where(jnp.asarray(in_band)[None], tab.transpose(2, 0, 1), NEG))
    return jnp.stack(out)


def _attn_band(ref, cur, prev):
    return jnp.concatenate([ref[prev, :], ref[cur, :]], axis=0).astype(BF16)


def _attn_head_masks():
    hi = lax.broadcasted_iota(jnp.int32, (1, 128), 1) >= 64
    return [jnp.logical_not(hi), hi]


def _attn_logits(qm, kband, bias, n, sub_blocks):
    kj = lax.broadcasted_iota(jnp.int32, (1, 2 * ATTN_BLK), 1)
    ok = jnp.logical_or(n % sub_blocks > 0, kj >= ATTN_BLK)
    return jnp.where(ok, _dot(qm, kband, _NT) * 0.125 + bias, NEG)


def _attn_specs(col0):
    qkv = [pl.BlockSpec((SEQ, 128), lambda p, s, c=col0 + 2 * i: (s, c + p)) for i in range(3)]
    seq = pl.BlockSpec((SEQ, 128), lambda p, s: (s, p))
    tab = pl.BlockSpec((2, ATTN_BLK, 2 * ATTN_BLK), lambda p, s: (p, 0, 0))
    return qkv, seq, tab


def _attn_rows(n):
    cur = pl.ds(pl.multiple_of(n * ATTN_BLK, ATTN_BLK), ATTN_BLK)
    prev = pl.ds(pl.multiple_of(jnp.maximum(n - 1, 0) * ATTN_BLK, ATTN_BLK), ATTN_BLK)
    return cur, prev


def _attn_fwd(qkv, col0, bias, sub_blocks, *, carry=None, name):
    nc = carry.n if carry is not None else 0

    def kern(*refs):
        q_ref, k_ref, v_ref, bias_ref = refs[:4]
        x_refs, (o_ref, l_ref), g_refs = refs[4:4 + nc], refs[4 + nc:6 + nc], refs[6 + nc:6 + 2 * nc]
        if nc:
            carried = _Carried(carry, x_refs, g_refs, refs[6 + 2 * nc:], pl.program_id(0) * NB + pl.program_id(1), 2 * NB)
            carried.before()
        masks = _attn_head_masks()

        def blk(n, carry):
            cur, prev = _attn_rows(n)
            q = q_ref[cur, :]
            kband, vband = _attn_band(k_ref, cur, prev), _attn_band(v_ref, cur, prev)
            out = jnp.zeros((ATTN_BLK, 128), F32)
            lse = jnp.zeros((ATTN_BLK, 128), F32)
            for hh, mk in enumerate(masks):
                logits = _attn_logits(jnp.where(mk, q, 0.0).astype(BF16), kband, bias_ref[hh], n, sub_blocks)
                m = jnp.max(logits, axis=-1, keepdims=True)
                p = jnp.exp(logits - m)
                den = jnp.sum(p, axis=-1, keepdims=True)
                out = jnp.where(mk, _dot((p / den).astype(BF16), vband, _NN), out)
                lse = jnp.where(mk, m + jnp.log(den), lse)
            o_ref[cur, :] = out
            l_ref[cur, :] = lse
            return carry

        lax.fori_loop(0, SEQ // ATTN_BLK, blk, 0)
        if nc:
            carried.after()

    qkv_specs, seq, tab = _attn_specs(col0)
    res = pl.pallas_call(
        kern, name=name, grid=(2, NB), in_specs=qkv_specs + [tab] + [_ANY] * nc,
        out_specs=[seq, seq] + [_ANY] * nc,
        out_shape=[jax.ShapeDtypeStruct((T, 256), F32)] * 2 + (carry.out_shapes() if nc else []),
        scratch_shapes=carry.sems() if nc else [],
        compiler_params=_cp(("arbitrary", "arbitrary") if nc else ("parallel", "parallel")),
    )(qkv, qkv, qkv, bias, *(carry.arrays if nc else []))
    return res[0], res[1], res[2:]


def _attn_bwd(qkv, col0, o, lse, do, dl, bias, sub_blocks, *, name):
    def kern(q_ref, k_ref, v_ref, o_ref, l_ref, do_ref, dl_ref, bias_ref, dq_ref, dk_ref, dv_ref, dbias_ref, dk_acc, dv_acc):
        masks = _attn_head_masks()

        @pl.when(pl.program_id(1) == 0)
        def _():
            dbias_ref[...] = jnp.zeros_like(dbias_ref)

        dk_acc[...] = jnp.zeros_like(dk_acc)
        dv_acc[...] = jnp.zeros_like(dv_acc)

        def blk(n, carry):
            cur, prev = _attn_rows(n)
            q, o_, lse_, do_, dl_ = q_ref[cur, :], o_ref[cur, :], l_ref[cur, :], do_ref[cur, :], dl_ref[cur, :]
            kband, vband = _attn_band(k_ref, cur, prev), _attn_band(v_ref, cur, prev)
            dq = jnp.zeros((ATTN_BLK, 128), F32)
            dkb = jnp.zeros((2 * ATTN_BLK, 128), F32)
            dvb = jnp.zeros((2 * ATTN_BLK, 128), F32)
            for hh, mk in enumerate(masks):
                qm = jnp.where(mk, q, 0.0).astype(BF16)
                dom = jnp.where(mk, do_, 0.0)
                domb = dom.astype(BF16)
                logits = _attn_logits(qm, kband, bias_ref[hh], n, sub_blocks)
                p = jnp.exp(logits - jnp.max(jnp.where(mk, lse_, -jnp.inf), axis=-1, keepdims=True))
                dd = jnp.sum(dom * o_, axis=-1, keepdims=True)
                dls = jnp.sum(jnp.where(mk, dl_, 0.0), axis=-1, keepdims=True)
                ds = p * (_dot(domb, vband, _NT) - dd + dls)
                dbias_ref[hh] += ds
                dsb = (ds * 0.125).astype(BF16)
                dq = jnp.where(mk, _dot(dsb, kband, _NN), dq)
                dkb = dkb + _dot(dsb, qm, _TN)
                dvb = dvb + _dot(p.astype(BF16), domb, _TN)
            dq_ref[cur, :] = dq.astype(dq_ref.dtype)
            dk_acc[prev, :] += dkb[:ATTN_BLK]
            dk_acc[cur, :] += dkb[ATTN_BLK:]
            dv_acc[prev, :] += dvb[:ATTN_BLK]
            dv_acc[cur, :] += dvb[ATTN_BLK:]
            return carry

        lax.fori_loop(0, SEQ // ATTN_BLK, blk, 0)
        dk_ref[...] = dk_acc[...].astype(dk_ref.dtype)
        dv_ref[...] = dv_acc[...].astype(dv_ref.dtype)

    qkv_specs, seq, tab = _attn_specs(col0)
    return pl.pallas_call(
        kern, name=name, grid=(2, NB),
        in_specs=qkv_specs + [seq] * 4 + [tab], out_specs=[seq, seq, seq, tab],
        out_shape=[jax.ShapeDtypeStruct((T, 256), BF16)] * 3 + [jax.ShapeDtypeStruct((4, ATTN_BLK, 2 * ATTN_BLK), F32)],
        scratch_shapes=[pltpu.VMEM((SEQ, 128), F32), pltpu.VMEM((SEQ, 128), F32)],
        compiler_params=_cp(("arbitrary", "arbitrary")),
    )(qkv, qkv, qkv, o, lse, do, dl, bias)


def _to_sub(t, dil):
    if dil == 1:
        return t
    return t.reshape(NB, SEQ // dil, dil, t.shape[-1]).transpose(0, 2, 1, 3).reshape(T, t.shape[-1])


def _from_sub(t, dil):
    if dil == 1:
        return t
    return t.reshape(NB, dil, SEQ // dil, t.shape[-1]).transpose(0, 2, 1, 3).reshape(T, t.shape[-1])


_ANY = pl.BlockSpec(memory_space=pl.ANY)
MESH = pl.DeviceIdType.MESH


def _slot(dev):
    return 4 * dev[0] + 2 * dev[1] + dev[2]


class _GatherSteps:
    def __init__(self, x_refs, out_refs, sems):
        self.x_refs, self.out_refs = x_refs, out_refs
        self.send_sems, self.recv_sems, self.local_sems = sems
        self.n = len(x_refs)
        x, y, c = lax.axis_index("x"), lax.axis_index("y"), lax.axis_index("c")
        self.c, self.me, self.sibling = c, (x, y, c), (x, y, 1 - c)
        self.chips = [(1 - x, y), (x, 1 - y), (1 - x, 1 - y)]

    def copies(self, k, block, to, own=False):
        return [pltpu.make_async_remote_copy(
            src_ref=self.x_refs[a] if own else self.out_refs[a].at[_slot(block)], dst_ref=self.out_refs[a].at[_slot(block)],
            send_sem=self.send_sems.at[k, a], recv_sem=self.recv_sems.at[k, a], device_id=to, device_id_type=MESH)
            for a in range(self.n)]

    def mine(self):
        return [pltpu.make_async_copy(self.x_refs[a], self.out_refs[a].at[_slot(self.me)], self.local_sems.at[a])
                for a in range(self.n)]

    def first(self):
        out = self.copies(0, self.me, self.sibling, own=True)
        for j, chip in enumerate(self.chips):
            out += self.copies(1 + j, self.me, (*chip, self.c), own=True)
        return out

    def passed(self, j):
        return self.copies(4 + j, (*self.chips[j], self.c), self.sibling)

    def start(self):
        for cp in self.mine() + self.first():
            cp.start()

    def pass_on(self):
        for j, chip in enumerate(self.chips):
            for cp in self.copies(1 + j, (*chip, self.c), self.me):
                cp.wait_recv()
            for cp in self.passed(j):
                cp.start()

    def finish(self):
        for cp in self.copies(0, self.sibling, self.me):
            cp.wait_recv()
        for j, chip in enumerate(self.chips):
            for cp in self.copies(4 + j, (*chip, 1 - self.c), self.me):
                cp.wait_recv()
        for cp in self.first() + [cp for j in range(3) for cp in self.passed(j)]:
            cp.wait_send()
        for cp in self.mine():
            cp.wait()


class _ChipSteps:
    def __init__(self, x_refs, out_refs, sems):
        self.x_refs, self.out_refs = x_refs, out_refs
        self.send_sems, self.recv_sems, self.local_sems = sems
        self.n = len(x_refs)
        x, y, self.c = lax.axis_index("x"), lax.axis_index("y"), lax.axis_index("c")
        self.my_chip = 2 * x + y
        self.chips = [(1 - x, y), (x, 1 - y), (1 - x, 1 - y)]

    def mine(self):
        return [pltpu.make_async_copy(self.x_refs[a].at[self.my_chip], self.out_refs[a].at[self.my_chip], self.local_sems.at[a])
                for a in range(self.n)]

    def copies(self, sending):
        return [pltpu.make_async_remote_copy(
            src_ref=self.x_refs[a].at[2 * px + py if sending else self.my_chip],
            dst_ref=self.out_refs[a].at[self.my_chip if sending else 2 * px + py],
            send_sem=self.send_sems.at[k, a], recv_sem=self.recv_sems.at[k, a], device_id=(px, py, self.c), device_id_type=MESH)
            for k, (px, py) in enumerate(self.chips) for a in range(self.n)]

    def start(self):
        for cp in self.mine() + self.copies(True):
            cp.start()

    def pass_on(self):
        pass

    def finish(self):
        for cp in self.copies(False):
            cp.wait_recv()
        for cp in self.copies(True):
            cp.wait_send()
        for cp in self.mine():
            cp.wait()


class _Carry:
    def __init__(self, kind, arrays):
        self.kind, self.arrays, self.n = kind, list(arrays), len(arrays)
        self.k = {"gather": 7, "chips": 3}[kind]

    def out_shapes(self):
        lead = (NDEV,) if self.kind == "gather" else ()
        return [jax.ShapeDtypeStruct((*lead, *a.shape), a.dtype) for a in self.arrays]

    def sems(self):
        return [pltpu.SemaphoreType.DMA((self.k, self.n)), pltpu.SemaphoreType.DMA((self.k, self.n)),
                pltpu.SemaphoreType.DMA((self.n,))]

    def steps(self, x_refs, out_refs, sems):
        return (_GatherSteps if self.kind == "gather" else _ChipSteps)(x_refs, out_refs, sems)

    def standalone(self, name):
        def kern(*refs):
            steps = self.steps(refs[:self.n], refs[self.n:2 * self.n], refs[2 * self.n:])
            steps.start()
            steps.pass_on()
            steps.finish()

        return pl.pallas_call(kern, name=name, in_specs=[_ANY] * self.n, out_specs=[_ANY] * self.n, out_shape=self.out_shapes(),
                              scratch_shapes=self.sems())(*self.arrays)


class _Carried:
    def __init__(self, carry, x_refs, out_refs, sems, step, n_steps):
        self.steps, self.step, self.n_steps = carry.steps(x_refs, out_refs, sems), step, n_steps

    def before(self):
        pl.when(self.step == 0)(self.steps.start)
        pl.when(self.step == self.n_steps // 2)(self.steps.pass_on)

    def after(self):
        pl.when(self.step == self.n_steps - 1)(self.steps.finish)


def _all_gather(shards, *, name):
    return _Carry("gather", shards).standalone(name)


def _exchange_sibling(arrs, *, name):
    n = len(arrs)

    def kern(*refs):
        x_refs, out_refs = refs[:n], refs[n:2 * n]
        send_sems, recv_sems = refs[2 * n:]
        x, y, c = lax.axis_index("x"), lax.axis_index("y"), lax.axis_index("c")
        sends = [pltpu.make_async_remote_copy(
            src_ref=x_refs[a].at[p, 1 - c], dst_ref=out_refs[a].at[p], send_sem=send_sems.at[p, a], recv_sem=recv_sems.at[p, a],
            device_id=(x, y, 1 - c), device_id_type=MESH) for p in range(4) for a in range(n)]
        for cp in sends:
            cp.start()
        for cp in sends:
            cp.wait_recv()
        for cp in sends:
            cp.wait_send()

    return pl.pallas_call(
        kern, name=name, in_specs=[_ANY] * n, out_specs=[_ANY] * n,
        out_shape=[jax.ShapeDtypeStruct((4, *a.shape[2:]), a.dtype) for a in arrs],
        scratch_shapes=[pltpu.SemaphoreType.DMA((4, n)), pltpu.SemaphoreType.DMA((4, n))],
    )(*arrs)


def _exchange_chips(arrs, *, name):
    return _Carry("chips", arrs).standalone(name)


def _add_own_half(full, other, c_idx, *, name):
    _, _, r, c = full.shape
    tile = r if r <= 512 else _pick(r, (512, 256))

    def kern(c_ref, f_ref, o_ref, out_ref):
        out_ref[...] = (f_ref[...].astype(F32) + o_ref[...].astype(F32)).astype(BF16)

    blk = pl.BlockSpec((None, tile, c), lambda p, i, c_ref: (p, i, 0))
    return pl.pallas_call(
        kern, name=name,
        grid_spec=pltpu.PrefetchScalarGridSpec(
            num_scalar_prefetch=1, grid=(4, r // tile),
            in_specs=[pl.BlockSpec((None, None, tile, c), lambda p, i, c_ref: (p, c_ref[0], i, 0)), blk], out_specs=blk),
        out_shape=jax.ShapeDtypeStruct((4, r, c), BF16), compiler_params=_cp(("parallel", "parallel")),
    )(c_idx, full, other)


def _adamw_reduce(parts, w, m, v, *, name):
    ns, r, c = parts[0].shape
    tile = _pick(r, (256, 128, 64, 32, 16, 8))

    def kern(*refs):
        p_refs, (w_ref, m_ref, v_ref, g_ref, d_ref, m2_ref, v2_ref) = refs[:DEPTH], refs[DEPTH:]
        for l in range(DEPTH):
            @pl.when(pl.program_id(0) == l)
            def _(p_ref=p_refs[l]):
                g = p_ref[0].astype(F32)
                for s in range(1, ns):
                    g = g + p_ref[s].astype(F32)
                g_ref[...] = g
                d_ref[...], m2_ref[...], v2_ref[...] = _adam_math(w_ref[...], g, m_ref[...], v_ref[...])

    p_specs = [pl.BlockSpec((ns, tile, c), lambda l_, i, l=l: (0, jnp.where(l_ == l, i, 0), 0)) for l in range(DEPTH)]
    blk = pl.BlockSpec((None, tile, c), lambda l_, i: (l_, i, 0))
    return pl.pallas_call(
        kern, name=name, grid=(DEPTH, r // tile), in_specs=p_specs + [blk, blk, blk],
        out_specs=[blk] * 4, out_shape=[jax.ShapeDtypeStruct((DEPTH, r, c), F32)] * 4,
        compiler_params=_cp(("arbitrary", "arbitrary")),
    )(*parts, w, m, v)


def _sum_slots(parts, *, name):
    ns, r, c = parts.shape
    tile = _pick(r, (256, 128, 64, 32, 16, 8))

    def kern(p_ref, o_ref):
        acc = p_ref[0].astype(F32)
        for d in range(1, ns):
            acc = acc + p_ref[d].astype(F32)
        o_ref[...] = acc

    return pl.pallas_call(
        kern, name=name, grid=(r // tile,), in_specs=[pl.BlockSpec((ns, tile, c), lambda i: (0, i, 0))],
        out_specs=pl.BlockSpec((tile, c), lambda i: (i, 0)), out_shape=jax.ShapeDtypeStruct((r, c), F32),
        compiler_params=_cp(("parallel",)),
    )(parts)


BIG = [
    ("ffn1_w13", (1024, 704), 1), ("ffn1_w2", (352, 1024), 0), ("w_in", (1024, 1154), 1), ("p_pool", (768, 128), 1),
    ("p_ssd", (128, 1024), 0), ("p_attn", (256, 128), 1), ("w_out", (128, 1024), 0), ("ffn2_w13", (1024, 704), 1),
    ("ffn2_w2", (352, 1024), 0),
]
BLOCKED = ("ffn1_w13", "ffn2_w13")


def _from_blocks(blk, ax):
    _, r, c = blk.shape
    return blk.transpose(1, 0, 2).reshape(r, NDEV * c) if ax == 1 else blk.reshape(NDEV * r, c)


def _to_blocks(full, ax):
    r, c = full.shape
    return full.reshape(r, NDEV, c // NDEV).transpose(1, 0, 2) if ax == 1 else full.reshape(NDEV, r // NDEV, c)


class _SmallPack:
    def __init__(self, shapes):
        self.shapes = shapes
        self.rows = [-(-int(np.prod(s)) // 128) for s in shapes]
        self.total = -(-sum(self.rows) // 256) * 256

    def pack(self, arrs):
        parts = []
        for a, s, r in zip(arrs, self.shapes, self.rows):
            assert tuple(a.shape) == tuple(s), (a.shape, s)
            flat = a.astype(F32).reshape(-1)
            parts.append(jnp.pad(flat, (0, r * 128 - flat.shape[0])).reshape(r, 128))
        parts.append(jnp.zeros((self.total - sum(self.rows), 128), F32))
        return jnp.concatenate(parts, axis=0)

    def unpack(self, packed):
        out, off = [], 0
        for s, r in zip(self.shapes, self.rows):
            out.append(packed[off:off + r].reshape(-1)[:int(np.prod(s))].reshape(s))
            off += r
        return out


W_IN_SEGMENTS = ([(O_Z, O_XBC), (O_XBC, O_DT), (O_G, O_END)] + [(o + 256 * gi, o + 256 * gi + 256) for gi in range(3)
                                                              for o in (O_Q, O_K, O_V)] + [(O_U, O_Z), (O_DT, O_Q)])
W_IN_SHARD = O_END // NDEV


def _w_in_to_padded(w):
    z = jnp.zeros((*w.shape[:-1], HC - DT0 - 16), w.dtype)
    return jnp.concatenate([w[..., a:b] for a, b in W_IN_SEGMENTS] + [z], axis=-1)


def _w_in_blocks_to_padded(blocks):
    pieces = []
    for a, b in W_IN_SEGMENTS:
        for d in range(a // W_IN_SHARD, (b - 1) // W_IN_SHARD + 1):
            pieces.append(blocks[d][:, max(a - W_IN_SHARD * d, 0):min(b - W_IN_SHARD * d, W_IN_SHARD)])
    pieces.append(jnp.zeros((blocks.shape[1], HC - DT0 - 16), blocks.dtype))
    return jnp.concatenate(pieces, axis=1)


def _w_in_padded_to_blocks(g):
    starts = np.cumsum([0] + [b - a for a, b in W_IN_SEGMENTS])
    by_origin = sorted(zip(W_IN_SEGMENTS, starts), key=lambda t: t[0][0])
    blocks = []
    for d in range(NDEV):
        lo_d, hi_d = W_IN_SHARD * d, W_IN_SHARD * (d + 1)
        pieces = [g[:, int(s) + max(a, lo_d) - a:int(s) + min(b, hi_d) - a] for (a, b), s in by_origin if max(a, lo_d) < min(b, hi_d)]
        blocks.append(jnp.concatenate(pieces, axis=1))
    return jnp.stack(blocks)


def _block_diag(pw):
    same_group = jnp.eye(4, dtype=pw.dtype)[:, None, :, None]
    return (same_group * pw[:, :, None, :]).reshape(POOL_W, POOL_W)


def _expand_heads(v):
    return jnp.repeat(v, 64).reshape(1, D)


def _ffn_fwd(x, x16, w13, w2, g, b, tag, carry_up=(), carry_f=()):
    act, act_t, h16, got_up = _ffn_up(x16, w13, carry=_Carry("gather", carry_up) if carry_up else None, name=f"{tag}_up")
    f = _mm(act, w2, carry=_Carry("gather", carry_f) if carry_f else None, name=f"{tag}_f")
    f, got_f = f if carry_f else (f, [])
    y, y16, y16t = _ln_fwd(x, f, g, b, scale=FFN_RES, name=f"{tag}_ln")
    return y, y16, y16t, (h16, act_t, f), list(got_up) + list(got_f)


def _ffn_bwd(x, x16t, w13, w2, g, b, saved, dy, tag, carry_dx=(), carry_dw13=()):
    h16, act_t, f = saved
    dres, df, dg, db = _ln_bwd(x, f, dy, g, b, scale=FFN_RES, name=f"{tag}_ln_bwd")
    dw2 = _mm(act_t, df, out_dtype=BF16, name=f"{tag}_dw2")
    dh = _ffn_down_bwd(df, w2, h16, name=f"{tag}_dh")
    dx = _mm(dh, w13, tb=True, add=dres, carry=_Carry("chips", carry_dx) if carry_dx else None, name=f"{tag}_dx")
    dx, got_dx = dx if carry_dx else (dx, [])
    dw13 = _mm(x16t, dh, out_blocked=True, out_dtype=BF16, carry=_Carry("chips", carry_dw13) if carry_dw13 else None,
               name=f"{tag}_dw13")
    dw13, got_dw13 = dw13 if carry_dw13 else (dw13, [])
    return dx, dw13, dw2, dg[0], db[0], list(got_dx) + list(got_dw13)


CARRIERS = {"ffn1_up": [], "mix_in": ["ffn1_w13", "p_pool", "p_ssd", "p_attn", "w_out"], "ssd": ["w_in"],
            "attn0": ["ffn1_w2", "ffn2_w2"], "attn1": [], "attn2": [], "ffn2_up": ["ffn2_w13"]}


def _layer_fwd(x0, x0_16, x0_16t, p, bias, nxt=None):
    def piece(spec):
        if isinstance(spec, tuple):
            half = nxt[spec[0]].shape[0] // 2
            return nxt[spec[0]][half * spec[1]:half * (spec[1] + 1)]
        return nxt[spec]

    send = {k: ([piece(s) for s in specs] if nxt is not None else []) for k, specs in CARRIERS.items()}
    gather = lambda k: _Carry("gather", send[k]) if nxt is not None else None
    got = {}
    x1, x1_16, x1_16t, s1, g1 = _ffn_fwd(x0, x0_16, p["ffn1_w13"], p["ffn1_w2"], p["ln1_g"], p["ln1_b"], "ffn1", send["ffn1_up"])
    got.update(zip(CARRIERS["ffn1_up"], g1))
    hcat = _mm(x1_16, p["w_in"], carry=gather("mix_in"), name="mix_in")
    if nxt is not None:
        hcat, g_in = hcat
        got.update(zip(CARRIERS["mix_in"], g_in))
    pooled, pooled_t = _pool_fwd(hcat, name="pool")
    ya_lin = _mm(pooled, p["pool_bd"], name="pool_lin")
    ya_pre, ya_pre_t = _affine_fwd(ya_lin, p["pool_b"], p["pool_scale"], name="pool_affine")
    ya = _mm(ya_pre, p["p_pool"], name="pool_out")
    xa = _conv_fwd(hcat, p["conv_w"], p["conv_b"], name="conv")
    dtx = jnp.repeat(hcat[:, DT0:DT0 + SSD_HEADS], 64, axis=1)
    ysc, hs, g_ssd = _ssd_fwd(xa, dtx, p["dt_bias"], p["a_log"], p["d_skip"], carry=gather("ssd"), name="ssd")
    got.update(zip(CARRIERS["ssd"], g_ssd))
    yb_pre, yb_pre_t = _gnorm_fwd(ysc, hcat, p["ssd_norm"], name="ssd_norm")
    yb = _mm(yb_pre, p["p_ssd"], name="ssd_out")
    outs, lses, attn_sv = [], [], []
    for gi, (_, dil) in enumerate(ATTN_CONFIGS):
        c0 = QKV0 + 768 * gi
        qkv, col0 = (hcat, c0 // 128) if dil == 1 else (_to_sub(hcat[:, c0:c0 + 768], dil), 0)
        o, l, g_at = _attn_fwd(qkv, col0, bias[gi], SEQ // dil // ATTN_BLK, carry=gather(f"attn{gi}"), name=f"attn{gi}")
        got.update(zip(CARRIERS[f"attn{gi}"], g_at))
        attn_sv.append((None if dil == 1 else qkv, o, l))
        outs.append(_from_sub(o, dil))
        lses.append(_from_sub(l, dil))
    yc_pre, yc_pre_t = _amerge_fwd(outs, lses, name="attn_merge")
    yc = _mm(yc_pre, p["p_attn"], name="attn_out")
    merged, merged_t = _merge_fwd(hcat, ya, yb, yc, p["gate_b"], name="merge")
    mix = _mm(merged, p["w_out"], name="mix_out")
    x2, x2_16, x2_16t = _ln_fwd(x1, mix, p["ln2_g"], p["ln2_b"], scale=1.0, name="ln2")
    x3, x3_16, x3_16t, s3, g3 = _ffn_fwd(x2, x2_16, p["ffn2_w13"], p["ffn2_w2"], p["ln3_g"], p["ln3_b"], "ffn2", send["ffn2_up"])
    got.update(zip(CARRIERS["ffn2_up"], g3))
    for n in {s[0] for s in got if isinstance(s, tuple)}:
        got[n] = jnp.concatenate([got.pop((n, 0)), got.pop((n, 1))], axis=1)
    saved = dict(x0=x0, x0_16t=x0_16t, s1=s1, x1=x1, x1_16t=x1_16t, hcat=hcat, pooled_t=pooled_t, ya_lin=ya_lin, ya_pre_t=ya_pre_t,
                 ya=ya, xa=xa, hs=hs, ysc=ysc, yb_pre_t=yb_pre_t, yb=yb, outs=outs, lses=lses, attn_sv=attn_sv, yc_pre_t=yc_pre_t,
                 yc=yc, merged_t=merged_t, mix=mix, x2=x2, x2_16t=x2_16t, s3=s3)
    return x3, x3_16, x3_16t, saved, got


BWD_CARRIERS = {"ffn2_dx": ["w_in"], "ffn2_dw13": ["ffn1_w13"], "mix_in_dx": ["ffn2_w13"], "mix_in_dw": ["ffn1_w2", "ffn2_w2"],
                "ffn1_dx": ["p_pool", "p_ssd", "p_attn", "w_out"]}


def _layer_bwd(p, bias, sv, dx3, pending=None):
    send = {k: ([pending[n] for n in names] if pending is not None else []) for k, names in BWD_CARRIERS.items()}
    done = {}
    g = {}
    dx2, g["ffn2_w13"], g["ffn2_w2"], g["ln3_g"], g["ln3_b"], got = _ffn_bwd(
        sv["x2"], sv["x2_16t"], p["ffn2_w13"], p["ffn2_w2"], p["ln3_g"], p["ln3_b"], sv["s3"], dx3, "ffn2",
        send["ffn2_dx"], send["ffn2_dw13"])
    done.update(zip(BWD_CARRIERS["ffn2_dx"] + BWD_CARRIERS["ffn2_dw13"], got))
    hcat = sv["hcat"]
    dres, dmix, dg2, db2 = _ln_bwd(sv["x1"], sv["mix"], dx2, p["ln2_g"], p["ln2_b"], scale=1.0, name="ln2_bwd")
    g["ln2_g"], g["ln2_b"] = dg2[0], db2[0]
    dmerged = _mm(dmix, p["w_out"], tb=True, name="mix_out_dx")
    g["w_out"] = _mm(sv["merged_t"], dmix, out_dtype=BF16, name="mix_out_dw")
    dg0, dg1, dg2_, dya, dyb, dyc, gb0, gb1, gb2 = _merge_bwd(hcat, sv["ya"], sv["yb"], sv["yc"], dmerged, p["gate_b"],
                                                               name="merge_bwd")
    g["gate_b"] = jnp.concatenate([gb0, gb1, gb2], axis=0)
    dya_pre = _mm(dya, p["p_pool"], tb=True, name="pool_out_dx")
    g["p_pool"] = _mm(sv["ya_pre_t"], dya, out_dtype=BF16, name="pool_out_dw")
    dya_lin, dpb, dps = _affine_bwd(sv["ya_lin"], dya_pre, p["pool_b"], p["pool_scale"], name="pool_affine_bwd")
    g["pool_b"], g["pool_scale"] = dpb[0].reshape(4, POOL_GDIM), dps[0]
    dpooled = _mm(dya_lin, p["pool_bd"], tb=True, name="pool_lin_dx")
    dbd = _mm(sv["pooled_t"], dya_lin, name="pool_lin_dw")
    g["pool_w"] = jnp.stack([dbd[i * POOL_GDIM:(i + 1) * POOL_GDIM, i * POOL_GDIM:(i + 1) * POOL_GDIM] for i in range(4)])
    du = _pool_bwd(dpooled, name="pool_bwd")
    dyb_pre = _mm(dyb, p["p_ssd"], tb=True, name="ssd_out_dx")
    g["p_ssd"] = _mm(sv["yb_pre_t"], dyb, out_dtype=BF16, name="ssd_out_dw")
    dysc, dz, dnw = _gnorm_bwd(sv["ysc"], hcat, dyb_pre, p["ssd_norm"], name="ssd_norm_bwd")
    g["ssd_norm"] = dnw[0]
    dtx = jnp.repeat(hcat[:, DT0:DT0 + SSD_HEADS], 64, axis=1)
    dxs, dbm, dcm, ddtx, ddtb, dalog, ddsk = _ssd_bwd(sv["xa"], dtx, p["dt_bias"], p["a_log"], p["d_skip"], sv["hs"], dysc,
                                                     name="ssd_bwd")
    g["dt_bias"], g["a_log"], g["d_skip"] = ddtb[0, ::64], dalog[0, ::64], ddsk[0, ::64]
    dxa = jnp.concatenate([dxs, dbm, dcm], axis=1)
    dxbc, dcw, dcb = _conv_bwd(hcat, p["conv_w"], p["conv_b"], dxa, name="conv_bwd")
    g["conv_w"], g["conv_b"] = dcw, dcb[0]
    ddt = jnp.pad(ddtx[:, ::64], ((0, 0), (0, HC - DT0 - SSD_HEADS))).astype(BF16)
    dyc_pre = _mm(dyc, p["p_attn"], tb=True, name="attn_out_dx")
    g["p_attn"] = _mm(sv["yc_pre_t"], dyc, out_dtype=BF16, name="attn_out_dw")
    am = _amerge_bwd(sv["outs"], sv["lses"], dyc_pre, name="attn_merge_bwd")
    dqkv, dbias = [], []
    for gi, (_, dil) in enumerate(ATTN_CONFIGS):
        qkv_sub, o, l = sv["attn_sv"][gi]
        qkv, col0 = (hcat, (QKV0 + 768 * gi) // 128) if dil == 1 else (qkv_sub, 0)
        dq, dk, dv, dbi = _attn_bwd(qkv, col0, o, l, _to_sub(am[gi], dil), _to_sub(am[3 + gi], dil), bias[gi],
                                    SEQ // dil // ATTN_BLK, name=f"attn{gi}_bwd")
        dqkv += [_from_sub(t, dil) for t in (dq, dk, dv)]
        dbias.append(dbi)
    g["attn_bias"] = jnp.stack(dbias)
    dhcat = jnp.concatenate([dz, dxbc, dg0, dg1, dg2_, *dqkv, du, ddt], axis=1)
    chips = lambda k: _Carry("chips", send[k]) if pending is not None else None
    dx1 = _mm(dhcat, p["w_in"], tb=True, add=dres, carry=chips("mix_in_dx"), name="mix_in_dx")
    g["w_in"] = _mm(sv["x1_16t"], dhcat, out_dtype=BF16, carry=chips("mix_in_dw"), name="mix_in_dw")
    if pending is not None:
        (dx1, got_dx), (g["w_in"], got_dw) = dx1, g["w_in"]
        done.update(zip(BWD_CARRIERS["mix_in_dx"] + BWD_CARRIERS["mix_in_dw"], list(got_dx) + list(got_dw)))
    dx0, g["ffn1_w13"], g["ffn1_w2"], g["ln1_g"], g["ln1_b"], got = _ffn_bwd(
        sv["x0"], sv["x0_16t"], p["ffn1_w13"], p["ffn1_w2"], p["ln1_g"], p["ln1_b"], sv["s1"], dx1, "ffn1", send["ffn1_dx"])
    done.update(zip(BWD_CARRIERS["ffn1_dx"], got))
    return dx0, g, done


def _local_step(x, tgt, lw, rel_bias, shards=None, prepare=None, chip_sums=None):
    bias, bias_vjp = jax.vjp(_attn_bias, rel_bias)
    y, y16, saved, lw = x, x.astype(BF16), [], list(lw)
    y16t = y16.T
    n_layers = len(lw) if shards is None else len(shards)
    for l in range(n_layers):
        nxt = shards[l + 1] if shards is not None and l + 1 < n_layers else None
        y, y16, y16t, sv, got = _layer_fwd(y, y16, y16t, lw[l], bias, nxt)
        saved.append(sv)
        if nxt is not None:
            lw.append(prepare(got, l + 1))
    dy, sq = _loss_kernel(y, tgt, name="loss")
    grads, sums, pending = [None] * n_layers, [None] * n_layers, None
    for l in reversed(range(n_layers)):
        dy, grads[l], done = _layer_bwd(lw[l], bias, saved[l], dy, pending)
        if pending is not None:
            sums[l + 1] = done
        pending = chip_sums(grads[l]) if chip_sums is not None else None
    if pending is not None:
        names = list(pending)
        sums[0] = dict(zip(names, _exchange_chips([pending[n] for n in names], name="exchange_grads_chips")))
    (d_rel,) = bias_vjp(sum(g.pop("attn_bias") for g in grads))
    return sq, dy, grads, d_rel, sums


SMALL_REPL = ["ln1_g", "ln1_b", "pool_w", "pool_b", "pool_scale", "conv_b", "dt_bias", "a_log", "d_skip", "ssd_norm",
              "ln2_g", "ln2_b", "ln3_g", "ln3_b", "rel_bias"]
SMALL_SHARD = ["gate_b", "conv_w"]
WEIGHTS = ['ffn1_w13', 'ffn1_w2', 'ln1_g', 'ln1_b', 'w_in', 'gate_b', 'pool_w', 'pool_b', 'pool_scale', 'conv_w', 'conv_b',
           'dt_bias', 'a_log', 'd_skip', 'ssd_norm', 'rel_bias', 'p_pool', 'p_ssd', 'p_attn', 'w_out', 'ln2_g', 'ln2_b',
           'ffn2_w13', 'ffn2_w2', 'ln3_g', 'ln3_b']


def _step(w, m, v, x, tgt):
    my_c = lax.axis_index("c")
    dev = 4 * lax.axis_index("x") + 2 * lax.axis_index("y") + my_c
    shards = [{n: w[n][l].astype(BF16) for n, _, _ in BIG} for l in range(DEPTH)]
    first = dict(zip([n for n, _, _ in BIG], _all_gather([shards[0][n] for n, _, _ in BIG], name="gather_weights")))
    sp_in = _SmallPack([w[n].shape for n in SMALL_SHARD])
    small_g = _all_gather([sp_in.pack([w[n] for n in SMALL_SHARD])], name="gather_small")[0]
    gate_b = jnp.stack([sp_in.unpack(small_g[d])[0] for d in range(NDEV)], axis=2).reshape(DEPTH, 3, D)
    conv_w = jnp.stack([sp_in.unpack(small_g[d])[1] for d in range(NDEV)], axis=2).reshape(DEPTH, 4, 2048)

    def prepare(blocks, l):
        p = {n: blocks[n] if n in BLOCKED else _from_blocks(blocks[n], ax) for n, _, ax in BIG if n != "w_in"}
        p["w_in"] = _w_in_blocks_to_padded(blocks["w_in"])
        p["pool_bd"] = _block_diag(w["pool_w"][l]).astype(BF16)
        p["gate_b"], p["conv_w"] = gate_b[l], conv_w[l]
        for n in ("ln1_g", "ln1_b", "ln2_g", "ln2_b", "ln3_g", "ln3_b", "pool_scale", "conv_b", "ssd_norm"):
            p[n] = w[n][l][None, :]
        p["pool_b"] = w["pool_b"][l].reshape(1, POOL_W)
        for n in ("dt_bias", "a_log", "d_skip"):
            p[n] = _expand_heads(w[n][l])
        return p

    c_idx = my_c.reshape(1).astype(jnp.int32)

    def chip_sums(g):
        parts = []
        for n, (r, c), ax in BIG:
            blocks = g[n] if n in BLOCKED else _w_in_padded_to_blocks(g[n]) if n == "w_in" else _to_blocks(g[n], ax)
            parts.append(blocks.reshape(4, 2, r, c))
        from_sibling = _exchange_sibling(parts, name="exchange_grads_cores")
        return {n: _add_own_half(p, o, c_idx, name=f"add_cores_{n}") for p, o, (n, _, _) in zip(parts, from_sibling, BIG)}

    sq, dx, grads, d_rel, sums = _local_step(x.reshape(T, D), tgt.reshape(T, D), [prepare(first, 0)], w["rel_bias"], shards,
                                             prepare, chip_sums)
    loss = lax.psum(jnp.sum(sq) * (0.5 / D), ("x", "y", "c"))
    small_names = SMALL_REPL + SMALL_SHARD
    small = {n: (d_rel if n == "rel_bias" else jnp.stack([g[n] for g in grads])) for n in small_names}
    sp = _SmallPack([small[n].shape for n in small_names])
    gsmall = sp.unpack(_sum_slots(_all_gather([sp.pack([small[n] for n in small_names])], name="gather_small_grads")[0],
                                  name="sum_small_grads"))
    gout = {}
    for n, gv in zip(small_names, gsmall):
        if n in SMALL_SHARD:
            width = w[n].shape[-1]
            gv = lax.dynamic_slice_in_dim(gv, dev * width, width, axis=2)
        gout[n] = gv
    delta, new_m, new_v = {}, {}, {}
    for n, _, _ in BIG:
        gout[n], delta[n], new_m[n], new_v[n] = _adamw_reduce([s[n] for s in sums], w[n], m[n], v[n], name=f"adamw_{n}")
    spa = _SmallPack([w[n].shape for n in small_names])
    res = _adamw(spa.pack([w[n] for n in small_names]), spa.pack([gout[n] for n in small_names]),
                 spa.pack([m[n] for n in small_names]), spa.pack([v[n] for n in small_names]), name="adamw_small")
    for out, packed in zip((delta, new_m, new_v), res):
        for n, val in zip(small_names, spa.unpack(packed)):
            out[n] = val
    return (loss, dx.reshape(NB, SEQ, D), *[gout[n] for n in WEIGHTS], *[delta[n] for n in WEIGHTS],
            *[new_m[n] for n in WEIGHTS], *[new_v[n] for n in WEIGHTS])


def kernel(x, ffn1_w13, ffn1_w2, ln1_g, ln1_b, w_in, gate_b, pool_w, pool_b, pool_scale, conv_w, conv_b, dt_bias, a_log, d_skip, ssd_norm, rel_bias, p_pool, p_ssd, p_attn, w_out, ln2_g, ln2_b, ffn2_w13, ffn2_w2, ln3_g, ln3_b, loss_target, m_ffn1_w13, m_ffn1_w2, m_ln1_g, m_ln1_b, m_w_in, m_gate_b, m_pool_w, m_pool_b, m_pool_scale, m_conv_w, m_conv_b, m_dt_bias, m_a_log, m_d_skip, m_ssd_norm, m_rel_bias, m_p_pool, m_p_ssd, m_p_attn, m_w_out, m_ln2_g, m_ln2_b, m_ffn2_w13, m_ffn2_w2, m_ln3_g, m_ln3_b, v_ffn1_w13, v_ffn1_w2, v_ln1_g, v_ln1_b, v_w_in, v_gate_b, v_pool_w, v_pool_b, v_pool_scale, v_conv_w, v_conv_b, v_dt_bias, v_a_log, v_d_skip, v_ssd_norm, v_rel_bias, v_p_pool, v_p_ssd, v_p_attn, v_w_out, v_ln2_g, v_ln2_b, v_ffn2_w13, v_ffn2_w2, v_ln3_g, v_ln3_b):
    given = dict(locals())
    w = {n: given[n] for n in WEIGHTS}
    m = {n: given["m_" + n] for n in WEIGHTS}
    v = {n: given["v_" + n] for n in WEIGHTS}
    return _step(w, m, v, x, loss_target)
```
